```python
import jax, jax.numpy as jnp
from jax import lax
import numpy as np

D_MODEL = 1024
BATCH = 8
SEQ = 4096
DEPTH = 1

A_HEADS = 8
A_KV_HEADS = 2
A_HEAD_DIM = 64
IDX_HEADS = 16
IDX_DIM = 64
IDX_TOPK_MAX = 256
Q_BLOCK = 128
B_HEADS = 8
B_KEY_DIM = 64
B_VAL_DIM = 64
CONV_WIDTH = 4
CHUNK = 64
ROPE_THETA = 500000.0
ROPE_FRACTION_DEN = 4
PEER_HEADS = 8
PEER_KEY_DIM = 128
PEER_N_KEYS = 128
PEER_N_EXPERTS = PEER_N_KEYS * PEER_N_KEYS
PEER_TOPK = 16
PEER_TOKEN_BLOCK = 128
EPS = 1e-6

A_WIDTH = A_HEADS * A_HEAD_DIM
KV_WIDTH = A_KV_HEADS * A_HEAD_DIM
B_QK_WIDTH = B_HEADS * B_KEY_DIM
B_V_WIDTH = B_HEADS * B_VAL_DIM
CONV_CHANNELS = 2 * B_QK_WIDTH + B_V_WIDTH
IN_SPLITS = (A_WIDTH, KV_WIDTH, KV_WIDTH, IDX_HEADS * IDX_DIM, IDX_DIM, IDX_HEADS,
             B_QK_WIDTH, B_QK_WIDTH, B_V_WIDTH, B_V_WIDTH, B_HEADS, B_HEADS, D_MODEL, D_MODEL)
IN_WIDTH = sum(IN_SPLITS)

kernel_name = 'hybrid_dsa_gdn_peer_adaln'


def rms_norm(x, gain=None):
    xf = x.astype(jnp.float32)
    y = xf * lax.rsqrt(jnp.mean(xf * xf, axis=-1, keepdims=True) + EPS)
    if gain is not None:
        y = y * gain.astype(jnp.float32)
    return y.astype(x.dtype)


def l2_normalize(x):
    return x * lax.rsqrt(jnp.sum(x * x, axis=-1, keepdims=True) + EPS)


def partial_rotary(x, positions):
    hd = x.shape[-1]
    rd = hd // ROPE_FRACTION_DEN
    half = rd // 2
    inv_freq = jnp.power(jnp.float32(ROPE_THETA), -jnp.arange(half, dtype=jnp.float32) * (2.0 / rd))
    ang = positions.astype(jnp.float32)[..., None] * inv_freq
    cos = jnp.cos(ang)[:, :, None, :].astype(x.dtype)
    sin = jnp.sin(ang)[:, :, None, :].astype(x.dtype)
    x1 = x[..., :half]
    x2 = x[..., half:rd]
    return jnp.concatenate([x1 * cos - x2 * sin, x2 * cos + x1 * sin, x[..., rd:]], axis=-1)


def causal_short_conv(x, w):
    width = w.shape[0]
    seq = x.shape[1]
    xp = jnp.pad(x, ((0, 0), (width - 1, 0), (0, 0)))
    out = xp[:, 0:seq] * w[0]
    for i in range(1, width):
        out = out + xp[:, i:i + seq] * w[i]
    return out


def dsa_attention(q, k, v, q_idx, k_idx, w_idx):
    bsz, seq, _, dh = q.shape
    topk = min(IDX_TOPK_MAX, seq // 4)
    n_blocks = seq // Q_BLOCK
    rep = A_HEADS // A_KV_HEADS
    idx_scale = (IDX_HEADS ** -0.5) * (IDX_DIM ** -0.5)
    spos = jnp.arange(seq)

    def block(blk):
        t0 = blk * Q_BLOCK
        q_b = lax.dynamic_slice_in_dim(q, t0, Q_BLOCK, axis=1)
        qi_b = lax.dynamic_slice_in_dim(q_idx, t0, Q_BLOCK, axis=1)
        wi_b = lax.dynamic_slice_in_dim(w_idx, t0, Q_BLOCK, axis=1).astype(jnp.float32) * idx_scale
        tpos = t0 + jnp.arange(Q_BLOCK)
        logits = jnp.einsum('bthd,bsd->bths', qi_b, k_idx).astype(jnp.float32)
        score = jnp.einsum('bth,bths->bts', wi_b, jax.nn.relu(logits))
        causal = spos[None, :] <= tpos[:, None]
        score = jnp.where(causal[None], score, -jnp.inf)
        _, sel = lax.top_k(score, topk)
        valid = sel <= tpos[None, :, None]
        k_sel = jax.vmap(lambda kb_, ib: kb_[ib])(k, sel)
        v_sel = jax.vmap(lambda vb_, ib: vb_[ib])(v, sel)
        qg = q_b.reshape(bsz, Q_BLOCK, A_KV_HEADS, rep, dh)
        s = jnp.einsum('btgrd,btkgd->btgrk', qg, k_sel).astype(jnp.float32) * (dh ** -0.5)
        s = jnp.where(valid[:, :, None, None, :], s, -jnp.inf)
        p = jax.nn.softmax(s, axis=-1).astype(v.dtype)
        o = jnp.einsum('btgrk,btkgd->btgrd', p, v_sel)
        return o.reshape(bsz, Q_BLOCK, A_HEADS * dh)

    out = lax.map(block, jnp.arange(n_blocks))
    return jnp.moveaxis(out, 0, 1).reshape(bsz, seq, A_HEADS * dh)


def gated_delta_rule(q, k, v, g, beta):
    bsz, seq, nh, dk = q.shape
    dv = v.shape[-1]
    n = seq // CHUNK

    def chunks(t):
        t = jnp.moveaxis(t, 2, 1)
        return t.reshape(t.shape[:2] + (n, CHUNK) + t.shape[3:])

    q = chunks(q * (dk ** -0.5))
    k = chunks(k)
    v = chunks(v)
    g = chunks(g)
    beta = chunks(beta)
    gc = jnp.cumsum(g, axis=-1)
    pos = jnp.arange(CHUNK)
    incl = pos[:, None] >= pos[None, :]
    strict = pos[:, None] > pos[None, :]
    decay = jnp.exp(jnp.where(incl, gc[..., :, None] - gc[..., None, :], -jnp.inf))
    kb = k * beta[..., None]
    vb = v * beta[..., None]
    lower = jnp.where(strict, jnp.einsum('bhncd,bhnjd->bhncj', kb, k) * decay, 0.0)
    eye = jnp.eye(CHUNK, dtype=q.dtype)
    tmat = lax.linalg.triangular_solve(eye + lower, jnp.broadcast_to(eye, lower.shape),
                                       left_side=True, lower=True, unit_diagonal=True)
    u = tmat @ vb
    kcd = tmat @ (kb * jnp.exp(gc)[..., None])
    intra = jnp.einsum('bhncd,bhnjd->bhncj', q, k) * decay

    def step(state, xs):
        qc, kc, uc, kcdc, gcc, intrac = xs
        v_new = uc - kcdc @ state
        out = (qc * jnp.exp(gcc)[..., None]) @ state + intrac @ v_new
        glast = gcc[..., -1]
        state = state * jnp.exp(glast)[..., None, None] + jnp.einsum(
            'bhcd,bhce->bhde', kc * jnp.exp(glast[..., None] - gcc)[..., None], v_new)
        return state, out

    xs = tuple(jnp.moveaxis(t, 2, 0) for t in (q, k, u, kcd, gc, intra))
    state0 = jnp.zeros((bsz, nh, dk, dv), q.dtype)
    _, out = lax.scan(step, state0, xs)
    out = jnp.moveaxis(out, 0, 2).reshape(bsz, nh, seq, dv)
    return jnp.moveaxis(out, 1, 2)


def token_mixers(n, positions, w_in, conv_w, a_log, dt_bias, norm_b_w, w_pa, w_pb, w_o):
    bsz, seq, _ = n.shape
    proj = n @ w_in
    offs = np.cumsum(IN_SPLITS)[:-1].tolist()
    (qa, ka, va, qi, ki, wi, qb, kb, vb, zb, bb, ab, gate_a, gate_b) = jnp.split(proj, offs, axis=-1)

    qa = partial_rotary(qa.reshape(bsz, seq, A_HEADS, A_HEAD_DIM), positions)
    ka = partial_rotary(ka.reshape(bsz, seq, A_KV_HEADS, A_HEAD_DIM), positions)
    va = va.reshape(bsz, seq, A_KV_HEADS, A_HEAD_DIM)
    qi = partial_rotary(qi.reshape(bsz, seq, IDX_HEADS, IDX_DIM), positions)
    ki = partial_rotary(ki.reshape(bsz, seq, 1, IDX_DIM), positions)[:, :, 0]
    o_a = dsa_attention(qa, ka, va, qi, ki, wi)

    qkv = jax.nn.silu(causal_short_conv(jnp.concatenate([qb, kb, vb], axis=-1), conv_w))
    qb, kb, vb = jnp.split(qkv, [B_QK_WIDTH, 2 * B_QK_WIDTH], axis=-1)
    qb = l2_normalize(qb.reshape(bsz, seq, B_HEADS, B_KEY_DIM).astype(jnp.float32))
    kb = l2_normalize(kb.reshape(bsz, seq, B_HEADS, B_KEY_DIM).astype(jnp.float32))
    vb = vb.reshape(bsz, seq, B_HEADS, B_VAL_DIM).astype(jnp.float32)
    beta = jax.nn.sigmoid(bb.astype(jnp.float32))
    g = -jnp.exp(a_log.astype(jnp.float32)) * jax.nn.softplus(ab.astype(jnp.float32) + dt_bias.astype(jnp.float32))
    o_b = gated_delta_rule(qb, kb, vb, g, beta)
    z = jax.nn.silu(zb.reshape(bsz, seq, B_HEADS, B_VAL_DIM).astype(jnp.float32))
    o_b = (rms_norm(o_b, norm_b_w) * z).astype(n.dtype).reshape(bsz, seq, B_V_WIDTH)

    merged = jax.nn.sigmoid(gate_a) * (o_a @ w_pa) + jax.nn.sigmoid(gate_b) * (o_b @ w_pb)
    return merged @ w_o


def peer_channel_mixer(xn, wq, subkeys, u_tab, v_tab):
    bsz, seq, d = xn.shape
    kk = PEER_TOPK
    q = (xn @ wq).reshape(bsz, seq, PEER_HEADS, 2, PEER_KEY_DIM // 2)
    s = jnp.einsum('bshpd,hpnd->bshpn', q, subkeys).astype(jnp.float32)
    v1, i1 = lax.top_k(s[..., 0, :], kk)
    v2, i2 = lax.top_k(s[..., 1, :], kk)
    cand = (v1[..., :, None] + v2[..., None, :]).reshape(bsz, seq, PEER_HEADS, kk * kk)
    vals, ci = lax.top_k(cand, kk)
    e1 = jnp.take_along_axis(i1, ci // kk, axis=-1)
    e2 = jnp.take_along_axis(i2, ci % kk, axis=-1)
    experts = e1 * PEER_N_KEYS + e2
    gates = jax.nn.softmax(vals, axis=-1).astype(xn.dtype)
    n_tok = bsz * seq
    nb = n_tok // PEER_TOKEN_BLOCK
    xs = (xn.reshape(nb, PEER_TOKEN_BLOCK, d),
          experts.reshape(nb, PEER_TOKEN_BLOCK, PEER_HEADS, kk),
          gates.reshape(nb, PEER_TOKEN_BLOCK, PEER_HEADS, kk))

    def block(args):
        xb, eb, gb = args
        a = jnp.einsum('td,thkd->thk', xb, u_tab[eb])
        coef = jax.nn.gelu(a, approximate=False) * gb
        return jnp.einsum('thk,thkd->td', coef, v_tab[eb])

    out = lax.map(block, xs)
    return out.reshape(bsz, seq, d)


def setup_inputs(seed: int = 0) -> dict:
    key = jax.random.key(seed)
    ks = jax.random.split(key, 20)
    f32 = jnp.float32
    x = jax.random.normal(ks[0], (BATCH, SEQ, D_MODEL), f32)
    c = jax.random.normal(ks[1], (BATCH, D_MODEL), f32)
    offset = jax.random.randint(ks[2], (BATCH, 1), 0, SEQ)
    positions = (offset + jnp.arange(SEQ)[None, :]).astype(jnp.int32)
    w_ada = jax.random.normal(ks[3], (DEPTH, D_MODEL, 6 * D_MODEL), f32) * D_MODEL ** -0.5
    b_ada = 0.02 * jax.random.normal(ks[4], (DEPTH, 6 * D_MODEL), f32)
    w_in = jax.random.normal(ks[5], (DEPTH, D_MODEL, IN_WIDTH), f32) * D_MODEL ** -0.5
    conv_w = jax.random.normal(ks[6], (DEPTH, CONV_WIDTH, CONV_CHANNELS), f32) * CONV_WIDTH ** -0.5
    a_log = jnp.log(jax.random.uniform(ks[7], (DEPTH, B_HEADS), f32, minval=1.0, maxval=16.0))
    dt = jnp.exp(jax.random.uniform(ks[8], (DEPTH, B_HEADS), f32, minval=math_log(1e-3), maxval=math_log(1e-1)))
    dt_bias = dt + jnp.log(-jnp.expm1(-dt))
    norm_b_w = 1.0 + 0.02 * jax.random.normal(ks[9], (DEPTH, B_VAL_DIM), f32)
    w_pa = jax.random.normal(ks[10], (DEPTH, A_WIDTH, D_MODEL), f32) * A_WIDTH ** -0.5
    w_pb = jax.random.normal(ks[11], (DEPTH, B_V_WIDTH, D_MODEL), f32) * B_V_WIDTH ** -0.5
    w_o = jax.random.normal(ks[12], (DEPTH, D_MODEL, D_MODEL), f32) * D_MODEL ** -0.5
    peer_wq = jax.random.normal(ks[13], (DEPTH, D_MODEL, PEER_HEADS * PEER_KEY_DIM), f32) * D_MODEL ** -0.5
    peer_subkeys = jax.random.normal(ks[14], (DEPTH, PEER_HEADS, 2, PEER_N_KEYS, PEER_KEY_DIM // 2), f32) * (PEER_KEY_DIM // 2) ** -0.5
    peer_u = jax.random.normal(ks[15], (DEPTH, PEER_N_EXPERTS, D_MODEL), f32) * D_MODEL ** -0.5
    peer_v = jax.random.normal(ks[16], (DEPTH, PEER_N_EXPERTS, D_MODEL), f32) * PEER_HEADS ** -0.5
    final_norm_w = 1.0 + 0.02 * jax.random.normal(ks[17], (D_MODEL,), f32)
    return {'x': x, 'c': c, 'positions': positions, 'w_ada': w_ada, 'b_ada': b_ada, 'w_in': w_in,
            'conv_w': conv_w, 'a_log': a_log, 'dt_bias': dt_bias, 'norm_b_w': norm_b_w,
            'w_pa': w_pa, 'w_pb': w_pb, 'w_o': w_o, 'peer_wq': peer_wq, 'peer_subkeys': peer_subkeys,
            'peer_u': peer_u, 'peer_v': peer_v, 'final_norm_w': final_norm_w}


def math_log(v):
    return float(np.log(v))


def reference(x, c, positions, w_ada, b_ada, w_in, conv_w, a_log, dt_bias, norm_b_w,
              w_pa, w_pb, w_o, peer_wq, peer_subkeys, peer_u, peer_v, final_norm_w):
    h = x
    c_act = jax.nn.silu(c)
    for layer in range(DEPTH):
        mod = c_act @ w_ada[layer] + b_ada[layer]
        sh1, sc1, gt1, sh2, sc2, gt2 = jnp.split(mod, 6, axis=-1)
        n1 = rms_norm(h) * (1.0 + sc1[:, None, :]) + sh1[:, None, :]
        y1 = token_mixers(n1, positions, w_in[layer], conv_w[layer], a_log[layer], dt_bias[layer],
                          norm_b_w[layer], w_pa[layer], w_pb[layer], w_o[layer])
        h = h + gt1[:, None, :] * y1
        n2 = rms_norm(h) * (1.0 + sc2[:, None, :]) + sh2[:, None, :]
        y2 = peer_channel_mixer(n2, peer_wq[layer], peer_subkeys[layer], peer_u[layer], peer_v[layer])
        h = h + gt2[:, None, :] * y2
    return rms_norm(h, final_norm_w)
```

```python
import jax, jax.numpy as jnp
from jax import lax
import numpy as np
from jax.experimental import pallas as pl
from jax.experimental.pallas import tpu as pltpu

D_MODEL = 1024
BATCH = 8
SEQ = 4096
DEPTH = 1

A_HEADS = 8
A_KV_HEADS = 2
A_HEAD_DIM = 64
IDX_HEADS = 16
IDX_DIM = 64
IDX_TOPK_MAX = 256
Q_BLOCK = 128
B_HEADS = 8
B_KEY_DIM = 64
B_VAL_DIM = 64
CONV_WIDTH = 4
CHUNK = 64
ROPE_THETA = 500000.0
ROPE_FRACTION_DEN = 4
PEER_HEADS = 8
PEER_KEY_DIM = 128
PEER_N_KEYS = 128
PEER_N_EXPERTS = PEER_N_KEYS * PEER_N_KEYS
PEER_TOPK = 16
PEER_TOKEN_BLOCK = 128
EPS = 1e-6

A_WIDTH = A_HEADS * A_HEAD_DIM
KV_WIDTH = A_KV_HEADS * A_HEAD_DIM
B_QK_WIDTH = B_HEADS * B_KEY_DIM
B_V_WIDTH = B_HEADS * B_VAL_DIM
CONV_CHANNELS = 2 * B_QK_WIDTH + B_V_WIDTH
IN_SPLITS = (A_WIDTH, KV_WIDTH, KV_WIDTH, IDX_HEADS * IDX_DIM, IDX_DIM, IDX_HEADS,
             B_QK_WIDTH, B_QK_WIDTH, B_V_WIDTH, B_V_WIDTH, B_HEADS, B_HEADS, D_MODEL, D_MODEL)
IN_WIDTH = sum(IN_SPLITS)


def rms_norm(x, gain=None):
    xf = x.astype(jnp.float32)
    y = xf * lax.rsqrt(jnp.mean(xf * xf, axis=-1, keepdims=True) + EPS)
    if gain is not None:
        y = y * gain.astype(jnp.float32)
    return y.astype(x.dtype)


def l2_normalize(x):
    return x * lax.rsqrt(jnp.sum(x * x, axis=-1, keepdims=True) + EPS)


def partial_rotary(x, positions):
    hd = x.shape[-1]
    rd = hd // ROPE_FRACTION_DEN
    half = rd // 2
    inv_freq = jnp.power(jnp.float32(ROPE_THETA), -jnp.arange(half, dtype=jnp.float32) * (2.0 / rd))
    ang = positions.astype(jnp.float32)[..., None] * inv_freq
    cos = jnp.cos(ang)[:, :, None, :].astype(x.dtype)
    sin = jnp.sin(ang)[:, :, None, :].astype(x.dtype)
    x1 = x[..., :half]
    x2 = x[..., half:rd]
    return jnp.concatenate([x1 * cos - x2 * sin, x2 * cos + x1 * sin, x[..., rd:]], axis=-1)


def causal_short_conv(x, w):
    width = w.shape[0]
    seq = x.shape[1]
    xp = jnp.pad(x, ((0, 0), (width - 1, 0), (0, 0)))
    out = xp[:, 0:seq] * w[0]
    for i in range(1, width):
        out = out + xp[:, i:i + seq] * w[i]
    return out


def dsa_attention(q, k, v, q_idx, k_idx, w_idx):
    bsz, seq, _, dh = q.shape
    topk = min(IDX_TOPK_MAX, seq // 4)
    n_blocks = seq // Q_BLOCK
    rep = A_HEADS // A_KV_HEADS
    idx_scale = (IDX_HEADS ** -0.5) * (IDX_DIM ** -0.5)
    spos = jnp.arange(seq)

    def block(blk):
        t0 = blk * Q_BLOCK
        q_b = lax.dynamic_slice_in_dim(q, t0, Q_BLOCK, axis=1)
        qi_b = lax.dynamic_slice_in_dim(q_idx, t0, Q_BLOCK, axis=1)
        wi_b = lax.dynamic_slice_in_dim(w_idx, t0, Q_BLOCK, axis=1).astype(jnp.float32) * idx_scale
        tpos = t0 + jnp.arange(Q_BLOCK)
        logits = jnp.einsum('bthd,bsd->bths', qi_b, k_idx).astype(jnp.float32)
        score = jnp.einsum('bth,bths->bts', wi_b, jax.nn.relu(logits))
        causal = spos[None, :] <= tpos[:, None]
        score = jnp.where(causal[None], score, -jnp.inf)
        _, sel = lax.top_k(score, topk)
        valid = sel <= tpos[None, :, None]
        k_sel = jax.vmap(lambda kb_, ib: kb_[ib])(k, sel)
        v_sel = jax.vmap(lambda vb_, ib: vb_[ib])(v, sel)
        qg = q_b.reshape(bsz, Q_BLOCK, A_KV_HEADS, rep, dh)
        s = jnp.einsum('btgrd,btkgd->btgrk', qg, k_sel).astype(jnp.float32) * (dh ** -0.5)
        s = jnp.where(valid[:, :, None, None, :], s, -jnp.inf)
        p = jax.nn.softmax(s, axis=-1).astype(v.dtype)
        o = jnp.einsum('btgrk,btkgd->btgrd', p, v_sel)
        return o.reshape(bsz, Q_BLOCK, A_HEADS * dh)

    out = lax.map(block, jnp.arange(n_blocks))
    return jnp.moveaxis(out, 0, 1).reshape(bsz, seq, A_HEADS * dh)


def gated_delta_rule(q, k, v, g, beta):
    bsz, seq, nh, dk = q.shape
    dv = v.shape[-1]
    n = seq // CHUNK

    def chunks(t):
        t = jnp.moveaxis(t, 2, 1)
        return t.reshape(t.shape[:2] + (n, CHUNK) + t.shape[3:])

    q = chunks(q * (dk ** -0.5))
    k = chunks(k)
    v = chunks(v)
    g = chunks(g)
    beta = chunks(beta)
    gc = jnp.cumsum(g, axis=-1)
    pos = jnp.arange(CHUNK)
    incl = pos[:, None] >= pos[None, :]
    strict = pos[:, None] > pos[None, :]
    decay = jnp.exp(jnp.where(incl, gc[..., :, None] - gc[..., None, :], -jnp.inf))
    kb = k * beta[..., None]
    vb = v * beta[..., None]
    lower = jnp.where(strict, jnp.einsum('bhncd,bhnjd->bhncj', kb, k) * decay, 0.0)
    eye = jnp.eye(CHUNK, dtype=q.dtype)
    tmat = lax.linalg.triangular_solve(eye + lower, jnp.broadcast_to(eye, lower.shape),
                                       left_side=True, lower=True, unit_diagonal=True)
    u = tmat @ vb
    kcd = tmat @ (kb * jnp.exp(gc)[..., None])
    intra = jnp.einsum('bhncd,bhnjd->bhncj', q, k) * decay

    def step(state, xs):
        qc, kc, uc, kcdc, gcc, intrac = xs
        v_new = uc - kcdc @ state
        out = (qc * jnp.exp(gcc)[..., None]) @ state + intrac @ v_new
        glast = gcc[..., -1]
        state = state * jnp.exp(glast)[..., None, None] + jnp.einsum(
            'bhcd,bhce->bhde', kc * jnp.exp(glast[..., None] - gcc)[..., None], v_new)
        return state, out

    xs = tuple(jnp.moveaxis(t, 2, 0) for t in (q, k, u, kcd, gc, intra))
    state0 = jnp.zeros((bsz, nh, dk, dv), q.dtype)
    _, out = lax.scan(step, state0, xs)
    out = jnp.moveaxis(out, 0, 2).reshape(bsz, nh, seq, dv)
    return jnp.moveaxis(out, 1, 2)


def token_mixers(n, positions, w_in, conv_w, a_log, dt_bias, norm_b_w, w_pa, w_pb, w_o):
    bsz, seq, _ = n.shape
    proj = n @ w_in
    offs = np.cumsum(IN_SPLITS)[:-1].tolist()
    (qa, ka, va, qi, ki, wi, qb, kb, vb, zb, bb, ab, gate_a, gate_b) = jnp.split(proj, offs, axis=-1)

    qa = partial_rotary(qa.reshape(bsz, seq, A_HEADS, A_HEAD_DIM), positions)
    ka = partial_rotary(ka.reshape(bsz, seq, A_KV_HEADS, A_HEAD_DIM), positions)
    va = va.reshape(bsz, seq, A_KV_HEADS, A_HEAD_DIM)
    qi = partial_rotary(qi.reshape(bsz, seq, IDX_HEADS, IDX_DIM), positions)
    ki = partial_rotary(ki.reshape(bsz, seq, 1, IDX_DIM), positions)[:, :, 0]
    o_a = dsa_attention(qa, ka, va, qi, ki, wi)

    qkv = jax.nn.silu(causal_short_conv(jnp.concatenate([qb, kb, vb], axis=-1), conv_w))
    qb, kb, vb = jnp.split(qkv, [B_QK_WIDTH, 2 * B_QK_WIDTH], axis=-1)
    qb = l2_normalize(qb.reshape(bsz, seq, B_HEADS, B_KEY_DIM).astype(jnp.float32))
    kb = l2_normalize(kb.reshape(bsz, seq, B_HEADS, B_KEY_DIM).astype(jnp.float32))
    vb = vb.reshape(bsz, seq, B_HEADS, B_VAL_DIM).astype(jnp.float32)
    beta = jax.nn.sigmoid(bb.astype(jnp.float32))
    g = -jnp.exp(a_log.astype(jnp.float32)) * jax.nn.softplus(ab.astype(jnp.float32) + dt_bias.astype(jnp.float32))
    o_b = gated_delta_rule(qb, kb, vb, g, beta)
    z = jax.nn.silu(zb.reshape(bsz, seq, B_HEADS, B_VAL_DIM).astype(jnp.float32))
    o_b = (rms_norm(o_b, norm_b_w) * z).astype(n.dtype).reshape(bsz, seq, B_V_WIDTH)

    merged = jax.nn.sigmoid(gate_a) * (o_a @ w_pa) + jax.nn.sigmoid(gate_b) * (o_b @ w_pb)
    return merged @ w_o


def peer_channel_mixer(xn, wq, subkeys, u_tab, v_tab):
    bsz, seq, d = xn.shape
    kk = PEER_TOPK
    q = (xn @ wq).reshape(bsz, seq, PEER_HEADS, 2, PEER_KEY_DIM // 2)
    s = jnp.einsum('bshpd,hpnd->bshpn', q, subkeys).astype(jnp.float32)
    v1, i1 = lax.top_k(s[..., 0, :], kk)
    v2, i2 = lax.top_k(s[..., 1, :], kk)
    cand = (v1[..., :, None] + v2[..., None, :]).reshape(bsz, seq, PEER_HEADS, kk * kk)
    vals, ci = lax.top_k(cand, kk)
    e1 = jnp.take_along_axis(i1, ci // kk, axis=-1)
    e2 = jnp.take_along_axis(i2, ci % kk, axis=-1)
    experts = e1 * PEER_N_KEYS + e2
    gates = jax.nn.softmax(vals, axis=-1).astype(xn.dtype)
    n_tok = bsz * seq
    nb = n_tok // PEER_TOKEN_BLOCK
    xs = (xn.reshape(nb, PEER_TOKEN_BLOCK, d),
          experts.reshape(nb, PEER_TOKEN_BLOCK, PEER_HEADS, kk),
          gates.reshape(nb, PEER_TOKEN_BLOCK, PEER_HEADS, kk))

    def block(args):
        xb, eb, gb = args
        a = jnp.einsum('td,thkd->thk', xb, u_tab[eb])
        coef = jax.nn.gelu(a, approximate=False) * gb
        return jnp.einsum('thk,thkd->td', coef, v_tab[eb])

    out = lax.map(block, xs)
    return out.reshape(bsz, seq, d)


def _final_norm_body(h_ref, g_ref, o_ref):
    h = h_ref[...]
    o_ref[...] = h * lax.rsqrt(jnp.mean(h * h, axis=-1, keepdims=True) + EPS) * g_ref[...]


def final_norm(h, gain):
    bsz, seq, d = h.shape
    rows = bsz * seq
    tile = 1024
    out = pl.pallas_call(
        _final_norm_body,
        grid=(rows // tile,),
        in_specs=[pl.BlockSpec((tile, d), lambda i: (i, 0)), pl.BlockSpec((1, d), lambda i: (0, 0))],
        out_specs=pl.BlockSpec((tile, d), lambda i: (i, 0)),
        out_shape=jax.ShapeDtypeStruct((rows, d), h.dtype),
        name="final_norm",
    )(h.reshape(rows, d), gain.reshape(1, d))
    return out.reshape(bsz, seq, d)


def kernel(x, c, positions, w_ada, b_ada, w_in, conv_w, a_log, dt_bias, norm_b_w,
           w_pa, w_pb, w_o, peer_wq, peer_subkeys, peer_u, peer_v, final_norm_w):
    h = x
    c_act = jax.nn.silu(c)
    for layer in range(DEPTH):
        mod = c_act @ w_ada[layer] + b_ada[layer]
        sh1, sc1, gt1, sh2, sc2, gt2 = jnp.split(mod, 6, axis=-1)
        n1 = rms_norm(h) * (1.0 + sc1[:, None, :]) + sh1[:, None, :]
        y1 = token_mixers(n1, positions, w_in[layer], conv_w[layer], a_log[layer], dt_bias[layer],
                          norm_b_w[layer], w_pa[layer], w_pb[layer], w_o[layer])
        h = h + gt1[:, None, :] * y1
        n2 = rms_norm(h) * (1.0 + sc2[:, None, :]) + sh2[:, None, :]
        y2 = peer_channel_mixer(n2, peer_wq[layer], peer_subkeys[layer], peer_u[layer], peer_v[layer])
        h = h + gt2[:, None, :] * y2
    return final_norm(h, final_norm_w)
```

```python
import functools

import jax, jax.numpy as jnp
from jax import lax
import numpy as np
from jax.experimental import pallas as pl
from jax.experimental.pallas import tpu as pltpu

D_MODEL = 1024
BATCH = 8
SEQ = 4096
DEPTH = 1

A_HEADS = 8
A_KV_HEADS = 2
A_HEAD_DIM = 64
IDX_HEADS = 16
IDX_DIM = 64
IDX_TOPK_MAX = 256
Q_BLOCK = 128
B_HEADS = 8
B_KEY_DIM = 64
B_VAL_DIM = 64
CONV_WIDTH = 4
CHUNK = 64
ROPE_THETA = 500000.0
ROPE_FRACTION_DEN = 4
PEER_HEADS = 8
PEER_KEY_DIM = 128
PEER_N_KEYS = 128
PEER_N_EXPERTS = PEER_N_KEYS * PEER_N_KEYS
PEER_TOPK = 16
PEER_TOKEN_BLOCK = 128
EPS = 1e-6

A_WIDTH = A_HEADS * A_HEAD_DIM
KV_WIDTH = A_KV_HEADS * A_HEAD_DIM
B_QK_WIDTH = B_HEADS * B_KEY_DIM
B_V_WIDTH = B_HEADS * B_VAL_DIM
CONV_CHANNELS = 2 * B_QK_WIDTH + B_V_WIDTH
IN_SPLITS = (A_WIDTH, KV_WIDTH, KV_WIDTH, IDX_HEADS * IDX_DIM, IDX_DIM, IDX_HEADS,
             B_QK_WIDTH, B_QK_WIDTH, B_V_WIDTH, B_V_WIDTH, B_HEADS, B_HEADS, D_MODEL, D_MODEL)
IN_WIDTH = sum(IN_SPLITS)


def rms_norm(x, gain=None):
    xf = x.astype(jnp.float32)
    y = xf * lax.rsqrt(jnp.mean(xf * xf, axis=-1, keepdims=True) + EPS)
    if gain is not None:
        y = y * gain.astype(jnp.float32)
    return y.astype(x.dtype)


def l2_normalize(x):
    return x * lax.rsqrt(jnp.sum(x * x, axis=-1, keepdims=True) + EPS)


def partial_rotary(x, positions):
    hd = x.shape[-1]
    rd = hd // ROPE_FRACTION_DEN
    half = rd // 2
    inv_freq = jnp.power(jnp.float32(ROPE_THETA), -jnp.arange(half, dtype=jnp.float32) * (2.0 / rd))
    ang = positions.astype(jnp.float32)[..., None] * inv_freq
    cos = jnp.cos(ang)[:, :, None, :].astype(x.dtype)
    sin = jnp.sin(ang)[:, :, None, :].astype(x.dtype)
    x1 = x[..., :half]
    x2 = x[..., half:rd]
    return jnp.concatenate([x1 * cos - x2 * sin, x2 * cos + x1 * sin, x[..., rd:]], axis=-1)


def causal_short_conv(x, w):
    width = w.shape[0]
    seq = x.shape[1]
    xp = jnp.pad(x, ((0, 0), (width - 1, 0), (0, 0)))
    out = xp[:, 0:seq] * w[0]
    for i in range(1, width):
        out = out + xp[:, i:i + seq] * w[i]
    return out


DSA_TQ = 256
DSA_TK = 256
INT_MIN = -2**31
NEG_BIG = -1e30


def _dsa_body(topk, qit_ref, w_ref, qat_ref, ki_ref, ka_ref, vat_ref, o_ref,
              key_ref, m_ref, l_ref, acc_ref):
    tq, tk = DSA_TQ, DSA_TK
    qb = pl.program_id(1)
    n_kv = qb + 1
    t_glob = qb * tq + lax.broadcasted_iota(jnp.int32, (1, tq), 1)
    row = lax.broadcasted_iota(jnp.int32, (tk, 1), 0)
    f32 = jnp.float32

    def p1(j, carry):
        kt = ki_ref[0, j]
        score = jnp.zeros((tk, tq), f32)
        for h in range(IDX_HEADS):
            lt = jnp.dot(kt, qit_ref[0, h], preferred_element_type=f32)
            score = score + w_ref[0, h:h + 1, :] * jnp.maximum(lt, 0.0)
        bits = lax.bitcast_convert_type(score + 0.0, jnp.int32)
        skey = jnp.where(bits >= 0, bits, bits ^ jnp.int32(0x7FFFFFFF))
        skey = jnp.where(j * tk + row <= t_glob, skey, jnp.int32(INT_MIN))
        key_ref[j] = skey
        return carry
    lax.fori_loop(0, n_kv, p1, 0)

    def count(pred):
        def body(j, acc):
            return acc + jnp.sum(jnp.where(pred(key_ref[j], j * tk + row), 1.0, 0.0), axis=0, keepdims=True)
        return lax.fori_loop(0, n_kv, body, jnp.zeros((1, tq), f32))

    kf = jnp.float32(topk)

    def bit_step(i, ku):
        cand_u = ku | lax.shift_left(jnp.int32(1), 31 - i)
        cand = cand_u ^ jnp.int32(INT_MIN)
        c = count(lambda k, s: k >= cand)
        return jnp.where(c >= kf, cand_u, ku)
    ku = lax.fori_loop(0, 32, bit_step, jnp.zeros((1, tq), jnp.int32))
    kth = ku ^ jnp.int32(INT_MIN)
    c_gt = count(lambda k, s: k > kth)
    c_ge = count(lambda k, s: k >= kth)
    short = kth == jnp.int32(INT_MIN)
    x0 = jnp.where(short, jnp.int32(-1), jnp.int32(2**30))
    need = kf - c_gt
    has_tie = jnp.max(jnp.where(jnp.logical_and(c_ge > kf, jnp.logical_not(short)), 1.0, 0.0)) > 0.0

    def tie_search():
        def step(i, x):
            bit = lax.shift_left(jnp.int32(1), 11 - i)
            probe = x + bit - 1
            c = count(lambda k, s: jnp.logical_and(k == kth, s <= probe))
            return jnp.where(c < need, x + bit, x)
        x = lax.fori_loop(0, 12, step, jnp.zeros((1, tq), jnp.int32))
        return jnp.where(short, jnp.int32(-1), x)
    x_lim = lax.cond(has_tie, tie_search, lambda: x0)

    m_ref[...] = jnp.full(m_ref.shape, NEG_BIG, f32)
    l_ref[...] = jnp.zeros(l_ref.shape, f32)
    acc_ref[...] = jnp.zeros(acc_ref.shape, f32)
    rep = A_HEADS // A_KV_HEADS

    def p3(j, carry):
        skey = key_ref[j]
        s_idx = j * tk + row
        sel = jnp.logical_or(skey > kth, jnp.logical_and(skey == kth, s_idx <= x_lim))
        for g in range(A_KV_HEADS):
            kt = ka_ref[0, g, j]
            vt = vat_ref[0, g, j]
            for r in range(rep):
                h = g * rep + r
                s = jnp.dot(kt, qat_ref[0, h], preferred_element_type=f32)
                s = jnp.where(sel, s, NEG_BIG)
                m_old = m_ref[h:h + 1, :]
                m_new = jnp.maximum(m_old, jnp.max(s, axis=0, keepdims=True))
                alpha = jnp.exp(m_old - m_new)
                p = jnp.exp(s - m_new)
                l_ref[h:h + 1, :] = alpha * l_ref[h:h + 1, :] + jnp.sum(p, axis=0, keepdims=True)
                acc_ref[h] = alpha * acc_ref[h] + jnp.dot(vt, p.astype(jnp.bfloat16), preferred_element_type=f32)
                m_ref[h:h + 1, :] = m_new
        return carry
    lax.fori_loop(0, n_kv, p3, 0)
    for h in range(A_HEADS):
        o_ref[0, h] = acc_ref[h] / l_ref[h:h + 1, :]


def dsa_attention_pallas(qa, ka, va, qi, ki, wi):
    bsz, seq = qa.shape[0], qa.shape[1]
    tq, tk = DSA_TQ, DSA_TK
    topk = min(IDX_TOPK_MAX, seq // 4)
    bf = jnp.bfloat16
    idx_scale = (IDX_HEADS ** -0.5) * (IDX_DIM ** -0.5)
    qit = jnp.transpose(qi, (0, 2, 3, 1)).astype(bf)
    w = jnp.transpose(wi.astype(jnp.float32) * idx_scale, (0, 2, 1))
    qat = jnp.transpose(qa * (A_HEAD_DIM ** -0.5), (0, 2, 3, 1)).astype(bf)
    kit = ki.astype(bf).reshape(bsz, seq // tk, tk, IDX_DIM)
    kat = jnp.transpose(ka, (0, 2, 1, 3)).astype(bf).reshape(bsz, A_KV_HEADS, seq // tk, tk, A_HEAD_DIM)
    vat = jnp.transpose(va.astype(bf).reshape(bsz, seq // tk, tk, A_KV_HEADS, A_HEAD_DIM), (0, 3, 1, 4, 2))
    n_kv = seq // tk
    out = pl.pallas_call(
        functools.partial(_dsa_body, topk),
        grid=(bsz, seq // tq),
        in_specs=[
            pl.BlockSpec((1, IDX_HEADS, IDX_DIM, tq), lambda b, q: (b, 0, 0, q)),
            pl.BlockSpec((1, IDX_HEADS, tq), lambda b, q: (b, 0, q)),
            pl.BlockSpec((1, A_HEADS, A_HEAD_DIM, tq), lambda b, q: (b, 0, 0, q)),
            pl.BlockSpec((1, n_kv, tk, IDX_DIM), lambda b, q: (b, 0, 0, 0)),
            pl.BlockSpec((1, A_KV_HEADS, n_kv, tk, A_HEAD_DIM), lambda b, q: (b, 0, 0, 0, 0)),
            pl.BlockSpec((1, A_KV_HEADS, n_kv, A_HEAD_DIM, tk), lambda b, q: (b, 0, 0, 0, 0)),
        ],
        out_specs=pl.BlockSpec((1, A_HEADS, A_HEAD_DIM, tq), lambda b, q: (b, 0, 0, q)),
        out_shape=jax.ShapeDtypeStruct((bsz, A_HEADS, A_HEAD_DIM, seq), jnp.float32),
        scratch_shapes=[
            pltpu.VMEM((n_kv, tk, tq), jnp.int32),
            pltpu.VMEM((A_HEADS, tq), jnp.float32),
            pltpu.VMEM((A_HEADS, tq), jnp.float32),
            pltpu.VMEM((A_HEADS, A_HEAD_DIM, tq), jnp.float32),
        ],
        compiler_params=pltpu.CompilerParams(dimension_semantics=("arbitrary", "arbitrary"),
                                             vmem_limit_bytes=48 * 1024 * 1024),
        name="dsa_attention",
    )(qit, w, qat, kit, kat, vat)
    return jnp.transpose(out, (0, 3, 1, 2)).reshape(bsz, seq, A_HEADS * A_HEAD_DIM)


def gated_delta_rule(q, k, v, g, beta):
    bsz, seq, nh, dk = q.shape
    dv = v.shape[-1]
    n = seq // CHUNK

    def chunks(t):
        t = jnp.moveaxis(t, 2, 1)
        return t.reshape(t.shape[:2] + (n, CHUNK) + t.shape[3:])

    q = chunks(q * (dk ** -0.5))
    k = chunks(k)
    v = chunks(v)
    g = chunks(g)
    beta = chunks(beta)
    gc = jnp.cumsum(g, axis=-1)
    pos = jnp.arange(CHUNK)
    incl = pos[:, None] >= pos[None, :]
    strict = pos[:, None] > pos[None, :]
    decay = jnp.exp(jnp.where(incl, gc[..., :, None] - gc[..., None, :], -jnp.inf))
    kb = k * beta[..., None]
    vb = v * beta[..., None]
    lower = jnp.where(strict, jnp.einsum('bhncd,bhnjd->bhncj', kb, k) * decay, 0.0)
    eye = jnp.eye(CHUNK, dtype=q.dtype)
    tmat = lax.linalg.triangular_solve(eye + lower, jnp.broadcast_to(eye, lower.shape),
                                       left_side=True, lower=True, unit_diagonal=True)
    u = tmat @ vb
    kcd = tmat @ (kb * jnp.exp(gc)[..., None])
    intra = jnp.einsum('bhncd,bhnjd->bhncj', q, k) * decay

    def step(state, xs):
        qc, kc, uc, kcdc, gcc, intrac = xs
        v_new = uc - kcdc @ state
        out = (qc * jnp.exp(gcc)[..., None]) @ state + intrac @ v_new
        glast = gcc[..., -1]
        state = state * jnp.exp(glast)[..., None, None] + jnp.einsum(
            'bhcd,bhce->bhde', kc * jnp.exp(glast[..., None] - gcc)[..., None], v_new)
        return state, out

    xs = tuple(jnp.moveaxis(t, 2, 0) for t in (q, k, u, kcd, gc, intra))
    state0 = jnp.zeros((bsz, nh, dk, dv), q.dtype)
    _, out = lax.scan(step, state0, xs)
    out = jnp.moveaxis(out, 0, 2).reshape(bsz, nh, seq, dv)
    return jnp.moveaxis(out, 1, 2)


def token_mixers(n, positions, w_in, conv_w, a_log, dt_bias, norm_b_w, w_pa, w_pb, w_o):
    bsz, seq, _ = n.shape
    proj = n @ w_in
    offs = np.cumsum(IN_SPLITS)[:-1].tolist()
    (qa, ka, va, qi, ki, wi, qb, kb, vb, zb, bb, ab, gate_a, gate_b) = jnp.split(proj, offs, axis=-1)

    qa = partial_rotary(qa.reshape(bsz, seq, A_HEADS, A_HEAD_DIM), positions)
    ka = partial_rotary(ka.reshape(bsz, seq, A_KV_HEADS, A_HEAD_DIM), positions)
    va = va.reshape(bsz, seq, A_KV_HEADS, A_HEAD_DIM)
    qi = partial_rotary(qi.reshape(bsz, seq, IDX_HEADS, IDX_DIM), positions)
    ki = partial_rotary(ki.reshape(bsz, seq, 1, IDX_DIM), positions)[:, :, 0]
    o_a = dsa_attention_pallas(qa, ka, va, qi, ki, wi)

    qkv = jax.nn.silu(causal_short_conv(jnp.concatenate([qb, kb, vb], axis=-1), conv_w))
    qb, kb, vb = jnp.split(qkv, [B_QK_WIDTH, 2 * B_QK_WIDTH], axis=-1)
    qb = l2_normalize(qb.reshape(bsz, seq, B_HEADS, B_KEY_DIM).astype(jnp.float32))
    kb = l2_normalize(kb.reshape(bsz, seq, B_HEADS, B_KEY_DIM).astype(jnp.float32))
    vb = vb.reshape(bsz, seq, B_HEADS, B_VAL_DIM).astype(jnp.float32)
    beta = jax.nn.sigmoid(bb.astype(jnp.float32))
    g = -jnp.exp(a_log.astype(jnp.float32)) * jax.nn.softplus(ab.astype(jnp.float32) + dt_bias.astype(jnp.float32))
    o_b = gated_delta_rule(qb, kb, vb, g, beta)
    z = jax.nn.silu(zb.reshape(bsz, seq, B_HEADS, B_VAL_DIM).astype(jnp.float32))
    o_b = (rms_norm(o_b, norm_b_w) * z).astype(n.dtype).reshape(bsz, seq, B_V_WIDTH)

    merged = jax.nn.sigmoid(gate_a) * (o_a @ w_pa) + jax.nn.sigmoid(gate_b) * (o_b @ w_pb)
    return merged @ w_o


def peer_channel_mixer(xn, wq, subkeys, u_tab, v_tab):
    bsz, seq, d = xn.shape
    kk = PEER_TOPK
    q = (xn @ wq).reshape(bsz, seq, PEER_HEADS, 2, PEER_KEY_DIM // 2)
    s = jnp.einsum('bshpd,hpnd->bshpn', q, subkeys).astype(jnp.float32)
    v1, i1 = lax.top_k(s[..., 0, :], kk)
    v2, i2 = lax.top_k(s[..., 1, :], kk)
    cand = (v1[..., :, None] + v2[..., None, :]).reshape(bsz, seq, PEER_HEADS, kk * kk)
    vals, ci = lax.top_k(cand, kk)
    e1 = jnp.take_along_axis(i1, ci // kk, axis=-1)
    e2 = jnp.take_along_axis(i2, ci % kk, axis=-1)
    experts = e1 * PEER_N_KEYS + e2
    gates = jax.nn.softmax(vals, axis=-1).astype(xn.dtype)
    n_tok = bsz * seq
    nb = n_tok // PEER_TOKEN_BLOCK
    xs = (xn.reshape(nb, PEER_TOKEN_BLOCK, d),
          experts.reshape(nb, PEER_TOKEN_BLOCK, PEER_HEADS, kk),
          gates.reshape(nb, PEER_TOKEN_BLOCK, PEER_HEADS, kk))

    def block(args):
        xb, eb, gb = args
        a = jnp.einsum('td,thkd->thk', xb, u_tab[eb])
        coef = jax.nn.gelu(a, approximate=False) * gb
        return jnp.einsum('thk,thkd->td', coef, v_tab[eb])

    out = lax.map(block, xs)
    return out.reshape(bsz, seq, d)


def _final_norm_body(h_ref, g_ref, o_ref):
    h = h_ref[...]
    o_ref[...] = h * lax.rsqrt(jnp.mean(h * h, axis=-1, keepdims=True) + EPS) * g_ref[...]


def final_norm(h, gain):
    bsz, seq, d = h.shape
    rows = bsz * seq
    tile = 1024
    out = pl.pallas_call(
        _final_norm_body,
        grid=(rows // tile,),
        in_specs=[pl.BlockSpec((tile, d), lambda i: (i, 0)), pl.BlockSpec((1, d), lambda i: (0, 0))],
        out_specs=pl.BlockSpec((tile, d), lambda i: (i, 0)),
        out_shape=jax.ShapeDtypeStruct((rows, d), h.dtype),
        name="final_norm",
    )(h.reshape(rows, d), gain.reshape(1, d))
    return out.reshape(bsz, seq, d)


def kernel(x, c, positions, w_ada, b_ada, w_in, conv_w, a_log, dt_bias, norm_b_w,
           w_pa, w_pb, w_o, peer_wq, peer_subkeys, peer_u, peer_v, final_norm_w):
    h = x
    c_act = jax.nn.silu(c)
    for layer in range(DEPTH):
        mod = c_act @ w_ada[layer] + b_ada[layer]
        sh1, sc1, gt1, sh2, sc2, gt2 = jnp.split(mod, 6, axis=-1)
        n1 = rms_norm(h) * (1.0 + sc1[:, None, :]) + sh1[:, None, :]
        y1 = token_mixers(n1, positions, w_in[layer], conv_w[layer], a_log[layer], dt_bias[layer],
                          norm_b_w[layer], w_pa[layer], w_pb[layer], w_o[layer])
        h = h + gt1[:, None, :] * y1
        n2 = rms_norm(h) * (1.0 + sc2[:, None, :]) + sh2[:, None, :]
        y2 = peer_channel_mixer(n2, peer_wq[layer], peer_subkeys[layer], peer_u[layer], peer_v[layer])
        h = h + gt2[:, None, :] * y2
    return final_norm(h, final_norm_w)
```

```python
import functools

import jax, jax.numpy as jnp
from jax import lax
import numpy as np
from jax.experimental import pallas as pl
from jax.experimental.pallas import tpu as pltpu

D_MODEL = 1024
BATCH = 8
SEQ = 4096
DEPTH = 1

A_HEADS = 8
A_KV_HEADS = 2
A_HEAD_DIM = 64
IDX_HEADS = 16
IDX_DIM = 64
IDX_TOPK_MAX = 256
Q_BLOCK = 128
B_HEADS = 8
B_KEY_DIM = 64
B_VAL_DIM = 64
CONV_WIDTH = 4
CHUNK = 64
ROPE_THETA = 500000.0
ROPE_FRACTION_DEN = 4
PEER_HEADS = 8
PEER_KEY_DIM = 128
PEER_N_KEYS = 128
PEER_N_EXPERTS = PEER_N_KEYS * PEER_N_KEYS
PEER_TOPK = 16
PEER_TOKEN_BLOCK = 128
EPS = 1e-6

A_WIDTH = A_HEADS * A_HEAD_DIM
KV_WIDTH = A_KV_HEADS * A_HEAD_DIM
B_QK_WIDTH = B_HEADS * B_KEY_DIM
B_V_WIDTH = B_HEADS * B_VAL_DIM
CONV_CHANNELS = 2 * B_QK_WIDTH + B_V_WIDTH
IN_SPLITS = (A_WIDTH, KV_WIDTH, KV_WIDTH, IDX_HEADS * IDX_DIM, IDX_DIM, IDX_HEADS,
             B_QK_WIDTH, B_QK_WIDTH, B_V_WIDTH, B_V_WIDTH, B_HEADS, B_HEADS, D_MODEL, D_MODEL)
IN_WIDTH = sum(IN_SPLITS)


def rms_norm(x, gain=None):
    xf = x.astype(jnp.float32)
    y = xf * lax.rsqrt(jnp.mean(xf * xf, axis=-1, keepdims=True) + EPS)
    if gain is not None:
        y = y * gain.astype(jnp.float32)
    return y.astype(x.dtype)


def l2_normalize(x):
    return x * lax.rsqrt(jnp.sum(x * x, axis=-1, keepdims=True) + EPS)


def partial_rotary(x, positions):
    hd = x.shape[-1]
    rd = hd // ROPE_FRACTION_DEN
    half = rd // 2
    inv_freq = jnp.power(jnp.float32(ROPE_THETA), -jnp.arange(half, dtype=jnp.float32) * (2.0 / rd))
    ang = positions.astype(jnp.float32)[..., None] * inv_freq
    cos = jnp.cos(ang)[:, :, None, :].astype(x.dtype)
    sin = jnp.sin(ang)[:, :, None, :].astype(x.dtype)
    x1 = x[..., :half]
    x2 = x[..., half:rd]
    return jnp.concatenate([x1 * cos - x2 * sin, x2 * cos + x1 * sin, x[..., rd:]], axis=-1)


def causal_short_conv(x, w):
    width = w.shape[0]
    seq = x.shape[1]
    xp = jnp.pad(x, ((0, 0), (width - 1, 0), (0, 0)))
    out = xp[:, 0:seq] * w[0]
    for i in range(1, width):
        out = out + xp[:, i:i + seq] * w[i]
    return out


DSA_TQ = 256
DSA_TK = 256
INT_MIN = -2**31
NEG_BIG = -1e30


def _dsa_body(topk, qit_ref, w_ref, qat_ref, ki_ref, ka_ref, vat_ref, o_ref,
              key_ref, m_ref, l_ref, acc_ref):
    tq, tk = DSA_TQ, DSA_TK
    qb = pl.program_id(1)
    n_kv = qb + 1
    t_glob = qb * tq + lax.broadcasted_iota(jnp.int32, (1, tq), 1)
    row = lax.broadcasted_iota(jnp.int32, (tk, 1), 0)
    f32 = jnp.float32

    def p1(j, carry):
        kt = ki_ref[0, j]
        score = jnp.zeros((tk, tq), f32)
        for h in range(IDX_HEADS):
            lt = jnp.dot(kt, qit_ref[0, h], preferred_element_type=f32)
            score = score + w_ref[0, h:h + 1, :] * jnp.maximum(lt, 0.0)
        bits = lax.bitcast_convert_type(score + 0.0, jnp.int32)
        skey = jnp.where(bits >= 0, bits, bits ^ jnp.int32(0x7FFFFFFF))
        skey = jnp.where(j * tk + row <= t_glob, skey, jnp.int32(INT_MIN))
        key_ref[j] = skey
        return carry
    lax.fori_loop(0, n_kv, p1, 0)

    def count(pred):
        def body(j, acc):
            return acc + jnp.sum(jnp.where(pred(key_ref[j], j * tk + row), 1.0, 0.0), axis=0, keepdims=True)
        return lax.fori_loop(0, n_kv, body, jnp.zeros((1, tq), f32))

    kf = jnp.float32(topk)

    def bit_step(i, ku):
        cand_u = ku | lax.shift_left(jnp.int32(1), 31 - i)
        cand = cand_u ^ jnp.int32(INT_MIN)
        c = count(lambda k, s: k >= cand)
        return jnp.where(c >= kf, cand_u, ku)
    ku = lax.fori_loop(0, 32, bit_step, jnp.zeros((1, tq), jnp.int32))
    kth = ku ^ jnp.int32(INT_MIN)
    c_gt = count(lambda k, s: k > kth)
    c_ge = count(lambda k, s: k >= kth)
    short = kth == jnp.int32(INT_MIN)
    x0 = jnp.where(short, jnp.int32(-1), jnp.int32(2**30))
    need = kf - c_gt
    has_tie = jnp.max(jnp.where(jnp.logical_and(c_ge > kf, jnp.logical_not(short)), 1.0, 0.0)) > 0.0

    def tie_search():
        def step(i, x):
            bit = lax.shift_left(jnp.int32(1), 11 - i)
            probe = x + bit - 1
            c = count(lambda k, s: jnp.logical_and(k == kth, s <= probe))
            return jnp.where(c < need, x + bit, x)
        x = lax.fori_loop(0, 12, step, jnp.zeros((1, tq), jnp.int32))
        return jnp.where(short, jnp.int32(-1), x)
    x_lim = lax.cond(has_tie, tie_search, lambda: x0)

    m_ref[...] = jnp.full(m_ref.shape, NEG_BIG, f32)
    l_ref[...] = jnp.zeros(l_ref.shape, f32)
    acc_ref[...] = jnp.zeros(acc_ref.shape, f32)
    rep = A_HEADS // A_KV_HEADS

    def p3(j, carry):
        skey = key_ref[j]
        s_idx = j * tk + row
        sel = jnp.logical_or(skey > kth, jnp.logical_and(skey == kth, s_idx <= x_lim))
        for g in range(A_KV_HEADS):
            kt = ka_ref[0, g, j]
            vt = vat_ref[0, g, j]
            for r in range(rep):
                h = g * rep + r
                s = jnp.dot(kt, qat_ref[0, h], preferred_element_type=f32)
                s = jnp.where(sel, s, NEG_BIG)
                m_old = m_ref[h:h + 1, :]
                m_new = jnp.maximum(m_old, jnp.max(s, axis=0, keepdims=True))
                alpha = jnp.exp(m_old - m_new)
                p = jnp.exp(s - m_new)
                l_ref[h:h + 1, :] = alpha * l_ref[h:h + 1, :] + jnp.sum(p, axis=0, keepdims=True)
                acc_ref[h] = alpha * acc_ref[h] + jnp.dot(vt, p.astype(jnp.bfloat16), preferred_element_type=f32)
                m_ref[h:h + 1, :] = m_new
        return carry
    lax.fori_loop(0, n_kv, p3, 0)
    for h in range(A_HEADS):
        o_ref[0, h] = acc_ref[h] / l_ref[h:h + 1, :]


def dsa_attention_pallas(qa, ka, va, qi, ki, wi):
    bsz, seq = qa.shape[0], qa.shape[1]
    tq, tk = DSA_TQ, DSA_TK
    topk = min(IDX_TOPK_MAX, seq // 4)
    bf = jnp.bfloat16
    idx_scale = (IDX_HEADS ** -0.5) * (IDX_DIM ** -0.5)
    qit = jnp.transpose(qi, (0, 2, 3, 1)).astype(bf)
    w = jnp.transpose(wi.astype(jnp.float32) * idx_scale, (0, 2, 1))
    qat = jnp.transpose(qa * (A_HEAD_DIM ** -0.5), (0, 2, 3, 1)).astype(bf)
    kit = ki.astype(bf).reshape(bsz, seq // tk, tk, IDX_DIM)
    kat = jnp.transpose(ka, (0, 2, 1, 3)).astype(bf).reshape(bsz, A_KV_HEADS, seq // tk, tk, A_HEAD_DIM)
    vat = jnp.transpose(va.astype(bf).reshape(bsz, seq // tk, tk, A_KV_HEADS, A_HEAD_DIM), (0, 3, 1, 4, 2))
    n_kv = seq // tk
    out = pl.pallas_call(
        functools.partial(_dsa_body, topk),
        grid=(bsz, seq // tq),
        in_specs=[
            pl.BlockSpec((1, IDX_HEADS, IDX_DIM, tq), lambda b, q: (b, 0, 0, q)),
            pl.BlockSpec((1, IDX_HEADS, tq), lambda b, q: (b, 0, q)),
            pl.BlockSpec((1, A_HEADS, A_HEAD_DIM, tq), lambda b, q: (b, 0, 0, q)),
            pl.BlockSpec((1, n_kv, tk, IDX_DIM), lambda b, q: (b, 0, 0, 0)),
            pl.BlockSpec((1, A_KV_HEADS, n_kv, tk, A_HEAD_DIM), lambda b, q: (b, 0, 0, 0, 0)),
            pl.BlockSpec((1, A_KV_HEADS, n_kv, A_HEAD_DIM, tk), lambda b, q: (b, 0, 0, 0, 0)),
        ],
        out_specs=pl.BlockSpec((1, A_HEADS, A_HEAD_DIM, tq), lambda b, q: (b, 0, 0, q)),
        out_shape=jax.ShapeDtypeStruct((bsz, A_HEADS, A_HEAD_DIM, seq), jnp.float32),
        scratch_shapes=[
            pltpu.VMEM((n_kv, tk, tq), jnp.int32),
            pltpu.VMEM((A_HEADS, tq), jnp.float32),
            pltpu.VMEM((A_HEADS, tq), jnp.float32),
            pltpu.VMEM((A_HEADS, A_HEAD_DIM, tq), jnp.float32),
        ],
        compiler_params=pltpu.CompilerParams(dimension_semantics=("arbitrary", "arbitrary"),
                                             vmem_limit_bytes=48 * 1024 * 1024),
        name="dsa_attention",
    )(qit, w, qat, kit, kat, vat)
    return jnp.transpose(out, (0, 3, 1, 2)).reshape(bsz, seq, A_HEADS * A_HEAD_DIM)


def gated_delta_rule(q, k, v, g, beta):
    bsz, seq, nh, dk = q.shape
    dv = v.shape[-1]
    n = seq // CHUNK

    def chunks(t):
        t = jnp.moveaxis(t, 2, 1)
        return t.reshape(t.shape[:2] + (n, CHUNK) + t.shape[3:])

    q = chunks(q * (dk ** -0.5))
    k = chunks(k)
    v = chunks(v)
    g = chunks(g)
    beta = chunks(beta)
    gc = jnp.cumsum(g, axis=-1)
    pos = jnp.arange(CHUNK)
    incl = pos[:, None] >= pos[None, :]
    strict = pos[:, None] > pos[None, :]
    decay = jnp.exp(jnp.where(incl, gc[..., :, None] - gc[..., None, :], -jnp.inf))
    kb = k * beta[..., None]
    vb = v * beta[..., None]
    lower = jnp.where(strict, jnp.einsum('bhncd,bhnjd->bhncj', kb, k) * decay, 0.0)
    eye = jnp.eye(CHUNK, dtype=q.dtype)
    tmat = lax.linalg.triangular_solve(eye + lower, jnp.broadcast_to(eye, lower.shape),
                                       left_side=True, lower=True, unit_diagonal=True)
    u = tmat @ vb
    kcd = tmat @ (kb * jnp.exp(gc)[..., None])
    intra = jnp.einsum('bhncd,bhnjd->bhncj', q, k) * decay

    def step(state, xs):
        qc, kc, uc, kcdc, gcc, intrac = xs
        v_new = uc - kcdc @ state
        out = (qc * jnp.exp(gcc)[..., None]) @ state + intrac @ v_new
        glast = gcc[..., -1]
        state = state * jnp.exp(glast)[..., None, None] + jnp.einsum(
            'bhcd,bhce->bhde', kc * jnp.exp(glast[..., None] - gcc)[..., None], v_new)
        return state, out

    xs = tuple(jnp.moveaxis(t, 2, 0) for t in (q, k, u, kcd, gc, intra))
    state0 = jnp.zeros((bsz, nh, dk, dv), q.dtype)
    _, out = lax.scan(step, state0, xs)
    out = jnp.moveaxis(out, 0, 2).reshape(bsz, nh, seq, dv)
    return jnp.moveaxis(out, 1, 2)


def token_mixers(n, positions, w_in, conv_w, a_log, dt_bias, norm_b_w, w_pa, w_pb, w_o):
    bsz, seq, _ = n.shape
    proj = n @ w_in
    offs = np.cumsum(IN_SPLITS)[:-1].tolist()
    (qa, ka, va, qi, ki, wi, qb, kb, vb, zb, bb, ab, gate_a, gate_b) = jnp.split(proj, offs, axis=-1)

    qa = partial_rotary(qa.reshape(bsz, seq, A_HEADS, A_HEAD_DIM), positions)
    ka = partial_rotary(ka.reshape(bsz, seq, A_KV_HEADS, A_HEAD_DIM), positions)
    va = va.reshape(bsz, seq, A_KV_HEADS, A_HEAD_DIM)
    qi = partial_rotary(qi.reshape(bsz, seq, IDX_HEADS, IDX_DIM), positions)
    ki = partial_rotary(ki.reshape(bsz, seq, 1, IDX_DIM), positions)[:, :, 0]
    o_a = dsa_attention_pallas(qa, ka, va, qi, ki, wi)

    qkv = jax.nn.silu(causal_short_conv(jnp.concatenate([qb, kb, vb], axis=-1), conv_w))
    qb, kb, vb = jnp.split(qkv, [B_QK_WIDTH, 2 * B_QK_WIDTH], axis=-1)
    qb = l2_normalize(qb.reshape(bsz, seq, B_HEADS, B_KEY_DIM).astype(jnp.float32))
    kb = l2_normalize(kb.reshape(bsz, seq, B_HEADS, B_KEY_DIM).astype(jnp.float32))
    vb = vb.reshape(bsz, seq, B_HEADS, B_VAL_DIM).astype(jnp.float32)
    beta = jax.nn.sigmoid(bb.astype(jnp.float32))
    g = -jnp.exp(a_log.astype(jnp.float32)) * jax.nn.softplus(ab.astype(jnp.float32) + dt_bias.astype(jnp.float32))
    o_b = gated_delta_rule(qb, kb, vb, g, beta)
    z = jax.nn.silu(zb.reshape(bsz, seq, B_HEADS, B_VAL_DIM).astype(jnp.float32))
    o_b = (rms_norm(o_b, norm_b_w) * z).astype(n.dtype).reshape(bsz, seq, B_V_WIDTH)

    merged = jax.nn.sigmoid(gate_a) * (o_a @ w_pa) + jax.nn.sigmoid(gate_b) * (o_b @ w_pb)
    return merged @ w_o


PEER_SLOTS = PEER_HEADS * PEER_TOPK
PEER_TB = 32
HALF_ROWS = 4
HI_MASK = -65536


def pack_table(tab):
    bits = lax.bitcast_convert_type(tab.astype(jnp.bfloat16), jnp.uint16).astype(jnp.uint32)
    half = tab.shape[1] // 2
    word = bits[:, :half] | (bits[:, half:] << 16)
    return lax.bitcast_convert_type(word, jnp.int32).reshape(tab.shape[0], HALF_ROWS, 128)


def _unpack(w):
    lo = lax.bitcast_convert_type(lax.shift_left(w, 16), jnp.float32)
    hi = lax.bitcast_convert_type(w & jnp.int32(HI_MASK), jnp.float32)
    return lo, hi


def _peer_u_body(idx_ref, x_ref, tab_ref, o_ref, s_ref, sb_ref):
    def tok(t, carry):
        xlo = x_ref[t, 0:HALF_ROWS, :]
        xhi = x_ref[t, HALF_ROWS:2 * HALF_ROWS, :]
        for k in range(PEER_SLOTS):
            lo, hi = _unpack(tab_ref[idx_ref[t, k]])
            s_ref[HALF_ROWS * k:HALF_ROWS * (k + 1), :] = lo * xlo + hi * xhi
        s4 = s_ref[pl.ds(0, PEER_SLOTS, stride=HALF_ROWS), :]
        for r in range(1, HALF_ROWS):
            s4 = s4 + s_ref[pl.ds(r, PEER_SLOTS, stride=HALF_ROWS), :]
        sb_ref[t] = s4
        return carry
    lax.fori_loop(0, PEER_TB, tok, 0)
    for t in range(PEER_TB):
        o_ref[t:t + 1, :] = jnp.sum(sb_ref[t].T, axis=0, keepdims=True)


def _peer_v_body(idx_ref, coef_ref, tab_ref, o_ref, cb_ref):
    for t in range(PEER_TB):
        cb_ref[t] = jnp.broadcast_to(coef_ref[t:t + 1, :], (PEER_SLOTS, 128)).T

    def tok(t, carry):
        nacc = 2
        acc = [jnp.zeros((HALF_ROWS, 128), jnp.float32) for _ in range(2 * nacc)]
        for k in range(PEER_SLOTS):
            lo, hi = _unpack(tab_ref[idx_ref[t, k]])
            c = jnp.broadcast_to(cb_ref[t, k:k + 1, :], (HALF_ROWS, 128))
            a = k % nacc
            acc[2 * a] = acc[2 * a] + c * lo
            acc[2 * a + 1] = acc[2 * a + 1] + c * hi
        o_ref[t, 0:HALF_ROWS, :] = acc[0] + acc[2]
        o_ref[t, HALF_ROWS:2 * HALF_ROWS, :] = acc[1] + acc[3]
        return carry
    lax.fori_loop(0, PEER_TB, tok, 0)


def _table_spec():
    return pl.BlockSpec((PEER_N_EXPERTS, HALF_ROWS, 128), lambda i: (0, 0, 0), pipeline_mode=pl.Buffered(1))


PEER_VMEM_LIMIT = 48 * 1024 * 1024


def peer_u(idx, x, tab):
    n = idx.shape[0]
    tb = PEER_TB
    return pl.pallas_call(
        _peer_u_body, grid=(n // tb,),
        in_specs=[pl.BlockSpec((tb, PEER_SLOTS), lambda i: (i, 0), memory_space=pltpu.SMEM),
                  pl.BlockSpec((tb, 2 * HALF_ROWS, 128), lambda i: (i, 0, 0)),
                  _table_spec()],
        out_specs=pl.BlockSpec((tb, PEER_SLOTS), lambda i: (i, 0)),
        out_shape=jax.ShapeDtypeStruct((n, PEER_SLOTS), jnp.float32),
        scratch_shapes=[pltpu.VMEM((HALF_ROWS * PEER_SLOTS, 128), jnp.float32),
                        pltpu.VMEM((tb, PEER_SLOTS, 128), jnp.float32)],
        compiler_params=pltpu.CompilerParams(dimension_semantics=("arbitrary",), vmem_limit_bytes=PEER_VMEM_LIMIT),
        name="peer_u")(idx, x.reshape(n, 2 * HALF_ROWS, 128), tab)


def peer_v(idx, coef, tab):
    n = idx.shape[0]
    tb = PEER_TB
    out = pl.pallas_call(
        _peer_v_body, grid=(n // tb,),
        in_specs=[pl.BlockSpec((tb, PEER_SLOTS), lambda i: (i, 0), memory_space=pltpu.SMEM),
                  pl.BlockSpec((tb, PEER_SLOTS), lambda i: (i, 0)),
                  _table_spec()],
        out_specs=pl.BlockSpec((tb, 2 * HALF_ROWS, 128), lambda i: (i, 0, 0)),
        out_shape=jax.ShapeDtypeStruct((n, 2 * HALF_ROWS, 128), jnp.float32),
        scratch_shapes=[pltpu.VMEM((tb, PEER_SLOTS, 128), jnp.float32)],
        compiler_params=pltpu.CompilerParams(dimension_semantics=("arbitrary",), vmem_limit_bytes=PEER_VMEM_LIMIT),
        name="peer_v")(idx, coef, tab)
    return out.reshape(n, 2 * HALF_ROWS * 128)


def peer_channel_mixer(xn, wq, subkeys, u_tab, v_tab):
    bsz, seq, d = xn.shape
    kk = PEER_TOPK
    q = (xn @ wq).reshape(bsz, seq, PEER_HEADS, 2, PEER_KEY_DIM // 2)
    s = jnp.einsum('bshpd,hpnd->bshpn', q, subkeys).astype(jnp.float32)
    v1, i1 = lax.top_k(s[..., 0, :], kk)
    v2, i2 = lax.top_k(s[..., 1, :], kk)
    cand = (v1[..., :, None] + v2[..., None, :]).reshape(bsz, seq, PEER_HEADS, kk * kk)
    vals, ci = lax.top_k(cand, kk)
    e1 = jnp.take_along_axis(i1, ci // kk, axis=-1)
    e2 = jnp.take_along_axis(i2, ci % kk, axis=-1)
    experts = e1 * PEER_N_KEYS + e2
    gates = jax.nn.softmax(vals, axis=-1).astype(xn.dtype)
    n_tok = bsz * seq
    idx = experts.reshape(n_tok, PEER_SLOTS).astype(jnp.int32)
    a = peer_u(idx, xn.reshape(n_tok, d), pack_table(u_tab))
    coef = jax.nn.gelu(a, approximate=False) * gates.reshape(n_tok, PEER_SLOTS)
    out = peer_v(idx, coef, pack_table(v_tab))
    return out.reshape(bsz, seq, d)


def _final_norm_body(h_ref, g_ref, o_ref):
    h = h_ref[...]
    o_ref[...] = h * lax.rsqrt(jnp.mean(h * h, axis=-1, keepdims=True) + EPS) * g_ref[...]


def final_norm(h, gain):
    bsz, seq, d = h.shape
    rows = bsz * seq
    tile = 1024
    out = pl.pallas_call(
        _final_norm_body,
        grid=(rows // tile,),
        in_specs=[pl.BlockSpec((tile, d), lambda i: (i, 0)), pl.BlockSpec((1, d), lambda i: (0, 0))],
        out_specs=pl.BlockSpec((tile, d), lambda i: (i, 0)),
        out_shape=jax.ShapeDtypeStruct((rows, d), h.dtype),
        name="final_norm",
    )(h.reshape(rows, d), gain.reshape(1, d))
    return out.reshape(bsz, seq, d)


def kernel(x, c, positions, w_ada, b_ada, w_in, conv_w, a_log, dt_bias, norm_b_w,
           w_pa, w_pb, w_o, peer_wq, peer_subkeys, peer_u, peer_v, final_norm_w):
    h = x
    c_act = jax.nn.silu(c)
    for layer in range(DEPTH):
        mod = c_act @ w_ada[layer] + b_ada[layer]
        sh1, sc1, gt1, sh2, sc2, gt2 = jnp.split(mod, 6, axis=-1)
        n1 = rms_norm(h) * (1.0 + sc1[:, None, :]) + sh1[:, None, :]
        y1 = token_mixers(n1, positions, w_in[layer], conv_w[layer], a_log[layer], dt_bias[layer],
                          norm_b_w[layer], w_pa[layer], w_pb[layer], w_o[layer])
        h = h + gt1[:, None, :] * y1
        n2 = rms_norm(h) * (1.0 + sc2[:, None, :]) + sh2[:, None, :]
        y2 = peer_channel_mixer(n2, peer_wq[layer], peer_subkeys[layer], peer_u[layer], peer_v[layer])
        h = h + gt2[:, None, :] * y2
    return final_norm(h, final_norm_w)
```

```python
import functools

import jax, jax.numpy as jnp
from jax import lax
import numpy as np
from jax.experimental import pallas as pl
from jax.experimental.pallas import tpu as pltpu

D_MODEL = 1024
BATCH = 8
SEQ = 4096
DEPTH = 1

A_HEADS = 8
A_KV_HEADS = 2
A_HEAD_DIM = 64
IDX_HEADS = 16
IDX_DIM = 64
IDX_TOPK_MAX = 256
Q_BLOCK = 128
B_HEADS = 8
B_KEY_DIM = 64
B_VAL_DIM = 64
CONV_WIDTH = 4
CHUNK = 64
ROPE_THETA = 500000.0
ROPE_FRACTION_DEN = 4
PEER_HEADS = 8
PEER_KEY_DIM = 128
PEER_N_KEYS = 128
PEER_N_EXPERTS = PEER_N_KEYS * PEER_N_KEYS
PEER_TOPK = 16
PEER_TOKEN_BLOCK = 128
EPS = 1e-6

A_WIDTH = A_HEADS * A_HEAD_DIM
KV_WIDTH = A_KV_HEADS * A_HEAD_DIM
B_QK_WIDTH = B_HEADS * B_KEY_DIM
B_V_WIDTH = B_HEADS * B_VAL_DIM
CONV_CHANNELS = 2 * B_QK_WIDTH + B_V_WIDTH
IN_SPLITS = (A_WIDTH, KV_WIDTH, KV_WIDTH, IDX_HEADS * IDX_DIM, IDX_DIM, IDX_HEADS,
             B_QK_WIDTH, B_QK_WIDTH, B_V_WIDTH, B_V_WIDTH, B_HEADS, B_HEADS, D_MODEL, D_MODEL)
IN_WIDTH = sum(IN_SPLITS)


def rms_norm(x, gain=None):
    xf = x.astype(jnp.float32)
    y = xf * lax.rsqrt(jnp.mean(xf * xf, axis=-1, keepdims=True) + EPS)
    if gain is not None:
        y = y * gain.astype(jnp.float32)
    return y.astype(x.dtype)


def l2_normalize(x):
    return x * lax.rsqrt(jnp.sum(x * x, axis=-1, keepdims=True) + EPS)


def partial_rotary(x, positions):
    hd = x.shape[-1]
    rd = hd // ROPE_FRACTION_DEN
    half = rd // 2
    inv_freq = jnp.power(jnp.float32(ROPE_THETA), -jnp.arange(half, dtype=jnp.float32) * (2.0 / rd))
    ang = positions.astype(jnp.float32)[..., None] * inv_freq
    cos = jnp.cos(ang)[:, :, None, :].astype(x.dtype)
    sin = jnp.sin(ang)[:, :, None, :].astype(x.dtype)
    x1 = x[..., :half]
    x2 = x[..., half:rd]
    return jnp.concatenate([x1 * cos - x2 * sin, x2 * cos + x1 * sin, x[..., rd:]], axis=-1)


def causal_short_conv(x, w):
    width = w.shape[0]
    seq = x.shape[1]
    xp = jnp.pad(x, ((0, 0), (width - 1, 0), (0, 0)))
    out = xp[:, 0:seq] * w[0]
    for i in range(1, width):
        out = out + xp[:, i:i + seq] * w[i]
    return out


DSA_TQ = 256
DSA_TK = 256
INT_MIN = -2**31
NEG_BIG = -1e30


def _dsa_body(topk, qit_ref, w_ref, qat_ref, ki_ref, ka_ref, vat_ref, o_ref,
              key_ref, m_ref, l_ref, acc_ref):
    tq, tk = DSA_TQ, DSA_TK
    qb = pl.program_id(1)
    n_kv = qb + 1
    t_glob = qb * tq + lax.broadcasted_iota(jnp.int32, (1, tq), 1)
    row = lax.broadcasted_iota(jnp.int32, (tk, 1), 0)
    f32 = jnp.float32

    def p1(j, carry):
        kt = ki_ref[0, j]
        score = jnp.zeros((tk, tq), f32)
        for h in range(IDX_HEADS):
            lt = jnp.dot(kt, qit_ref[0, h], preferred_element_type=f32)
            score = score + w_ref[0, h:h + 1, :] * jnp.maximum(lt, 0.0)
        bits = lax.bitcast_convert_type(score + 0.0, jnp.int32)
        skey = jnp.where(bits >= 0, bits, bits ^ jnp.int32(0x7FFFFFFF))
        skey = jnp.where(j * tk + row <= t_glob, skey, jnp.int32(INT_MIN))
        key_ref[j] = skey
        return carry
    lax.fori_loop(0, n_kv, p1, 0)

    def count(pred):
        def body(j, acc):
            return acc + jnp.sum(jnp.where(pred(key_ref[j], j * tk + row), 1.0, 0.0), axis=0, keepdims=True)
        return lax.fori_loop(0, n_kv, body, jnp.zeros((1, tq), f32))

    kf = jnp.float32(topk)

    def bit_step(i, ku):
        cand_u = ku | lax.shift_left(jnp.int32(1), 31 - i)
        cand = cand_u ^ jnp.int32(INT_MIN)
        c = count(lambda k, s: k >= cand)
        return jnp.where(c >= kf, cand_u, ku)
    ku = lax.fori_loop(0, 32, bit_step, jnp.zeros((1, tq), jnp.int32))
    kth = ku ^ jnp.int32(INT_MIN)
    c_gt = count(lambda k, s: k > kth)
    c_ge = count(lambda k, s: k >= kth)
    short = kth == jnp.int32(INT_MIN)
    x0 = jnp.where(short, jnp.int32(-1), jnp.int32(2**30))
    need = kf - c_gt
    has_tie = jnp.max(jnp.where(jnp.logical_and(c_ge > kf, jnp.logical_not(short)), 1.0, 0.0)) > 0.0

    def tie_search():
        def step(i, x):
            bit = lax.shift_left(jnp.int32(1), 11 - i)
            probe = x + bit - 1
            c = count(lambda k, s: jnp.logical_and(k == kth, s <= probe))
            return jnp.where(c < need, x + bit, x)
        x = lax.fori_loop(0, 12, step, jnp.zeros((1, tq), jnp.int32))
        return jnp.where(short, jnp.int32(-1), x)
    x_lim = lax.cond(has_tie, tie_search, lambda: x0)

    m_ref[...] = jnp.full(m_ref.shape, NEG_BIG, f32)
    l_ref[...] = jnp.zeros(l_ref.shape, f32)
    acc_ref[...] = jnp.zeros(acc_ref.shape, f32)
    rep = A_HEADS // A_KV_HEADS

    def p3(j, carry):
        skey = key_ref[j]
        s_idx = j * tk + row
        sel = jnp.logical_or(skey > kth, jnp.logical_and(skey == kth, s_idx <= x_lim))
        for g in range(A_KV_HEADS):
            kt = ka_ref[0, g, j]
            vt = vat_ref[0, g, j]
            for r in range(rep):
                h = g * rep + r
                s = jnp.dot(kt, qat_ref[0, h], preferred_element_type=f32)
                s = jnp.where(sel, s, NEG_BIG)
                m_old = m_ref[h:h + 1, :]
                m_new = jnp.maximum(m_old, jnp.max(s, axis=0, keepdims=True))
                alpha = jnp.exp(m_old - m_new)
                p = jnp.exp(s - m_new)
                l_ref[h:h + 1, :] = alpha * l_ref[h:h + 1, :] + jnp.sum(p, axis=0, keepdims=True)
                acc_ref[h] = alpha * acc_ref[h] + jnp.dot(vt, p.astype(jnp.bfloat16), preferred_element_type=f32)
                m_ref[h:h + 1, :] = m_new
        return carry
    lax.fori_loop(0, n_kv, p3, 0)
    for h in range(A_HEADS):
        o_ref[0, h] = acc_ref[h] / l_ref[h:h + 1, :]


def dsa_attention_pallas(qa, ka, va, qi, ki, wi):
    bsz, seq = qa.shape[0], qa.shape[1]
    tq, tk = DSA_TQ, DSA_TK
    topk = min(IDX_TOPK_MAX, seq // 4)
    bf = jnp.bfloat16
    idx_scale = (IDX_HEADS ** -0.5) * (IDX_DIM ** -0.5)
    qit = jnp.transpose(qi, (0, 2, 3, 1)).astype(bf)
    w = jnp.transpose(wi.astype(jnp.float32) * idx_scale, (0, 2, 1))
    qat = jnp.transpose(qa * (A_HEAD_DIM ** -0.5), (0, 2, 3, 1)).astype(bf)
    kit = ki.astype(bf).reshape(bsz, seq // tk, tk, IDX_DIM)
    kat = jnp.transpose(ka, (0, 2, 1, 3)).astype(bf).reshape(bsz, A_KV_HEADS, seq // tk, tk, A_HEAD_DIM)
    vat = jnp.transpose(va.astype(bf).reshape(bsz, seq // tk, tk, A_KV_HEADS, A_HEAD_DIM), (0, 3, 1, 4, 2))
    n_kv = seq // tk
    out = pl.pallas_call(
        functools.partial(_dsa_body, topk),
        grid=(bsz, seq // tq),
        in_specs=[
            pl.BlockSpec((1, IDX_HEADS, IDX_DIM, tq), lambda b, q: (b, 0, 0, q)),
            pl.BlockSpec((1, IDX_HEADS, tq), lambda b, q: (b, 0, q)),
            pl.BlockSpec((1, A_HEADS, A_HEAD_DIM, tq), lambda b, q: (b, 0, 0, q)),
            pl.BlockSpec((1, n_kv, tk, IDX_DIM), lambda b, q: (b, 0, 0, 0)),
            pl.BlockSpec((1, A_KV_HEADS, n_kv, tk, A_HEAD_DIM), lambda b, q: (b, 0, 0, 0, 0)),
            pl.BlockSpec((1, A_KV_HEADS, n_kv, A_HEAD_DIM, tk), lambda b, q: (b, 0, 0, 0, 0)),
        ],
        out_specs=pl.BlockSpec((1, A_HEADS, A_HEAD_DIM, tq), lambda b, q: (b, 0, 0, q)),
        out_shape=jax.ShapeDtypeStruct((bsz, A_HEADS, A_HEAD_DIM, seq), jnp.float32),
        scratch_shapes=[
            pltpu.VMEM((n_kv, tk, tq), jnp.int32),
            pltpu.VMEM((A_HEADS, tq), jnp.float32),
            pltpu.VMEM((A_HEADS, tq), jnp.float32),
            pltpu.VMEM((A_HEADS, A_HEAD_DIM, tq), jnp.float32),
        ],
        compiler_params=pltpu.CompilerParams(dimension_semantics=("arbitrary", "arbitrary"),
                                             vmem_limit_bytes=48 * 1024 * 1024),
        name="dsa_attention",
    )(qit, w, qat, kit, kat, vat)
    return jnp.transpose(out, (0, 3, 1, 2)).reshape(bsz, seq, A_HEADS * A_HEAD_DIM)


GDN_G = 8
GDN_AUX = 8


def _gdn_body(q_ref, k_ref, v_ref, aux_ref, gcr_ref, o_ref, s_ref):
    c_sz = CHUNK
    f32, bf = jnp.float32, jnp.bfloat16

    @pl.when(pl.program_id(1) == 0)
    def _():
        s_ref[...] = jnp.zeros(s_ref.shape, f32)

    ri = lax.broadcasted_iota(jnp.int32, (c_sz, c_sz), 0)
    ci = lax.broadcasted_iota(jnp.int32, (c_sz, c_sz), 1)
    incl, strict = ri >= ci, ri > ci
    eye = jnp.where(ri == ci, 1.0, 0.0).astype(f32)
    nt = (((1,), (1,)), ((), ()))
    dot = lambda a, b: jnp.dot(a.astype(bf), b.astype(bf), preferred_element_type=f32)
    dot_nt = lambda a, b: lax.dot_general(a.astype(bf), b.astype(bf), nt, preferred_element_type=f32)
    hp = lambda a, b: jnp.dot(a, b, precision=lax.Precision.HIGHEST, preferred_element_type=f32)

    def chunk(c, carry):
        r0 = pl.multiple_of(c * c_sz, c_sz)
        hs = range(B_HEADS)
        q = [q_ref[0, h, pl.ds(r0, c_sz), :] for h in hs]
        k = [k_ref[0, h, pl.ds(r0, c_sz), :] for h in hs]
        v = [v_ref[0, h, pl.ds(r0, c_sz), :] for h in hs]
        aux = [aux_ref[0, h, pl.ds(r0, c_sz), :] for h in hs]
        gcr = [gcr_ref[0, h, pl.ds(c, 1), :] for h in hs]
        s = [s_ref[h] for h in hs]
        gcc = [a[:, 0:1] for a in aux]
        beta = [a[:, 1:2] for a in aux]
        decay = [jnp.exp(jnp.where(incl, gcc[h] - gcr[h], -jnp.inf)) for h in hs]
        kb = [k[h] * beta[h] for h in hs]
        vb = [v[h] * beta[h] for h in hs]
        low = [jnp.where(strict, dot_nt(kb[h], k[h]) * decay[h], 0.0) for h in hs]
        t = [eye - low[h] for h in hs]
        p = [hp(low[h], low[h]) for h in hs]
        for _ in range(4):
            t = [hp(t[h], eye + p[h]) for h in hs]
            p = [hp(p[h], p[h]) for h in hs]
        t = [hp(t[h], eye + p[h]) for h in hs]
        u = [dot(t[h], vb[h]) for h in hs]
        kcd = [dot(t[h], kb[h] * jnp.exp(gcc[h])) for h in hs]
        intra = [dot_nt(q[h], k[h]) * decay[h] for h in hs]
        v_new = [u[h] - dot(kcd[h], s[h]) for h in hs]
        out = [dot(q[h] * jnp.exp(gcc[h]), s[h]) + dot(intra[h], v_new[h]) for h in hs]
        glast = [g[:, c_sz - 1:c_sz] for g in gcr]
        kd = [k[h] * jnp.exp(glast[h] - gcc[h]) for h in hs]
        s_new = [s[h] * jnp.exp(glast[h]) + dot(kd[h].T, v_new[h]) for h in hs]
        for h in hs:
            s_ref[h] = s_new[h]
            o_ref[0, h, pl.ds(r0, c_sz), :] = out[h]
        return carry
    lax.fori_loop(0, GDN_G, chunk, 0)


def gated_delta_rule_pallas(q, k, v, g, beta):
    bsz, seq, nh, dk = q.shape
    dv = v.shape[-1]
    n = seq // CHUNK
    hm = lambda t: jnp.moveaxis(t, 2, 1)
    gc = jnp.cumsum(hm(g).reshape(bsz, nh, n, CHUNK), axis=-1)
    aux = jnp.stack([gc.reshape(bsz, nh, seq), hm(beta)] + [jnp.zeros((bsz, nh, seq), jnp.float32)] * (GDN_AUX - 2), axis=-1)
    rows = GDN_G * CHUNK
    blk = lambda w: pl.BlockSpec((1, nh, rows, w), lambda b, c: (b, 0, c, 0))
    out = pl.pallas_call(
        _gdn_body, grid=(bsz, n // GDN_G),
        in_specs=[blk(dk), blk(dk), blk(dv), blk(GDN_AUX),
                  pl.BlockSpec((1, nh, GDN_G, CHUNK), lambda b, c: (b, 0, c, 0))],
        out_specs=blk(dv),
        out_shape=jax.ShapeDtypeStruct((bsz, nh, seq, dv), jnp.float32),
        scratch_shapes=[pltpu.VMEM((nh, dk, dv), jnp.float32)],
        compiler_params=pltpu.CompilerParams(dimension_semantics=("arbitrary", "arbitrary"),
                                             vmem_limit_bytes=40 * 1024 * 1024),
        name="gated_delta_rule")(hm(q * (dk ** -0.5)), hm(k), hm(v), aux, gc)
    return jnp.moveaxis(out, 1, 2)


def token_mixers(n, positions, w_in, conv_w, a_log, dt_bias, norm_b_w, w_pa, w_pb, w_o):
    bsz, seq, _ = n.shape
    proj = n @ w_in
    offs = np.cumsum(IN_SPLITS)[:-1].tolist()
    (qa, ka, va, qi, ki, wi, qb, kb, vb, zb, bb, ab, gate_a, gate_b) = jnp.split(proj, offs, axis=-1)

    qa = partial_rotary(qa.reshape(bsz, seq, A_HEADS, A_HEAD_DIM), positions)
    ka = partial_rotary(ka.reshape(bsz, seq, A_KV_HEADS, A_HEAD_DIM), positions)
    va = va.reshape(bsz, seq, A_KV_HEADS, A_HEAD_DIM)
    qi = partial_rotary(qi.reshape(bsz, seq, IDX_HEADS, IDX_DIM), positions)
    ki = partial_rotary(ki.reshape(bsz, seq, 1, IDX_DIM), positions)[:, :, 0]
    o_a = dsa_attention_pallas(qa, ka, va, qi, ki, wi)

    qkv = jax.nn.silu(causal_short_conv(jnp.concatenate([qb, kb, vb], axis=-1), conv_w))
    qb, kb, vb = jnp.split(qkv, [B_QK_WIDTH, 2 * B_QK_WIDTH], axis=-1)
    qb = l2_normalize(qb.reshape(bsz, seq, B_HEADS, B_KEY_DIM).astype(jnp.float32))
    kb = l2_normalize(kb.reshape(bsz, seq, B_HEADS, B_KEY_DIM).astype(jnp.float32))
    vb = vb.reshape(bsz, seq, B_HEADS, B_VAL_DIM).astype(jnp.float32)
    beta = jax.nn.sigmoid(bb.astype(jnp.float32))
    g = -jnp.exp(a_log.astype(jnp.float32)) * jax.nn.softplus(ab.astype(jnp.float32) + dt_bias.astype(jnp.float32))
    o_b = gated_delta_rule_pallas(qb, kb, vb, g, beta)
    z = jax.nn.silu(zb.reshape(bsz, seq, B_HEADS, B_VAL_DIM).astype(jnp.float32))
    o_b = (rms_norm(o_b, norm_b_w) * z).astype(n.dtype).reshape(bsz, seq, B_V_WIDTH)

    merged = jax.nn.sigmoid(gate_a) * (o_a @ w_pa) + jax.nn.sigmoid(gate_b) * (o_b @ w_pb)
    return merged @ w_o


PEER_SLOTS = PEER_HEADS * PEER_TOPK
PEER_TB = 32
HALF_ROWS = 4
HI_MASK = -65536


def pack_table(tab):
    bits = lax.bitcast_convert_type(tab.astype(jnp.bfloat16), jnp.uint16).astype(jnp.uint32)
    half = tab.shape[1] // 2
    word = bits[:, :half] | (bits[:, half:] << 16)
    return lax.bitcast_convert_type(word, jnp.int32).reshape(tab.shape[0], HALF_ROWS, 128)


def _unpack(w):
    lo = lax.bitcast_convert_type(lax.shift_left(w, 16), jnp.float32)
    hi = lax.bitcast_convert_type(w & jnp.int32(HI_MASK), jnp.float32)
    return lo, hi


def _peer_u_body(idx_ref, x_ref, tab_ref, o_ref, s_ref, sb_ref):
    def tok(t, carry):
        xlo = x_ref[t, 0:HALF_ROWS, :]
        xhi = x_ref[t, HALF_ROWS:2 * HALF_ROWS, :]
        for k in range(PEER_SLOTS):
            lo, hi = _unpack(tab_ref[idx_ref[t, k]])
            s_ref[HALF_ROWS * k:HALF_ROWS * (k + 1), :] = lo * xlo + hi * xhi
        s4 = s_ref[pl.ds(0, PEER_SLOTS, stride=HALF_ROWS), :]
        for r in range(1, HALF_ROWS):
            s4 = s4 + s_ref[pl.ds(r, PEER_SLOTS, stride=HALF_ROWS), :]
        sb_ref[t] = s4
        return carry
    lax.fori_loop(0, PEER_TB, tok, 0)
    for t in range(PEER_TB):
        o_ref[t:t + 1, :] = jnp.sum(sb_ref[t].T, axis=0, keepdims=True)


def _peer_v_body(idx_ref, coef_ref, tab_ref, o_ref, cb_ref):
    for t in range(PEER_TB):
        cb_ref[t] = jnp.broadcast_to(coef_ref[t:t + 1, :], (PEER_SLOTS, 128)).T

    def tok(t, carry):
        nacc = 2
        acc = [jnp.zeros((HALF_ROWS, 128), jnp.float32) for _ in range(2 * nacc)]
        for k in range(PEER_SLOTS):
            lo, hi = _unpack(tab_ref[idx_ref[t, k]])
            c = jnp.broadcast_to(cb_ref[t, k:k + 1, :], (HALF_ROWS, 128))
            a = k % nacc
            acc[2 * a] = acc[2 * a] + c * lo
            acc[2 * a + 1] = acc[2 * a + 1] + c * hi
        o_ref[t, 0:HALF_ROWS, :] = acc[0] + acc[2]
        o_ref[t, HALF_ROWS:2 * HALF_ROWS, :] = acc[1] + acc[3]
        return carry
    lax.fori_loop(0, PEER_TB, tok, 0)


def _table_spec():
    return pl.BlockSpec((PEER_N_EXPERTS, HALF_ROWS, 128), lambda i: (0, 0, 0), pipeline_mode=pl.Buffered(1))


PEER_VMEM_LIMIT = 48 * 1024 * 1024


def peer_u(idx, x, tab):
    n = idx.shape[0]
    tb = PEER_TB
    return pl.pallas_call(
        _peer_u_body, grid=(n // tb,),
        in_specs=[pl.BlockSpec((tb, PEER_SLOTS), lambda i: (i, 0), memory_space=pltpu.SMEM),
                  pl.BlockSpec((tb, 2 * HALF_ROWS, 128), lambda i: (i, 0, 0)),
                  _table_spec()],
        out_specs=pl.BlockSpec((tb, PEER_SLOTS), lambda i: (i, 0)),
        out_shape=jax.ShapeDtypeStruct((n, PEER_SLOTS), jnp.float32),
        scratch_shapes=[pltpu.VMEM((HALF_ROWS * PEER_SLOTS, 128), jnp.float32),
                        pltpu.VMEM((tb, PEER_SLOTS, 128), jnp.float32)],
        compiler_params=pltpu.CompilerParams(dimension_semantics=("arbitrary",), vmem_limit_bytes=PEER_VMEM_LIMIT),
        name="peer_u")(idx, x.reshape(n, 2 * HALF_ROWS, 128), tab)


def peer_v(idx, coef, tab):
    n = idx.shape[0]
    tb = PEER_TB
    out = pl.pallas_call(
        _peer_v_body, grid=(n // tb,),
        in_specs=[pl.BlockSpec((tb, PEER_SLOTS), lambda i: (i, 0), memory_space=pltpu.SMEM),
                  pl.BlockSpec((tb, PEER_SLOTS), lambda i: (i, 0)),
                  _table_spec()],
        out_specs=pl.BlockSpec((tb, 2 * HALF_ROWS, 128), lambda i: (i, 0, 0)),
        out_shape=jax.ShapeDtypeStruct((n, 2 * HALF_ROWS, 128), jnp.float32),
        scratch_shapes=[pltpu.VMEM((tb, PEER_SLOTS, 128), jnp.float32)],
        compiler_params=pltpu.CompilerParams(dimension_semantics=("arbitrary",), vmem_limit_bytes=PEER_VMEM_LIMIT),
        name="peer_v")(idx, coef, tab)
    return out.reshape(n, 2 * HALF_ROWS * 128)


PEER_TT = 256


def _extract_top(ref, n_out):
    rows, t = ref.shape
    rid = lax.broadcasted_iota(jnp.int32, (rows, t), 0)
    vals, idxs = [], []
    for _ in range(n_out):
        s = ref[...]
        m = jnp.max(s, axis=0, keepdims=True)
        ix = jnp.min(jnp.where(s == m, rid, rows), axis=0, keepdims=True)
        ref[...] = jnp.where(rid == ix, -jnp.inf, s)
        vals.append(m)
        idxs.append(ix)
    return vals, idxs


def _peer_route_body(x_ref, wq_ref, sk_ref, idx_ref, gate_ref, s_ref, cand_ref, v_ref, i_ref, et_ref, gt_ref):
    kk, nk = PEER_TOPK, PEER_N_KEYS
    f32 = jnp.float32
    q = jnp.dot(x_ref[...].astype(jnp.bfloat16), wq_ref[...], preferred_element_type=f32).astype(jnp.bfloat16)
    nt = (((1,), (1,)), ((), ()))
    row16 = lax.broadcasted_iota(jnp.int32, (kk, PEER_TT), 0)
    for h in range(PEER_HEADS):
        qh = q[:, h * PEER_KEY_DIM:(h + 1) * PEER_KEY_DIM]
        s_ref[...] = lax.dot_general(sk_ref[h], qh, nt, preferred_element_type=f32)
        for p in range(2):
            vals, idxs = _extract_top(s_ref.at[p * nk:(p + 1) * nk, :], kk)
            for i in range(kk):
                v_ref[p, i:i + 1, :] = vals[i]
                i_ref[p, i:i + 1, :] = idxs[i]
        v2 = v_ref[1]
        for i in range(kk):
            cand_ref[i * kk:(i + 1) * kk, :] = v_ref[0, i:i + 1, :] + v2
        vals, cis = _extract_top(cand_ref, kk)
        i1, i2 = i_ref[0], i_ref[1]
        es = [jnp.exp(v - vals[0]) for v in vals]
        den = es[0]
        for e in es[1:]:
            den = den + e
        for k in range(kk):
            ci = cis[k]
            e1 = jnp.sum(jnp.where(row16 == lax.shift_right_logical(ci, 4), i1, 0), axis=0, keepdims=True)
            e2 = jnp.sum(jnp.where(row16 == (ci & 15), i2, 0), axis=0, keepdims=True)
            et_ref[h * kk + k:h * kk + k + 1, :] = e1 * nk + e2
            gt_ref[h * kk + k:h * kk + k + 1, :] = es[k] / den
    idx_ref[...] = lax.bitcast_convert_type(lax.bitcast_convert_type(et_ref[...], f32).T, jnp.int32)
    gate_ref[...] = gt_ref[...].T


def peer_route(xn, wq, subkeys):
    n = xn.shape[0]
    tt = PEER_TT
    half = PEER_KEY_DIM // 2
    z = jnp.zeros((PEER_HEADS, PEER_N_KEYS, half), subkeys.dtype)
    skbd = jnp.concatenate([jnp.concatenate([subkeys[:, 0], z], axis=-1),
                            jnp.concatenate([z, subkeys[:, 1]], axis=-1)], axis=1).astype(jnp.bfloat16)
    return pl.pallas_call(
        _peer_route_body, grid=(n // tt,),
        in_specs=[pl.BlockSpec((tt, D_MODEL), lambda i: (i, 0)),
                  pl.BlockSpec((D_MODEL, PEER_HEADS * PEER_KEY_DIM), lambda i: (0, 0)),
                  pl.BlockSpec((PEER_HEADS, 2 * PEER_N_KEYS, PEER_KEY_DIM), lambda i: (0, 0, 0))],
        out_specs=[pl.BlockSpec((tt, PEER_SLOTS), lambda i: (i, 0)), pl.BlockSpec((tt, PEER_SLOTS), lambda i: (i, 0))],
        out_shape=[jax.ShapeDtypeStruct((n, PEER_SLOTS), jnp.int32), jax.ShapeDtypeStruct((n, PEER_SLOTS), jnp.float32)],
        scratch_shapes=[pltpu.VMEM((2 * PEER_N_KEYS, tt), jnp.float32),
                        pltpu.VMEM((PEER_TOPK * PEER_TOPK, tt), jnp.float32),
                        pltpu.VMEM((2, PEER_TOPK, tt), jnp.float32),
                        pltpu.VMEM((2, PEER_TOPK, tt), jnp.int32),
                        pltpu.VMEM((PEER_SLOTS, tt), jnp.int32),
                        pltpu.VMEM((PEER_SLOTS, tt), jnp.float32)],
        compiler_params=pltpu.CompilerParams(dimension_semantics=("arbitrary",), vmem_limit_bytes=32 * 1024 * 1024),
        name="peer_route")(xn, wq.astype(jnp.bfloat16), skbd)


def peer_channel_mixer(xn, wq, subkeys, u_tab, v_tab):
    bsz, seq, d = xn.shape
    n_tok = bsz * seq
    x2 = xn.reshape(n_tok, d)
    idx, gates = peer_route(x2, wq, subkeys)
    a = peer_u(idx, x2, pack_table(u_tab))
    coef = jax.nn.gelu(a, approximate=False) * gates
    out = peer_v(idx, coef, pack_table(v_tab))
    return out.reshape(bsz, seq, d)


def _final_norm_body(h_ref, g_ref, o_ref):
    h = h_ref[...]
    o_ref[...] = h * lax.rsqrt(jnp.mean(h * h, axis=-1, keepdims=True) + EPS) * g_ref[...]


def final_norm(h, gain):
    bsz, seq, d = h.shape
    rows = bsz * seq
    tile = 1024
    out = pl.pallas_call(
        _final_norm_body,
        grid=(rows // tile,),
        in_specs=[pl.BlockSpec((tile, d), lambda i: (i, 0)), pl.BlockSpec((1, d), lambda i: (0, 0))],
        out_specs=pl.BlockSpec((tile, d), lambda i: (i, 0)),
        out_shape=jax.ShapeDtypeStruct((rows, d), h.dtype),
        name="final_norm",
    )(h.reshape(rows, d), gain.reshape(1, d))
    return out.reshape(bsz, seq, d)


def kernel(x, c, positions, w_ada, b_ada, w_in, conv_w, a_log, dt_bias, norm_b_w,
           w_pa, w_pb, w_o, peer_wq, peer_subkeys, peer_u, peer_v, final_norm_w):
    h = x
    c_act = jax.nn.silu(c)
    for layer in range(DEPTH):
        mod = c_act @ w_ada[layer] + b_ada[layer]
        sh1, sc1, gt1, sh2, sc2, gt2 = jnp.split(mod, 6, axis=-1)
        n1 = rms_norm(h) * (1.0 + sc1[:, None, :]) + sh1[:, None, :]
        y1 = token_mixers(n1, positions, w_in[layer], conv_w[layer], a_log[layer], dt_bias[layer],
                          norm_b_w[layer], w_pa[layer], w_pb[layer], w_o[layer])
        h = h + gt1[:, None, :] * y1
        n2 = rms_norm(h) * (1.0 + sc2[:, None, :]) + sh2[:, None, :]
        y2 = peer_channel_mixer(n2, peer_wq[layer], peer_subkeys[layer], peer_u[layer], peer_v[layer])
        h = h + gt2[:, None, :] * y2
    return final_norm(h, final_norm_w)
```

```python
import functools

import jax, jax.numpy as jnp
from jax import lax
import numpy as np
from jax.experimental import pallas as pl
from jax.experimental.pallas import tpu as pltpu

D_MODEL = 1024
BATCH = 8
SEQ = 4096
DEPTH = 1

A_HEADS = 8
A_KV_HEADS = 2
A_HEAD_DIM = 64
IDX_HEADS = 16
IDX_DIM = 64
IDX_TOPK_MAX = 256
Q_BLOCK = 128
B_HEADS = 8
B_KEY_DIM = 64
B_VAL_DIM = 64
CONV_WIDTH = 4
CHUNK = 64
ROPE_THETA = 500000.0
ROPE_FRACTION_DEN = 4
PEER_HEADS = 8
PEER_KEY_DIM = 128
PEER_N_KEYS = 128
PEER_N_EXPERTS = PEER_N_KEYS * PEER_N_KEYS
PEER_TOPK = 16
PEER_TOKEN_BLOCK = 128
EPS = 1e-6

A_WIDTH = A_HEADS * A_HEAD_DIM
KV_WIDTH = A_KV_HEADS * A_HEAD_DIM
B_QK_WIDTH = B_HEADS * B_KEY_DIM
B_V_WIDTH = B_HEADS * B_VAL_DIM
CONV_CHANNELS = 2 * B_QK_WIDTH + B_V_WIDTH
IN_SPLITS = (A_WIDTH, KV_WIDTH, KV_WIDTH, IDX_HEADS * IDX_DIM, IDX_DIM, IDX_HEADS,
             B_QK_WIDTH, B_QK_WIDTH, B_V_WIDTH, B_V_WIDTH, B_HEADS, B_HEADS, D_MODEL, D_MODEL)
IN_WIDTH = sum(IN_SPLITS)


def rms_norm(x, gain=None):
    xf = x.astype(jnp.float32)
    y = xf * lax.rsqrt(jnp.mean(xf * xf, axis=-1, keepdims=True) + EPS)
    if gain is not None:
        y = y * gain.astype(jnp.float32)
    return y.astype(x.dtype)


def l2_normalize(x):
    return x * lax.rsqrt(jnp.sum(x * x, axis=-1, keepdims=True) + EPS)


def partial_rotary(x, positions):
    hd = x.shape[-1]
    rd = hd // ROPE_FRACTION_DEN
    half = rd // 2
    inv_freq = jnp.power(jnp.float32(ROPE_THETA), -jnp.arange(half, dtype=jnp.float32) * (2.0 / rd))
    ang = positions.astype(jnp.float32)[..., None] * inv_freq
    cos = jnp.cos(ang)[:, :, None, :].astype(x.dtype)
    sin = jnp.sin(ang)[:, :, None, :].astype(x.dtype)
    x1 = x[..., :half]
    x2 = x[..., half:rd]
    return jnp.concatenate([x1 * cos - x2 * sin, x2 * cos + x1 * sin, x[..., rd:]], axis=-1)


def causal_short_conv(x, w):
    width = w.shape[0]
    seq = x.shape[1]
    xp = jnp.pad(x, ((0, 0), (width - 1, 0), (0, 0)))
    out = xp[:, 0:seq] * w[0]
    for i in range(1, width):
        out = out + xp[:, i:i + seq] * w[i]
    return out


DSA_TQ = 256
DSA_TK = 256
INT_MIN = -2**31
NEG_BIG = -1e30


def _dsa_body(topk, qit_ref, w_ref, qat_ref, ki_ref, ka_ref, vat_ref, o_ref,
              key_ref):
    tq, tk = DSA_TQ, DSA_TK
    qb = pl.program_id(1)
    n_kv = qb + 1
    t_glob = qb * tq + lax.broadcasted_iota(jnp.int32, (1, tq), 1)
    row = lax.broadcasted_iota(jnp.int32, (tk, 1), 0)
    f32 = jnp.float32

    def p1(j, carry):
        kt = ki_ref[0, j]
        score = jnp.zeros((tk, tq), f32)
        for h in range(IDX_HEADS):
            lt = jnp.dot(kt, qit_ref[0, h], preferred_element_type=f32)
            score = score + w_ref[0, h:h + 1, :] * jnp.maximum(lt, 0.0)
        bits = lax.bitcast_convert_type(score + 0.0, jnp.int32)
        skey = jnp.where(bits >= 0, bits, bits ^ jnp.int32(0x7FFFFFFF))
        skey = jnp.where(j * tk + row <= t_glob, skey, jnp.int32(INT_MIN))
        key_ref[j] = skey
        return carry
    lax.fori_loop(0, n_kv, p1, 0)

    def count(pred):
        def body(j, acc):
            hit = jnp.where(pred(key_ref[j], j * tk + row), 1.0, 0.0)
            return acc + jnp.sum(hit.reshape(tk // 8, 8, tq), axis=0)
        acc = lax.fori_loop(0, n_kv, body, jnp.zeros((8, tq), f32))
        return jnp.sum(acc, axis=0, keepdims=True)

    kf = jnp.float32(topk)

    def bit_step(i, ku):
        cand_u = ku | lax.shift_left(jnp.int32(1), 31 - i)
        cand = cand_u ^ jnp.int32(INT_MIN)
        c = count(lambda k, s: k >= cand)
        return jnp.where(c >= kf, cand_u, ku)
    ku = lax.fori_loop(0, 32, bit_step, jnp.zeros((1, tq), jnp.int32))
    kth = ku ^ jnp.int32(INT_MIN)
    c_gt = count(lambda k, s: k > kth)
    c_ge = count(lambda k, s: k >= kth)
    short = kth == jnp.int32(INT_MIN)
    x0 = jnp.where(short, jnp.int32(-1), jnp.int32(2**30))
    need = kf - c_gt
    has_tie = jnp.max(jnp.where(jnp.logical_and(c_ge > kf, jnp.logical_not(short)), 1.0, 0.0)) > 0.0

    def tie_search():
        def step(i, x):
            bit = lax.shift_left(jnp.int32(1), 11 - i)
            probe = x + bit - 1
            c = count(lambda k, s: jnp.logical_and(k == kth, s <= probe))
            return jnp.where(c < need, x + bit, x)
        x = lax.fori_loop(0, 12, step, jnp.zeros((1, tq), jnp.int32))
        return jnp.where(short, jnp.int32(-1), x)
    x_lim = lax.cond(has_tie, tie_search, lambda: x0)

    rep = A_HEADS // A_KV_HEADS
    hs = range(A_HEADS)

    def p3(j, carry):
        m, l, acc = carry
        skey = key_ref[j]
        s_idx = j * tk + row
        sel = jnp.logical_or(skey > kth, jnp.logical_and(skey == kth, s_idx <= x_lim))
        kt = [ka_ref[0, g, j] for g in range(A_KV_HEADS)]
        vt = [vat_ref[0, g, j] for g in range(A_KV_HEADS)]
        s = [jnp.where(sel, jnp.dot(kt[h // rep], qat_ref[0, h], preferred_element_type=f32), NEG_BIG) for h in hs]
        m_new = [jnp.maximum(m[h], jnp.max(s[h], axis=0, keepdims=True)) for h in hs]
        alpha = [jnp.exp(m[h] - m_new[h]) for h in hs]
        p = [jnp.exp(s[h] - m_new[h]) for h in hs]
        l_new = [alpha[h] * l[h] + jnp.sum(p[h], axis=0, keepdims=True) for h in hs]
        acc_new = [alpha[h] * acc[h] + jnp.dot(vt[h // rep], p[h].astype(jnp.bfloat16), preferred_element_type=f32)
                   for h in hs]
        return tuple(m_new), tuple(l_new), tuple(acc_new)

    init = (tuple(jnp.full((1, tq), NEG_BIG, f32) for _ in hs), tuple(jnp.zeros((1, tq), f32) for _ in hs),
            tuple(jnp.zeros((A_HEAD_DIM, tq), f32) for _ in hs))
    _, l_fin, acc_fin = lax.fori_loop(0, n_kv, p3, init)
    for h in hs:
        o_ref[0, h] = acc_fin[h] / l_fin[h]


def dsa_attention_pallas(qa, ka, va, qi, ki, wi):
    bsz, seq = qa.shape[0], qa.shape[1]
    tq, tk = DSA_TQ, DSA_TK
    topk = min(IDX_TOPK_MAX, seq // 4)
    bf = jnp.bfloat16
    idx_scale = (IDX_HEADS ** -0.5) * (IDX_DIM ** -0.5)
    qit = jnp.transpose(qi, (0, 2, 3, 1)).astype(bf)
    w = jnp.transpose(wi.astype(jnp.float32) * idx_scale, (0, 2, 1))
    qat = jnp.transpose(qa * (A_HEAD_DIM ** -0.5), (0, 2, 3, 1)).astype(bf)
    kit = ki.astype(bf).reshape(bsz, seq // tk, tk, IDX_DIM)
    kat = jnp.transpose(ka, (0, 2, 1, 3)).astype(bf).reshape(bsz, A_KV_HEADS, seq // tk, tk, A_HEAD_DIM)
    vat = jnp.transpose(va.astype(bf).reshape(bsz, seq // tk, tk, A_KV_HEADS, A_HEAD_DIM), (0, 3, 1, 4, 2))
    n_kv = seq // tk
    out = pl.pallas_call(
        functools.partial(_dsa_body, topk),
        grid=(bsz, seq // tq),
        in_specs=[
            pl.BlockSpec((1, IDX_HEADS, IDX_DIM, tq), lambda b, q: (b, 0, 0, q)),
            pl.BlockSpec((1, IDX_HEADS, tq), lambda b, q: (b, 0, q)),
            pl.BlockSpec((1, A_HEADS, A_HEAD_DIM, tq), lambda b, q: (b, 0, 0, q)),
            pl.BlockSpec((1, n_kv, tk, IDX_DIM), lambda b, q: (b, 0, 0, 0)),
            pl.BlockSpec((1, A_KV_HEADS, n_kv, tk, A_HEAD_DIM), lambda b, q: (b, 0, 0, 0, 0)),
            pl.BlockSpec((1, A_KV_HEADS, n_kv, A_HEAD_DIM, tk), lambda b, q: (b, 0, 0, 0, 0)),
        ],
        out_specs=pl.BlockSpec((1, A_HEADS, A_HEAD_DIM, tq), lambda b, q: (b, 0, 0, q)),
        out_shape=jax.ShapeDtypeStruct((bsz, A_HEADS, A_HEAD_DIM, seq), jnp.float32),
        scratch_shapes=[
            pltpu.VMEM((n_kv, tk, tq), jnp.int32),
        ],
        compiler_params=pltpu.CompilerParams(dimension_semantics=("arbitrary", "arbitrary"),
                                             vmem_limit_bytes=48 * 1024 * 1024),
        name="dsa_attention",
    )(qit, w, qat, kit, kat, vat)
    return jnp.transpose(out, (0, 3, 1, 2)).reshape(bsz, seq, A_HEADS * A_HEAD_DIM)


GDN_G = 8
GDN_AUX = 8


def _gdn_body(q_ref, k_ref, v_ref, aux_ref, gcr_ref, o_ref, s_ref):
    c_sz = CHUNK
    f32, bf = jnp.float32, jnp.bfloat16

    @pl.when(pl.program_id(1) == 0)
    def _():
        s_ref[...] = jnp.zeros(s_ref.shape, f32)

    ri = lax.broadcasted_iota(jnp.int32, (c_sz, c_sz), 0)
    ci = lax.broadcasted_iota(jnp.int32, (c_sz, c_sz), 1)
    incl, strict = ri >= ci, ri > ci
    eye = jnp.where(ri == ci, 1.0, 0.0).astype(f32)
    nt = (((1,), (1,)), ((), ()))
    dot = lambda a, b: jnp.dot(a.astype(bf), b.astype(bf), preferred_element_type=f32)
    dot_nt = lambda a, b: lax.dot_general(a.astype(bf), b.astype(bf), nt, preferred_element_type=f32)
    hp = lambda a, b: jnp.dot(a, b, precision=lax.Precision.HIGHEST, preferred_element_type=f32)

    def chunk(c, carry):
        r0 = pl.multiple_of(c * c_sz, c_sz)
        hs = range(B_HEADS)
        q = [q_ref[0, h, pl.ds(r0, c_sz), :] for h in hs]
        k = [k_ref[0, h, pl.ds(r0, c_sz), :] for h in hs]
        v = [v_ref[0, h, pl.ds(r0, c_sz), :] for h in hs]
        aux = [aux_ref[0, h, pl.ds(r0, c_sz), :] for h in hs]
        gcr = [gcr_ref[0, h, pl.ds(c, 1), :] for h in hs]
        s = [s_ref[h] for h in hs]
        gcc = [a[:, 0:1] for a in aux]
        beta = [a[:, 1:2] for a in aux]
        decay = [jnp.exp(jnp.where(incl, gcc[h] - gcr[h], -jnp.inf)) for h in hs]
        kb = [k[h] * beta[h] for h in hs]
        vb = [v[h] * beta[h] for h in hs]
        low = [jnp.where(strict, dot_nt(kb[h], k[h]) * decay[h], 0.0) for h in hs]
        t = [eye - low[h] for h in hs]
        p = [hp(low[h], low[h]) for h in hs]
        for _ in range(4):
            t = [hp(t[h], eye + p[h]) for h in hs]
            p = [hp(p[h], p[h]) for h in hs]
        t = [hp(t[h], eye + p[h]) for h in hs]
        u = [dot(t[h], vb[h]) for h in hs]
        kcd = [dot(t[h], kb[h] * jnp.exp(gcc[h])) for h in hs]
        intra = [dot_nt(q[h], k[h]) * decay[h] for h in hs]
        v_new = [u[h] - dot(kcd[h], s[h]) for h in hs]
        out = [dot(q[h] * jnp.exp(gcc[h]), s[h]) + dot(intra[h], v_new[h]) for h in hs]
        glast = [g[:, c_sz - 1:c_sz] for g in gcr]
        kd = [k[h] * jnp.exp(glast[h] - gcc[h]) for h in hs]
        s_new = [s[h] * jnp.exp(glast[h]) + dot(kd[h].T, v_new[h]) for h in hs]
        for h in hs:
            s_ref[h] = s_new[h]
            o_ref[0, h, pl.ds(r0, c_sz), :] = out[h]
        return carry
    lax.fori_loop(0, GDN_G, chunk, 0)


def gated_delta_rule_pallas(q, k, v, g, beta):
    bsz, seq, nh, dk = q.shape
    dv = v.shape[-1]
    n = seq // CHUNK
    hm = lambda t: jnp.moveaxis(t, 2, 1)
    gc = jnp.cumsum(hm(g).reshape(bsz, nh, n, CHUNK), axis=-1)
    aux = jnp.stack([gc.reshape(bsz, nh, seq), hm(beta)] + [jnp.zeros((bsz, nh, seq), jnp.float32)] * (GDN_AUX - 2), axis=-1)
    rows = GDN_G * CHUNK
    blk = lambda w: pl.BlockSpec((1, nh, rows, w), lambda b, c: (b, 0, c, 0))
    out = pl.pallas_call(
        _gdn_body, grid=(bsz, n // GDN_G),
        in_specs=[blk(dk), blk(dk), blk(dv), blk(GDN_AUX),
                  pl.BlockSpec((1, nh, GDN_G, CHUNK), lambda b, c: (b, 0, c, 0))],
        out_specs=blk(dv),
        out_shape=jax.ShapeDtypeStruct((bsz, nh, seq, dv), jnp.float32),
        scratch_shapes=[pltpu.VMEM((nh, dk, dv), jnp.float32)],
        compiler_params=pltpu.CompilerParams(dimension_semantics=("arbitrary", "arbitrary"),
                                             vmem_limit_bytes=40 * 1024 * 1024),
        name="gated_delta_rule")(hm(q * (dk ** -0.5)), hm(k), hm(v), aux, gc)
    return jnp.moveaxis(out, 1, 2)


def token_mixers(n, positions, w_in, conv_w, a_log, dt_bias, norm_b_w, w_pa, w_pb, w_o):
    bsz, seq, _ = n.shape
    proj = n @ w_in
    offs = np.cumsum(IN_SPLITS)[:-1].tolist()
    (qa, ka, va, qi, ki, wi, qb, kb, vb, zb, bb, ab, gate_a, gate_b) = jnp.split(proj, offs, axis=-1)

    qa = partial_rotary(qa.reshape(bsz, seq, A_HEADS, A_HEAD_DIM), positions)
    ka = partial_rotary(ka.reshape(bsz, seq, A_KV_HEADS, A_HEAD_DIM), positions)
    va = va.reshape(bsz, seq, A_KV_HEADS, A_HEAD_DIM)
    qi = partial_rotary(qi.reshape(bsz, seq, IDX_HEADS, IDX_DIM), positions)
    ki = partial_rotary(ki.reshape(bsz, seq, 1, IDX_DIM), positions)[:, :, 0]
    o_a = dsa_attention_pallas(qa, ka, va, qi, ki, wi)

    qkv = jax.nn.silu(causal_short_conv(jnp.concatenate([qb, kb, vb], axis=-1), conv_w))
    qb, kb, vb = jnp.split(qkv, [B_QK_WIDTH, 2 * B_QK_WIDTH], axis=-1)
    qb = l2_normalize(qb.reshape(bsz, seq, B_HEADS, B_KEY_DIM).astype(jnp.float32))
    kb = l2_normalize(kb.reshape(bsz, seq, B_HEADS, B_KEY_DIM).astype(jnp.float32))
    vb = vb.reshape(bsz, seq, B_HEADS, B_VAL_DIM).astype(jnp.float32)
    beta = jax.nn.sigmoid(bb.astype(jnp.float32))
    g = -jnp.exp(a_log.astype(jnp.float32)) * jax.nn.softplus(ab.astype(jnp.float32) + dt_bias.astype(jnp.float32))
    o_b = gated_delta_rule_pallas(qb, kb, vb, g, beta)
    z = jax.nn.silu(zb.reshape(bsz, seq, B_HEADS, B_VAL_DIM).astype(jnp.float32))
    o_b = (rms_norm(o_b, norm_b_w) * z).astype(n.dtype).reshape(bsz, seq, B_V_WIDTH)

    merged = jax.nn.sigmoid(gate_a) * (o_a @ w_pa) + jax.nn.sigmoid(gate_b) * (o_b @ w_pb)
    return merged @ w_o


PEER_SLOTS = PEER_HEADS * PEER_TOPK
PEER_TB = 32
HALF_ROWS = 4
HI_MASK = -65536


def pack_table(tab):
    bits = lax.bitcast_convert_type(tab.astype(jnp.bfloat16), jnp.uint16).astype(jnp.uint32)
    half = tab.shape[1] // 2
    word = bits[:, :half] | (bits[:, half:] << 16)
    return lax.bitcast_convert_type(word, jnp.int32).reshape(tab.shape[0] * HALF_ROWS, 128)


def _table_row(tab_ref, rows, k):
    return _unpack(tab_ref[pl.ds(pl.multiple_of(rows[k], HALF_ROWS), HALF_ROWS), :])


def _unpack(w):
    lo = lax.bitcast_convert_type(lax.shift_left(w, 16), jnp.float32)
    hi = lax.bitcast_convert_type(w & jnp.int32(HI_MASK), jnp.float32)
    return lo, hi


def _peer_u_body(idx_ref, x_ref, tab_ref, o_ref, s_ref, sb_ref):
    def tok(t, carry):
        xlo = x_ref[t, 0:HALF_ROWS, :]
        xhi = x_ref[t, HALF_ROWS:2 * HALF_ROWS, :]
        rows = idx_ref.at[t]
        for k in range(PEER_SLOTS):
            lo, hi = _table_row(tab_ref, rows, k)
            s_ref[HALF_ROWS * k:HALF_ROWS * (k + 1), :] = lo * xlo + hi * xhi
        s4 = s_ref[pl.ds(0, PEER_SLOTS, stride=HALF_ROWS), :]
        for r in range(1, HALF_ROWS):
            s4 = s4 + s_ref[pl.ds(r, PEER_SLOTS, stride=HALF_ROWS), :]
        sb_ref[t] = s4
        return carry
    lax.fori_loop(0, PEER_TB, tok, 0)
    for t in range(PEER_TB):
        o_ref[t:t + 1, :] = jnp.sum(sb_ref[t].T, axis=0, keepdims=True)


def _peer_v_body(idx_ref, coef_ref, tab_ref, o_ref, cb_ref):
    for t in range(PEER_TB):
        cb_ref[t] = jnp.broadcast_to(coef_ref[t:t + 1, :], (PEER_SLOTS, 128)).T

    def tok(t, carry):
        nacc = 2
        acc = [jnp.zeros((HALF_ROWS, 128), jnp.float32) for _ in range(2 * nacc)]
        rows = idx_ref.at[t]
        for k in range(PEER_SLOTS):
            lo, hi = _table_row(tab_ref, rows, k)
            c = jnp.broadcast_to(cb_ref[t, k:k + 1, :], (HALF_ROWS, 128))
            a = k % nacc
            acc[2 * a] = acc[2 * a] + c * lo
            acc[2 * a + 1] = acc[2 * a + 1] + c * hi
        o_ref[t, 0:HALF_ROWS, :] = acc[0] + acc[2]
        o_ref[t, HALF_ROWS:2 * HALF_ROWS, :] = acc[1] + acc[3]
        return carry
    lax.fori_loop(0, PEER_TB, tok, 0)


def _table_spec():
    return pl.BlockSpec((PEER_N_EXPERTS * HALF_ROWS, 128), lambda i: (0, 0), pipeline_mode=pl.Buffered(1))


PEER_VMEM_LIMIT = 48 * 1024 * 1024


def peer_u(idx, x, tab):
    n = idx.shape[0]
    tb = PEER_TB
    return pl.pallas_call(
        _peer_u_body, grid=(n // tb,),
        in_specs=[pl.BlockSpec((tb, PEER_SLOTS), lambda i: (i, 0), memory_space=pltpu.SMEM),
                  pl.BlockSpec((tb, 2 * HALF_ROWS, 128), lambda i: (i, 0, 0)),
                  _table_spec()],
        out_specs=pl.BlockSpec((tb, PEER_SLOTS), lambda i: (i, 0)),
        out_shape=jax.ShapeDtypeStruct((n, PEER_SLOTS), jnp.float32),
        scratch_shapes=[pltpu.VMEM((HALF_ROWS * PEER_SLOTS, 128), jnp.float32),
                        pltpu.VMEM((tb, PEER_SLOTS, 128), jnp.float32)],
        compiler_params=pltpu.CompilerParams(dimension_semantics=("arbitrary",), vmem_limit_bytes=PEER_VMEM_LIMIT),
        name="peer_u")(idx, x.reshape(n, 2 * HALF_ROWS, 128), tab)


def peer_v(idx, coef, tab):
    n = idx.shape[0]
    tb = PEER_TB
    out = pl.pallas_call(
        _peer_v_body, grid=(n // tb,),
        in_specs=[pl.BlockSpec((tb, PEER_SLOTS), lambda i: (i, 0), memory_space=pltpu.SMEM),
                  pl.BlockSpec((tb, PEER_SLOTS), lambda i: (i, 0)),
                  _table_spec()],
        out_specs=pl.BlockSpec((tb, 2 * HALF_ROWS, 128), lambda i: (i, 0, 0)),
        out_shape=jax.ShapeDtypeStruct((n, 2 * HALF_ROWS, 128), jnp.float32),
        scratch_shapes=[pltpu.VMEM((tb, PEER_SLOTS, 128), jnp.float32)],
        compiler_params=pltpu.CompilerParams(dimension_semantics=("arbitrary",), vmem_limit_bytes=PEER_VMEM_LIMIT),
        name="peer_v")(idx, coef, tab)
    return out.reshape(n, 2 * HALF_ROWS * 128)


PEER_TT = 256


def _extract_top(ref, n_out):
    rows, t = ref.shape
    rid = lax.broadcasted_iota(jnp.int32, (rows, t), 0)
    vals, idxs = [], []
    for _ in range(n_out):
        s = ref[...]
        m = jnp.max(s, axis=0, keepdims=True)
        ix = jnp.min(jnp.where(s == m, rid, rows), axis=0, keepdims=True)
        ref[...] = jnp.where(rid == ix, -jnp.inf, s)
        vals.append(m)
        idxs.append(ix)
    return vals, idxs


def _peer_route_body(x_ref, wq_ref, sk_ref, idx_ref, gate_ref, s_ref, cand_ref, v_ref, i_ref, et_ref, gt_ref):
    kk, nk = PEER_TOPK, PEER_N_KEYS
    f32 = jnp.float32
    q = jnp.dot(x_ref[...].astype(jnp.bfloat16), wq_ref[...], preferred_element_type=f32).astype(jnp.bfloat16)
    nt = (((1,), (1,)), ((), ()))
    row16 = lax.broadcasted_iota(jnp.int32, (kk, PEER_TT), 0)
    for h in range(PEER_HEADS):
        qh = q[:, h * PEER_KEY_DIM:(h + 1) * PEER_KEY_DIM]
        s_ref[...] = lax.dot_general(sk_ref[h], qh, nt, preferred_element_type=f32)
        for p in range(2):
            vals, idxs = _extract_top(s_ref.at[p * nk:(p + 1) * nk, :], kk)
            for i in range(kk):
                v_ref[p, i:i + 1, :] = vals[i]
                i_ref[p, i:i + 1, :] = idxs[i]
        v2 = v_ref[1]
        for i in range(kk):
            cand_ref[i * kk:(i + 1) * kk, :] = v_ref[0, i:i + 1, :] + v2
        vals, cis = _extract_top(cand_ref, kk)
        i1, i2 = i_ref[0], i_ref[1]
        es = [jnp.exp(v - vals[0]) for v in vals]
        den = es[0]
        for e in es[1:]:
            den = den + e
        for k in range(kk):
            ci = cis[k]
            e1 = jnp.sum(jnp.where(row16 == lax.shift_right_logical(ci, 4), i1, 0), axis=0, keepdims=True)
            e2 = jnp.sum(jnp.where(row16 == (ci & 15), i2, 0), axis=0, keepdims=True)
            et_ref[h * kk + k:h * kk + k + 1, :] = (e1 * nk + e2) * HALF_ROWS
            gt_ref[h * kk + k:h * kk + k + 1, :] = es[k] / den
    idx_ref[...] = lax.bitcast_convert_type(lax.bitcast_convert_type(et_ref[...], f32).T, jnp.int32)
    gate_ref[...] = gt_ref[...].T


def peer_route(xn, wq, subkeys):
    n = xn.shape[0]
    tt = PEER_TT
    half = PEER_KEY_DIM // 2
    z = jnp.zeros((PEER_HEADS, PEER_N_KEYS, half), subkeys.dtype)
    skbd = jnp.concatenate([jnp.concatenate([subkeys[:, 0], z], axis=-1),
                            jnp.concatenate([z, subkeys[:, 1]], axis=-1)], axis=1).astype(jnp.bfloat16)
    return pl.pallas_call(
        _peer_route_body, grid=(n // tt,),
        in_specs=[pl.BlockSpec((tt, D_MODEL), lambda i: (i, 0)),
                  pl.BlockSpec((D_MODEL, PEER_HEADS * PEER_KEY_DIM), lambda i: (0, 0)),
                  pl.BlockSpec((PEER_HEADS, 2 * PEER_N_KEYS, PEER_KEY_DIM), lambda i: (0, 0, 0))],
        out_specs=[pl.BlockSpec((tt, PEER_SLOTS), lambda i: (i, 0)), pl.BlockSpec((tt, PEER_SLOTS), lambda i: (i, 0))],
        out_shape=[jax.ShapeDtypeStruct((n, PEER_SLOTS), jnp.int32), jax.ShapeDtypeStruct((n, PEER_SLOTS), jnp.float32)],
        scratch_shapes=[pltpu.VMEM((2 * PEER_N_KEYS, tt), jnp.float32),
                        pltpu.VMEM((PEER_TOPK * PEER_TOPK, tt), jnp.float32),
                        pltpu.VMEM((2, PEER_TOPK, tt), jnp.float32),
                        pltpu.VMEM((2, PEER_TOPK, tt), jnp.int32),
                        pltpu.VMEM((PEER_SLOTS, tt), jnp.int32),
                        pltpu.VMEM((PEER_SLOTS, tt), jnp.float32)],
        compiler_params=pltpu.CompilerParams(dimension_semantics=("arbitrary",), vmem_limit_bytes=32 * 1024 * 1024),
        name="peer_route")(xn, wq.astype(jnp.bfloat16), skbd)


def peer_channel_mixer(xn, wq, subkeys, u_tab, v_tab):
    bsz, seq, d = xn.shape
    n_tok = bsz * seq
    x2 = xn.reshape(n_tok, d)
    idx, gates = peer_route(x2, wq, subkeys)
    a = peer_u(idx, x2, pack_table(u_tab))
    coef = jax.nn.gelu(a, approximate=False) * gates
    out = peer_v(idx, coef, pack_table(v_tab))
    return out.reshape(bsz, seq, d)


def _final_norm_body(h_ref, g_ref, o_ref):
    h = h_ref[...]
    o_ref[...] = h * lax.rsqrt(jnp.mean(h * h, axis=-1, keepdims=True) + EPS) * g_ref[...]


def final_norm(h, gain):
    bsz, seq, d = h.shape
    rows = bsz * seq
    tile = 1024
    out = pl.pallas_call(
        _final_norm_body,
        grid=(rows // tile,),
        in_specs=[pl.BlockSpec((tile, d), lambda i: (i, 0)), pl.BlockSpec((1, d), lambda i: (0, 0))],
        out_specs=pl.BlockSpec((tile, d), lambda i: (i, 0)),
        out_shape=jax.ShapeDtypeStruct((rows, d), h.dtype),
        name="final_norm",
    )(h.reshape(rows, d), gain.reshape(1, d))
    return out.reshape(bsz, seq, d)


def kernel(x, c, positions, w_ada, b_ada, w_in, conv_w, a_log, dt_bias, norm_b_w,
           w_pa, w_pb, w_o, peer_wq, peer_subkeys, peer_u, peer_v, final_norm_w):
    h = x
    c_act = jax.nn.silu(c)
    for layer in range(DEPTH):
        mod = c_act @ w_ada[layer] + b_ada[layer]
        sh1, sc1, gt1, sh2, sc2, gt2 = jnp.split(mod, 6, axis=-1)
        n1 = rms_norm(h) * (1.0 + sc1[:, None, :]) + sh1[:, None, :]
        y1 = token_mixers(n1, positions, w_in[layer], conv_w[layer], a_log[layer], dt_bias[layer],
                          norm_b_w[layer], w_pa[layer], w_pb[layer], w_o[layer])
        h = h + gt1[:, None, :] * y1
        n2 = rms_norm(h) * (1.0 + sc2[:, None, :]) + sh2[:, None, :]
        y2 = peer_channel_mixer(n2, peer_wq[layer], peer_subkeys[layer], peer_u[layer], peer_v[layer])
        h = h + gt2[:, None, :] * y2
    return final_norm(h, final_norm_w)
```

```python
import functools

import jax, jax.numpy as jnp
from jax import lax
import numpy as np
from jax.experimental import pallas as pl
from jax.experimental.pallas import tpu as pltpu

D_MODEL = 1024
BATCH = 8
SEQ = 4096
DEPTH = 1

A_HEADS = 8
A_KV_HEADS = 2
A_HEAD_DIM = 64
IDX_HEADS = 16
IDX_DIM = 64
IDX_TOPK_MAX = 256
Q_BLOCK = 128
B_HEADS = 8
B_KEY_DIM = 64
B_VAL_DIM = 64
CONV_WIDTH = 4
CHUNK = 64
ROPE_THETA = 500000.0
ROPE_FRACTION_DEN = 4
PEER_HEADS = 8
PEER_KEY_DIM = 128
PEER_N_KEYS = 128
PEER_N_EXPERTS = PEER_N_KEYS * PEER_N_KEYS
PEER_TOPK = 16
PEER_TOKEN_BLOCK = 128
EPS = 1e-6

A_WIDTH = A_HEADS * A_HEAD_DIM
KV_WIDTH = A_KV_HEADS * A_HEAD_DIM
B_QK_WIDTH = B_HEADS * B_KEY_DIM
B_V_WIDTH = B_HEADS * B_VAL_DIM
CONV_CHANNELS = 2 * B_QK_WIDTH + B_V_WIDTH
IN_SPLITS = (A_WIDTH, KV_WIDTH, KV_WIDTH, IDX_HEADS * IDX_DIM, IDX_DIM, IDX_HEADS,
             B_QK_WIDTH, B_QK_WIDTH, B_V_WIDTH, B_V_WIDTH, B_HEADS, B_HEADS, D_MODEL, D_MODEL)
IN_WIDTH = sum(IN_SPLITS)


def rms_norm(x, gain=None):
    xf = x.astype(jnp.float32)
    y = xf * lax.rsqrt(jnp.mean(xf * xf, axis=-1, keepdims=True) + EPS)
    if gain is not None:
        y = y * gain.astype(jnp.float32)
    return y.astype(x.dtype)


def l2_normalize(x):
    return x * lax.rsqrt(jnp.sum(x * x, axis=-1, keepdims=True) + EPS)


def partial_rotary(x, positions):
    hd = x.shape[-1]
    rd = hd // ROPE_FRACTION_DEN
    half = rd // 2
    inv_freq = jnp.power(jnp.float32(ROPE_THETA), -jnp.arange(half, dtype=jnp.float32) * (2.0 / rd))
    ang = positions.astype(jnp.float32)[..., None] * inv_freq
    cos = jnp.cos(ang)[:, :, None, :].astype(x.dtype)
    sin = jnp.sin(ang)[:, :, None, :].astype(x.dtype)
    x1 = x[..., :half]
    x2 = x[..., half:rd]
    return jnp.concatenate([x1 * cos - x2 * sin, x2 * cos + x1 * sin, x[..., rd:]], axis=-1)


def causal_short_conv(x, w):
    width = w.shape[0]
    seq = x.shape[1]
    xp = jnp.pad(x, ((0, 0), (width - 1, 0), (0, 0)))
    out = xp[:, 0:seq] * w[0]
    for i in range(1, width):
        out = out + xp[:, i:i + seq] * w[i]
    return out


DSA_TQ = 256
DSA_TK = 256
INT_MIN = -2**31
NEG_BIG = -1e30


def _dsa_body(topk, qit_ref, w_ref, qat_ref, ki_ref, ka_ref, vat_ref, o_ref,
              key_ref):
    tq, tk = DSA_TQ, DSA_TK
    qb = pl.program_id(1)
    n_kv = qb + 1
    t_glob = qb * tq + lax.broadcasted_iota(jnp.int32, (1, tq), 1)
    row = lax.broadcasted_iota(jnp.int32, (tk, 1), 0)
    f32 = jnp.float32

    def p1(j, carry):
        kt = ki_ref[0, j]
        score = jnp.zeros((tk, tq), f32)
        for h in range(IDX_HEADS):
            lt = jnp.dot(kt, qit_ref[0, h], preferred_element_type=f32)
            score = score + w_ref[0, h:h + 1, :] * jnp.maximum(lt, 0.0)
        bits = lax.bitcast_convert_type(score + 0.0, jnp.int32)
        skey = jnp.where(bits >= 0, bits, bits ^ jnp.int32(0x7FFFFFFF))
        skey = jnp.where(j * tk + row <= t_glob, skey, jnp.int32(INT_MIN))
        key_ref[j] = skey
        return carry
    lax.fori_loop(0, n_kv, p1, 0)

    def count(pred):
        def body(j, acc):
            hit = jnp.where(pred(key_ref[j], j * tk + row), 1.0, 0.0)
            return acc + jnp.sum(hit.reshape(tk // 8, 8, tq), axis=0)
        acc = lax.fori_loop(0, n_kv, body, jnp.zeros((8, tq), f32))
        return jnp.sum(acc, axis=0, keepdims=True)

    kf = jnp.float32(topk)

    def bit_step(i, ku):
        cand_u = ku | lax.shift_left(jnp.int32(1), 31 - i)
        cand = cand_u ^ jnp.int32(INT_MIN)
        c = count(lambda k, s: k >= cand)
        return jnp.where(c >= kf, cand_u, ku)
    ku = lax.fori_loop(0, 32, bit_step, jnp.zeros((1, tq), jnp.int32))
    kth = ku ^ jnp.int32(INT_MIN)
    c_gt = count(lambda k, s: k > kth)
    c_ge = count(lambda k, s: k >= kth)
    short = kth == jnp.int32(INT_MIN)
    x0 = jnp.where(short, jnp.int32(-1), jnp.int32(2**30))
    need = kf - c_gt
    has_tie = jnp.max(jnp.where(jnp.logical_and(c_ge > kf, jnp.logical_not(short)), 1.0, 0.0)) > 0.0

    def tie_search():
        def step(i, x):
            bit = lax.shift_left(jnp.int32(1), 11 - i)
            probe = x + bit - 1
            c = count(lambda k, s: jnp.logical_and(k == kth, s <= probe))
            return jnp.where(c < need, x + bit, x)
        x = lax.fori_loop(0, 12, step, jnp.zeros((1, tq), jnp.int32))
        return jnp.where(short, jnp.int32(-1), x)
    x_lim = lax.cond(has_tie, tie_search, lambda: x0)

    rep = A_HEADS // A_KV_HEADS
    hs = range(A_HEADS)

    def p3(j, carry):
        m, l, acc = carry
        skey = key_ref[j]
        s_idx = j * tk + row
        sel = jnp.logical_or(skey > kth, jnp.logical_and(skey == kth, s_idx <= x_lim))
        kt = [ka_ref[0, g, j] for g in range(A_KV_HEADS)]
        vt = [vat_ref[0, g, j] for g in range(A_KV_HEADS)]
        s = [jnp.where(sel, jnp.dot(kt[h // rep], qat_ref[0, h], preferred_element_type=f32), NEG_BIG) for h in hs]
        m_new = [jnp.maximum(m[h], jnp.max(s[h], axis=0, keepdims=True)) for h in hs]
        alpha = [jnp.exp(m[h] - m_new[h]) for h in hs]
        p = [jnp.exp(s[h] - m_new[h]) for h in hs]
        l_new = [alpha[h] * l[h] + jnp.sum(p[h], axis=0, keepdims=True) for h in hs]
        acc_new = [alpha[h] * acc[h] + jnp.dot(vt[h // rep], p[h].astype(jnp.bfloat16), preferred_element_type=f32)
                   for h in hs]
        return tuple(m_new), tuple(l_new), tuple(acc_new)

    init = (tuple(jnp.full((1, tq), NEG_BIG, f32) for _ in hs), tuple(jnp.zeros((1, tq), f32) for _ in hs),
            tuple(jnp.zeros((A_HEAD_DIM, tq), f32) for _ in hs))
    _, l_fin, acc_fin = lax.fori_loop(0, n_kv, p3, init)
    for h in hs:
        o_ref[0, h] = acc_fin[h] / l_fin[h]


def dsa_attention_pallas(qa, ka, va, qi, ki, wi):
    bsz, seq = qa.shape[0], qa.shape[1]
    tq, tk = DSA_TQ, DSA_TK
    topk = min(IDX_TOPK_MAX, seq // 4)
    bf = jnp.bfloat16
    idx_scale = (IDX_HEADS ** -0.5) * (IDX_DIM ** -0.5)
    qit = jnp.transpose(qi, (0, 2, 3, 1)).astype(bf)
    w = jnp.transpose(wi.astype(jnp.float32) * idx_scale, (0, 2, 1))
    qat = jnp.transpose(qa * (A_HEAD_DIM ** -0.5), (0, 2, 3, 1)).astype(bf)
    kit = ki.astype(bf).reshape(bsz, seq // tk, tk, IDX_DIM)
    kat = jnp.transpose(ka, (0, 2, 1, 3)).astype(bf).reshape(bsz, A_KV_HEADS, seq // tk, tk, A_HEAD_DIM)
    vat = jnp.transpose(va.astype(bf).reshape(bsz, seq // tk, tk, A_KV_HEADS, A_HEAD_DIM), (0, 3, 1, 4, 2))
    n_kv = seq // tk
    out = pl.pallas_call(
        functools.partial(_dsa_body, topk),
        grid=(bsz, seq // tq),
        in_specs=[
            pl.BlockSpec((1, IDX_HEADS, IDX_DIM, tq), lambda b, q: (b, 0, 0, q)),
            pl.BlockSpec((1, IDX_HEADS, tq), lambda b, q: (b, 0, q)),
            pl.BlockSpec((1, A_HEADS, A_HEAD_DIM, tq), lambda b, q: (b, 0, 0, q)),
            pl.BlockSpec((1, n_kv, tk, IDX_DIM), lambda b, q: (b, 0, 0, 0)),
            pl.BlockSpec((1, A_KV_HEADS, n_kv, tk, A_HEAD_DIM), lambda b, q: (b, 0, 0, 0, 0)),
            pl.BlockSpec((1, A_KV_HEADS, n_kv, A_HEAD_DIM, tk), lambda b, q: (b, 0, 0, 0, 0)),
        ],
        out_specs=pl.BlockSpec((1, A_HEADS, A_HEAD_DIM, tq), lambda b, q: (b, 0, 0, q)),
        out_shape=jax.ShapeDtypeStruct((bsz, A_HEADS, A_HEAD_DIM, seq), jnp.float32),
        scratch_shapes=[
            pltpu.VMEM((n_kv, tk, tq), jnp.int32),
        ],
        compiler_params=pltpu.CompilerParams(dimension_semantics=("arbitrary", "arbitrary"),
                                             vmem_limit_bytes=48 * 1024 * 1024),
        name="dsa_attention",
    )(qit, w, qat, kit, kat, vat)
    return jnp.transpose(out, (0, 3, 1, 2)).reshape(bsz, seq, A_HEADS * A_HEAD_DIM)


GDN_G = 8
GDN_AUX = 8


def _gdn_body(q_ref, k_ref, v_ref, aux_ref, gcr_ref, nw_ref, o_ref, s_ref):
    c_sz = CHUNK
    f32, bf = jnp.float32, jnp.bfloat16

    @pl.when(pl.program_id(1) == 0)
    def _():
        s_ref[...] = jnp.zeros(s_ref.shape, f32)

    ri = lax.broadcasted_iota(jnp.int32, (c_sz, c_sz), 0)
    ci = lax.broadcasted_iota(jnp.int32, (c_sz, c_sz), 1)
    incl, strict = ri >= ci, ri > ci
    eye = jnp.where(ri == ci, 1.0, 0.0).astype(f32)
    nt = (((1,), (1,)), ((), ()))
    dot = lambda a, b: jnp.dot(a.astype(bf), b.astype(bf), preferred_element_type=f32)
    dot_nt = lambda a, b: lax.dot_general(a.astype(bf), b.astype(bf), nt, preferred_element_type=f32)
    hp = lambda a, b: jnp.dot(a, b, precision=lax.Precision.HIGHEST, preferred_element_type=f32)

    def chunk(c, carry):
        r0 = pl.multiple_of(c * c_sz, c_sz)
        hs = range(B_HEADS)
        q = [q_ref[0, h, pl.ds(r0, c_sz), :] for h in hs]
        k = [k_ref[0, h, pl.ds(r0, c_sz), :] for h in hs]
        v = [v_ref[0, h, pl.ds(r0, c_sz), :] for h in hs]
        aux = [aux_ref[0, h, pl.ds(r0, c_sz), :] for h in hs]
        gcr = [gcr_ref[0, h, pl.ds(c, 1), :] for h in hs]
        s = [s_ref[h] for h in hs]
        gcc = [a[:, 0:1] for a in aux]
        beta = [a[:, 1:2] for a in aux]
        decay = [jnp.exp(jnp.where(incl, gcc[h] - gcr[h], -jnp.inf)) for h in hs]
        kb = [k[h] * beta[h] for h in hs]
        vb = [v[h] * beta[h] for h in hs]
        low = [jnp.where(strict, dot_nt(kb[h], k[h]) * decay[h], 0.0) for h in hs]
        t = [eye - low[h] for h in hs]
        p = [hp(low[h], low[h]) for h in hs]
        for _ in range(4):
            t = [hp(t[h], eye + p[h]) for h in hs]
            p = [hp(p[h], p[h]) for h in hs]
        t = [hp(t[h], eye + p[h]) for h in hs]
        u = [dot(t[h], vb[h]) for h in hs]
        kcd = [dot(t[h], kb[h] * jnp.exp(gcc[h])) for h in hs]
        intra = [dot_nt(q[h], k[h]) * decay[h] for h in hs]
        v_new = [u[h] - dot(kcd[h], s[h]) for h in hs]
        out = [dot(q[h] * jnp.exp(gcc[h]), s[h]) + dot(intra[h], v_new[h]) for h in hs]
        glast = [g[:, c_sz - 1:c_sz] for g in gcr]
        kd = [k[h] * jnp.exp(glast[h] - gcc[h]) for h in hs]
        s_new = [s[h] * jnp.exp(glast[h]) + dot(kd[h].T, v_new[h]) for h in hs]
        for h in hs:
            s_ref[h] = s_new[h]
            o = out[h]
            o = o * lax.rsqrt(jnp.mean(o * o, axis=-1, keepdims=True) + EPS) * nw_ref[...]
            o_ref[0, h, pl.ds(r0, c_sz), :] = o
        return carry
    lax.fori_loop(0, GDN_G, chunk, 0)


def gated_delta_rule_pallas(q, k, v, g, beta, norm_w):
    bsz, seq, nh, dk = q.shape
    dv = v.shape[-1]
    n = seq // CHUNK
    hm = lambda t: jnp.moveaxis(t, 2, 1)
    gc = jnp.cumsum(hm(g).reshape(bsz, nh, n, CHUNK), axis=-1)
    aux = jnp.stack([gc.reshape(bsz, nh, seq), hm(beta)] + [jnp.zeros((bsz, nh, seq), jnp.float32)] * (GDN_AUX - 2), axis=-1)
    rows = GDN_G * CHUNK
    blk = lambda w: pl.BlockSpec((1, nh, rows, w), lambda b, c: (b, 0, c, 0))
    out = pl.pallas_call(
        _gdn_body, grid=(bsz, n // GDN_G),
        in_specs=[blk(dk), blk(dk), blk(dv), blk(GDN_AUX),
                  pl.BlockSpec((1, nh, GDN_G, CHUNK), lambda b, c: (b, 0, c, 0)),
                  pl.BlockSpec((1, dv), lambda b, c: (0, 0))],
        out_specs=blk(dv),
        out_shape=jax.ShapeDtypeStruct((bsz, nh, seq, dv), jnp.float32),
        scratch_shapes=[pltpu.VMEM((nh, dk, dv), jnp.float32)],
        compiler_params=pltpu.CompilerParams(dimension_semantics=("arbitrary", "arbitrary"),
                                             vmem_limit_bytes=40 * 1024 * 1024),
        name="gated_delta_rule")(hm(q * (dk ** -0.5)), hm(k), hm(v), aux, gc,
                                 norm_w.astype(jnp.float32).reshape(1, dv))
    return jnp.moveaxis(out, 1, 2)


IN_NAMES = ("qa", "ka", "va", "qi", "ki", "wi", "qb", "kb", "vb", "zb", "bb", "ab", "gate_a", "gate_b")
PROJ_ORDER = ("gate_a", "gate_b", "qi", "qa", "qb", "kb", "vb", "zb", "ka", "va", "ki", "wi", "bb", "ab")
LANES = 128


def _proj_layout():
    src, off = {}, 0
    for name, w in zip(IN_NAMES, IN_SPLITS):
        src[name] = (off, w)
        off += w
    dst, off = {}, 0
    for name in PROJ_ORDER:
        dst[name] = (off, src[name][1])
        off += src[name][1]
    return src, dst, -(-off // LANES) * LANES


PROJ_SRC, PROJ_DST, PROJ_WIDTH = _proj_layout()
PROJ_TM = 256
PROJ_VMEM_LIMIT = 48 * 1024 * 1024


def permute_w_in(w_in):
    cols = [w_in[:, PROJ_SRC[n][0]:PROJ_SRC[n][0] + PROJ_SRC[n][1]] for n in PROJ_ORDER]
    cols.append(jnp.zeros((w_in.shape[0], PROJ_WIDTH - sum(c.shape[1] for c in cols)), w_in.dtype))
    return jnp.concatenate(cols, axis=1).astype(jnp.bfloat16)


def proj_piece(proj, name):
    off, w = PROJ_DST[name]
    return proj[..., off:off + w]


def _modulated_norm(x, scale, shift):
    return x * lax.rsqrt(jnp.mean(x * x, axis=-1, keepdims=True) + EPS) * scale + shift


def _sigmoid(v):
    return 1.0 / (1.0 + jnp.exp(-v))


def _ada_body(c_ref, w_ref, b_ref, o_ref):
    c = c_ref[...]
    o_ref[...] = jnp.dot((c * _sigmoid(c)).astype(jnp.bfloat16), w_ref[...].astype(jnp.bfloat16),
                         preferred_element_type=jnp.float32) + b_ref[...]


def ada_modulation(c, w_ada, b_ada):
    bsz, d = c.shape
    n = w_ada.shape[1]
    tn = D_MODEL
    return pl.pallas_call(
        _ada_body, grid=(n // tn,),
        in_specs=[pl.BlockSpec((bsz, d), lambda j: (0, 0)), pl.BlockSpec((d, tn), lambda j: (0, j)),
                  pl.BlockSpec((1, tn), lambda j: (0, j))],
        out_specs=pl.BlockSpec((bsz, tn), lambda j: (0, j)),
        out_shape=jax.ShapeDtypeStruct((bsz, n), jnp.float32),
        name="ada_modulation")(c, w_ada, b_ada.reshape(1, n))


def _in_proj_body(x_ref, sc_ref, sh_ref, w_ref, o_ref):
    n1 = _modulated_norm(x_ref[...], sc_ref[0], sh_ref[0])
    o_ref[...] = jnp.dot(n1.astype(jnp.bfloat16), w_ref[...], preferred_element_type=jnp.float32)


def in_proj(x2, scale, shift, w_perm, seq):
    n, d = x2.shape
    tm = PROJ_TM
    per = seq // tm
    vec = pl.BlockSpec((1, 1, d), lambda i: (i // per, 0, 0))
    return pl.pallas_call(
        _in_proj_body, grid=(n // tm,),
        in_specs=[pl.BlockSpec((tm, d), lambda i: (i, 0)), vec, vec,
                  pl.BlockSpec((d, PROJ_WIDTH), lambda i: (0, 0), pipeline_mode=pl.Buffered(1))],
        out_specs=pl.BlockSpec((tm, PROJ_WIDTH), lambda i: (i, 0)),
        out_shape=jax.ShapeDtypeStruct((n, PROJ_WIDTH), jnp.float32),
        compiler_params=pltpu.CompilerParams(dimension_semantics=("arbitrary",), vmem_limit_bytes=PROJ_VMEM_LIMIT),
        name="in_proj")(x2, scale, shift, w_perm)


def _merge_body(oa_ref, ob_ref, z_ref, ga_ref, gb_ref, x_ref, gt_ref, sc_ref, sh_ref, wpa_ref, wpb_ref, wo_ref,
                h_ref, n2_ref):
    f32, bf = jnp.float32, jnp.bfloat16
    z = z_ref[...]
    ob = ob_ref[...] * (z * _sigmoid(z))
    ya = jnp.dot(oa_ref[...].astype(bf), wpa_ref[...], preferred_element_type=f32)
    yb = jnp.dot(ob.astype(bf), wpb_ref[...], preferred_element_type=f32)
    merged = _sigmoid(ga_ref[...]) * ya + _sigmoid(gb_ref[...]) * yb
    y1 = jnp.dot(merged.astype(bf), wo_ref[...], preferred_element_type=f32)
    h = x_ref[...] + gt_ref[0] * y1
    h_ref[...] = h
    n2_ref[...] = _modulated_norm(h, sc_ref[0], sh_ref[0])


def merge(o_a, o_b, proj, x2, gt1, scale2, shift2, w_pa, w_pb, w_o, seq):
    n, d = x2.shape
    tm = PROJ_TM
    per = seq // tm
    bf = jnp.bfloat16
    vec = pl.BlockSpec((1, 1, d), lambda i: (i // per, 0, 0))

    def col(name):
        off, w = PROJ_DST[name]
        return pl.BlockSpec((tm, w), lambda i: (i, off // w))
    row = lambda w: pl.BlockSpec((tm, w), lambda i: (i, 0))
    res = lambda a, b: pl.BlockSpec((a, b), lambda i: (0, 0))
    return pl.pallas_call(
        _merge_body, grid=(n // tm,),
        in_specs=[row(A_WIDTH), row(B_V_WIDTH), col("zb"), col("gate_a"), col("gate_b"), row(d), vec, vec, vec,
                  res(A_WIDTH, d), res(B_V_WIDTH, d), res(d, d)],
        out_specs=[row(d), row(d)],
        out_shape=[jax.ShapeDtypeStruct((n, d), jnp.float32), jax.ShapeDtypeStruct((n, d), jnp.float32)],
        compiler_params=pltpu.CompilerParams(dimension_semantics=("arbitrary",), vmem_limit_bytes=PROJ_VMEM_LIMIT),
        name="merge")(o_a, o_b, proj, proj, proj, x2, gt1, scale2, shift2,
                      w_pa.astype(bf), w_pb.astype(bf), w_o.astype(bf))


def token_mixers(proj, positions, conv_w, a_log, dt_bias, norm_b_w):
    bsz, seq, _ = proj.shape
    (qa, ka, va, qi, ki, wi, qb, kb, vb, bb, ab) = [
        proj_piece(proj, n) for n in ("qa", "ka", "va", "qi", "ki", "wi", "qb", "kb", "vb", "bb", "ab")]

    qa = partial_rotary(qa.reshape(bsz, seq, A_HEADS, A_HEAD_DIM), positions)
    ka = partial_rotary(ka.reshape(bsz, seq, A_KV_HEADS, A_HEAD_DIM), positions)
    va = va.reshape(bsz, seq, A_KV_HEADS, A_HEAD_DIM)
    qi = partial_rotary(qi.reshape(bsz, seq, IDX_HEADS, IDX_DIM), positions)
    ki = partial_rotary(ki.reshape(bsz, seq, 1, IDX_DIM), positions)[:, :, 0]
    o_a = dsa_attention_pallas(qa, ka, va, qi, ki, wi)

    qkv = jax.nn.silu(causal_short_conv(jnp.concatenate([qb, kb, vb], axis=-1), conv_w))
    qb, kb, vb = jnp.split(qkv, [B_QK_WIDTH, 2 * B_QK_WIDTH], axis=-1)
    qb = l2_normalize(qb.reshape(bsz, seq, B_HEADS, B_KEY_DIM).astype(jnp.float32))
    kb = l2_normalize(kb.reshape(bsz, seq, B_HEADS, B_KEY_DIM).astype(jnp.float32))
    vb = vb.reshape(bsz, seq, B_HEADS, B_VAL_DIM).astype(jnp.float32)
    beta = jax.nn.sigmoid(bb.astype(jnp.float32))
    g = -jnp.exp(a_log.astype(jnp.float32)) * jax.nn.softplus(ab.astype(jnp.float32) + dt_bias.astype(jnp.float32))
    o_b = gated_delta_rule_pallas(qb, kb, vb, g, beta, norm_b_w)
    return o_a, o_b.reshape(bsz, seq, B_V_WIDTH)


PEER_SLOTS = PEER_HEADS * PEER_TOPK
PEER_TB = 32
HALF_ROWS = 4
HI_MASK = -65536


def pack_table(tab):
    bits = lax.bitcast_convert_type(tab.astype(jnp.bfloat16), jnp.uint16).astype(jnp.uint32)
    half = tab.shape[1] // 2
    word = bits[:, :half] | (bits[:, half:] << 16)
    return lax.bitcast_convert_type(word, jnp.int32).reshape(tab.shape[0] * HALF_ROWS, 128)


def _table_row(tab_ref, rows, k):
    return _unpack(tab_ref[pl.ds(pl.multiple_of(rows[k], HALF_ROWS), HALF_ROWS), :])


def _unpack(w):
    lo = lax.bitcast_convert_type(lax.shift_left(w, 16), jnp.float32)
    hi = lax.bitcast_convert_type(w & jnp.int32(HI_MASK), jnp.float32)
    return lo, hi


def _peer_u_body(idx_ref, x_ref, tab_ref, o_ref, s_ref, sb_ref):
    def tok(t, carry):
        xlo = x_ref[t, 0:HALF_ROWS, :]
        xhi = x_ref[t, HALF_ROWS:2 * HALF_ROWS, :]
        rows = idx_ref.at[t]
        for k in range(PEER_SLOTS):
            lo, hi = _table_row(tab_ref, rows, k)
            s_ref[HALF_ROWS * k:HALF_ROWS * (k + 1), :] = lo * xlo + hi * xhi
        s4 = s_ref[pl.ds(0, PEER_SLOTS, stride=HALF_ROWS), :]
        for r in range(1, HALF_ROWS):
            s4 = s4 + s_ref[pl.ds(r, PEER_SLOTS, stride=HALF_ROWS), :]
        sb_ref[t] = s4
        return carry
    lax.fori_loop(0, PEER_TB, tok, 0)
    for t in range(PEER_TB):
        o_ref[t:t + 1, :] = jnp.sum(sb_ref[t].T, axis=0, keepdims=True)


def _peer_v_body(idx_ref, coef_ref, tab_ref, o_ref, cb_ref):
    for t in range(PEER_TB):
        cb_ref[t] = jnp.broadcast_to(coef_ref[t:t + 1, :], (PEER_SLOTS, 128)).T

    def tok(t, carry):
        nacc = 2
        acc = [jnp.zeros((HALF_ROWS, 128), jnp.float32) for _ in range(2 * nacc)]
        rows = idx_ref.at[t]
        for k in range(PEER_SLOTS):
            lo, hi = _table_row(tab_ref, rows, k)
            c = jnp.broadcast_to(cb_ref[t, k:k + 1, :], (HALF_ROWS, 128))
            a = k % nacc
            acc[2 * a] = acc[2 * a] + c * lo
            acc[2 * a + 1] = acc[2 * a + 1] + c * hi
        o_ref[t, 0:HALF_ROWS, :] = acc[0] + acc[2]
        o_ref[t, HALF_ROWS:2 * HALF_ROWS, :] = acc[1] + acc[3]
        return carry
    lax.fori_loop(0, PEER_TB, tok, 0)


def _table_spec():
    return pl.BlockSpec((PEER_N_EXPERTS * HALF_ROWS, 128), lambda i: (0, 0), pipeline_mode=pl.Buffered(1))


PEER_VMEM_LIMIT = 48 * 1024 * 1024


def peer_u(idx, x, tab):
    n = idx.shape[0]
    tb = PEER_TB
    return pl.pallas_call(
        _peer_u_body, grid=(n // tb,),
        in_specs=[pl.BlockSpec((tb, PEER_SLOTS), lambda i: (i, 0), memory_space=pltpu.SMEM),
                  pl.BlockSpec((tb, 2 * HALF_ROWS, 128), lambda i: (i, 0, 0)),
                  _table_spec()],
        out_specs=pl.BlockSpec((tb, PEER_SLOTS), lambda i: (i, 0)),
        out_shape=jax.ShapeDtypeStruct((n, PEER_SLOTS), jnp.float32),
        scratch_shapes=[pltpu.VMEM((HALF_ROWS * PEER_SLOTS, 128), jnp.float32),
                        pltpu.VMEM((tb, PEER_SLOTS, 128), jnp.float32)],
        compiler_params=pltpu.CompilerParams(dimension_semantics=("arbitrary",), vmem_limit_bytes=PEER_VMEM_LIMIT),
        name="peer_u")(idx, x.reshape(n, 2 * HALF_ROWS, 128), tab)


def peer_v(idx, coef, tab):
    n = idx.shape[0]
    tb = PEER_TB
    out = pl.pallas_call(
        _peer_v_body, grid=(n // tb,),
        in_specs=[pl.BlockSpec((tb, PEER_SLOTS), lambda i: (i, 0), memory_space=pltpu.SMEM),
                  pl.BlockSpec((tb, PEER_SLOTS), lambda i: (i, 0)),
                  _table_spec()],
        out_specs=pl.BlockSpec((tb, 2 * HALF_ROWS, 128), lambda i: (i, 0, 0)),
        out_shape=jax.ShapeDtypeStruct((n, 2 * HALF_ROWS, 128), jnp.float32),
        scratch_shapes=[pltpu.VMEM((tb, PEER_SLOTS, 128), jnp.float32)],
        compiler_params=pltpu.CompilerParams(dimension_semantics=("arbitrary",), vmem_limit_bytes=PEER_VMEM_LIMIT),
        name="peer_v")(idx, coef, tab)
    return out.reshape(n, 2 * HALF_ROWS * 128)


PEER_TT = 256


def _extract_top(ref, n_out):
    rows, t = ref.shape
    rid = lax.broadcasted_iota(jnp.int32, (rows, t), 0)
    vals, idxs = [], []
    for _ in range(n_out):
        s = ref[...]
        m = jnp.max(s, axis=0, keepdims=True)
        ix = jnp.min(jnp.where(s == m, rid, rows), axis=0, keepdims=True)
        ref[...] = jnp.where(rid == ix, -jnp.inf, s)
        vals.append(m)
        idxs.append(ix)
    return vals, idxs


def _peer_route_body(x_ref, wq_ref, sk_ref, idx_ref, gate_ref, s_ref, cand_ref, v_ref, i_ref, et_ref, gt_ref):
    kk, nk = PEER_TOPK, PEER_N_KEYS
    f32 = jnp.float32
    q = jnp.dot(x_ref[...].astype(jnp.bfloat16), wq_ref[...], preferred_element_type=f32).astype(jnp.bfloat16)
    nt = (((1,), (1,)), ((), ()))
    row16 = lax.broadcasted_iota(jnp.int32, (kk, PEER_TT), 0)
    for h in range(PEER_HEADS):
        qh = q[:, h * PEER_KEY_DIM:(h + 1) * PEER_KEY_DIM]
        s_ref[...] = lax.dot_general(sk_ref[h], qh, nt, preferred_element_type=f32)
        for p in range(2):
            vals, idxs = _extract_top(s_ref.at[p * nk:(p + 1) * nk, :], kk)
            for i in range(kk):
                v_ref[p, i:i + 1, :] = vals[i]
                i_ref[p, i:i + 1, :] = idxs[i]
        v2 = v_ref[1]
        for i in range(kk):
            cand_ref[i * kk:(i + 1) * kk, :] = v_ref[0, i:i + 1, :] + v2
        vals, cis = _extract_top(cand_ref, kk)
        i1, i2 = i_ref[0], i_ref[1]
        es = [jnp.exp(v - vals[0]) for v in vals]
        den = es[0]
        for e in es[1:]:
            den = den + e
        for k in range(kk):
            ci = cis[k]
            e1 = jnp.sum(jnp.where(row16 == lax.shift_right_logical(ci, 4), i1, 0), axis=0, keepdims=True)
            e2 = jnp.sum(jnp.where(row16 == (ci & 15), i2, 0), axis=0, keepdims=True)
            et_ref[h * kk + k:h * kk + k + 1, :] = (e1 * nk + e2) * HALF_ROWS
            gt_ref[h * kk + k:h * kk + k + 1, :] = es[k] / den
    idx_ref[...] = lax.bitcast_convert_type(lax.bitcast_convert_type(et_ref[...], f32).T, jnp.int32)
    gate_ref[...] = gt_ref[...].T


def peer_route(xn, wq, subkeys):
    n = xn.shape[0]
    tt = PEER_TT
    half = PEER_KEY_DIM // 2
    z = jnp.zeros((PEER_HEADS, PEER_N_KEYS, half), subkeys.dtype)
    skbd = jnp.concatenate([jnp.concatenate([subkeys[:, 0], z], axis=-1),
                            jnp.concatenate([z, subkeys[:, 1]], axis=-1)], axis=1).astype(jnp.bfloat16)
    return pl.pallas_call(
        _peer_route_body, grid=(n // tt,),
        in_specs=[pl.BlockSpec((tt, D_MODEL), lambda i: (i, 0)),
                  pl.BlockSpec((D_MODEL, PEER_HEADS * PEER_KEY_DIM), lambda i: (0, 0)),
                  pl.BlockSpec((PEER_HEADS, 2 * PEER_N_KEYS, PEER_KEY_DIM), lambda i: (0, 0, 0))],
        out_specs=[pl.BlockSpec((tt, PEER_SLOTS), lambda i: (i, 0)), pl.BlockSpec((tt, PEER_SLOTS), lambda i: (i, 0))],
        out_shape=[jax.ShapeDtypeStruct((n, PEER_SLOTS), jnp.int32), jax.ShapeDtypeStruct((n, PEER_SLOTS), jnp.float32)],
        scratch_shapes=[pltpu.VMEM((2 * PEER_N_KEYS, tt), jnp.float32),
                        pltpu.VMEM((PEER_TOPK * PEER_TOPK, tt), jnp.float32),
                        pltpu.VMEM((2, PEER_TOPK, tt), jnp.float32),
                        pltpu.VMEM((2, PEER_TOPK, tt), jnp.int32),
                        pltpu.VMEM((PEER_SLOTS, tt), jnp.int32),
                        pltpu.VMEM((PEER_SLOTS, tt), jnp.float32)],
        compiler_params=pltpu.CompilerParams(dimension_semantics=("arbitrary",), vmem_limit_bytes=32 * 1024 * 1024),
        name="peer_route")(xn, wq.astype(jnp.bfloat16), skbd)


def peer_channel_mixer(xn, wq, subkeys, u_tab, v_tab):
    bsz, seq, d = xn.shape
    n_tok = bsz * seq
    x2 = xn.reshape(n_tok, d)
    idx, gates = peer_route(x2, wq, subkeys)
    a = peer_u(idx, x2, pack_table(u_tab))
    coef = jax.nn.gelu(a, approximate=False) * gates
    out = peer_v(idx, coef, pack_table(v_tab))
    return out.reshape(bsz, seq, d)


def _residual_body(h_ref, y_ref, gt_ref, o_ref):
    o_ref[...] = h_ref[...] + gt_ref[0] * y_ref[...]


def _final_norm_body(h_ref, y_ref, gt_ref, g_ref, o_ref):
    h = h_ref[...] + gt_ref[0] * y_ref[...]
    o_ref[...] = h * lax.rsqrt(jnp.mean(h * h, axis=-1, keepdims=True) + EPS) * g_ref[...]


def gated_residual(h2, y2, gt, seq, gain=None):
    n, d = h2.shape
    tm = min(1024, seq)
    per = seq // tm
    row = pl.BlockSpec((tm, d), lambda i: (i, 0))
    specs = [row, row, pl.BlockSpec((1, 1, d), lambda i: (i // per, 0, 0))]
    args = [h2, y2, gt]
    if gain is not None:
        specs.append(pl.BlockSpec((1, d), lambda i: (0, 0)))
        args.append(gain.astype(jnp.float32).reshape(1, d))
    return pl.pallas_call(
        _residual_body if gain is None else _final_norm_body, grid=(n // tm,),
        in_specs=specs, out_specs=row, out_shape=jax.ShapeDtypeStruct((n, d), h2.dtype),
        name="gated_residual" if gain is None else "final_norm")(*args)


def kernel(x, c, positions, w_ada, b_ada, w_in, conv_w, a_log, dt_bias, norm_b_w,
           w_pa, w_pb, w_o, peer_wq, peer_subkeys, peer_u, peer_v, final_norm_w):
    bsz, seq, d = x.shape
    h = x.reshape(bsz * seq, d)
    for layer in range(DEPTH):
        mod = ada_modulation(c, w_ada[layer], b_ada[layer])
        sh1, sc1, gt1, sh2, sc2, gt2 = [m.reshape(bsz, 1, d) for m in jnp.split(mod, 6, axis=-1)]
        proj = in_proj(h, 1.0 + sc1, sh1, permute_w_in(w_in[layer]), seq)
        o_a, o_b = token_mixers(proj.reshape(bsz, seq, PROJ_WIDTH), positions, conv_w[layer], a_log[layer],
                                dt_bias[layer], norm_b_w[layer])
        h, n2 = merge(o_a.reshape(bsz * seq, A_WIDTH), o_b.reshape(bsz * seq, B_V_WIDTH), proj, h, gt1,
                      1.0 + sc2, sh2, w_pa[layer], w_pb[layer], w_o[layer], seq)
        y2 = peer_channel_mixer(n2.reshape(bsz, seq, d), peer_wq[layer], peer_subkeys[layer], peer_u[layer],
                                peer_v[layer])
        last = layer == DEPTH - 1
        h = gated_residual(h, y2.reshape(bsz * seq, d), gt2, seq, final_norm_w if last else None)
    return h.reshape(bsz, seq, d)
```

```python
import functools

import jax, jax.numpy as jnp
from jax import lax
from jax.experimental import pallas as pl
from jax.experimental.pallas import tpu as pltpu

D_MODEL = 1024
DEPTH = 1

A_HEADS = 8
A_KV_HEADS = 2
A_HEAD_DIM = 64
IDX_HEADS = 16
IDX_DIM = 64
IDX_TOPK_MAX = 256
B_HEADS = 8
B_KEY_DIM = 64
B_VAL_DIM = 64
CONV_WIDTH = 4
CHUNK = 64
ROPE_THETA = 500000.0
ROPE_FRACTION_DEN = 4
PEER_HEADS = 8
PEER_KEY_DIM = 128
PEER_N_KEYS = 128
PEER_N_EXPERTS = PEER_N_KEYS * PEER_N_KEYS
PEER_TOPK = 16
EPS = 1e-6

A_WIDTH = A_HEADS * A_HEAD_DIM
KV_WIDTH = A_KV_HEADS * A_HEAD_DIM
B_QK_WIDTH = B_HEADS * B_KEY_DIM
B_V_WIDTH = B_HEADS * B_VAL_DIM
CONV_CHANNELS = 2 * B_QK_WIDTH + B_V_WIDTH
IN_SPLITS = (A_WIDTH, KV_WIDTH, KV_WIDTH, IDX_HEADS * IDX_DIM, IDX_DIM, IDX_HEADS,
             B_QK_WIDTH, B_QK_WIDTH, B_V_WIDTH, B_V_WIDTH, B_HEADS, B_HEADS, D_MODEL, D_MODEL)


def causal_short_conv(x, w):
    width = w.shape[0]
    seq = x.shape[1]
    xp = jnp.pad(x, ((0, 0), (width - 1, 0), (0, 0)))
    out = xp[:, 0:seq] * w[0]
    for i in range(1, width):
        out = out + xp[:, i:i + seq] * w[i]
    return out


DSA_TQ = 256
DSA_TK = 256
INT_MIN = -2**31
NEG_BIG = -1e30


def _dsa_body(topk, qit_ref, w_ref, qat_ref, ki_ref, ka_ref, vat_ref, o_ref,
              key_ref):
    tq, tk = DSA_TQ, DSA_TK
    qb = pl.program_id(1)
    n_kv = qb + 1
    t_glob = qb * tq + lax.broadcasted_iota(jnp.int32, (1, tq), 1)
    row = lax.broadcasted_iota(jnp.int32, (tk, 1), 0)
    f32 = jnp.float32

    def p1(j, carry):
        kt = ki_ref[0, j]
        score = jnp.zeros((tk, tq), f32)
        for h in range(IDX_HEADS):
            lt = jnp.dot(kt, qit_ref[0, h], preferred_element_type=f32)
            score = score + w_ref[0, h:h + 1, :] * jnp.maximum(lt, 0.0)
        bits = lax.bitcast_convert_type(score + 0.0, jnp.int32)
        skey = jnp.where(bits >= 0, bits, bits ^ jnp.int32(0x7FFFFFFF))
        skey = jnp.where(j * tk + row <= t_glob, skey, jnp.int32(INT_MIN))
        key_ref[j] = skey
        return carry
    lax.fori_loop(0, n_kv, p1, 0)

    def count(pred):
        def body(j, acc):
            hit = jnp.where(pred(key_ref[j], j * tk + row), 1.0, 0.0)
            return acc + jnp.sum(hit.reshape(tk // 8, 8, tq), axis=0)
        acc = lax.fori_loop(0, n_kv, body, jnp.zeros((8, tq), f32))
        return jnp.sum(acc, axis=0, keepdims=True)

    kf = jnp.float32(topk)

    def bit_step(i, ku):
        cand_u = ku | lax.shift_left(jnp.int32(1), 31 - i)
        cand = cand_u ^ jnp.int32(INT_MIN)
        c = count(lambda k, s: k >= cand)
        return jnp.where(c >= kf, cand_u, ku)
    ku = lax.fori_loop(0, 32, bit_step, jnp.zeros((1, tq), jnp.int32))
    kth = ku ^ jnp.int32(INT_MIN)
    c_gt = count(lambda k, s: k > kth)
    c_ge = count(lambda k, s: k >= kth)
    short = kth == jnp.int32(INT_MIN)
    x0 = jnp.where(short, jnp.int32(-1), jnp.int32(2**30))
    need = kf - c_gt
    has_tie = jnp.max(jnp.where(jnp.logical_and(c_ge > kf, jnp.logical_not(short)), 1.0, 0.0)) > 0.0

    def tie_search():
        def step(i, x):
            bit = lax.shift_left(jnp.int32(1), 11 - i)
            probe = x + bit - 1
            c = count(lambda k, s: jnp.logical_and(k == kth, s <= probe))
            return jnp.where(c < need, x + bit, x)
        x = lax.fori_loop(0, 12, step, jnp.zeros((1, tq), jnp.int32))
        return jnp.where(short, jnp.int32(-1), x)
    x_lim = lax.cond(has_tie, tie_search, lambda: x0)

    rep = A_HEADS // A_KV_HEADS
    hs = range(A_HEADS)

    def p3(j, carry):
        m, l, acc = carry
        skey = key_ref[j]
        s_idx = j * tk + row
        sel = jnp.logical_or(skey > kth, jnp.logical_and(skey == kth, s_idx <= x_lim))
        kt = [ka_ref[0, g, j] for g in range(A_KV_HEADS)]
        vt = [vat_ref[0, g, j] for g in range(A_KV_HEADS)]
        s = [jnp.where(sel, jnp.dot(kt[h // rep], qat_ref[0, h], preferred_element_type=f32), NEG_BIG) for h in hs]
        m_new = [jnp.maximum(m[h], jnp.max(s[h], axis=0, keepdims=True)) for h in hs]
        alpha = [jnp.exp(m[h] - m_new[h]) for h in hs]
        p = [jnp.exp(s[h] - m_new[h]) for h in hs]
        l_new = [alpha[h] * l[h] + jnp.sum(p[h], axis=0, keepdims=True) for h in hs]
        acc_new = [alpha[h] * acc[h] + jnp.dot(vt[h // rep], p[h].astype(jnp.bfloat16), preferred_element_type=f32)
                   for h in hs]
        return tuple(m_new), tuple(l_new), tuple(acc_new)

    init = (tuple(jnp.full((1, tq), NEG_BIG, f32) for _ in hs), tuple(jnp.zeros((1, tq), f32) for _ in hs),
            tuple(jnp.zeros((A_HEAD_DIM, tq), f32) for _ in hs))
    _, l_fin, acc_fin = lax.fori_loop(0, n_kv, p3, init)
    for h in hs:
        o_ref[0, h] = acc_fin[h] / l_fin[h]


ROPE_HALF = A_HEAD_DIM // ROPE_FRACTION_DEN // 2


def _rope_t(xt, n_heads, cos, sin, scale=None):
    outs = []
    for h in range(n_heads):
        b = h * A_HEAD_DIM
        x1, x2, rest = xt[b:b + ROPE_HALF], xt[b + ROPE_HALF:b + 2 * ROPE_HALF], xt[b + 2 * ROPE_HALF:b + A_HEAD_DIM]
        o = jnp.concatenate([x1 * cos - x2 * sin, x2 * cos + x1 * sin, rest], axis=0)
        outs.append(o if scale is None else o * scale)
    return outs


def _dsa_prep_body(qi_ref, qa_ref, ka_ref, va_ref, sm_ref, cos_ref, sin_ref,
                   qit_ref, qat_ref, w_ref, kit_ref, kat_ref, vat_ref, bg_ref):
    bf = jnp.bfloat16
    cos, sin = cos_ref[0], sin_ref[0]
    for h, o in enumerate(_rope_t(qi_ref[...].T, IDX_HEADS, cos, sin)):
        qit_ref[0, h] = o.astype(bf)
    for h, o in enumerate(_rope_t(qa_ref[...].T, A_HEADS, cos, sin, A_HEAD_DIM ** -0.5)):
        qat_ref[0, h] = o.astype(bf)
    kat = _rope_t(ka_ref[...].T, A_KV_HEADS, cos, sin)
    vt = va_ref[...].T
    for g in range(A_KV_HEADS):
        kat_ref[0, g, 0] = kat[g].T.astype(bf)
        vat_ref[0, g, 0] = vt[g * A_HEAD_DIM:(g + 1) * A_HEAD_DIM].astype(bf)
    smt = sm_ref[...].T
    kit_ref[0, 0] = _rope_t(smt[0:IDX_DIM], 1, cos, sin)[0].T.astype(bf)
    w_ref[0] = smt[IDX_DIM:IDX_DIM + IDX_HEADS] * ((IDX_HEADS ** -0.5) * (IDX_DIM ** -0.5))
    bg_ref[0] = smt[IDX_DIM + IDX_HEADS:IDX_DIM + IDX_HEADS + 2 * B_HEADS]


def dsa_prep(proj, positions, bsz, seq):
    t = DSA_TQ
    per = seq // t
    assert DSA_TK == t and PROJ_ORDER[-4:] == ("ki", "wi", "bb", "ab") and PROJ_DST["ki"][0] + LANES == PROJ_WIDTH
    rd = A_HEAD_DIM // ROPE_FRACTION_DEN
    inv_freq = jnp.power(jnp.float32(ROPE_THETA), -jnp.arange(ROPE_HALF, dtype=jnp.float32) * (2.0 / rd))
    ang = positions.astype(jnp.float32)[:, None, :] * inv_freq[None, :, None]

    def col(name, width=None):
        off, w = PROJ_DST[name]
        w = width or w
        return pl.BlockSpec((t, w), lambda i: (i, off // w))
    trig = pl.BlockSpec((1, ROPE_HALF, t), lambda i: (i // per, 0, i % per))
    bf, f32 = jnp.bfloat16, jnp.float32
    sd = jax.ShapeDtypeStruct
    n_kv = seq // DSA_TK
    return pl.pallas_call(
        _dsa_prep_body, grid=(bsz * per,),
        in_specs=[col("qi"), col("qa"), col("ka"), col("va"), col("ki", LANES), trig, trig],
        out_specs=[pl.BlockSpec((1, IDX_HEADS, IDX_DIM, t), lambda i: (i // per, 0, 0, i % per)),
                   pl.BlockSpec((1, A_HEADS, A_HEAD_DIM, t), lambda i: (i // per, 0, 0, i % per)),
                   pl.BlockSpec((1, IDX_HEADS, t), lambda i: (i // per, 0, i % per)),
                   pl.BlockSpec((1, 1, t, IDX_DIM), lambda i: (i // per, i % per, 0, 0)),
                   pl.BlockSpec((1, A_KV_HEADS, 1, t, A_HEAD_DIM), lambda i: (i // per, 0, i % per, 0, 0)),
                   pl.BlockSpec((1, A_KV_HEADS, 1, A_HEAD_DIM, t), lambda i: (i // per, 0, i % per, 0, 0)),
                   pl.BlockSpec((1, 2 * B_HEADS, t), lambda i: (i // per, 0, i % per))],
        out_shape=[sd((bsz, IDX_HEADS, IDX_DIM, seq), bf), sd((bsz, A_HEADS, A_HEAD_DIM, seq), bf),
                   sd((bsz, IDX_HEADS, seq), f32), sd((bsz, n_kv, DSA_TK, IDX_DIM), bf),
                   sd((bsz, A_KV_HEADS, n_kv, DSA_TK, A_HEAD_DIM), bf),
                   sd((bsz, A_KV_HEADS, n_kv, A_HEAD_DIM, DSA_TK), bf), sd((bsz, 2 * B_HEADS, seq), f32)],
        compiler_params=pltpu.CompilerParams(dimension_semantics=("arbitrary",), vmem_limit_bytes=32 * 1024 * 1024),
        name="dsa_prep")(proj, proj, proj, proj, proj, jnp.cos(ang), jnp.sin(ang))


def dsa_attention_pallas(qit, w, qat, kit, kat, vat):
    bsz, seq = qit.shape[0], qit.shape[-1]
    tq, tk = DSA_TQ, DSA_TK
    topk = min(IDX_TOPK_MAX, seq // 4)
    n_kv = seq // tk
    return pl.pallas_call(
        functools.partial(_dsa_body, topk),
        grid=(bsz, seq // tq),
        in_specs=[
            pl.BlockSpec((1, IDX_HEADS, IDX_DIM, tq), lambda b, q: (b, 0, 0, q)),
            pl.BlockSpec((1, IDX_HEADS, tq), lambda b, q: (b, 0, q)),
            pl.BlockSpec((1, A_HEADS, A_HEAD_DIM, tq), lambda b, q: (b, 0, 0, q)),
            pl.BlockSpec((1, n_kv, tk, IDX_DIM), lambda b, q: (b, 0, 0, 0)),
            pl.BlockSpec((1, A_KV_HEADS, n_kv, tk, A_HEAD_DIM), lambda b, q: (b, 0, 0, 0, 0)),
            pl.BlockSpec((1, A_KV_HEADS, n_kv, A_HEAD_DIM, tk), lambda b, q: (b, 0, 0, 0, 0)),
        ],
        out_specs=pl.BlockSpec((1, A_HEADS, A_HEAD_DIM, tq), lambda b, q: (b, 0, 0, q)),
        out_shape=jax.ShapeDtypeStruct((bsz, A_HEADS, A_HEAD_DIM, seq), jnp.float32),
        scratch_shapes=[
            pltpu.VMEM((n_kv, tk, tq), jnp.int32),
        ],
        compiler_params=pltpu.CompilerParams(dimension_semantics=("arbitrary", "arbitrary"),
                                             vmem_limit_bytes=48 * 1024 * 1024),
        name="dsa_attention",
    )(qit, w, qat, kit, kat, vat)


GDN_G = 8
GDN_AUX = 8
GDN_BASE = 8


def _gdn_body(q_ref, k_ref, v_ref, aux_ref, gcr_ref, nw_ref, o_ref, s_ref):
    c_sz = CHUNK
    f32, bf = jnp.float32, jnp.bfloat16

    @pl.when(pl.program_id(1) == 0)
    def _():
        s_ref[...] = jnp.zeros(s_ref.shape, f32)

    ri = lax.broadcasted_iota(jnp.int32, (c_sz, c_sz), 0)
    ci = lax.broadcasted_iota(jnp.int32, (c_sz, c_sz), 1)
    incl, strict = ri >= ci, ri > ci
    eye = jnp.where(ri == ci, 1.0, 0.0).astype(f32)
    blk = lambda w: (ri // w) == (ci // w)
    diag8 = blk(GDN_BASE)
    sub_blocks = []
    w = GDN_BASE
    while w < c_sz:
        sub_blocks.append(jnp.logical_and(blk(2 * w), jnp.logical_not(blk(w))))
        w *= 2
    nt = (((1,), (1,)), ((), ()))
    dot = lambda a, b: jnp.dot(a.astype(bf), b.astype(bf), preferred_element_type=f32)
    dot_nt = lambda a, b: lax.dot_general(a.astype(bf), b.astype(bf), nt, preferred_element_type=f32)

    def split(a):
        hi = a.astype(bf)
        return hi, (a - hi.astype(f32)).astype(bf)

    def hp(a, b):
        a_hi, a_lo = split(a)
        b_hi, b_lo = split(b)
        mm = lambda u, w: jnp.dot(u, w, preferred_element_type=f32)
        return mm(a_hi, b_hi) + (mm(a_hi, b_lo) + mm(a_lo, b_hi))

    hs = range(B_HEADS)

    def chunk_pair(i, carry):
        cs = [2 * i, 2 * i + 1]
        r0 = [pl.multiple_of(c * c_sz, c_sz) for c in cs]
        items = [(j, h) for j in range(2) for h in hs]
        hd = lambda ref, j, h: ref[0, pl.ds(r0[j], c_sz), h * B_KEY_DIM:(h + 1) * B_KEY_DIM]
        l2n = lambda t: t * lax.rsqrt(jnp.sum(t * t, axis=-1, keepdims=True) + EPS)
        q = [l2n(hd(q_ref, j, h)) * (B_KEY_DIM ** -0.5) for j, h in items]
        k = [l2n(hd(k_ref, j, h)) for j, h in items]
        v = [hd(v_ref, j, h) for j, h in items]
        aux = [aux_ref[0, h, pl.ds(r0[j], c_sz), :] for j, h in items]
        gcr = [gcr_ref[0, h, pl.ds(cs[j], 1), :] for j, h in items]
        n_it = range(len(items))
        gcc = [a[:, 0:1] for a in aux]
        beta = [a[:, 1:2] for a in aux]
        decay = [jnp.exp(jnp.where(incl, gcc[n] - gcr[n], -jnp.inf)) for n in n_it]
        kb = [k[n] * beta[n] for n in n_it]
        vb = [v[n] * beta[n] for n in n_it]
        low = [jnp.where(strict, dot_nt(kb[n], k[n]) * decay[n], 0.0) for n in n_it]
        dg = [jnp.where(diag8, low[n], 0.0) for n in n_it]
        t = [eye - dg[n] for n in n_it]
        p = [hp(dg[n], dg[n]) for n in n_it]
        t = [hp(t[n], eye + p[n]) for n in n_it]
        p = [hp(p[n], p[n]) for n in n_it]
        t = [hp(t[n], eye + p[n]) for n in n_it]
        for below in sub_blocks:
            lb = [jnp.where(below, low[n], 0.0) for n in n_it]
            lt = [hp(lb[n], t[n]) for n in n_it]
            t = [t[n] - hp(t[n], lt[n]) for n in n_it]
        u = [dot(t[n], vb[n]) for n in n_it]
        kcd = [dot(t[n], kb[n] * jnp.exp(gcc[n])) for n in n_it]
        intra = [dot_nt(q[n], k[n]) * decay[n] for n in n_it]
        qg = [q[n] * jnp.exp(gcc[n]) for n in n_it]
        glast = [g[:, c_sz - 1:c_sz] for g in gcr]
        kdt = [(k[n] * jnp.exp(glast[n] - gcc[n])).T for n in n_it]
        s = [s_ref[h] for h in hs]
        for j in range(2):
            ix = [j * B_HEADS + h for h in hs]
            v_new = [u[ix[h]] - dot(kcd[ix[h]], s[h]) for h in hs]
            out = [dot(qg[ix[h]], s[h]) + dot(intra[ix[h]], v_new[h]) for h in hs]
            s = [s[h] * jnp.exp(glast[ix[h]]) + dot(kdt[ix[h]], v_new[h]) for h in hs]
            for h in hs:
                o = out[h]
                o = o * lax.rsqrt(jnp.mean(o * o, axis=-1, keepdims=True) + EPS) * nw_ref[...]
                o_ref[0, pl.ds(r0[j], c_sz), h * B_VAL_DIM:(h + 1) * B_VAL_DIM] = o
        for h in hs:
            s_ref[h] = s[h]
        return carry
    lax.fori_loop(0, GDN_G // 2, chunk_pair, 0)


def gated_delta_rule_pallas(qkv, g, beta, norm_w):
    bsz, seq, _ = qkv.shape
    nh, dk, dv = B_HEADS, B_KEY_DIM, B_VAL_DIM
    n = seq // CHUNK
    gc = jnp.cumsum(g.reshape(bsz, nh, n, CHUNK), axis=-1)
    aux = jnp.stack([gc.reshape(bsz, nh, seq), beta] + [jnp.zeros((bsz, nh, seq), jnp.float32)] * (GDN_AUX - 2), axis=-1)
    rows = GDN_G * CHUNK
    wide = lambda j: pl.BlockSpec((1, rows, nh * dk), lambda b, c: (b, c, j))
    out = pl.pallas_call(
        _gdn_body, grid=(bsz, n // GDN_G),
        in_specs=[wide(0), wide(1), wide(2),
                  pl.BlockSpec((1, nh, rows, GDN_AUX), lambda b, c: (b, 0, c, 0)),
                  pl.BlockSpec((1, nh, GDN_G, CHUNK), lambda b, c: (b, 0, c, 0)),
                  pl.BlockSpec((1, dv), lambda b, c: (0, 0))],
        out_specs=wide(0),
        out_shape=jax.ShapeDtypeStruct((bsz, seq, nh * dv), jnp.float32),
        scratch_shapes=[pltpu.VMEM((nh, dk, dv), jnp.float32)],
        compiler_params=pltpu.CompilerParams(dimension_semantics=("arbitrary", "arbitrary"),
                                             vmem_limit_bytes=40 * 1024 * 1024),
        name="gated_delta_rule")(qkv, qkv, qkv, aux, gc, norm_w.astype(jnp.float32).reshape(1, dv))
    return out


IN_NAMES = ("qa", "ka", "va", "qi", "ki", "wi", "qb", "kb", "vb", "zb", "bb", "ab", "gate_a", "gate_b")
PROJ_ORDER = ("gate_a", "gate_b", "qi", "qa", "qb", "kb", "vb", "zb", "ka", "va", "ki", "wi", "bb", "ab")
LANES = 128


def _proj_layout():
    src, off = {}, 0
    for name, w in zip(IN_NAMES, IN_SPLITS):
        src[name] = (off, w)
        off += w
    dst, off = {}, 0
    for name in PROJ_ORDER:
        dst[name] = (off, src[name][1])
        off += src[name][1]
    return src, dst, -(-off // LANES) * LANES


PROJ_SRC, PROJ_DST, PROJ_WIDTH = _proj_layout()
PROJ_TM = 256
PROJ_VMEM_LIMIT = 48 * 1024 * 1024


def permute_w_in(w_in):
    cols = [w_in[:, PROJ_SRC[n][0]:PROJ_SRC[n][0] + PROJ_SRC[n][1]] for n in PROJ_ORDER]
    cols.append(jnp.zeros((w_in.shape[0], PROJ_WIDTH - sum(c.shape[1] for c in cols)), w_in.dtype))
    return jnp.concatenate(cols, axis=1).astype(jnp.bfloat16)


def proj_piece(proj, name):
    off, w = PROJ_DST[name]
    return proj[..., off:off + w]


def _modulated_norm(x, scale, shift):
    return x * lax.rsqrt(jnp.mean(x * x, axis=-1, keepdims=True) + EPS) * scale + shift


def _sigmoid(v):
    return 1.0 / (1.0 + jnp.exp(-v))


def _ada_body(c_ref, w_ref, b_ref, o_ref):
    c = c_ref[...]
    o_ref[...] = jnp.dot((c * _sigmoid(c)).astype(jnp.bfloat16), w_ref[...].astype(jnp.bfloat16),
                         preferred_element_type=jnp.float32) + b_ref[...]


def ada_modulation(c, w_ada, b_ada):
    bsz, d = c.shape
    n = w_ada.shape[1]
    tn = D_MODEL
    return pl.pallas_call(
        _ada_body, grid=(n // tn,),
        in_specs=[pl.BlockSpec((bsz, d), lambda j: (0, 0)), pl.BlockSpec((d, tn), lambda j: (0, j)),
                  pl.BlockSpec((1, tn), lambda j: (0, j))],
        out_specs=pl.BlockSpec((bsz, tn), lambda j: (0, j)),
        out_shape=jax.ShapeDtypeStruct((bsz, n), jnp.float32),
        name="ada_modulation")(c, w_ada, b_ada.reshape(1, n))


def _in_proj_body(x_ref, sc_ref, sh_ref, w_ref, o_ref):
    n1 = _modulated_norm(x_ref[...], sc_ref[0], sh_ref[0])
    o_ref[...] = jnp.dot(n1.astype(jnp.bfloat16), w_ref[...], preferred_element_type=jnp.float32)


def in_proj(x2, scale, shift, w_perm, seq):
    n, d = x2.shape
    tm = PROJ_TM
    per = seq // tm
    vec = pl.BlockSpec((1, 1, d), lambda i: (i // per, 0, 0))
    return pl.pallas_call(
        _in_proj_body, grid=(n // tm,),
        in_specs=[pl.BlockSpec((tm, d), lambda i: (i, 0)), vec, vec,
                  pl.BlockSpec((d, PROJ_WIDTH), lambda i: (0, 0), pipeline_mode=pl.Buffered(1))],
        out_specs=pl.BlockSpec((tm, PROJ_WIDTH), lambda i: (i, 0)),
        out_shape=jax.ShapeDtypeStruct((n, PROJ_WIDTH), jnp.float32),
        compiler_params=pltpu.CompilerParams(dimension_semantics=("arbitrary",), vmem_limit_bytes=PROJ_VMEM_LIMIT),
        name="in_proj")(x2, scale, shift, w_perm)


def _merge_body(oa_ref, ob_ref, z_ref, ga_ref, gb_ref, x_ref, gt_ref, sc_ref, sh_ref, wpa_ref, wpb_ref, wo_ref,
                h_ref, n2_ref):
    f32, bf = jnp.float32, jnp.bfloat16
    z = z_ref[...]
    ob = ob_ref[...] * (z * _sigmoid(z))
    oa = oa_ref[0].reshape(A_WIDTH, oa_ref.shape[-1]).T
    ya = jnp.dot(oa.astype(bf), wpa_ref[...], preferred_element_type=f32)
    yb = jnp.dot(ob.astype(bf), wpb_ref[...], preferred_element_type=f32)
    merged = _sigmoid(ga_ref[...]) * ya + _sigmoid(gb_ref[...]) * yb
    y1 = jnp.dot(merged.astype(bf), wo_ref[...], preferred_element_type=f32)
    h = x_ref[...] + gt_ref[0] * y1
    h_ref[...] = h
    n2_ref[...] = _modulated_norm(h, sc_ref[0], sh_ref[0])


def merge(o_a, o_b, proj, x2, gt1, scale2, shift2, w_pa, w_pb, w_o, seq):
    n, d = x2.shape
    tm = PROJ_TM
    per = seq // tm
    bf = jnp.bfloat16
    vec = pl.BlockSpec((1, 1, d), lambda i: (i // per, 0, 0))

    def col(name):
        off, w = PROJ_DST[name]
        return pl.BlockSpec((tm, w), lambda i: (i, off // w))
    row = lambda w: pl.BlockSpec((tm, w), lambda i: (i, 0))
    res = lambda a, b: pl.BlockSpec((a, b), lambda i: (0, 0))
    return pl.pallas_call(
        _merge_body, grid=(n // tm,),
        in_specs=[pl.BlockSpec((1, A_HEADS, A_HEAD_DIM, tm), lambda i: (i // per, 0, 0, i % per)), row(B_V_WIDTH), col("zb"), col("gate_a"), col("gate_b"), row(d), vec, vec, vec,
                  res(A_WIDTH, d), res(B_V_WIDTH, d), res(d, d)],
        out_specs=[row(d), row(d)],
        out_shape=[jax.ShapeDtypeStruct((n, d), jnp.float32), jax.ShapeDtypeStruct((n, d), jnp.float32)],
        compiler_params=pltpu.CompilerParams(dimension_semantics=("arbitrary",), vmem_limit_bytes=PROJ_VMEM_LIMIT),
        name="merge")(o_a, o_b, proj, proj, proj, x2, gt1, scale2, shift2,
                      w_pa.astype(bf), w_pb.astype(bf), w_o.astype(bf))


def token_mixers(proj, positions, conv_w, a_log, dt_bias, norm_b_w, bsz, seq):
    qit, qat, w, kit, kat, vat, bg = dsa_prep(proj, positions, bsz, seq)
    o_a = dsa_attention_pallas(qit, w, qat, kit, kat, vat)

    lo, hi = PROJ_DST["qb"][0], PROJ_DST["vb"][0] + PROJ_DST["vb"][1]
    assert hi - lo == CONV_CHANNELS
    qkv = jax.nn.silu(causal_short_conv(proj[:, lo:hi].reshape(bsz, seq, CONV_CHANNELS), conv_w))
    beta = jax.nn.sigmoid(bg[:, :B_HEADS])
    g = -jnp.exp(a_log.astype(jnp.float32))[:, None] * jax.nn.softplus(bg[:, B_HEADS:] + dt_bias.astype(jnp.float32)[:, None])
    o_b = gated_delta_rule_pallas(qkv, g, beta, norm_b_w)
    return o_a, o_b


PEER_SLOTS = PEER_HEADS * PEER_TOPK
PEER_TB = 32
HALF_ROWS = 4
HI_MASK = -65536


def pack_table(tab):
    bits = lax.bitcast_convert_type(tab.astype(jnp.bfloat16), jnp.uint16).astype(jnp.uint32)
    half = tab.shape[1] // 2
    word = bits[:, :half] | (bits[:, half:] << 16)
    return lax.bitcast_convert_type(word, jnp.int32).reshape(tab.shape[0] * HALF_ROWS, 128)


def _table_row(tab_ref, rows, k):
    return _unpack(tab_ref[pl.ds(pl.multiple_of(rows[k], HALF_ROWS), HALF_ROWS), :])


def _unpack(w):
    lo = lax.bitcast_convert_type(lax.shift_left(w, 16), jnp.float32)
    hi = lax.bitcast_convert_type(w & jnp.int32(HI_MASK), jnp.float32)
    return lo, hi


def _peer_u_body(idx_ref, x_ref, tab_ref, o_ref, s_ref, sb_ref):
    def tok(t, carry):
        xlo = x_ref[t, 0:HALF_ROWS, :]
        xhi = x_ref[t, HALF_ROWS:2 * HALF_ROWS, :]
        rows = idx_ref.at[t]
        for k in range(PEER_SLOTS):
            lo, hi = _table_row(tab_ref, rows, k)
            s_ref[HALF_ROWS * k:HALF_ROWS * (k + 1), :] = lo * xlo + hi * xhi
        s4 = s_ref[pl.ds(0, PEER_SLOTS, stride=HALF_ROWS), :]
        for r in range(1, HALF_ROWS):
            s4 = s4 + s_ref[pl.ds(r, PEER_SLOTS, stride=HALF_ROWS), :]
        sb_ref[t] = s4
        return carry
    lax.fori_loop(0, PEER_TB, tok, 0)
    for t in range(PEER_TB):
        o_ref[t:t + 1, :] = jnp.sum(sb_ref[t].T, axis=0, keepdims=True)


def _peer_v_body(idx_ref, coef_ref, tab_ref, o_ref, cb_ref):
    for t in range(PEER_TB):
        cb_ref[t] = jnp.broadcast_to(coef_ref[t:t + 1, :], (PEER_SLOTS, 128)).T

    def tok(t, carry):
        nacc = 2
        acc = [jnp.zeros((HALF_ROWS, 128), jnp.float32) for _ in range(2 * nacc)]
        rows = idx_ref.at[t]
        for k in range(PEER_SLOTS):
            lo, hi = _table_row(tab_ref, rows, k)
            c = jnp.broadcast_to(cb_ref[t, k:k + 1, :], (HALF_ROWS, 128))
            a = k % nacc
            acc[2 * a] = acc[2 * a] + c * lo
            acc[2 * a + 1] = acc[2 * a + 1] + c * hi
        o_ref[t, 0:HALF_ROWS, :] = acc[0] + acc[2]
        o_ref[t, HALF_ROWS:2 * HALF_ROWS, :] = acc[1] + acc[3]
        return carry
    lax.fori_loop(0, PEER_TB, tok, 0)


def _table_spec():
    return pl.BlockSpec((PEER_N_EXPERTS * HALF_ROWS, 128), lambda i: (0, 0), pipeline_mode=pl.Buffered(1))


PEER_VMEM_LIMIT = 48 * 1024 * 1024


def peer_u(idx, x, tab):
    n = idx.shape[0]
    tb = PEER_TB
    return pl.pallas_call(
        _peer_u_body, grid=(n // tb,),
        in_specs=[pl.BlockSpec((tb, PEER_SLOTS), lambda i: (i, 0), memory_space=pltpu.SMEM),
                  pl.BlockSpec((tb, 2 * HALF_ROWS, 128), lambda i: (i, 0, 0)),
                  _table_spec()],
        out_specs=pl.BlockSpec((tb, PEER_SLOTS), lambda i: (i, 0)),
        out_shape=jax.ShapeDtypeStruct((n, PEER_SLOTS), jnp.float32),
        scratch_shapes=[pltpu.VMEM((HALF_ROWS * PEER_SLOTS, 128), jnp.float32),
                        pltpu.VMEM((tb, PEER_SLOTS, 128), jnp.float32)],
        compiler_params=pltpu.CompilerParams(dimension_semantics=("arbitrary",), vmem_limit_bytes=PEER_VMEM_LIMIT),
        name="peer_u")(idx, x.reshape(n, 2 * HALF_ROWS, 128), tab)


def peer_v(idx, coef, tab):
    n = idx.shape[0]
    tb = PEER_TB
    out = pl.pallas_call(
        _peer_v_body, grid=(n // tb,),
        in_specs=[pl.BlockSpec((tb, PEER_SLOTS), lambda i: (i, 0), memory_space=pltpu.SMEM),
                  pl.BlockSpec((tb, PEER_SLOTS), lambda i: (i, 0)),
                  _table_spec()],
        out_specs=pl.BlockSpec((tb, 2 * HALF_ROWS, 128), lambda i: (i, 0, 0)),
        out_shape=jax.ShapeDtypeStruct((n, 2 * HALF_ROWS, 128), jnp.float32),
        scratch_shapes=[pltpu.VMEM((tb, PEER_SLOTS, 128), jnp.float32)],
        compiler_params=pltpu.CompilerParams(dimension_semantics=("arbitrary",), vmem_limit_bytes=PEER_VMEM_LIMIT),
        name="peer_v")(idx, coef, tab)
    return out.reshape(n, 2 * HALF_ROWS * 128)


PEER_TT = 256


def _extract_top(ref, n_out):
    rows, t = ref.shape
    rid = lax.broadcasted_iota(jnp.int32, (rows, t), 0)
    vals, idxs = [], []
    for _ in range(n_out):
        s = ref[...]
        m = jnp.max(s, axis=0, keepdims=True)
        ix = jnp.min(jnp.where(s == m, rid, rows), axis=0, keepdims=True)
        ref[...] = jnp.where(rid == ix, -jnp.inf, s)
        vals.append(m)
        idxs.append(ix)
    return vals, idxs


def _peer_route_body(x_ref, wq_ref, sk_ref, idx_ref, gate_ref, s_ref, cand_ref, v_ref, i_ref, et_ref, gt_ref):
    kk, nk = PEER_TOPK, PEER_N_KEYS
    f32 = jnp.float32
    q = jnp.dot(x_ref[...].astype(jnp.bfloat16), wq_ref[...], preferred_element_type=f32).astype(jnp.bfloat16)
    nt = (((1,), (1,)), ((), ()))
    row16 = lax.broadcasted_iota(jnp.int32, (kk, PEER_TT), 0)
    for h in range(PEER_HEADS):
        qh = q[:, h * PEER_KEY_DIM:(h + 1) * PEER_KEY_DIM]
        s_ref[...] = lax.dot_general(sk_ref[h], qh, nt, preferred_element_type=f32)
        for p in range(2):
            vals, idxs = _extract_top(s_ref.at[p * nk:(p + 1) * nk, :], kk)
            for i in range(kk):
                v_ref[p, i:i + 1, :] = vals[i]
                i_ref[p, i:i + 1, :] = idxs[i]
        v2 = v_ref[1]
        for i in range(kk):
            cand_ref[i * kk:(i + 1) * kk, :] = v_ref[0, i:i + 1, :] + v2
        vals, cis = _extract_top(cand_ref, kk)
        i1, i2 = i_ref[0], i_ref[1]
        es = [jnp.exp(v - vals[0]) for v in vals]
        den = es[0]
        for e in es[1:]:
            den = den + e
        for k in range(kk):
            ci = cis[k]
            e1 = jnp.sum(jnp.where(row16 == lax.shift_right_logical(ci, 4), i1, 0), axis=0, keepdims=True)
            e2 = jnp.sum(jnp.where(row16 == (ci & 15), i2, 0), axis=0, keepdims=True)
            et_ref[h * kk + k:h * kk + k + 1, :] = (e1 * nk + e2) * HALF_ROWS
            gt_ref[h * kk + k:h * kk + k + 1, :] = es[k] / den
    idx_ref[...] = lax.bitcast_convert_type(lax.bitcast_convert_type(et_ref[...], f32).T, jnp.int32)
    gate_ref[...] = gt_ref[...].T


def peer_route(xn, wq, subkeys):
    n = xn.shape[0]
    tt = PEER_TT
    half = PEER_KEY_DIM // 2
    z = jnp.zeros((PEER_HEADS, PEER_N_KEYS, half), subkeys.dtype)
    skbd = jnp.concatenate([jnp.concatenate([subkeys[:, 0], z], axis=-1),
                            jnp.concatenate([z, subkeys[:, 1]], axis=-1)], axis=1).astype(jnp.bfloat16)
    return pl.pallas_call(
        _peer_route_body, grid=(n // tt,),
        in_specs=[pl.BlockSpec((tt, D_MODEL), lambda i: (i, 0)),
                  pl.BlockSpec((D_MODEL, PEER_HEADS * PEER_KEY_DIM), lambda i: (0, 0)),
                  pl.BlockSpec((PEER_HEADS, 2 * PEER_N_KEYS, PEER_KEY_DIM), lambda i: (0, 0, 0))],
        out_specs=[pl.BlockSpec((tt, PEER_SLOTS), lambda i: (i, 0)), pl.BlockSpec((tt, PEER_SLOTS), lambda i: (i, 0))],
        out_shape=[jax.ShapeDtypeStruct((n, PEER_SLOTS), jnp.int32), jax.ShapeDtypeStruct((n, PEER_SLOTS), jnp.float32)],
        scratch_shapes=[pltpu.VMEM((2 * PEER_N_KEYS, tt), jnp.float32),
                        pltpu.VMEM((PEER_TOPK * PEER_TOPK, tt), jnp.float32),
                        pltpu.VMEM((2, PEER_TOPK, tt), jnp.float32),
                        pltpu.VMEM((2, PEER_TOPK, tt), jnp.int32),
                        pltpu.VMEM((PEER_SLOTS, tt), jnp.int32),
                        pltpu.VMEM((PEER_SLOTS, tt), jnp.float32)],
        compiler_params=pltpu.CompilerParams(dimension_semantics=("arbitrary",), vmem_limit_bytes=32 * 1024 * 1024),
        name="peer_route")(xn, wq.astype(jnp.bfloat16), skbd)


def peer_channel_mixer(xn, wq, subkeys, u_tab, v_tab):
    bsz, seq, d = xn.shape
    n_tok = bsz * seq
    x2 = xn.reshape(n_tok, d)
    idx, gates = peer_route(x2, wq, subkeys)
    a = peer_u(idx, x2, pack_table(u_tab))
    coef = jax.nn.gelu(a, approximate=False) * gates
    out = peer_v(idx, coef, pack_table(v_tab))
    return out.reshape(bsz, seq, d)


def _residual_body(h_ref, y_ref, gt_ref, o_ref):
    o_ref[...] = h_ref[...] + gt_ref[0] * y_ref[...]


def _final_norm_body(h_ref, y_ref, gt_ref, g_ref, o_ref):
    h = h_ref[...] + gt_ref[0] * y_ref[...]
    o_ref[...] = h * lax.rsqrt(jnp.mean(h * h, axis=-1, keepdims=True) + EPS) * g_ref[...]


def gated_residual(h2, y2, gt, seq, gain=None):
    n, d = h2.shape
    tm = min(1024, seq)
    per = seq // tm
    row = pl.BlockSpec((tm, d), lambda i: (i, 0))
    specs = [row, row, pl.BlockSpec((1, 1, d), lambda i: (i // per, 0, 0))]
    args = [h2, y2, gt]
    if gain is not None:
        specs.append(pl.BlockSpec((1, d), lambda i: (0, 0)))
        args.append(gain.astype(jnp.float32).reshape(1, d))
    return pl.pallas_call(
        _residual_body if gain is None else _final_norm_body, grid=(n // tm,),
        in_specs=specs, out_specs=row, out_shape=jax.ShapeDtypeStruct((n, d), h2.dtype),
        name="gated_residual" if gain is None else "final_norm")(*args)


def kernel(x, c, positions, w_ada, b_ada, w_in, conv_w, a_log, dt_bias, norm_b_w,
           w_pa, w_pb, w_o, peer_wq, peer_subkeys, peer_u, peer_v, final_norm_w):
    bsz, seq, d = x.shape
    h = x.reshape(bsz * seq, d)
    for layer in range(DEPTH):
        mod = ada_modulation(c, w_ada[layer], b_ada[layer])
        sh1, sc1, gt1, sh2, sc2, gt2 = [m.reshape(bsz, 1, d) for m in jnp.split(mod, 6, axis=-1)]
        proj = in_proj(h, 1.0 + sc1, sh1, permute_w_in(w_in[layer]), seq)
        o_a, o_b = token_mixers(proj, positions, conv_w[layer], a_log[layer], dt_bias[layer], norm_b_w[layer],
                                bsz, seq)
        h, n2 = merge(o_a, o_b.reshape(bsz * seq, B_V_WIDTH), proj, h, gt1,
                      1.0 + sc2, sh2, w_pa[layer], w_pb[layer], w_o[layer], seq)
        y2 = peer_channel_mixer(n2.reshape(bsz, seq, d), peer_wq[layer], peer_subkeys[layer], peer_u[layer],
                                peer_v[layer])
        last = layer == DEPTH - 1
        h = gated_residual(h, y2.reshape(bsz * seq, d), gt2, seq, final_norm_w if last else None)
    return h.reshape(bsz, seq, d)
```

```python
import functools

import jax, jax.numpy as jnp
from jax import lax
from jax.experimental import pallas as pl
from jax.experimental.pallas import tpu as pltpu

D_MODEL = 1024
DEPTH = 1

A_HEADS = 8
A_KV_HEADS = 2
A_HEAD_DIM = 64
IDX_HEADS = 16
IDX_DIM = 64
IDX_TOPK_MAX = 256
B_HEADS = 8
B_KEY_DIM = 64
B_VAL_DIM = 64
CONV_WIDTH = 4
CHUNK = 64
ROPE_THETA = 500000.0
ROPE_FRACTION_DEN = 4
PEER_HEADS = 8
PEER_KEY_DIM = 128
PEER_N_KEYS = 128
PEER_N_EXPERTS = PEER_N_KEYS * PEER_N_KEYS
PEER_TOPK = 16
EPS = 1e-6

A_WIDTH = A_HEADS * A_HEAD_DIM
KV_WIDTH = A_KV_HEADS * A_HEAD_DIM
B_QK_WIDTH = B_HEADS * B_KEY_DIM
B_V_WIDTH = B_HEADS * B_VAL_DIM
CONV_CHANNELS = 2 * B_QK_WIDTH + B_V_WIDTH
IN_SPLITS = (A_WIDTH, KV_WIDTH, KV_WIDTH, IDX_HEADS * IDX_DIM, IDX_DIM, IDX_HEADS,
             B_QK_WIDTH, B_QK_WIDTH, B_V_WIDTH, B_V_WIDTH, B_HEADS, B_HEADS, D_MODEL, D_MODEL)


def causal_short_conv(x, w):
    width = w.shape[0]
    seq = x.shape[1]
    xp = jnp.pad(x, ((0, 0), (width - 1, 0), (0, 0)))
    out = xp[:, 0:seq] * w[0]
    for i in range(1, width):
        out = out + xp[:, i:i + seq] * w[i]
    return out


DSA_TQ = 256
DSA_TK = 256
INT_MIN = -2**31
NEG_BIG = -1e30


def _dsa_body(topk, qit_ref, w_ref, qat_ref, ki_ref, ka_ref, vat_ref, o_ref,
              key_ref):
    tq, tk = DSA_TQ, DSA_TK
    qb = pl.program_id(1)
    n_kv = qb + 1
    t_glob = qb * tq + lax.broadcasted_iota(jnp.int32, (1, tq), 1)
    row = lax.broadcasted_iota(jnp.int32, (tk, 1), 0)
    f32 = jnp.float32

    def p1(j, carry):
        kt = ki_ref[0, j]
        score = jnp.zeros((tk, tq), f32)
        for h in range(IDX_HEADS):
            lt = jnp.dot(kt, qit_ref[0, h], preferred_element_type=f32)
            score = score + w_ref[0, h:h + 1, :] * jnp.maximum(lt, 0.0)
        bits = lax.bitcast_convert_type(score + 0.0, jnp.int32)
        skey = jnp.where(bits >= 0, bits, bits ^ jnp.int32(0x7FFFFFFF))
        skey = jnp.where(j * tk + row <= t_glob, skey, jnp.int32(INT_MIN))
        key_ref[j] = skey
        return carry
    lax.fori_loop(0, n_kv, p1, 0)

    def count(pred):
        def body(j, acc):
            hit = jnp.where(pred(key_ref[j], j * tk + row), 1.0, 0.0)
            return acc + jnp.sum(hit.reshape(tk // 8, 8, tq), axis=0)
        acc = lax.fori_loop(0, n_kv, body, jnp.zeros((8, tq), f32))
        return jnp.sum(acc, axis=0, keepdims=True)

    kf = jnp.float32(topk)

    def bit_step(i, ku):
        cand_u = ku | lax.shift_left(jnp.int32(1), 31 - i)
        cand = cand_u ^ jnp.int32(INT_MIN)
        c = count(lambda k, s: k >= cand)
        return jnp.where(c >= kf, cand_u, ku)
    ku = lax.fori_loop(0, 32, bit_step, jnp.zeros((1, tq), jnp.int32))
    kth = ku ^ jnp.int32(INT_MIN)
    c_gt = count(lambda k, s: k > kth)
    c_ge = count(lambda k, s: k >= kth)
    short = kth == jnp.int32(INT_MIN)
    x0 = jnp.where(short, jnp.int32(-1), jnp.int32(2**30))
    need = kf - c_gt
    has_tie = jnp.max(jnp.where(jnp.logical_and(c_ge > kf, jnp.logical_not(short)), 1.0, 0.0)) > 0.0

    def tie_search():
        def step(i, x):
            bit = lax.shift_left(jnp.int32(1), 11 - i)
            probe = x + bit - 1
            c = count(lambda k, s: jnp.logical_and(k == kth, s <= probe))
            return jnp.where(c < need, x + bit, x)
        x = lax.fori_loop(0, 12, step, jnp.zeros((1, tq), jnp.int32))
        return jnp.where(short, jnp.int32(-1), x)
    x_lim = lax.cond(has_tie, tie_search, lambda: x0)

    rep = A_HEADS // A_KV_HEADS
    hs = range(A_HEADS)

    def p3(j, carry):
        m, l, acc = carry
        skey = key_ref[j]
        s_idx = j * tk + row
        sel = jnp.logical_or(skey > kth, jnp.logical_and(skey == kth, s_idx <= x_lim))
        kt = [ka_ref[0, g, j] for g in range(A_KV_HEADS)]
        vt = [vat_ref[0, g, j] for g in range(A_KV_HEADS)]
        s = [jnp.where(sel, jnp.dot(kt[h // rep], qat_ref[0, h], preferred_element_type=f32), NEG_BIG) for h in hs]
        m_new = [jnp.maximum(m[h], jnp.max(s[h], axis=0, keepdims=True)) for h in hs]
        alpha = [jnp.exp(m[h] - m_new[h]) for h in hs]
        p = [jnp.exp(s[h] - m_new[h]) for h in hs]
        l_new = [alpha[h] * l[h] + jnp.sum(p[h], axis=0, keepdims=True) for h in hs]
        acc_new = [alpha[h] * acc[h] + jnp.dot(vt[h // rep], p[h].astype(jnp.bfloat16), preferred_element_type=f32)
                   for h in hs]
        return tuple(m_new), tuple(l_new), tuple(acc_new)

    init = (tuple(jnp.full((1, tq), NEG_BIG, f32) for _ in hs), tuple(jnp.zeros((1, tq), f32) for _ in hs),
            tuple(jnp.zeros((A_HEAD_DIM, tq), f32) for _ in hs))
    _, l_fin, acc_fin = lax.fori_loop(0, n_kv, p3, init)
    for h in hs:
        o_ref[0, h] = acc_fin[h] / l_fin[h]


ROPE_HALF = A_HEAD_DIM // ROPE_FRACTION_DEN // 2


def _rope_t(xt, n_heads, cos, sin, scale=None):
    outs = []
    for h in range(n_heads):
        b = h * A_HEAD_DIM
        x1, x2, rest = xt[b:b + ROPE_HALF], xt[b + ROPE_HALF:b + 2 * ROPE_HALF], xt[b + 2 * ROPE_HALF:b + A_HEAD_DIM]
        o = jnp.concatenate([x1 * cos - x2 * sin, x2 * cos + x1 * sin, rest], axis=0)
        outs.append(o if scale is None else o * scale)
    return outs


def _dsa_prep_body(qi_ref, qa_ref, ka_ref, va_ref, sm_ref, cos_ref, sin_ref,
                   qit_ref, qat_ref, w_ref, kit_ref, kat_ref, vat_ref, bg_ref):
    bf = jnp.bfloat16
    cos, sin = cos_ref[0], sin_ref[0]
    for h, o in enumerate(_rope_t(qi_ref[...].T, IDX_HEADS, cos, sin)):
        qit_ref[0, h] = o.astype(bf)
    for h, o in enumerate(_rope_t(qa_ref[...].T, A_HEADS, cos, sin, A_HEAD_DIM ** -0.5)):
        qat_ref[0, h] = o.astype(bf)
    kat = _rope_t(ka_ref[...].T, A_KV_HEADS, cos, sin)
    vt = va_ref[...].T
    for g in range(A_KV_HEADS):
        kat_ref[0, g, 0] = kat[g].T.astype(bf)
        vat_ref[0, g, 0] = vt[g * A_HEAD_DIM:(g + 1) * A_HEAD_DIM].astype(bf)
    smt = sm_ref[...].T
    kit_ref[0, 0] = _rope_t(smt[0:IDX_DIM], 1, cos, sin)[0].T.astype(bf)
    w_ref[0] = smt[IDX_DIM:IDX_DIM + IDX_HEADS] * ((IDX_HEADS ** -0.5) * (IDX_DIM ** -0.5))
    bg_ref[0] = smt[IDX_DIM + IDX_HEADS:IDX_DIM + IDX_HEADS + 2 * B_HEADS]


def dsa_prep(proj, positions, bsz, seq):
    t = DSA_TQ
    per = seq // t
    assert DSA_TK == t and PROJ_ORDER[-4:] == ("ki", "wi", "bb", "ab") and PROJ_DST["ki"][0] + LANES == PROJ_WIDTH
    rd = A_HEAD_DIM // ROPE_FRACTION_DEN
    inv_freq = jnp.power(jnp.float32(ROPE_THETA), -jnp.arange(ROPE_HALF, dtype=jnp.float32) * (2.0 / rd))
    ang = positions.astype(jnp.float32)[:, None, :] * inv_freq[None, :, None]

    def col(name, width=None):
        off, w = PROJ_DST[name]
        w = width or w
        return pl.BlockSpec((t, w), lambda i: (i, off // w))
    trig = pl.BlockSpec((1, ROPE_HALF, t), lambda i: (i // per, 0, i % per))
    bf, f32 = jnp.bfloat16, jnp.float32
    sd = jax.ShapeDtypeStruct
    n_kv = seq // DSA_TK
    return pl.pallas_call(
        _dsa_prep_body, grid=(bsz * per,),
        in_specs=[col("qi"), col("qa"), col("ka"), col("va"), col("ki", LANES), trig, trig],
        out_specs=[pl.BlockSpec((1, IDX_HEADS, IDX_DIM, t), lambda i: (i // per, 0, 0, i % per)),
                   pl.BlockSpec((1, A_HEADS, A_HEAD_DIM, t), lambda i: (i // per, 0, 0, i % per)),
                   pl.BlockSpec((1, IDX_HEADS, t), lambda i: (i // per, 0, i % per)),
                   pl.BlockSpec((1, 1, t, IDX_DIM), lambda i: (i // per, i % per, 0, 0)),
                   pl.BlockSpec((1, A_KV_HEADS, 1, t, A_HEAD_DIM), lambda i: (i // per, 0, i % per, 0, 0)),
                   pl.BlockSpec((1, A_KV_HEADS, 1, A_HEAD_DIM, t), lambda i: (i // per, 0, i % per, 0, 0)),
                   pl.BlockSpec((1, 2 * B_HEADS, t), lambda i: (i // per, 0, i % per))],
        out_shape=[sd((bsz, IDX_HEADS, IDX_DIM, seq), bf), sd((bsz, A_HEADS, A_HEAD_DIM, seq), bf),
                   sd((bsz, IDX_HEADS, seq), f32), sd((bsz, n_kv, DSA_TK, IDX_DIM), bf),
                   sd((bsz, A_KV_HEADS, n_kv, DSA_TK, A_HEAD_DIM), bf),
                   sd((bsz, A_KV_HEADS, n_kv, A_HEAD_DIM, DSA_TK), bf), sd((bsz, 2 * B_HEADS, seq), f32)],
        compiler_params=pltpu.CompilerParams(dimension_semantics=("arbitrary",), vmem_limit_bytes=32 * 1024 * 1024),
        name="dsa_prep")(proj, proj, proj, proj, proj, jnp.cos(ang), jnp.sin(ang))


def dsa_attention_pallas(qit, w, qat, kit, kat, vat):
    bsz, seq = qit.shape[0], qit.shape[-1]
    tq, tk = DSA_TQ, DSA_TK
    topk = min(IDX_TOPK_MAX, seq // 4)
    n_kv = seq // tk
    return pl.pallas_call(
        functools.partial(_dsa_body, topk),
        grid=(bsz, seq // tq),
        in_specs=[
            pl.BlockSpec((1, IDX_HEADS, IDX_DIM, tq), lambda b, q: (b, 0, 0, q)),
            pl.BlockSpec((1, IDX_HEADS, tq), lambda b, q: (b, 0, q)),
            pl.BlockSpec((1, A_HEADS, A_HEAD_DIM, tq), lambda b, q: (b, 0, 0, q)),
            pl.BlockSpec((1, n_kv, tk, IDX_DIM), lambda b, q: (b, 0, 0, 0)),
            pl.BlockSpec((1, A_KV_HEADS, n_kv, tk, A_HEAD_DIM), lambda b, q: (b, 0, 0, 0, 0)),
            pl.BlockSpec((1, A_KV_HEADS, n_kv, A_HEAD_DIM, tk), lambda b, q: (b, 0, 0, 0, 0)),
        ],
        out_specs=pl.BlockSpec((1, A_HEADS, A_HEAD_DIM, tq), lambda b, q: (b, 0, 0, q)),
        out_shape=jax.ShapeDtypeStruct((bsz, A_HEADS, A_HEAD_DIM, seq), jnp.float32),
        scratch_shapes=[
            pltpu.VMEM((n_kv, tk, tq), jnp.int32),
        ],
        compiler_params=pltpu.CompilerParams(dimension_semantics=("arbitrary", "arbitrary"),
                                             vmem_limit_bytes=48 * 1024 * 1024),
        name="dsa_attention",
    )(qit, w, qat, kit, kat, vat)


GDN_G = 8
GDN_AUX = 8
GDN_BASE = 8


def _gdn_body(q_ref, k_ref, v_ref, aux_ref, gcr_ref, nw_ref, o_ref, s_ref):
    c_sz = CHUNK
    f32, bf = jnp.float32, jnp.bfloat16

    @pl.when(pl.program_id(1) == 0)
    def _():
        s_ref[...] = jnp.zeros(s_ref.shape, f32)

    ri = lax.broadcasted_iota(jnp.int32, (c_sz, c_sz), 0)
    ci = lax.broadcasted_iota(jnp.int32, (c_sz, c_sz), 1)
    incl, strict = ri >= ci, ri > ci
    eye = jnp.where(ri == ci, 1.0, 0.0).astype(f32)
    blk = lambda w: (ri // w) == (ci // w)
    diag8 = blk(GDN_BASE)
    sub_blocks = []
    w = GDN_BASE
    while w < c_sz:
        sub_blocks.append(jnp.logical_and(blk(2 * w), jnp.logical_not(blk(w))))
        w *= 2
    nt = (((1,), (1,)), ((), ()))
    dot = lambda a, b: jnp.dot(a.astype(bf), b.astype(bf), preferred_element_type=f32)
    dot_nt = lambda a, b: lax.dot_general(a.astype(bf), b.astype(bf), nt, preferred_element_type=f32)

    def split(a):
        hi = a.astype(bf)
        return hi, (a - hi.astype(f32)).astype(bf)

    def hp(a, b):
        a_hi, a_lo = split(a)
        b_hi, b_lo = split(b)
        mm = lambda u, w: jnp.dot(u, w, preferred_element_type=f32)
        return mm(a_hi, b_hi) + (mm(a_hi, b_lo) + mm(a_lo, b_hi))

    hs = range(B_HEADS)

    def chunk_pair(i, carry):
        cs = [2 * i, 2 * i + 1]
        r0 = [pl.multiple_of(c * c_sz, c_sz) for c in cs]
        items = [(j, h) for j in range(2) for h in hs]
        hd = lambda ref, j, h: ref[0, pl.ds(r0[j], c_sz), h * B_KEY_DIM:(h + 1) * B_KEY_DIM]
        l2n = lambda t: t * lax.rsqrt(jnp.sum(t * t, axis=-1, keepdims=True) + EPS)
        q = [l2n(hd(q_ref, j, h)) * (B_KEY_DIM ** -0.5) for j, h in items]
        k = [l2n(hd(k_ref, j, h)) for j, h in items]
        v = [hd(v_ref, j, h) for j, h in items]
        aux = [aux_ref[0, h, pl.ds(r0[j], c_sz), :] for j, h in items]
        gcr = [gcr_ref[0, h, pl.ds(cs[j], 1), :] for j, h in items]
        n_it = range(len(items))
        gcc = [a[:, 0:1] for a in aux]
        beta = [a[:, 1:2] for a in aux]
        decay = [jnp.exp(jnp.where(incl, gcc[n] - gcr[n], -jnp.inf)) for n in n_it]
        kb = [k[n] * beta[n] for n in n_it]
        vb = [v[n] * beta[n] for n in n_it]
        low = [jnp.where(strict, dot_nt(kb[n], k[n]) * decay[n], 0.0) for n in n_it]
        dg = [jnp.where(diag8, low[n], 0.0) for n in n_it]
        t = [eye - dg[n] for n in n_it]
        p = [hp(dg[n], dg[n]) for n in n_it]
        t = [hp(t[n], eye + p[n]) for n in n_it]
        p = [hp(p[n], p[n]) for n in n_it]
        t = [hp(t[n], eye + p[n]) for n in n_it]
        for below in sub_blocks:
            lb = [jnp.where(below, low[n], 0.0) for n in n_it]
            lt = [hp(lb[n], t[n]) for n in n_it]
            t = [t[n] - hp(t[n], lt[n]) for n in n_it]
        u = [dot(t[n], vb[n]) for n in n_it]
        kcd = [dot(t[n], kb[n] * jnp.exp(gcc[n])) for n in n_it]
        intra = [dot_nt(q[n], k[n]) * decay[n] for n in n_it]
        qg = [q[n] * jnp.exp(gcc[n]) for n in n_it]
        glast = [g[:, c_sz - 1:c_sz] for g in gcr]
        kdt = [(k[n] * jnp.exp(glast[n] - gcc[n])).T for n in n_it]
        s = [s_ref[h] for h in hs]
        for j in range(2):
            ix = [j * B_HEADS + h for h in hs]
            v_new = [u[ix[h]] - dot(kcd[ix[h]], s[h]) for h in hs]
            out = [dot(qg[ix[h]], s[h]) + dot(intra[ix[h]], v_new[h]) for h in hs]
            s = [s[h] * jnp.exp(glast[ix[h]]) + dot(kdt[ix[h]], v_new[h]) for h in hs]
            for h in hs:
                o = out[h]
                o = o * lax.rsqrt(jnp.mean(o * o, axis=-1, keepdims=True) + EPS) * nw_ref[...]
                o_ref[0, pl.ds(r0[j], c_sz), h * B_VAL_DIM:(h + 1) * B_VAL_DIM] = o
        for h in hs:
            s_ref[h] = s[h]
        return carry
    lax.fori_loop(0, GDN_G // 2, chunk_pair, 0)


def gated_delta_rule_pallas(qkv, g, beta, norm_w):
    bsz, seq, _ = qkv.shape
    nh, dk, dv = B_HEADS, B_KEY_DIM, B_VAL_DIM
    n = seq // CHUNK
    gc = jnp.cumsum(g.reshape(bsz, nh, n, CHUNK), axis=-1)
    aux = jnp.stack([gc.reshape(bsz, nh, seq), beta] + [jnp.zeros((bsz, nh, seq), jnp.float32)] * (GDN_AUX - 2), axis=-1)
    rows = GDN_G * CHUNK
    wide = lambda j: pl.BlockSpec((1, rows, nh * dk), lambda b, c: (b, c, j))
    out = pl.pallas_call(
        _gdn_body, grid=(bsz, n // GDN_G),
        in_specs=[wide(0), wide(1), wide(2),
                  pl.BlockSpec((1, nh, rows, GDN_AUX), lambda b, c: (b, 0, c, 0)),
                  pl.BlockSpec((1, nh, GDN_G, CHUNK), lambda b, c: (b, 0, c, 0)),
                  pl.BlockSpec((1, dv), lambda b, c: (0, 0))],
        out_specs=wide(0),
        out_shape=jax.ShapeDtypeStruct((bsz, seq, nh * dv), jnp.float32),
        scratch_shapes=[pltpu.VMEM((nh, dk, dv), jnp.float32)],
        compiler_params=pltpu.CompilerParams(dimension_semantics=("arbitrary", "arbitrary"),
                                             vmem_limit_bytes=40 * 1024 * 1024),
        name="gated_delta_rule")(qkv, qkv, qkv, aux, gc, norm_w.astype(jnp.float32).reshape(1, dv))
    return out


IN_NAMES = ("qa", "ka", "va", "qi", "ki", "wi", "qb", "kb", "vb", "zb", "bb", "ab", "gate_a", "gate_b")
PROJ_ORDER = ("gate_a", "gate_b", "qi", "qa", "qb", "kb", "vb", "zb", "ka", "va", "ki", "wi", "bb", "ab")
LANES = 128


def _proj_layout():
    src, off = {}, 0
    for name, w in zip(IN_NAMES, IN_SPLITS):
        src[name] = (off, w)
        off += w
    dst, off = {}, 0
    for name in PROJ_ORDER:
        dst[name] = (off, src[name][1])
        off += src[name][1]
    return src, dst, -(-off // LANES) * LANES


PROJ_SRC, PROJ_DST, PROJ_WIDTH = _proj_layout()
PROJ_TM = 256
PROJ_VMEM_LIMIT = 48 * 1024 * 1024


def permute_w_in(w_in):
    cols = [w_in[:, PROJ_SRC[n][0]:PROJ_SRC[n][0] + PROJ_SRC[n][1]] for n in PROJ_ORDER]
    cols.append(jnp.zeros((w_in.shape[0], PROJ_WIDTH - sum(c.shape[1] for c in cols)), w_in.dtype))
    return jnp.concatenate(cols, axis=1).astype(jnp.bfloat16)


def proj_piece(proj, name):
    off, w = PROJ_DST[name]
    return proj[..., off:off + w]


def _modulated_norm(x, scale, shift):
    return x * lax.rsqrt(jnp.mean(x * x, axis=-1, keepdims=True) + EPS) * scale + shift


def _sigmoid(v):
    return 1.0 / (1.0 + jnp.exp(-v))


def _ada_body(c_ref, w_ref, b_ref, o_ref):
    c = c_ref[...]
    o_ref[...] = jnp.dot((c * _sigmoid(c)).astype(jnp.bfloat16), w_ref[...].astype(jnp.bfloat16),
                         preferred_element_type=jnp.float32) + b_ref[...]


def ada_modulation(c, w_ada, b_ada):
    bsz, d = c.shape
    n = w_ada.shape[1]
    tn = D_MODEL
    return pl.pallas_call(
        _ada_body, grid=(n // tn,),
        in_specs=[pl.BlockSpec((bsz, d), lambda j: (0, 0)), pl.BlockSpec((d, tn), lambda j: (0, j)),
                  pl.BlockSpec((1, tn), lambda j: (0, j))],
        out_specs=pl.BlockSpec((bsz, tn), lambda j: (0, j)),
        out_shape=jax.ShapeDtypeStruct((bsz, n), jnp.float32),
        name="ada_modulation")(c, w_ada, b_ada.reshape(1, n))


def _in_proj_body(x_ref, sc_ref, sh_ref, w_ref, o_ref):
    n1 = _modulated_norm(x_ref[...], sc_ref[0], sh_ref[0])
    o_ref[...] = jnp.dot(n1.astype(jnp.bfloat16), w_ref[...], preferred_element_type=jnp.float32)


def in_proj(x2, scale, shift, w_perm, seq):
    n, d = x2.shape
    tm = PROJ_TM
    per = seq // tm
    vec = pl.BlockSpec((1, 1, d), lambda i: (i // per, 0, 0))
    return pl.pallas_call(
        _in_proj_body, grid=(n // tm,),
        in_specs=[pl.BlockSpec((tm, d), lambda i: (i, 0)), vec, vec,
                  pl.BlockSpec((d, PROJ_WIDTH), lambda i: (0, 0), pipeline_mode=pl.Buffered(1))],
        out_specs=pl.BlockSpec((tm, PROJ_WIDTH), lambda i: (i, 0)),
        out_shape=jax.ShapeDtypeStruct((n, PROJ_WIDTH), jnp.float32),
        compiler_params=pltpu.CompilerParams(dimension_semantics=("arbitrary",), vmem_limit_bytes=PROJ_VMEM_LIMIT),
        name="in_proj")(x2, scale, shift, w_perm)


def _merge_body(oa_ref, ob_ref, z_ref, ga_ref, gb_ref, x_ref, gt_ref, sc_ref, sh_ref, wpa_ref, wpb_ref, wo_ref,
                h_ref, n2_ref):
    f32, bf = jnp.float32, jnp.bfloat16
    z = z_ref[...]
    ob = ob_ref[...] * (z * _sigmoid(z))
    oa = oa_ref[0].reshape(A_WIDTH, oa_ref.shape[-1]).T
    ya = jnp.dot(oa.astype(bf), wpa_ref[...], preferred_element_type=f32)
    yb = jnp.dot(ob.astype(bf), wpb_ref[...], preferred_element_type=f32)
    merged = _sigmoid(ga_ref[...]) * ya + _sigmoid(gb_ref[...]) * yb
    y1 = jnp.dot(merged.astype(bf), wo_ref[...], preferred_element_type=f32)
    h = x_ref[...] + gt_ref[0] * y1
    h_ref[...] = h
    n2_ref[...] = _modulated_norm(h, sc_ref[0], sh_ref[0])


def merge(o_a, o_b, proj, x2, gt1, scale2, shift2, w_pa, w_pb, w_o, seq):
    n, d = x2.shape
    tm = PROJ_TM
    per = seq // tm
    bf = jnp.bfloat16
    vec = pl.BlockSpec((1, 1, d), lambda i: (i // per, 0, 0))

    def col(name):
        off, w = PROJ_DST[name]
        return pl.BlockSpec((tm, w), lambda i: (i, off // w))
    row = lambda w: pl.BlockSpec((tm, w), lambda i: (i, 0))
    res = lambda a, b: pl.BlockSpec((a, b), lambda i: (0, 0))
    return pl.pallas_call(
        _merge_body, grid=(n // tm,),
        in_specs=[pl.BlockSpec((1, A_HEADS, A_HEAD_DIM, tm), lambda i: (i // per, 0, 0, i % per)), row(B_V_WIDTH), col("zb"), col("gate_a"), col("gate_b"), row(d), vec, vec, vec,
                  res(A_WIDTH, d), res(B_V_WIDTH, d), res(d, d)],
        out_specs=[row(d), row(d)],
        out_shape=[jax.ShapeDtypeStruct((n, d), jnp.float32), jax.ShapeDtypeStruct((n, d), jnp.float32)],
        compiler_params=pltpu.CompilerParams(dimension_semantics=("arbitrary",), vmem_limit_bytes=PROJ_VMEM_LIMIT),
        name="merge")(o_a, o_b, proj, proj, proj, x2, gt1, scale2, shift2,
                      w_pa.astype(bf), w_pb.astype(bf), w_o.astype(bf))


def token_mixers(proj, positions, conv_w, a_log, dt_bias, norm_b_w, bsz, seq):
    qit, qat, w, kit, kat, vat, bg = dsa_prep(proj, positions, bsz, seq)
    o_a = dsa_attention_pallas(qit, w, qat, kit, kat, vat)

    lo, hi = PROJ_DST["qb"][0], PROJ_DST["vb"][0] + PROJ_DST["vb"][1]
    assert hi - lo == CONV_CHANNELS
    qkv = jax.nn.silu(causal_short_conv(proj[:, lo:hi].reshape(bsz, seq, CONV_CHANNELS), conv_w))
    beta = jax.nn.sigmoid(bg[:, :B_HEADS])
    g = -jnp.exp(a_log.astype(jnp.float32))[:, None] * jax.nn.softplus(bg[:, B_HEADS:] + dt_bias.astype(jnp.float32)[:, None])
    o_b = gated_delta_rule_pallas(qkv, g, beta, norm_b_w)
    return o_a, o_b


PEER_SLOTS = PEER_HEADS * PEER_TOPK
PEER_TB = 32
HALF_ROWS = 4
HI_MASK = -65536


def pack_table(tab):
    bits = lax.bitcast_convert_type(tab.astype(jnp.bfloat16), jnp.uint16).astype(jnp.uint32)
    half = tab.shape[1] // 2
    word = bits[:, :half] | (bits[:, half:] << 16)
    return lax.bitcast_convert_type(word, jnp.int32).reshape(tab.shape[0] * HALF_ROWS, 128)


def _table_row(tab_ref, rows, k):
    return _unpack(tab_ref[pl.ds(pl.multiple_of(rows[k], HALF_ROWS), HALF_ROWS), :])


def _unpack(w):
    lo = lax.bitcast_convert_type(lax.shift_left(w, 16), jnp.float32)
    hi = lax.bitcast_convert_type(w & jnp.int32(HI_MASK), jnp.float32)
    return lo, hi


def _peer_u_body(idx_ref, x_ref, gate_ref, tab_ref, o_ref, s_ref, sb_ref):
    def lane_sums(t):
        a = jnp.sum(sb_ref[t].T, axis=0, keepdims=True)
        o_ref[pl.ds(t, 1), :] = 0.5 * a * (1.0 + lax.erf(a * (2.0 ** -0.5))) * gate_ref[pl.ds(t, 1), :]

    sb_ref[0] = jnp.zeros(sb_ref.shape[1:], jnp.float32)

    def tok(t, carry):
        lane_sums(jnp.maximum(t - 1, 0))
        xlo = x_ref[t, 0:HALF_ROWS, :]
        xhi = x_ref[t, HALF_ROWS:2 * HALF_ROWS, :]
        rows = idx_ref.at[t]
        for k in range(PEER_SLOTS):
            lo, hi = _table_row(tab_ref, rows, k)
            s_ref[HALF_ROWS * k:HALF_ROWS * (k + 1), :] = lo * xlo + hi * xhi
        s4 = s_ref[pl.ds(0, PEER_SLOTS, stride=HALF_ROWS), :]
        for r in range(1, HALF_ROWS):
            s4 = s4 + s_ref[pl.ds(r, PEER_SLOTS, stride=HALF_ROWS), :]
        sb_ref[t] = s4
        return carry
    lax.fori_loop(0, PEER_TB, tok, 0, unroll=2)
    lane_sums(PEER_TB - 1)


def _peer_v_body(idx_ref, coef_ref, tab_ref, o_ref, cb_ref):
    def spread(t):
        return jnp.broadcast_to(coef_ref[pl.ds(t, 1), :], (PEER_SLOTS, 128)).T

    def tok(t, weights):
        nxt = spread(jnp.minimum(t + 1, PEER_TB - 1))
        cb_ref[...] = weights
        nacc = 2
        acc = [jnp.zeros((HALF_ROWS, 128), jnp.float32) for _ in range(2 * nacc)]
        rows = idx_ref.at[t]
        for k in range(PEER_SLOTS):
            lo, hi = _table_row(tab_ref, rows, k)
            c = jnp.broadcast_to(cb_ref[k:k + 1, :], (HALF_ROWS, 128))
            a = k % nacc
            acc[2 * a] = acc[2 * a] + c * lo
            acc[2 * a + 1] = acc[2 * a + 1] + c * hi
        o_ref[t, 0:HALF_ROWS, :] = acc[0] + acc[2]
        o_ref[t, HALF_ROWS:2 * HALF_ROWS, :] = acc[1] + acc[3]
        return nxt
    lax.fori_loop(0, PEER_TB, tok, spread(0))


def _table_spec():
    return pl.BlockSpec((PEER_N_EXPERTS * HALF_ROWS, 128), lambda i: (0, 0), pipeline_mode=pl.Buffered(1))


PEER_VMEM_LIMIT = 48 * 1024 * 1024


def peer_u(idx, x, gates, tab):
    n = idx.shape[0]
    tb = PEER_TB
    return pl.pallas_call(
        _peer_u_body, grid=(n // tb,),
        in_specs=[pl.BlockSpec((tb, PEER_SLOTS), lambda i: (i, 0), memory_space=pltpu.SMEM),
                  pl.BlockSpec((tb, 2 * HALF_ROWS, 128), lambda i: (i, 0, 0)),
                  pl.BlockSpec((tb, PEER_SLOTS), lambda i: (i, 0)),
                  _table_spec()],
        out_specs=pl.BlockSpec((tb, PEER_SLOTS), lambda i: (i, 0)),
        out_shape=jax.ShapeDtypeStruct((n, PEER_SLOTS), jnp.float32),
        scratch_shapes=[pltpu.VMEM((HALF_ROWS * PEER_SLOTS, 128), jnp.float32),
                        pltpu.VMEM((tb, PEER_SLOTS, 128), jnp.float32)],
        compiler_params=pltpu.CompilerParams(dimension_semantics=("arbitrary",), vmem_limit_bytes=PEER_VMEM_LIMIT),
        name="peer_u")(idx, x.reshape(n, 2 * HALF_ROWS, 128), gates, tab)


def peer_v(idx, coef, tab):
    n = idx.shape[0]
    tb = PEER_TB
    out = pl.pallas_call(
        _peer_v_body, grid=(n // tb,),
        in_specs=[pl.BlockSpec((tb, PEER_SLOTS), lambda i: (i, 0), memory_space=pltpu.SMEM),
                  pl.BlockSpec((tb, PEER_SLOTS), lambda i: (i, 0)),
                  _table_spec()],
        out_specs=pl.BlockSpec((tb, 2 * HALF_ROWS, 128), lambda i: (i, 0, 0)),
        out_shape=jax.ShapeDtypeStruct((n, 2 * HALF_ROWS, 128), jnp.float32),
        scratch_shapes=[pltpu.VMEM((PEER_SLOTS, 128), jnp.float32)],
        compiler_params=pltpu.CompilerParams(dimension_semantics=("arbitrary",), vmem_limit_bytes=PEER_VMEM_LIMIT),
        name="peer_v")(idx, coef, tab)
    return out.reshape(n, 2 * HALF_ROWS * 128)


PEER_TT = 256


def _extract_top(ref, n_out):
    rows, t = ref.shape
    rid = lax.broadcasted_iota(jnp.int32, (rows, t), 0)
    vals, idxs = [], []
    for _ in range(n_out):
        s = ref[...]
        m = jnp.max(s, axis=0, keepdims=True)
        ix = jnp.min(jnp.where(s == m, rid, rows), axis=0, keepdims=True)
        ref[...] = jnp.where(rid == ix, -jnp.inf, s)
        vals.append(m)
        idxs.append(ix)
    return vals, idxs


def _peer_route_body(x_ref, wq_ref, sk_ref, idx_ref, gate_ref, s_ref, cand_ref, v_ref, i_ref, et_ref, gt_ref):
    kk, nk = PEER_TOPK, PEER_N_KEYS
    f32 = jnp.float32
    q = jnp.dot(x_ref[...].astype(jnp.bfloat16), wq_ref[...], preferred_element_type=f32).astype(jnp.bfloat16)
    nt = (((1,), (1,)), ((), ()))
    row16 = lax.broadcasted_iota(jnp.int32, (kk, PEER_TT), 0)
    for h in range(PEER_HEADS):
        qh = q[:, h * PEER_KEY_DIM:(h + 1) * PEER_KEY_DIM]
        s_ref[...] = lax.dot_general(sk_ref[h], qh, nt, preferred_element_type=f32)
        for p in range(2):
            vals, idxs = _extract_top(s_ref.at[p * nk:(p + 1) * nk, :], kk)
            for i in range(kk):
                v_ref[p, i:i + 1, :] = vals[i]
                i_ref[p, i:i + 1, :] = idxs[i]
        v2 = v_ref[1]
        for i in range(kk):
            cand_ref[i * kk:(i + 1) * kk, :] = v_ref[0, i:i + 1, :] + v2
        vals, cis = _extract_top(cand_ref, kk)
        i1, i2 = i_ref[0], i_ref[1]
        es = [jnp.exp(v - vals[0]) for v in vals]
        den = es[0]
        for e in es[1:]:
            den = den + e
        for k in range(kk):
            ci = cis[k]
            e1 = jnp.sum(jnp.where(row16 == lax.shift_right_logical(ci, 4), i1, 0), axis=0, keepdims=True)
            e2 = jnp.sum(jnp.where(row16 == (ci & 15), i2, 0), axis=0, keepdims=True)
            et_ref[h * kk + k:h * kk + k + 1, :] = (e1 * nk + e2) * HALF_ROWS
            gt_ref[h * kk + k:h * kk + k + 1, :] = es[k] / den
    idx_ref[...] = lax.bitcast_convert_type(lax.bitcast_convert_type(et_ref[...], f32).T, jnp.int32)
    gate_ref[...] = gt_ref[...].T


def peer_route(xn, wq, subkeys):
    n = xn.shape[0]
    tt = PEER_TT
    half = PEER_KEY_DIM // 2
    z = jnp.zeros((PEER_HEADS, PEER_N_KEYS, half), subkeys.dtype)
    skbd = jnp.concatenate([jnp.concatenate([subkeys[:, 0], z], axis=-1),
                            jnp.concatenate([z, subkeys[:, 1]], axis=-1)], axis=1).astype(jnp.bfloat16)
    return pl.pallas_call(
        _peer_route_body, grid=(n // tt,),
        in_specs=[pl.BlockSpec((tt, D_MODEL), lambda i: (i, 0)),
                  pl.BlockSpec((D_MODEL, PEER_HEADS * PEER_KEY_DIM), lambda i: (0, 0)),
                  pl.BlockSpec((PEER_HEADS, 2 * PEER_N_KEYS, PEER_KEY_DIM), lambda i: (0, 0, 0))],
        out_specs=[pl.BlockSpec((tt, PEER_SLOTS), lambda i: (i, 0)), pl.BlockSpec((tt, PEER_SLOTS), lambda i: (i, 0))],
        out_shape=[jax.ShapeDtypeStruct((n, PEER_SLOTS), jnp.int32), jax.ShapeDtypeStruct((n, PEER_SLOTS), jnp.float32)],
        scratch_shapes=[pltpu.VMEM((2 * PEER_N_KEYS, tt), jnp.float32),
                        pltpu.VMEM((PEER_TOPK * PEER_TOPK, tt), jnp.float32),
                        pltpu.VMEM((2, PEER_TOPK, tt), jnp.float32),
                        pltpu.VMEM((2, PEER_TOPK, tt), jnp.int32),
                        pltpu.VMEM((PEER_SLOTS, tt), jnp.int32),
                        pltpu.VMEM((PEER_SLOTS, tt), jnp.float32)],
        compiler_params=pltpu.CompilerParams(dimension_semantics=("arbitrary",), vmem_limit_bytes=32 * 1024 * 1024),
        name="peer_route")(xn, wq.astype(jnp.bfloat16), skbd)


def peer_channel_mixer(xn, wq, subkeys, u_tab, v_tab):
    bsz, seq, d = xn.shape
    n_tok = bsz * seq
    x2 = xn.reshape(n_tok, d)
    idx, gates = peer_route(x2, wq, subkeys)
    coef = peer_u(idx, x2, gates, pack_table(u_tab))
    out = peer_v(idx, coef, pack_table(v_tab))
    return out.reshape(bsz, seq, d)


def _residual_body(h_ref, y_ref, gt_ref, o_ref):
    o_ref[...] = h_ref[...] + gt_ref[0] * y_ref[...]


def _final_norm_body(h_ref, y_ref, gt_ref, g_ref, o_ref):
    h = h_ref[...] + gt_ref[0] * y_ref[...]
    o_ref[...] = h * lax.rsqrt(jnp.mean(h * h, axis=-1, keepdims=True) + EPS) * g_ref[...]


def gated_residual(h2, y2, gt, seq, gain=None):
    n, d = h2.shape
    tm = min(1024, seq)
    per = seq // tm
    row = pl.BlockSpec((tm, d), lambda i: (i, 0))
    specs = [row, row, pl.BlockSpec((1, 1, d), lambda i: (i // per, 0, 0))]
    args = [h2, y2, gt]
    if gain is not None:
        specs.append(pl.BlockSpec((1, d), lambda i: (0, 0)))
        args.append(gain.astype(jnp.float32).reshape(1, d))
    return pl.pallas_call(
        _residual_body if gain is None else _final_norm_body, grid=(n // tm,),
        in_specs=specs, out_specs=row, out_shape=jax.ShapeDtypeStruct((n, d), h2.dtype),
        name="gated_residual" if gain is None else "final_norm")(*args)


def kernel(x, c, positions, w_ada, b_ada, w_in, conv_w, a_log, dt_bias, norm_b_w,
           w_pa, w_pb, w_o, peer_wq, peer_subkeys, peer_u, peer_v, final_norm_w):
    bsz, seq, d = x.shape
    h = x.reshape(bsz * seq, d)
    for layer in range(DEPTH):
        mod = ada_modulation(c, w_ada[layer], b_ada[layer])
        sh1, sc1, gt1, sh2, sc2, gt2 = [m.reshape(bsz, 1, d) for m in jnp.split(mod, 6, axis=-1)]
        proj = in_proj(h, 1.0 + sc1, sh1, permute_w_in(w_in[layer]), seq)
        o_a, o_b = token_mixers(proj, positions, conv_w[layer], a_log[layer], dt_bias[layer], norm_b_w[layer],
                                bsz, seq)
        h, n2 = merge(o_a, o_b.reshape(bsz * seq, B_V_WIDTH), proj, h, gt1,
                      1.0 + sc2, sh2, w_pa[layer], w_pb[layer], w_o[layer], seq)
        y2 = peer_channel_mixer(n2.reshape(bsz, seq, d), peer_wq[layer], peer_subkeys[layer], peer_u[layer],
                                peer_v[layer])
        last = layer == DEPTH - 1
        h = gated_residual(h, y2.reshape(bsz * seq, d), gt2, seq, final_norm_w if last else None)
    return h.reshape(bsz, seq, d)
```

```python
import functools

import jax, jax.numpy as jnp
from jax import lax
from jax.experimental import pallas as pl
from jax.experimental.pallas import tpu as pltpu

D_MODEL = 1024
DEPTH = 1

A_HEADS = 8
A_KV_HEADS = 2
A_HEAD_DIM = 64
IDX_HEADS = 16
IDX_DIM = 64
IDX_TOPK_MAX = 256
B_HEADS = 8
B_KEY_DIM = 64
B_VAL_DIM = 64
CONV_WIDTH = 4
CHUNK = 64
ROPE_THETA = 500000.0
ROPE_FRACTION_DEN = 4
PEER_HEADS = 8
PEER_KEY_DIM = 128
PEER_N_KEYS = 128
PEER_N_EXPERTS = PEER_N_KEYS * PEER_N_KEYS
PEER_TOPK = 16
EPS = 1e-6

A_WIDTH = A_HEADS * A_HEAD_DIM
KV_WIDTH = A_KV_HEADS * A_HEAD_DIM
B_QK_WIDTH = B_HEADS * B_KEY_DIM
B_V_WIDTH = B_HEADS * B_VAL_DIM
IN_SPLITS = (A_WIDTH, KV_WIDTH, KV_WIDTH, IDX_HEADS * IDX_DIM, IDX_DIM, IDX_HEADS,
             B_QK_WIDTH, B_QK_WIDTH, B_V_WIDTH, B_V_WIDTH, B_HEADS, B_HEADS, D_MODEL, D_MODEL)


DSA_TQ = 256
DSA_TK = 256
INT_MIN = -2**31
NEG_BIG = -1e30


def _dsa_body(topk, qit_ref, w_ref, qat_ref, ki_ref, ka_ref, vat_ref, o_ref,
              key_ref):
    tq, tk = DSA_TQ, DSA_TK
    qb = pl.program_id(1)
    n_kv = qb + 1
    t_glob = qb * tq + lax.broadcasted_iota(jnp.int32, (1, tq), 1)
    row = lax.broadcasted_iota(jnp.int32, (tk, 1), 0)
    f32 = jnp.float32

    def p1(j, carry):
        kt = ki_ref[0, j]
        score = jnp.zeros((tk, tq), f32)
        for h in range(IDX_HEADS):
            lt = jnp.dot(kt, qit_ref[0, h], preferred_element_type=f32)
            score = score + w_ref[0, h:h + 1, :] * jnp.maximum(lt, 0.0)
        bits = lax.bitcast_convert_type(score + 0.0, jnp.int32)
        skey = jnp.where(bits >= 0, bits, bits ^ jnp.int32(0x7FFFFFFF))
        skey = jnp.where(j * tk + row <= t_glob, skey, jnp.int32(INT_MIN))
        key_ref[j] = skey
        return carry
    lax.fori_loop(0, n_kv, p1, 0)

    def count(pred):
        def body(j, acc):
            hit = jnp.where(pred(key_ref[j], j * tk + row), 1.0, 0.0)
            return acc + jnp.sum(hit.reshape(tk // 8, 8, tq), axis=0)
        acc = lax.fori_loop(0, n_kv, body, jnp.zeros((8, tq), f32))
        return jnp.sum(acc, axis=0, keepdims=True)

    kf = jnp.float32(topk)

    def bit_step(i, ku):
        cand_u = ku | lax.shift_left(jnp.int32(1), 31 - i)
        cand = cand_u ^ jnp.int32(INT_MIN)
        c = count(lambda k, s: k >= cand)
        return jnp.where(c >= kf, cand_u, ku)
    ku = lax.fori_loop(0, 32, bit_step, jnp.zeros((1, tq), jnp.int32))
    kth = ku ^ jnp.int32(INT_MIN)
    c_gt = count(lambda k, s: k > kth)
    c_ge = count(lambda k, s: k >= kth)
    short = kth == jnp.int32(INT_MIN)
    x0 = jnp.where(short, jnp.int32(-1), jnp.int32(2**30))
    need = kf - c_gt
    has_tie = jnp.max(jnp.where(jnp.logical_and(c_ge > kf, jnp.logical_not(short)), 1.0, 0.0)) > 0.0

    def tie_search():
        def step(i, x):
            bit = lax.shift_left(jnp.int32(1), 11 - i)
            probe = x + bit - 1
            c = count(lambda k, s: jnp.logical_and(k == kth, s <= probe))
            return jnp.where(c < need, x + bit, x)
        x = lax.fori_loop(0, 12, step, jnp.zeros((1, tq), jnp.int32))
        return jnp.where(short, jnp.int32(-1), x)
    x_lim = lax.cond(has_tie, tie_search, lambda: x0)

    rep = A_HEADS // A_KV_HEADS
    hs = range(A_HEADS)

    def p3(j, carry):
        m, l, acc = carry
        skey = key_ref[j]
        s_idx = j * tk + row
        sel = jnp.logical_or(skey > kth, jnp.logical_and(skey == kth, s_idx <= x_lim))
        kt = [ka_ref[0, g, j] for g in range(A_KV_HEADS)]
        vt = [vat_ref[0, g, j] for g in range(A_KV_HEADS)]
        s = [jnp.where(sel, jnp.dot(kt[h // rep], qat_ref[0, h], preferred_element_type=f32), NEG_BIG) for h in hs]
        m_new = [jnp.maximum(m[h], jnp.max(s[h], axis=0, keepdims=True)) for h in hs]
        alpha = [jnp.exp(m[h] - m_new[h]) for h in hs]
        p = [jnp.exp(s[h] - m_new[h]) for h in hs]
        l_new = [alpha[h] * l[h] + jnp.sum(p[h], axis=0, keepdims=True) for h in hs]
        acc_new = [alpha[h] * acc[h] + jnp.dot(vt[h // rep], p[h].astype(jnp.bfloat16), preferred_element_type=f32)
                   for h in hs]
        return tuple(m_new), tuple(l_new), tuple(acc_new)

    init = (tuple(jnp.full((1, tq), NEG_BIG, f32) for _ in hs), tuple(jnp.zeros((1, tq), f32) for _ in hs),
            tuple(jnp.zeros((A_HEAD_DIM, tq), f32) for _ in hs))
    _, l_fin, acc_fin = lax.fori_loop(0, n_kv, p3, init)
    for h in hs:
        o_ref[0, h] = acc_fin[h] / l_fin[h]


ROPE_HALF = A_HEAD_DIM // ROPE_FRACTION_DEN // 2


def _rope_t(xt, n_heads, cos, sin, scale=None):
    outs = []
    for h in range(n_heads):
        b = h * A_HEAD_DIM
        x1, x2, rest = xt[b:b + ROPE_HALF], xt[b + ROPE_HALF:b + 2 * ROPE_HALF], xt[b + 2 * ROPE_HALF:b + A_HEAD_DIM]
        o = jnp.concatenate([x1 * cos - x2 * sin, x2 * cos + x1 * sin, rest], axis=0)
        outs.append(o if scale is None else o * scale)
    return outs


def _dsa_prep_body(qi_ref, qa_ref, ka_ref, va_ref, sm_ref, cos_ref, sin_ref,
                   qit_ref, qat_ref, w_ref, kit_ref, kat_ref, vat_ref, bg_ref):
    bf = jnp.bfloat16
    cos, sin = cos_ref[0], sin_ref[0]
    for h, o in enumerate(_rope_t(qi_ref[...].T, IDX_HEADS, cos, sin)):
        qit_ref[0, h] = o.astype(bf)
    for h, o in enumerate(_rope_t(qa_ref[...].T, A_HEADS, cos, sin, A_HEAD_DIM ** -0.5)):
        qat_ref[0, h] = o.astype(bf)
    kat = _rope_t(ka_ref[...].T, A_KV_HEADS, cos, sin)
    vt = va_ref[...].T
    for g in range(A_KV_HEADS):
        kat_ref[0, g, 0] = kat[g].T.astype(bf)
        vat_ref[0, g, 0] = vt[g * A_HEAD_DIM:(g + 1) * A_HEAD_DIM].astype(bf)
    smt = sm_ref[...].T
    kit_ref[0, 0] = _rope_t(smt[0:IDX_DIM], 1, cos, sin)[0].T.astype(bf)
    w_ref[0] = smt[IDX_DIM:IDX_DIM + IDX_HEADS] * ((IDX_HEADS ** -0.5) * (IDX_DIM ** -0.5))
    bg_ref[0] = smt[IDX_DIM + IDX_HEADS:IDX_DIM + IDX_HEADS + 2 * B_HEADS]


def dsa_prep(proj, positions, bsz, seq):
    t = DSA_TQ
    per = seq // t
    assert DSA_TK == t and PROJ_ORDER[-4:] == ("ki", "wi", "bb", "ab") and PROJ_DST["ki"][0] + LANES == PROJ_WIDTH
    rd = A_HEAD_DIM // ROPE_FRACTION_DEN
    inv_freq = jnp.power(jnp.float32(ROPE_THETA), -jnp.arange(ROPE_HALF, dtype=jnp.float32) * (2.0 / rd))
    ang = positions.astype(jnp.float32)[:, None, :] * inv_freq[None, :, None]

    def col(name, width=None):
        off, w = PROJ_DST[name]
        w = width or w
        return pl.BlockSpec((t, w), lambda i: (i, off // w))
    trig = pl.BlockSpec((1, ROPE_HALF, t), lambda i: (i // per, 0, i % per))
    bf, f32 = jnp.bfloat16, jnp.float32
    sd = jax.ShapeDtypeStruct
    n_kv = seq // DSA_TK
    return pl.pallas_call(
        _dsa_prep_body, grid=(bsz * per,),
        in_specs=[col("qi"), col("qa"), col("ka"), col("va"), col("ki", LANES), trig, trig],
        out_specs=[pl.BlockSpec((1, IDX_HEADS, IDX_DIM, t), lambda i: (i // per, 0, 0, i % per)),
                   pl.BlockSpec((1, A_HEADS, A_HEAD_DIM, t), lambda i: (i // per, 0, 0, i % per)),
                   pl.BlockSpec((1, IDX_HEADS, t), lambda i: (i // per, 0, i % per)),
                   pl.BlockSpec((1, 1, t, IDX_DIM), lambda i: (i // per, i % per, 0, 0)),
                   pl.BlockSpec((1, A_KV_HEADS, 1, t, A_HEAD_DIM), lambda i: (i // per, 0, i % per, 0, 0)),
                   pl.BlockSpec((1, A_KV_HEADS, 1, A_HEAD_DIM, t), lambda i: (i // per, 0, i % per, 0, 0)),
                   pl.BlockSpec((1, 2 * B_HEADS, t), lambda i: (i // per, 0, i % per))],
        out_shape=[sd((bsz, IDX_HEADS, IDX_DIM, seq), bf), sd((bsz, A_HEADS, A_HEAD_DIM, seq), bf),
                   sd((bsz, IDX_HEADS, seq), f32), sd((bsz, n_kv, DSA_TK, IDX_DIM), bf),
                   sd((bsz, A_KV_HEADS, n_kv, DSA_TK, A_HEAD_DIM), bf),
                   sd((bsz, A_KV_HEADS, n_kv, A_HEAD_DIM, DSA_TK), bf), sd((bsz, 2 * B_HEADS, seq), f32)],
        compiler_params=pltpu.CompilerParams(dimension_semantics=("arbitrary",), vmem_limit_bytes=32 * 1024 * 1024),
        name="dsa_prep")(proj, proj, proj, proj, proj, jnp.cos(ang), jnp.sin(ang))


def dsa_attention_pallas(qit, w, qat, kit, kat, vat):
    bsz, seq = qit.shape[0], qit.shape[-1]
    tq, tk = DSA_TQ, DSA_TK
    topk = min(IDX_TOPK_MAX, seq // 4)
    n_kv = seq // tk
    return pl.pallas_call(
        functools.partial(_dsa_body, topk),
        grid=(bsz, seq // tq),
        in_specs=[
            pl.BlockSpec((1, IDX_HEADS, IDX_DIM, tq), lambda b, q: (b, 0, 0, q)),
            pl.BlockSpec((1, IDX_HEADS, tq), lambda b, q: (b, 0, q)),
            pl.BlockSpec((1, A_HEADS, A_HEAD_DIM, tq), lambda b, q: (b, 0, 0, q)),
            pl.BlockSpec((1, n_kv, tk, IDX_DIM), lambda b, q: (b, 0, 0, 0)),
            pl.BlockSpec((1, A_KV_HEADS, n_kv, tk, A_HEAD_DIM), lambda b, q: (b, 0, 0, 0, 0)),
            pl.BlockSpec((1, A_KV_HEADS, n_kv, A_HEAD_DIM, tk), lambda b, q: (b, 0, 0, 0, 0)),
        ],
        out_specs=pl.BlockSpec((1, A_HEADS, A_HEAD_DIM, tq), lambda b, q: (b, 0, 0, q)),
        out_shape=jax.ShapeDtypeStruct((bsz, A_HEADS, A_HEAD_DIM, seq), jnp.float32),
        scratch_shapes=[
            pltpu.VMEM((n_kv, tk, tq), jnp.int32),
        ],
        compiler_params=pltpu.CompilerParams(dimension_semantics=("arbitrary", "arbitrary"),
                                             vmem_limit_bytes=48 * 1024 * 1024),
        name="dsa_attention",
    )(qit, w, qat, kit, kat, vat)


GDN_G = 8
GDN_AUX = 8
GDN_BASE = 8
GDN_PAD = 8


def _gdn_body(q_ref, k_ref, v_ref, cw_ref, aux_ref, gcr_ref, nw_ref, o_ref, s_ref, xs_ref, cs_ref):
    c_sz = CHUNK
    f32, bf = jnp.float32, jnp.bfloat16

    rows = GDN_G * c_sz
    halo = CONV_WIDTH - 1

    @pl.when(pl.program_id(1) == 0)
    def _():
        s_ref[...] = jnp.zeros(s_ref.shape, f32)
        xs_ref[:, 0:GDN_PAD, :] = jnp.zeros((3, GDN_PAD, xs_ref.shape[-1]), f32)

    for j, ref in enumerate((q_ref, k_ref, v_ref)):
        xs_ref[j, GDN_PAD:GDN_PAD + rows, :] = ref[0]
        acc = xs_ref[j, GDN_PAD - halo:GDN_PAD - halo + rows, :] * cw_ref[j, 0:1, :]
        for i in range(1, CONV_WIDTH):
            acc = acc + xs_ref[j, GDN_PAD - halo + i:GDN_PAD - halo + i + rows, :] * cw_ref[j, i:i + 1, :]
        cs_ref[j] = acc * _sigmoid(acc)
        xs_ref[j, GDN_PAD - halo:GDN_PAD, :] = xs_ref[j, GDN_PAD + rows - halo:GDN_PAD + rows, :]

    ri = lax.broadcasted_iota(jnp.int32, (c_sz, c_sz), 0)
    ci = lax.broadcasted_iota(jnp.int32, (c_sz, c_sz), 1)
    incl, strict = ri >= ci, ri > ci
    eye = jnp.where(ri == ci, 1.0, 0.0).astype(f32)
    blk = lambda w: (ri // w) == (ci // w)
    diag8 = blk(GDN_BASE)
    sub_blocks = []
    w = GDN_BASE
    while w < c_sz:
        sub_blocks.append(jnp.logical_and(blk(2 * w), jnp.logical_not(blk(w))))
        w *= 2
    nt = (((1,), (1,)), ((), ()))
    dot = lambda a, b: jnp.dot(a.astype(bf), b.astype(bf), preferred_element_type=f32)
    dot_nt = lambda a, b: lax.dot_general(a.astype(bf), b.astype(bf), nt, preferred_element_type=f32)

    def split(a):
        hi = a.astype(bf)
        return hi, (a - hi.astype(f32)).astype(bf)

    def hp(a, b):
        a_hi, a_lo = split(a)
        b_hi, b_lo = split(b)
        mm = lambda u, w: jnp.dot(u, w, preferred_element_type=f32)
        return mm(a_hi, b_hi) + (mm(a_hi, b_lo) + mm(a_lo, b_hi))

    hs = range(B_HEADS)

    def chunk_pair(i, carry):
        cs = [2 * i, 2 * i + 1]
        r0 = [pl.multiple_of(c * c_sz, c_sz) for c in cs]
        items = [(j, h) for j in range(2) for h in hs]
        hd = lambda a, j, h: cs_ref[a, pl.ds(r0[j], c_sz), h * B_KEY_DIM:(h + 1) * B_KEY_DIM]
        l2n = lambda t: t * lax.rsqrt(jnp.sum(t * t, axis=-1, keepdims=True) + EPS)
        q = [l2n(hd(0, j, h)) * (B_KEY_DIM ** -0.5) for j, h in items]
        k = [l2n(hd(1, j, h)) for j, h in items]
        v = [hd(2, j, h) for j, h in items]
        aux = [aux_ref[0, h, pl.ds(r0[j], c_sz), :] for j, h in items]
        gcr = [gcr_ref[0, h, pl.ds(cs[j], 1), :] for j, h in items]
        n_it = range(len(items))
        gcc = [a[:, 0:1] for a in aux]
        beta = [a[:, 1:2] for a in aux]
        decay = [jnp.exp(jnp.where(incl, gcc[n] - gcr[n], -jnp.inf)) for n in n_it]
        kb = [k[n] * beta[n] for n in n_it]
        vb = [v[n] * beta[n] for n in n_it]
        low = [jnp.where(strict, dot_nt(kb[n], k[n]) * decay[n], 0.0) for n in n_it]
        dg = [jnp.where(diag8, low[n], 0.0) for n in n_it]
        t = [eye - dg[n] for n in n_it]
        p = [hp(dg[n], dg[n]) for n in n_it]
        t = [hp(t[n], eye + p[n]) for n in n_it]
        p = [hp(p[n], p[n]) for n in n_it]
        t = [hp(t[n], eye + p[n]) for n in n_it]
        for below in sub_blocks:
            lb = [jnp.where(below, low[n], 0.0) for n in n_it]
            lt = [hp(lb[n], t[n]) for n in n_it]
            t = [t[n] - hp(t[n], lt[n]) for n in n_it]
        u = [dot(t[n], vb[n]) for n in n_it]
        kcd = [dot(t[n], kb[n] * jnp.exp(gcc[n])) for n in n_it]
        intra = [dot_nt(q[n], k[n]) * decay[n] for n in n_it]
        qg = [q[n] * jnp.exp(gcc[n]) for n in n_it]
        glast = [g[:, c_sz - 1:c_sz] for g in gcr]
        kdt = [(k[n] * jnp.exp(glast[n] - gcc[n])).T for n in n_it]
        s = [s_ref[h] for h in hs]
        for j in range(2):
            ix = [j * B_HEADS + h for h in hs]
            v_new = [u[ix[h]] - dot(kcd[ix[h]], s[h]) for h in hs]
            out = [dot(qg[ix[h]], s[h]) + dot(intra[ix[h]], v_new[h]) for h in hs]
            s = [s[h] * jnp.exp(glast[ix[h]]) + dot(kdt[ix[h]], v_new[h]) for h in hs]
            for h in hs:
                o = out[h]
                o = o * lax.rsqrt(jnp.mean(o * o, axis=-1, keepdims=True) + EPS) * nw_ref[...]
                o_ref[0, pl.ds(r0[j], c_sz), h * B_VAL_DIM:(h + 1) * B_VAL_DIM] = o
        for h in hs:
            s_ref[h] = s[h]
        return carry
    lax.fori_loop(0, GDN_G // 2, chunk_pair, 0)


def gated_delta_rule_pallas(proj, conv_w, g, beta, norm_w, bsz, seq):
    nh, dk, dv = B_HEADS, B_KEY_DIM, B_VAL_DIM
    n = seq // CHUNK
    gc = jnp.cumsum(g.reshape(bsz, nh, n, CHUNK), axis=-1)
    aux = jnp.stack([gc.reshape(bsz, nh, seq), beta] + [jnp.zeros((bsz, nh, seq), jnp.float32)] * (GDN_AUX - 2), axis=-1)
    rows = GDN_G * CHUNK
    width = nh * dk

    def col(name):
        off, w = PROJ_DST[name]
        assert w == width
        return pl.BlockSpec((1, rows, w), lambda b, c: (b, c, off // w))
    cw = jnp.transpose(conv_w.astype(jnp.float32).reshape(CONV_WIDTH, 3, width), (1, 0, 2))
    proj3 = proj.reshape(bsz, seq, proj.shape[-1])
    return pl.pallas_call(
        _gdn_body, grid=(bsz, n // GDN_G),
        in_specs=[col("qb"), col("kb"), col("vb"),
                  pl.BlockSpec((3, CONV_WIDTH, width), lambda b, c: (0, 0, 0)),
                  pl.BlockSpec((1, nh, rows, GDN_AUX), lambda b, c: (b, 0, c, 0)),
                  pl.BlockSpec((1, nh, GDN_G, CHUNK), lambda b, c: (b, 0, c, 0)),
                  pl.BlockSpec((1, dv), lambda b, c: (0, 0))],
        out_specs=pl.BlockSpec((1, rows, nh * dv), lambda b, c: (b, c, 0)),
        out_shape=jax.ShapeDtypeStruct((bsz, seq, nh * dv), jnp.float32),
        scratch_shapes=[pltpu.VMEM((nh, dk, dv), jnp.float32),
                        pltpu.VMEM((3, GDN_PAD + rows, width), jnp.float32),
                        pltpu.VMEM((3, rows, width), jnp.float32)],
        compiler_params=pltpu.CompilerParams(dimension_semantics=("arbitrary", "arbitrary"),
                                             vmem_limit_bytes=40 * 1024 * 1024),
        name="gated_delta_rule")(proj3, proj3, proj3, cw, aux, gc, norm_w.astype(jnp.float32).reshape(1, dv))


IN_NAMES = ("qa", "ka", "va", "qi", "ki", "wi", "qb", "kb", "vb", "zb", "bb", "ab", "gate_a", "gate_b")
PROJ_ORDER = ("gate_a", "gate_b", "qi", "qa", "qb", "kb", "vb", "zb", "ka", "va", "ki", "wi", "bb", "ab")
LANES = 128


def _proj_layout():
    src, off = {}, 0
    for name, w in zip(IN_NAMES, IN_SPLITS):
        src[name] = (off, w)
        off += w
    dst, off = {}, 0
    for name in PROJ_ORDER:
        dst[name] = (off, src[name][1])
        off += src[name][1]
    return src, dst, -(-off // LANES) * LANES


PROJ_SRC, PROJ_DST, PROJ_WIDTH = _proj_layout()
PROJ_TM = 256
PROJ_VMEM_LIMIT = 48 * 1024 * 1024


def permute_w_in(w_in):
    cols = [w_in[:, PROJ_SRC[n][0]:PROJ_SRC[n][0] + PROJ_SRC[n][1]] for n in PROJ_ORDER]
    cols.append(jnp.zeros((w_in.shape[0], PROJ_WIDTH - sum(c.shape[1] for c in cols)), w_in.dtype))
    return jnp.concatenate(cols, axis=1).astype(jnp.bfloat16)


def _modulated_norm(x, scale, shift):
    return x * lax.rsqrt(jnp.mean(x * x, axis=-1, keepdims=True) + EPS) * scale + shift


def _sigmoid(v):
    return 1.0 / (1.0 + jnp.exp(-v))


def _ada_body(c_ref, w_ref, b_ref, o_ref):
    c = c_ref[...]
    o_ref[...] = jnp.dot((c * _sigmoid(c)).astype(jnp.bfloat16), w_ref[...].astype(jnp.bfloat16),
                         preferred_element_type=jnp.float32) + b_ref[...]


def ada_modulation(c, w_ada, b_ada):
    bsz, d = c.shape
    n = w_ada.shape[1]
    tn = D_MODEL
    return pl.pallas_call(
        _ada_body, grid=(n // tn,),
        in_specs=[pl.BlockSpec((bsz, d), lambda j: (0, 0)), pl.BlockSpec((d, tn), lambda j: (0, j)),
                  pl.BlockSpec((1, tn), lambda j: (0, j))],
        out_specs=pl.BlockSpec((bsz, tn), lambda j: (0, j)),
        out_shape=jax.ShapeDtypeStruct((bsz, n), jnp.float32),
        name="ada_modulation")(c, w_ada, b_ada.reshape(1, n))


def _in_proj_body(x_ref, sc_ref, sh_ref, w_ref, o_ref):
    n1 = _modulated_norm(x_ref[...], sc_ref[0], sh_ref[0])
    o_ref[...] = jnp.dot(n1.astype(jnp.bfloat16), w_ref[...], preferred_element_type=jnp.float32)


def in_proj(x2, scale, shift, w_perm, seq):
    n, d = x2.shape
    tm = PROJ_TM
    per = seq // tm
    vec = pl.BlockSpec((1, 1, d), lambda i: (i // per, 0, 0))
    return pl.pallas_call(
        _in_proj_body, grid=(n // tm,),
        in_specs=[pl.BlockSpec((tm, d), lambda i: (i, 0)), vec, vec,
                  pl.BlockSpec((d, PROJ_WIDTH), lambda i: (0, 0), pipeline_mode=pl.Buffered(1))],
        out_specs=pl.BlockSpec((tm, PROJ_WIDTH), lambda i: (i, 0)),
        out_shape=jax.ShapeDtypeStruct((n, PROJ_WIDTH), jnp.float32),
        compiler_params=pltpu.CompilerParams(dimension_semantics=("arbitrary",), vmem_limit_bytes=PROJ_VMEM_LIMIT),
        name="in_proj")(x2, scale, shift, w_perm)


def _merge_body(oa_ref, ob_ref, z_ref, ga_ref, gb_ref, x_ref, gt_ref, sc_ref, sh_ref, wpa_ref, wpb_ref, wo_ref,
                h_ref, n2_ref):
    f32, bf = jnp.float32, jnp.bfloat16
    z = z_ref[...]
    ob = ob_ref[...] * (z * _sigmoid(z))
    oa = oa_ref[0].reshape(A_WIDTH, oa_ref.shape[-1]).T
    ya = jnp.dot(oa.astype(bf), wpa_ref[...], preferred_element_type=f32)
    yb = jnp.dot(ob.astype(bf), wpb_ref[...], preferred_element_type=f32)
    merged = _sigmoid(ga_ref[...]) * ya + _sigmoid(gb_ref[...]) * yb
    y1 = jnp.dot(merged.astype(bf), wo_ref[...], preferred_element_type=f32)
    h = x_ref[...] + gt_ref[0] * y1
    h_ref[...] = h
    n2_ref[...] = _modulated_norm(h, sc_ref[0], sh_ref[0])


def merge(o_a, o_b, proj, x2, gt1, scale2, shift2, w_pa, w_pb, w_o, seq):
    n, d = x2.shape
    tm = PROJ_TM
    per = seq // tm
    bf = jnp.bfloat16
    vec = pl.BlockSpec((1, 1, d), lambda i: (i // per, 0, 0))

    def col(name):
        off, w = PROJ_DST[name]
        return pl.BlockSpec((tm, w), lambda i: (i, off // w))
    row = lambda w: pl.BlockSpec((tm, w), lambda i: (i, 0))
    res = lambda a, b: pl.BlockSpec((a, b), lambda i: (0, 0))
    return pl.pallas_call(
        _merge_body, grid=(n // tm,),
        in_specs=[pl.BlockSpec((1, A_HEADS, A_HEAD_DIM, tm), lambda i: (i // per, 0, 0, i % per)), row(B_V_WIDTH), col("zb"), col("gate_a"), col("gate_b"), row(d), vec, vec, vec,
                  res(A_WIDTH, d), res(B_V_WIDTH, d), res(d, d)],
        out_specs=[row(d), row(d)],
        out_shape=[jax.ShapeDtypeStruct((n, d), jnp.float32), jax.ShapeDtypeStruct((n, d), jnp.float32)],
        compiler_params=pltpu.CompilerParams(dimension_semantics=("arbitrary",), vmem_limit_bytes=PROJ_VMEM_LIMIT),
        name="merge")(o_a, o_b, proj, proj, proj, x2, gt1, scale2, shift2,
                      w_pa.astype(bf), w_pb.astype(bf), w_o.astype(bf))


def token_mixers(proj, positions, conv_w, a_log, dt_bias, norm_b_w, bsz, seq):
    qit, qat, w, kit, kat, vat, bg = dsa_prep(proj, positions, bsz, seq)
    o_a = dsa_attention_pallas(qit, w, qat, kit, kat, vat)

    beta = jax.nn.sigmoid(bg[:, :B_HEADS])
    g = -jnp.exp(a_log.astype(jnp.float32))[:, None] * jax.nn.softplus(bg[:, B_HEADS:] + dt_bias.astype(jnp.float32)[:, None])
    o_b = gated_delta_rule_pallas(proj, conv_w, g, beta, norm_b_w, bsz, seq)
    return o_a, o_b


PEER_SLOTS = PEER_HEADS * PEER_TOPK
PEER_TB = 32
HALF_ROWS = 4
HI_MASK = -65536


def pack_table(tab):
    bits = lax.bitcast_convert_type(tab.astype(jnp.bfloat16), jnp.uint16).astype(jnp.uint32)
    half = tab.shape[1] // 2
    word = bits[:, :half] | (bits[:, half:] << 16)
    return lax.bitcast_convert_type(word, jnp.int32).reshape(tab.shape[0] * HALF_ROWS, 128)


def _table_row(tab_ref, rows, k):
    return _unpack(tab_ref[pl.ds(pl.multiple_of(rows[k], HALF_ROWS), HALF_ROWS), :])


def _unpack(w):
    lo = lax.bitcast_convert_type(lax.shift_left(w, 16), jnp.float32)
    hi = lax.bitcast_convert_type(w & jnp.int32(HI_MASK), jnp.float32)
    return lo, hi


def _peer_u_body(idx_ref, x_ref, gate_ref, tab_ref, o_ref, s_ref, sb_ref):
    def lane_sums(t):
        a = jnp.sum(sb_ref[t].T, axis=0, keepdims=True)
        o_ref[pl.ds(t, 1), :] = 0.5 * a * (1.0 + lax.erf(a * (2.0 ** -0.5))) * gate_ref[pl.ds(t, 1), :]

    sb_ref[0] = jnp.zeros(sb_ref.shape[1:], jnp.float32)

    def tok(t, carry):
        lane_sums(jnp.maximum(t - 1, 0))
        x8 = x_ref[pl.ds(t, 1), :].reshape(2 * HALF_ROWS, 128)
        xlo, xhi = x8[0:HALF_ROWS], x8[HALF_ROWS:2 * HALF_ROWS]
        rows = idx_ref.at[t]
        for k in range(PEER_SLOTS):
            lo, hi = _table_row(tab_ref, rows, k)
            s_ref[HALF_ROWS * k:HALF_ROWS * (k + 1), :] = lo * xlo + hi * xhi
        s4 = s_ref[pl.ds(0, PEER_SLOTS, stride=HALF_ROWS), :]
        for r in range(1, HALF_ROWS):
            s4 = s4 + s_ref[pl.ds(r, PEER_SLOTS, stride=HALF_ROWS), :]
        sb_ref[t] = s4
        return carry
    lax.fori_loop(0, PEER_TB, tok, 0, unroll=2)
    lane_sums(PEER_TB - 1)


def _peer_v_body(idx_ref, coef_ref, tab_ref, o_ref, cb_ref):
    def spread(t):
        return jnp.broadcast_to(coef_ref[pl.ds(t, 1), :], (PEER_SLOTS, 128)).T

    def tok(t, weights):
        nxt = spread(jnp.minimum(t + 1, PEER_TB - 1))
        cb_ref[...] = weights
        nacc = 2
        acc = [jnp.zeros((HALF_ROWS, 128), jnp.float32) for _ in range(2 * nacc)]
        rows = idx_ref.at[t]
        for k in range(PEER_SLOTS):
            lo, hi = _table_row(tab_ref, rows, k)
            c = jnp.broadcast_to(cb_ref[k:k + 1, :], (HALF_ROWS, 128))
            a = k % nacc
            acc[2 * a] = acc[2 * a] + c * lo
            acc[2 * a + 1] = acc[2 * a + 1] + c * hi
        half = HALF_ROWS * 128
        o_ref[pl.ds(t, 1), 0:half] = (acc[0] + acc[2]).reshape(1, half)
        o_ref[pl.ds(t, 1), half:2 * half] = (acc[1] + acc[3]).reshape(1, half)
        return nxt
    lax.fori_loop(0, PEER_TB, tok, spread(0))


def _table_spec():
    return pl.BlockSpec((PEER_N_EXPERTS * HALF_ROWS, 128), lambda i: (0, 0), pipeline_mode=pl.Buffered(1))


PEER_VMEM_LIMIT = 48 * 1024 * 1024


def peer_u(idx, x, gates, tab):
    n = idx.shape[0]
    tb = PEER_TB
    return pl.pallas_call(
        _peer_u_body, grid=(n // tb,),
        in_specs=[pl.BlockSpec((tb, PEER_SLOTS), lambda i: (i, 0), memory_space=pltpu.SMEM),
                  pl.BlockSpec((tb, x.shape[1]), lambda i: (i, 0)),
                  pl.BlockSpec((tb, PEER_SLOTS), lambda i: (i, 0)),
                  _table_spec()],
        out_specs=pl.BlockSpec((tb, PEER_SLOTS), lambda i: (i, 0)),
        out_shape=jax.ShapeDtypeStruct((n, PEER_SLOTS), jnp.float32),
        scratch_shapes=[pltpu.VMEM((HALF_ROWS * PEER_SLOTS, 128), jnp.float32),
                        pltpu.VMEM((tb, PEER_SLOTS, 128), jnp.float32)],
        compiler_params=pltpu.CompilerParams(dimension_semantics=("arbitrary",), vmem_limit_bytes=PEER_VMEM_LIMIT),
        name="peer_u")(idx, x, gates, tab)


def peer_v(idx, coef, tab):
    n = idx.shape[0]
    tb = PEER_TB
    return pl.pallas_call(
        _peer_v_body, grid=(n // tb,),
        in_specs=[pl.BlockSpec((tb, PEER_SLOTS), lambda i: (i, 0), memory_space=pltpu.SMEM),
                  pl.BlockSpec((tb, PEER_SLOTS), lambda i: (i, 0)),
                  _table_spec()],
        out_specs=pl.BlockSpec((tb, 2 * HALF_ROWS * 128), lambda i: (i, 0)),
        out_shape=jax.ShapeDtypeStruct((n, 2 * HALF_ROWS * 128), jnp.float32),
        scratch_shapes=[pltpu.VMEM((PEER_SLOTS, 128), jnp.float32)],
        compiler_params=pltpu.CompilerParams(dimension_semantics=("arbitrary",), vmem_limit_bytes=PEER_VMEM_LIMIT),
        name="peer_v")(idx, coef, tab)


PEER_TT = 256
PAD_ID = 2**30


def _extract_top(ref, n_out, rid=None):
    rows, t = ref.shape
    if rid is None:
        rid = lax.broadcasted_iota(jnp.int32, (rows, t), 0)
    vals, idxs = [], []
    for _ in range(n_out):
        s = ref[...]
        m = jnp.max(s, axis=0, keepdims=True)
        ix = jnp.min(jnp.where(s == m, rid, jnp.int32(PAD_ID)), axis=0, keepdims=True)
        ref[...] = jnp.where(rid == ix, -jnp.inf, s)
        vals.append(m)
        idxs.append(ix)
    return vals, idxs


def _peer_route_body(x_ref, wq_ref, sk_ref, cid_ref, idx_ref, gate_ref, s_ref, cand_ref, v_ref, i_ref, et_ref, gt_ref):
    kk, nk = PEER_TOPK, PEER_N_KEYS
    f32 = jnp.float32
    q = jnp.dot(x_ref[...].astype(jnp.bfloat16), wq_ref[...], preferred_element_type=f32).astype(jnp.bfloat16)
    nt = (((1,), (1,)), ((), ()))
    row16 = lax.broadcasted_iota(jnp.int32, (kk, PEER_TT), 0)
    for h in range(PEER_HEADS):
        qh = q[:, h * PEER_KEY_DIM:(h + 1) * PEER_KEY_DIM]
        s_ref[...] = lax.dot_general(sk_ref[h], qh, nt, preferred_element_type=f32)
        for p in range(2):
            vals, idxs = _extract_top(s_ref.at[p * nk:(p + 1) * nk, :], kk)
            for i in range(kk):
                v_ref[p, i:i + 1, :] = vals[i]
                i_ref[p, i:i + 1, :] = idxs[i]
        off = 0
        for i in range(kk):
            n_j = kk // (i + 1)
            cand_ref[off:off + n_j, :] = v_ref[0, i:i + 1, :] + v_ref[1, 0:n_j, :]
            off += n_j
        cand_ref[off:, :] = jnp.full((cand_ref.shape[0] - off, PEER_TT), -jnp.inf, f32)
        vals, cis = _extract_top(cand_ref, kk, cid_ref[...])
        i1, i2 = i_ref[0], i_ref[1]
        es = [jnp.exp(v - vals[0]) for v in vals]
        den = es[0]
        for e in es[1:]:
            den = den + e
        for k in range(kk):
            ci = cis[k]
            e1 = jnp.sum(jnp.where(row16 == lax.shift_right_logical(ci, 4), i1, 0), axis=0, keepdims=True)
            e2 = jnp.sum(jnp.where(row16 == (ci & 15), i2, 0), axis=0, keepdims=True)
            et_ref[h * kk + k:h * kk + k + 1, :] = (e1 * nk + e2) * HALF_ROWS
            gt_ref[h * kk + k:h * kk + k + 1, :] = es[k] / den
    idx_ref[...] = lax.bitcast_convert_type(lax.bitcast_convert_type(et_ref[...], f32).T, jnp.int32)
    gate_ref[...] = gt_ref[...].T


def _pair_cells():
    kk = PEER_TOPK
    ids = [i * kk + j for i in range(kk) for j in range(kk // (i + 1))]
    return ids + [PAD_ID] * (-len(ids) % 8)


def peer_route(xn, wq, subkeys):
    n = xn.shape[0]
    tt = PEER_TT
    cells = _pair_cells()
    cid = jnp.broadcast_to(jnp.asarray(cells, jnp.int32)[:, None], (len(cells), tt))
    half = PEER_KEY_DIM // 2
    z = jnp.zeros((PEER_HEADS, PEER_N_KEYS, half), subkeys.dtype)
    skbd = jnp.concatenate([jnp.concatenate([subkeys[:, 0], z], axis=-1),
                            jnp.concatenate([z, subkeys[:, 1]], axis=-1)], axis=1).astype(jnp.bfloat16)
    return pl.pallas_call(
        _peer_route_body, grid=(n // tt,),
        in_specs=[pl.BlockSpec((tt, D_MODEL), lambda i: (i, 0)),
                  pl.BlockSpec((D_MODEL, PEER_HEADS * PEER_KEY_DIM), lambda i: (0, 0)),
                  pl.BlockSpec((PEER_HEADS, 2 * PEER_N_KEYS, PEER_KEY_DIM), lambda i: (0, 0, 0)),
                  pl.BlockSpec((len(cells), tt), lambda i: (0, 0))],
        out_specs=[pl.BlockSpec((tt, PEER_SLOTS), lambda i: (i, 0)), pl.BlockSpec((tt, PEER_SLOTS), lambda i: (i, 0))],
        out_shape=[jax.ShapeDtypeStruct((n, PEER_SLOTS), jnp.int32), jax.ShapeDtypeStruct((n, PEER_SLOTS), jnp.float32)],
        scratch_shapes=[pltpu.VMEM((2 * PEER_N_KEYS, tt), jnp.float32),
                        pltpu.VMEM((len(cells), tt), jnp.float32),
                        pltpu.VMEM((2, PEER_TOPK, tt), jnp.float32),
                        pltpu.VMEM((2, PEER_TOPK, tt), jnp.int32),
                        pltpu.VMEM((PEER_SLOTS, tt), jnp.int32),
                        pltpu.VMEM((PEER_SLOTS, tt), jnp.float32)],
        compiler_params=pltpu.CompilerParams(dimension_semantics=("arbitrary",), vmem_limit_bytes=32 * 1024 * 1024),
        name="peer_route")(xn, wq.astype(jnp.bfloat16), skbd, cid)


def peer_channel_mixer(xn, wq, subkeys, u_tab, v_tab):
    bsz, seq, d = xn.shape
    n_tok = bsz * seq
    x2 = xn.reshape(n_tok, d)
    idx, gates = peer_route(x2, wq, subkeys)
    coef = peer_u(idx, x2, gates, pack_table(u_tab))
    out = peer_v(idx, coef, pack_table(v_tab))
    return out.reshape(bsz, seq, d)


def _residual_body(h_ref, y_ref, gt_ref, o_ref):
    o_ref[...] = h_ref[...] + gt_ref[0] * y_ref[...]


def _final_norm_body(h_ref, y_ref, gt_ref, g_ref, o_ref):
    h = h_ref[...] + gt_ref[0] * y_ref[...]
    o_ref[...] = h * lax.rsqrt(jnp.mean(h * h, axis=-1, keepdims=True) + EPS) * g_ref[...]


def gated_residual(h2, y2, gt, seq, gain=None):
    n, d = h2.shape
    tm = min(1024, seq)
    per = seq // tm
    row = pl.BlockSpec((tm, d), lambda i: (i, 0))
    specs = [row, row, pl.BlockSpec((1, 1, d), lambda i: (i // per, 0, 0))]
    args = [h2, y2, gt]
    if gain is not None:
        specs.append(pl.BlockSpec((1, d), lambda i: (0, 0)))
        args.append(gain.astype(jnp.float32).reshape(1, d))
    return pl.pallas_call(
        _residual_body if gain is None else _final_norm_body, grid=(n // tm,),
        in_specs=specs, out_specs=row, out_shape=jax.ShapeDtypeStruct((n, d), h2.dtype),
        name="gated_residual" if gain is None else "final_norm")(*args)


def kernel(x, c, positions, w_ada, b_ada, w_in, conv_w, a_log, dt_bias, norm_b_w,
           w_pa, w_pb, w_o, peer_wq, peer_subkeys, peer_u, peer_v, final_norm_w):
    bsz, seq, d = x.shape
    h = x.reshape(bsz * seq, d)
    for layer in range(DEPTH):
        mod = ada_modulation(c, w_ada[layer], b_ada[layer])
        sh1, sc1, gt1, sh2, sc2, gt2 = [m.reshape(bsz, 1, d) for m in jnp.split(mod, 6, axis=-1)]
        proj = in_proj(h, 1.0 + sc1, sh1, permute_w_in(w_in[layer]), seq)
        o_a, o_b = token_mixers(proj, positions, conv_w[layer], a_log[layer], dt_bias[layer], norm_b_w[layer],
                                bsz, seq)
        h, n2 = merge(o_a, o_b.reshape(bsz * seq, B_V_WIDTH), proj, h, gt1,
                      1.0 + sc2, sh2, w_pa[layer], w_pb[layer], w_o[layer], seq)
        y2 = peer_channel_mixer(n2.reshape(bsz, seq, d), peer_wq[layer], peer_subkeys[layer], peer_u[layer],
                                peer_v[layer])
        last = layer == DEPTH - 1
        h = gated_residual(h, y2.reshape(bsz * seq, d), gt2, seq, final_norm_w if last else None)
    return h.reshape(bsz, seq, d)
```

```python
import functools

import jax, jax.numpy as jnp
from jax import lax
from jax.experimental import pallas as pl
from jax.experimental.pallas import tpu as pltpu

D_MODEL = 1024
DEPTH = 1

A_HEADS = 8
A_KV_HEADS = 2
A_HEAD_DIM = 64
IDX_HEADS = 16
IDX_DIM = 64
IDX_TOPK_MAX = 256
B_HEADS = 8
B_KEY_DIM = 64
B_VAL_DIM = 64
CONV_WIDTH = 4
CHUNK = 64
ROPE_THETA = 500000.0
ROPE_FRACTION_DEN = 4
PEER_HEADS = 8
PEER_KEY_DIM = 128
PEER_N_KEYS = 128
PEER_N_EXPERTS = PEER_N_KEYS * PEER_N_KEYS
PEER_TOPK = 16
EPS = 1e-6

A_WIDTH = A_HEADS * A_HEAD_DIM
KV_WIDTH = A_KV_HEADS * A_HEAD_DIM
B_QK_WIDTH = B_HEADS * B_KEY_DIM
B_V_WIDTH = B_HEADS * B_VAL_DIM
IN_SPLITS = (A_WIDTH, KV_WIDTH, KV_WIDTH, IDX_HEADS * IDX_DIM, IDX_DIM, IDX_HEADS,
             B_QK_WIDTH, B_QK_WIDTH, B_V_WIDTH, B_V_WIDTH, B_HEADS, B_HEADS, D_MODEL, D_MODEL)


DSA_TQ = 256
DSA_TK = 256
INT_MIN = -2**31
NEG_BIG = -1e30


def _dsa_body(topk, qit_ref, w_ref, qat_ref, ki_ref, ka_ref, vat_ref, o_ref,
              key_ref):
    tq, tk = DSA_TQ, DSA_TK
    qb = pl.program_id(1)
    n_kv = qb + 1
    t_glob = qb * tq + lax.broadcasted_iota(jnp.int32, (1, tq), 1)
    row = lax.broadcasted_iota(jnp.int32, (tk, 1), 0)
    f32 = jnp.float32

    def p1(j, carry):
        kt = ki_ref[0, j]
        score = jnp.zeros((tk, tq), f32)
        for h in range(IDX_HEADS):
            lt = jnp.dot(kt, qit_ref[0, h], preferred_element_type=f32)
            score = score + w_ref[0, h:h + 1, :] * jnp.maximum(lt, 0.0)
        bits = lax.bitcast_convert_type(score + 0.0, jnp.int32)
        skey = jnp.where(bits >= 0, bits, bits ^ jnp.int32(0x7FFFFFFF))
        skey = jnp.where(j * tk + row <= t_glob, skey, jnp.int32(INT_MIN))
        key_ref[j] = skey
        return carry
    lax.fori_loop(0, n_kv, p1, 0)

    def count(pred):
        def body(j, acc):
            hit = jnp.where(pred(key_ref[j], j * tk + row), 1.0, 0.0)
            return acc + jnp.sum(hit.reshape(tk // 8, 8, tq), axis=0)
        acc = lax.fori_loop(0, n_kv, body, jnp.zeros((8, tq), f32))
        return jnp.sum(acc, axis=0, keepdims=True)

    kf = jnp.float32(topk)

    def bit_step(state):
        i, ku, settled = state
        cand_u = ku | lax.shift_left(jnp.int32(1), 31 - i)
        cand = cand_u ^ jnp.int32(INT_MIN)
        c = count(lambda k, s: k >= cand)
        return i + 1, jnp.where(c >= kf, cand_u, ku), jnp.maximum(settled, jnp.where(c == kf, 1.0, 0.0))

    def unsettled(state):
        i, _, settled = state
        return jnp.logical_and(i < 32, jnp.min(settled) < 1.0)
    settled0 = jnp.where(t_glob + 1 < topk, 1.0, 0.0)
    _, ku, _ = lax.while_loop(unsettled, bit_step, (jnp.int32(0), jnp.zeros((1, tq), jnp.int32), settled0))
    kth = ku ^ jnp.int32(INT_MIN)
    c_gt = count(lambda k, s: k > kth)
    c_ge = count(lambda k, s: k >= kth)
    short = kth == jnp.int32(INT_MIN)
    x0 = jnp.where(short, jnp.int32(-1), jnp.int32(2**30))
    need = kf - c_gt
    has_tie = jnp.max(jnp.where(jnp.logical_and(c_ge > kf, jnp.logical_not(short)), 1.0, 0.0)) > 0.0

    def tie_search():
        def step(i, x):
            bit = lax.shift_left(jnp.int32(1), 11 - i)
            probe = x + bit - 1
            c = count(lambda k, s: jnp.logical_and(k == kth, s <= probe))
            return jnp.where(c < need, x + bit, x)
        x = lax.fori_loop(0, 12, step, jnp.zeros((1, tq), jnp.int32))
        return jnp.where(short, jnp.int32(-1), x)
    x_lim = lax.cond(has_tie, tie_search, lambda: x0)

    rep = A_HEADS // A_KV_HEADS
    hs = range(A_HEADS)

    def p3(j, carry):
        m, l, acc = carry
        skey = key_ref[j]
        s_idx = j * tk + row
        sel = jnp.logical_or(skey > kth, jnp.logical_and(skey == kth, s_idx <= x_lim))
        kt = [ka_ref[0, g, j] for g in range(A_KV_HEADS)]
        vt = [vat_ref[0, g, j] for g in range(A_KV_HEADS)]
        s = [jnp.where(sel, jnp.dot(kt[h // rep], qat_ref[0, h], preferred_element_type=f32), NEG_BIG) for h in hs]
        m_new = [jnp.maximum(m[h], jnp.max(s[h], axis=0, keepdims=True)) for h in hs]
        alpha = [jnp.exp(m[h] - m_new[h]) for h in hs]
        p = [jnp.exp(s[h] - m_new[h]) for h in hs]
        l_new = [alpha[h] * l[h] + jnp.sum(p[h], axis=0, keepdims=True) for h in hs]
        acc_new = [alpha[h] * acc[h] + jnp.dot(vt[h // rep], p[h].astype(jnp.bfloat16), preferred_element_type=f32)
                   for h in hs]
        return tuple(m_new), tuple(l_new), tuple(acc_new)

    init = (tuple(jnp.full((1, tq), NEG_BIG, f32) for _ in hs), tuple(jnp.zeros((1, tq), f32) for _ in hs),
            tuple(jnp.zeros((A_HEAD_DIM, tq), f32) for _ in hs))
    _, l_fin, acc_fin = lax.fori_loop(0, n_kv, p3, init)
    for h in hs:
        o_ref[0, h] = acc_fin[h] / l_fin[h]


ROPE_HALF = A_HEAD_DIM // ROPE_FRACTION_DEN // 2


def _rope_t(xt, n_heads, cos, sin, scale=None):
    outs = []
    for h in range(n_heads):
        b = h * A_HEAD_DIM
        x1, x2, rest = xt[b:b + ROPE_HALF], xt[b + ROPE_HALF:b + 2 * ROPE_HALF], xt[b + 2 * ROPE_HALF:b + A_HEAD_DIM]
        o = jnp.concatenate([x1 * cos - x2 * sin, x2 * cos + x1 * sin, rest], axis=0)
        outs.append(o if scale is None else o * scale)
    return outs


def _dsa_prep_body(qi_ref, qa_ref, ka_ref, va_ref, sm_ref, cos_ref, sin_ref,
                   qit_ref, qat_ref, w_ref, kit_ref, kat_ref, vat_ref, bg_ref):
    bf = jnp.bfloat16
    cos, sin = cos_ref[0], sin_ref[0]
    for h, o in enumerate(_rope_t(qi_ref[...].T, IDX_HEADS, cos, sin)):
        qit_ref[0, h] = o.astype(bf)
    for h, o in enumerate(_rope_t(qa_ref[...].T, A_HEADS, cos, sin, A_HEAD_DIM ** -0.5)):
        qat_ref[0, h] = o.astype(bf)
    kat = _rope_t(ka_ref[...].T, A_KV_HEADS, cos, sin)
    vt = va_ref[...].T
    for g in range(A_KV_HEADS):
        kat_ref[0, g, 0] = kat[g].T.astype(bf)
        vat_ref[0, g, 0] = vt[g * A_HEAD_DIM:(g + 1) * A_HEAD_DIM].astype(bf)
    smt = sm_ref[...].T
    kit_ref[0, 0] = _rope_t(smt[0:IDX_DIM], 1, cos, sin)[0].T.astype(bf)
    w_ref[0] = smt[IDX_DIM:IDX_DIM + IDX_HEADS] * ((IDX_HEADS ** -0.5) * (IDX_DIM ** -0.5))
    bg_ref[0] = smt[IDX_DIM + IDX_HEADS:IDX_DIM + IDX_HEADS + 2 * B_HEADS]


def dsa_prep(proj, positions, bsz, seq):
    t = DSA_TQ
    per = seq // t
    assert DSA_TK == t and PROJ_ORDER[-4:] == ("ki", "wi", "bb", "ab") and PROJ_DST["ki"][0] + LANES == PROJ_WIDTH
    rd = A_HEAD_DIM // ROPE_FRACTION_DEN
    inv_freq = jnp.power(jnp.float32(ROPE_THETA), -jnp.arange(ROPE_HALF, dtype=jnp.float32) * (2.0 / rd))
    ang = positions.astype(jnp.float32)[:, None, :] * inv_freq[None, :, None]

    def col(name, width=None):
        off, w = PROJ_DST[name]
        w = width or w
        return pl.BlockSpec((t, w), lambda i: (i, off // w))
    trig = pl.BlockSpec((1, ROPE_HALF, t), lambda i: (i // per, 0, i % per))
    bf, f32 = jnp.bfloat16, jnp.float32
    sd = jax.ShapeDtypeStruct
    n_kv = seq // DSA_TK
    return pl.pallas_call(
        _dsa_prep_body, grid=(bsz * per,),
        in_specs=[col("qi"), col("qa"), col("ka"), col("va"), col("ki", LANES), trig, trig],
        out_specs=[pl.BlockSpec((1, IDX_HEADS, IDX_DIM, t), lambda i: (i // per, 0, 0, i % per)),
                   pl.BlockSpec((1, A_HEADS, A_HEAD_DIM, t), lambda i: (i // per, 0, 0, i % per)),
                   pl.BlockSpec((1, IDX_HEADS, t), lambda i: (i // per, 0, i % per)),
                   pl.BlockSpec((1, 1, t, IDX_DIM), lambda i: (i // per, i % per, 0, 0)),
                   pl.BlockSpec((1, A_KV_HEADS, 1, t, A_HEAD_DIM), lambda i: (i // per, 0, i % per, 0, 0)),
                   pl.BlockSpec((1, A_KV_HEADS, 1, A_HEAD_DIM, t), lambda i: (i // per, 0, i % per, 0, 0)),
                   pl.BlockSpec((1, 2 * B_HEADS, t), lambda i: (i // per, 0, i % per))],
        out_shape=[sd((bsz, IDX_HEADS, IDX_DIM, seq), bf), sd((bsz, A_HEADS, A_HEAD_DIM, seq), bf),
                   sd((bsz, IDX_HEADS, seq), f32), sd((bsz, n_kv, DSA_TK, IDX_DIM), bf),
                   sd((bsz, A_KV_HEADS, n_kv, DSA_TK, A_HEAD_DIM), bf),
                   sd((bsz, A_KV_HEADS, n_kv, A_HEAD_DIM, DSA_TK), bf), sd((bsz, 2 * B_HEADS, seq), f32)],
        compiler_params=pltpu.CompilerParams(dimension_semantics=("arbitrary",), vmem_limit_bytes=32 * 1024 * 1024),
        name="dsa_prep")(proj, proj, proj, proj, proj, jnp.cos(ang), jnp.sin(ang))


def dsa_attention_pallas(qit, w, qat, kit, kat, vat):
    bsz, seq = qit.shape[0], qit.shape[-1]
    tq, tk = DSA_TQ, DSA_TK
    topk = min(IDX_TOPK_MAX, seq // 4)
    n_kv = seq // tk
    return pl.pallas_call(
        functools.partial(_dsa_body, topk),
        grid=(bsz, seq // tq),
        in_specs=[
            pl.BlockSpec((1, IDX_HEADS, IDX_DIM, tq), lambda b, q: (b, 0, 0, q)),
            pl.BlockSpec((1, IDX_HEADS, tq), lambda b, q: (b, 0, q)),
            pl.BlockSpec((1, A_HEADS, A_HEAD_DIM, tq), lambda b, q: (b, 0, 0, q)),
            pl.BlockSpec((1, n_kv, tk, IDX_DIM), lambda b, q: (b, 0, 0, 0)),
            pl.BlockSpec((1, A_KV_HEADS, n_kv, tk, A_HEAD_DIM), lambda b, q: (b, 0, 0, 0, 0)),
            pl.BlockSpec((1, A_KV_HEADS, n_kv, A_HEAD_DIM, tk), lambda b, q: (b, 0, 0, 0, 0)),
        ],
        out_specs=pl.BlockSpec((1, A_HEADS, A_HEAD_DIM, tq), lambda b, q: (b, 0, 0, q)),
        out_shape=jax.ShapeDtypeStruct((bsz, A_HEADS, A_HEAD_DIM, seq), jnp.float32),
        scratch_shapes=[
            pltpu.VMEM((n_kv, tk, tq), jnp.int32),
        ],
        compiler_params=pltpu.CompilerParams(dimension_semantics=("arbitrary", "arbitrary"),
                                             vmem_limit_bytes=48 * 1024 * 1024),
        name="dsa_attention",
    )(qit, w, qat, kit, kat, vat)


GDN_G = 8
GDN_BASE = 8
GDN_PAD = 8


def _gdn_body(q_ref, k_ref, v_ref, cw_ref, gc_ref, bt_ref, nw_ref, o_ref, s_ref, xs_ref, cs_ref):
    c_sz = CHUNK
    f32, bf = jnp.float32, jnp.bfloat16

    rows = GDN_G * c_sz
    halo = CONV_WIDTH - 1

    @pl.when(pl.program_id(1) == 0)
    def _():
        s_ref[...] = jnp.zeros(s_ref.shape, f32)
        xs_ref[:, 0:GDN_PAD, :] = jnp.zeros((3, GDN_PAD, xs_ref.shape[-1]), f32)

    for j, ref in enumerate((q_ref, k_ref, v_ref)):
        xs_ref[j, GDN_PAD:GDN_PAD + rows, :] = ref[0]
        acc = xs_ref[j, GDN_PAD - halo:GDN_PAD - halo + rows, :] * cw_ref[j, 0:1, :]
        for i in range(1, CONV_WIDTH):
            acc = acc + xs_ref[j, GDN_PAD - halo + i:GDN_PAD - halo + i + rows, :] * cw_ref[j, i:i + 1, :]
        cs_ref[j] = acc * _sigmoid(acc)
        xs_ref[j, GDN_PAD - halo:GDN_PAD, :] = xs_ref[j, GDN_PAD + rows - halo:GDN_PAD + rows, :]

    ri = lax.broadcasted_iota(jnp.int32, (c_sz, c_sz), 0)
    ci = lax.broadcasted_iota(jnp.int32, (c_sz, c_sz), 1)
    incl, strict = ri >= ci, ri > ci
    eye = jnp.where(ri == ci, 1.0, 0.0).astype(f32)
    blk = lambda w: (ri // w) == (ci // w)
    diag8 = blk(GDN_BASE)
    sub_blocks = []
    w = GDN_BASE
    while w < c_sz:
        sub_blocks.append(jnp.logical_and(blk(2 * w), jnp.logical_not(blk(w))))
        w *= 2
    nt = (((1,), (1,)), ((), ()))
    dot = lambda a, b: jnp.dot(a.astype(bf), b.astype(bf), preferred_element_type=f32)
    dot_nt = lambda a, b: lax.dot_general(a.astype(bf), b.astype(bf), nt, preferred_element_type=f32)

    def split(a):
        hi = a.astype(bf)
        return hi, (a - hi.astype(f32)).astype(bf)

    def hp(a, b):
        a_hi, a_lo = split(a)
        b_hi, b_lo = split(b)
        mm = lambda u, w: jnp.dot(u, w, preferred_element_type=f32)
        return mm(a_hi, b_hi) + (mm(a_hi, b_lo) + mm(a_lo, b_hi))

    hs = range(B_HEADS)

    def chunk_pair(i, carry):
        cs = [2 * i, 2 * i + 1]
        r0 = [pl.multiple_of(c * c_sz, c_sz) for c in cs]
        items = [(j, h) for j in range(2) for h in hs]
        hd = lambda a, j, h: cs_ref[a, pl.ds(r0[j], c_sz), h * B_KEY_DIM:(h + 1) * B_KEY_DIM]
        l2n = lambda t: t * lax.rsqrt(jnp.sum(t * t, axis=-1, keepdims=True) + EPS)
        q = [l2n(hd(0, j, h)) * (B_KEY_DIM ** -0.5) for j, h in items]
        k = [l2n(hd(1, j, h)) for j, h in items]
        v = [hd(2, j, h) for j, h in items]
        gc8 = [gc_ref[0, c] for c in cs]
        gct = [g8.T for g8 in gc8]
        btt = [bt_ref[0, c].T for c in cs]
        gcr = [gc8[j][h:h + 1, :] for j, h in items]
        n_it = range(len(items))
        gcc = [gct[j][:, h:h + 1] for j, h in items]
        beta = [btt[j][:, h:h + 1] for j, h in items]
        decay = [jnp.exp(jnp.where(incl, gcc[n] - gcr[n], -jnp.inf)) for n in n_it]
        kb = [k[n] * beta[n] for n in n_it]
        vb = [v[n] * beta[n] for n in n_it]
        low = [jnp.where(strict, dot_nt(kb[n], k[n]) * decay[n], 0.0) for n in n_it]
        dg = [jnp.where(diag8, low[n], 0.0) for n in n_it]
        t = [eye - dg[n] for n in n_it]
        p = [hp(dg[n], dg[n]) for n in n_it]
        t = [hp(t[n], eye + p[n]) for n in n_it]
        p = [hp(p[n], p[n]) for n in n_it]
        t = [hp(t[n], eye + p[n]) for n in n_it]
        for below in sub_blocks:
            lb = [jnp.where(below, low[n], 0.0) for n in n_it]
            lt = [hp(lb[n], t[n]) for n in n_it]
            t = [t[n] - hp(t[n], lt[n]) for n in n_it]
        u = [dot(t[n], vb[n]) for n in n_it]
        kcd = [dot(t[n], kb[n] * jnp.exp(gcc[n])) for n in n_it]
        intra = [dot_nt(q[n], k[n]) * decay[n] for n in n_it]
        qg = [q[n] * jnp.exp(gcc[n]) for n in n_it]
        glast = [g[:, c_sz - 1:c_sz] for g in gcr]
        kdt = [(k[n] * jnp.exp(glast[n] - gcc[n])).T for n in n_it]
        s = [s_ref[h] for h in hs]
        for j in range(2):
            ix = [j * B_HEADS + h for h in hs]
            v_new = [u[ix[h]] - dot(kcd[ix[h]], s[h]) for h in hs]
            out = [dot(qg[ix[h]], s[h]) + dot(intra[ix[h]], v_new[h]) for h in hs]
            s = [s[h] * jnp.exp(glast[ix[h]]) + dot(kdt[ix[h]], v_new[h]) for h in hs]
            for h in hs:
                o = out[h]
                o = o * lax.rsqrt(jnp.mean(o * o, axis=-1, keepdims=True) + EPS) * nw_ref[...]
                o_ref[0, pl.ds(r0[j], c_sz), h * B_VAL_DIM:(h + 1) * B_VAL_DIM] = o
        for h in hs:
            s_ref[h] = s[h]
        return carry
    lax.fori_loop(0, GDN_G // 2, chunk_pair, 0)


def gated_delta_rule_pallas(proj, conv_w, g, beta, norm_w, bsz, seq):
    nh, dk, dv = B_HEADS, B_KEY_DIM, B_VAL_DIM
    n = seq // CHUNK
    chunked = lambda t: jnp.moveaxis(t.reshape(bsz, nh, n, CHUNK), 1, 2)
    gc = jnp.cumsum(chunked(g), axis=-1)
    side = pl.BlockSpec((1, GDN_G, nh, CHUNK), lambda b, c: (b, c, 0, 0))
    rows = GDN_G * CHUNK
    width = nh * dk

    def col(name):
        off, w = PROJ_DST[name]
        assert w == width
        return pl.BlockSpec((1, rows, w), lambda b, c: (b, c, off // w))
    cw = jnp.transpose(conv_w.astype(jnp.float32).reshape(CONV_WIDTH, 3, width), (1, 0, 2))
    proj3 = proj.reshape(bsz, seq, proj.shape[-1])
    return pl.pallas_call(
        _gdn_body, grid=(bsz, n // GDN_G),
        in_specs=[col("qb"), col("kb"), col("vb"),
                  pl.BlockSpec((3, CONV_WIDTH, width), lambda b, c: (0, 0, 0)),
                  side, side, pl.BlockSpec((1, dv), lambda b, c: (0, 0))],
        out_specs=pl.BlockSpec((1, rows, nh * dv), lambda b, c: (b, c, 0)),
        out_shape=jax.ShapeDtypeStruct((bsz, seq, nh * dv), jnp.float32),
        scratch_shapes=[pltpu.VMEM((nh, dk, dv), jnp.float32),
                        pltpu.VMEM((3, GDN_PAD + rows, width), jnp.float32),
                        pltpu.VMEM((3, rows, width), jnp.float32)],
        compiler_params=pltpu.CompilerParams(dimension_semantics=("arbitrary", "arbitrary"),
                                             vmem_limit_bytes=40 * 1024 * 1024),
        name="gated_delta_rule")(proj3, proj3, proj3, cw, gc, chunked(beta), norm_w.astype(jnp.float32).reshape(1, dv))


IN_NAMES = ("qa", "ka", "va", "qi", "ki", "wi", "qb", "kb", "vb", "zb", "bb", "ab", "gate_a", "gate_b")
PROJ_ORDER = ("gate_a", "gate_b", "qi", "qa", "qb", "kb", "vb", "zb", "ka", "va", "ki", "wi", "bb", "ab")
LANES = 128


def _proj_layout():
    src, off = {}, 0
    for name, w in zip(IN_NAMES, IN_SPLITS):
        src[name] = (off, w)
        off += w
    dst, off = {}, 0
    for name in PROJ_ORDER:
        dst[name] = (off, src[name][1])
        off += src[name][1]
    return src, dst, -(-off // LANES) * LANES


PROJ_SRC, PROJ_DST, PROJ_WIDTH = _proj_layout()
PROJ_TM = 256
PROJ_VMEM_LIMIT = 48 * 1024 * 1024


def permute_w_in(w_in):
    cols = [w_in[:, PROJ_SRC[n][0]:PROJ_SRC[n][0] + PROJ_SRC[n][1]] for n in PROJ_ORDER]
    cols.append(jnp.zeros((w_in.shape[0], PROJ_WIDTH - sum(c.shape[1] for c in cols)), w_in.dtype))
    return jnp.concatenate(cols, axis=1).astype(jnp.bfloat16)


def _modulated_norm(x, scale, shift):
    return x * lax.rsqrt(jnp.mean(x * x, axis=-1, keepdims=True) + EPS) * scale + shift


def _sigmoid(v):
    return 1.0 / (1.0 + jnp.exp(-v))


def _ada_body(c_ref, w_ref, b_ref, o_ref):
    c = c_ref[...]
    o_ref[...] = jnp.dot((c * _sigmoid(c)).astype(jnp.bfloat16), w_ref[...].astype(jnp.bfloat16),
                         preferred_element_type=jnp.float32) + b_ref[...]


def ada_modulation(c, w_ada, b_ada):
    bsz, d = c.shape
    n = w_ada.shape[1]
    tn = D_MODEL
    return pl.pallas_call(
        _ada_body, grid=(n // tn,),
        in_specs=[pl.BlockSpec((bsz, d), lambda j: (0, 0)), pl.BlockSpec((d, tn), lambda j: (0, j)),
                  pl.BlockSpec((1, tn), lambda j: (0, j))],
        out_specs=pl.BlockSpec((bsz, tn), lambda j: (0, j)),
        out_shape=jax.ShapeDtypeStruct((bsz, n), jnp.float32),
        name="ada_modulation")(c, w_ada, b_ada.reshape(1, n))


def _in_proj_body(x_ref, sc_ref, sh_ref, w_ref, o_ref):
    n1 = _modulated_norm(x_ref[...], sc_ref[0], sh_ref[0])
    o_ref[...] = jnp.dot(n1.astype(jnp.bfloat16), w_ref[...], preferred_element_type=jnp.float32)


def in_proj(x2, scale, shift, w_perm, seq):
    n, d = x2.shape
    tm = PROJ_TM
    per = seq // tm
    vec = pl.BlockSpec((1, 1, d), lambda i: (i // per, 0, 0))
    return pl.pallas_call(
        _in_proj_body, grid=(n // tm,),
        in_specs=[pl.BlockSpec((tm, d), lambda i: (i, 0)), vec, vec,
                  pl.BlockSpec((d, PROJ_WIDTH), lambda i: (0, 0), pipeline_mode=pl.Buffered(1))],
        out_specs=pl.BlockSpec((tm, PROJ_WIDTH), lambda i: (i, 0)),
        out_shape=jax.ShapeDtypeStruct((n, PROJ_WIDTH), jnp.float32),
        compiler_params=pltpu.CompilerParams(dimension_semantics=("arbitrary",), vmem_limit_bytes=PROJ_VMEM_LIMIT),
        name="in_proj")(x2, scale, shift, w_perm)


def _merge_body(oa_ref, ob_ref, z_ref, ga_ref, gb_ref, x_ref, gt_ref, sc_ref, sh_ref, wpa_ref, wpb_ref, wo_ref,
                h_ref, n2_ref):
    f32, bf = jnp.float32, jnp.bfloat16
    z = z_ref[...]
    ob = ob_ref[...] * (z * _sigmoid(z))
    oa = oa_ref[0].reshape(A_WIDTH, oa_ref.shape[-1]).T
    ya = jnp.dot(oa.astype(bf), wpa_ref[...], preferred_element_type=f32)
    yb = jnp.dot(ob.astype(bf), wpb_ref[...], preferred_element_type=f32)
    merged = _sigmoid(ga_ref[...]) * ya + _sigmoid(gb_ref[...]) * yb
    y1 = jnp.dot(merged.astype(bf), wo_ref[...], preferred_element_type=f32)
    h = x_ref[...] + gt_ref[0] * y1
    h_ref[...] = h
    n2_ref[...] = _modulated_norm(h, sc_ref[0], sh_ref[0])


def merge(o_a, o_b, proj, x2, gt1, scale2, shift2, w_pa, w_pb, w_o, seq):
    n, d = x2.shape
    tm = PROJ_TM
    per = seq // tm
    bf = jnp.bfloat16
    vec = pl.BlockSpec((1, 1, d), lambda i: (i // per, 0, 0))

    def col(name):
        off, w = PROJ_DST[name]
        return pl.BlockSpec((tm, w), lambda i: (i, off // w))
    row = lambda w: pl.BlockSpec((tm, w), lambda i: (i, 0))
    res = lambda a, b: pl.BlockSpec((a, b), lambda i: (0, 0))
    return pl.pallas_call(
        _merge_body, grid=(n // tm,),
        in_specs=[pl.BlockSpec((1, A_HEADS, A_HEAD_DIM, tm), lambda i: (i // per, 0, 0, i % per)), row(B_V_WIDTH), col("zb"), col("gate_a"), col("gate_b"), row(d), vec, vec, vec,
                  res(A_WIDTH, d), res(B_V_WIDTH, d), res(d, d)],
        out_specs=[row(d), row(d)],
        out_shape=[jax.ShapeDtypeStruct((n, d), jnp.float32), jax.ShapeDtypeStruct((n, d), jnp.float32)],
        compiler_params=pltpu.CompilerParams(dimension_semantics=("arbitrary",), vmem_limit_bytes=PROJ_VMEM_LIMIT),
        name="merge")(o_a, o_b, proj, proj, proj, x2, gt1, scale2, shift2,
                      w_pa.astype(bf), w_pb.astype(bf), w_o.astype(bf))


def token_mixers(proj, positions, conv_w, a_log, dt_bias, norm_b_w, bsz, seq):
    qit, qat, w, kit, kat, vat, bg = dsa_prep(proj, positions, bsz, seq)
    o_a = dsa_attention_pallas(qit, w, qat, kit, kat, vat)

    beta = jax.nn.sigmoid(bg[:, :B_HEADS])
    g = -jnp.exp(a_log.astype(jnp.float32))[:, None] * jax.nn.softplus(bg[:, B_HEADS:] + dt_bias.astype(jnp.float32)[:, None])
    o_b = gated_delta_rule_pallas(proj, conv_w, g, beta, norm_b_w, bsz, seq)
    return o_a, o_b


PEER_SLOTS = PEER_HEADS * PEER_TOPK
PEER_TB = 32
HALF_ROWS = 4
HI_MASK = -65536


def pack_table(tab):
    bits = lax.bitcast_convert_type(tab.astype(jnp.bfloat16), jnp.uint16).astype(jnp.uint32)
    half = tab.shape[1] // 2
    word = bits[:, :half] | (bits[:, half:] << 16)
    return lax.bitcast_convert_type(word, jnp.int32).reshape(tab.shape[0] * HALF_ROWS, 128)


def _table_row(tab_ref, rows, k):
    return _unpack(tab_ref[pl.ds(pl.multiple_of(rows[k], HALF_ROWS), HALF_ROWS), :])


def _unpack(w):
    lo = lax.bitcast_convert_type(lax.shift_left(w, 16), jnp.float32)
    hi = lax.bitcast_convert_type(w & jnp.int32(HI_MASK), jnp.float32)
    return lo, hi


def _peer_u_body(idx_ref, x_ref, gate_ref, tab_ref, o_ref, s_ref, sb_ref):
    def lane_sums(t):
        a = jnp.sum(sb_ref[t].T, axis=0, keepdims=True)
        o_ref[pl.ds(t, 1), :] = 0.5 * a * (1.0 + lax.erf(a * (2.0 ** -0.5))) * gate_ref[pl.ds(t, 1), :]

    sb_ref[0] = jnp.zeros(sb_ref.shape[1:], jnp.float32)

    def tok(t, carry):
        lane_sums(jnp.maximum(t - 1, 0))
        x8 = x_ref[pl.ds(t, 1), :].reshape(2 * HALF_ROWS, 128)
        xlo, xhi = x8[0:HALF_ROWS], x8[HALF_ROWS:2 * HALF_ROWS]
        rows = idx_ref.at[t]
        for k in range(PEER_SLOTS):
            lo, hi = _table_row(tab_ref, rows, k)
            s_ref[HALF_ROWS * k:HALF_ROWS * (k + 1), :] = lo * xlo + hi * xhi
        s4 = s_ref[pl.ds(0, PEER_SLOTS, stride=HALF_ROWS), :]
        for r in range(1, HALF_ROWS):
            s4 = s4 + s_ref[pl.ds(r, PEER_SLOTS, stride=HALF_ROWS), :]
        sb_ref[t] = s4
        return carry
    lax.fori_loop(0, PEER_TB, tok, 0, unroll=2)
    lane_sums(PEER_TB - 1)


def _peer_v_body(idx_ref, coef_ref, tab_ref, o_ref, cb_ref):
    def spread(t):
        return jnp.broadcast_to(coef_ref[pl.ds(t, 1), :], (PEER_SLOTS, 128)).T

    def tok(t, weights):
        nxt = spread(jnp.minimum(t + 1, PEER_TB - 1))
        cb_ref[...] = weights
        nacc = 2
        acc = [jnp.zeros((HALF_ROWS, 128), jnp.float32) for _ in range(2 * nacc)]
        rows = idx_ref.at[t]
        for k in range(PEER_SLOTS):
            lo, hi = _table_row(tab_ref, rows, k)
            c = jnp.broadcast_to(cb_ref[k:k + 1, :], (HALF_ROWS, 128))
            a = k % nacc
            acc[2 * a] = acc[2 * a] + c * lo
            acc[2 * a + 1] = acc[2 * a + 1] + c * hi
        half = HALF_ROWS * 128
        o_ref[pl.ds(t, 1), 0:half] = (acc[0] + acc[2]).reshape(1, half)
        o_ref[pl.ds(t, 1), half:2 * half] = (acc[1] + acc[3]).reshape(1, half)
        return nxt
    lax.fori_loop(0, PEER_TB, tok, spread(0))


def _table_spec():
    return pl.BlockSpec((PEER_N_EXPERTS * HALF_ROWS, 128), lambda i: (0, 0), pipeline_mode=pl.Buffered(1))


PEER_VMEM_LIMIT = 48 * 1024 * 1024


def peer_u(idx, x, gates, tab):
    n = idx.shape[0]
    tb = PEER_TB
    return pl.pallas_call(
        _peer_u_body, grid=(n // tb,),
        in_specs=[pl.BlockSpec((tb, PEER_SLOTS), lambda i: (i, 0), memory_space=pltpu.SMEM),
                  pl.BlockSpec((tb, x.shape[1]), lambda i: (i, 0)),
                  pl.BlockSpec((tb, PEER_SLOTS), lambda i: (i, 0)),
                  _table_spec()],
        out_specs=pl.BlockSpec((tb, PEER_SLOTS), lambda i: (i, 0)),
        out_shape=jax.ShapeDtypeStruct((n, PEER_SLOTS), jnp.float32),
        scratch_shapes=[pltpu.VMEM((HALF_ROWS * PEER_SLOTS, 128), jnp.float32),
                        pltpu.VMEM((tb, PEER_SLOTS, 128), jnp.float32)],
        compiler_params=pltpu.CompilerParams(dimension_semantics=("arbitrary",), vmem_limit_bytes=PEER_VMEM_LIMIT),
        name="peer_u")(idx, x, gates, tab)


def peer_v(idx, coef, tab):
    n = idx.shape[0]
    tb = PEER_TB
    return pl.pallas_call(
        _peer_v_body, grid=(n // tb,),
        in_specs=[pl.BlockSpec((tb, PEER_SLOTS), lambda i: (i, 0), memory_space=pltpu.SMEM),
                  pl.BlockSpec((tb, PEER_SLOTS), lambda i: (i, 0)),
                  _table_spec()],
        out_specs=pl.BlockSpec((tb, 2 * HALF_ROWS * 128), lambda i: (i, 0)),
        out_shape=jax.ShapeDtypeStruct((n, 2 * HALF_ROWS * 128), jnp.float32),
        scratch_shapes=[pltpu.VMEM((PEER_SLOTS, 128), jnp.float32)],
        compiler_params=pltpu.CompilerParams(dimension_semantics=("arbitrary",), vmem_limit_bytes=PEER_VMEM_LIMIT),
        name="peer_v")(idx, coef, tab)


PEER_TT = 256
PAD_ID = 2**30


def _extract_top(ref, n_out, rid=None):
    rows, t = ref.shape
    if rid is None:
        rid = lax.broadcasted_iota(jnp.int32, (rows, t), 0)
    vals, idxs = [], []
    for _ in range(n_out):
        s = ref[...]
        m = jnp.max(s, axis=0, keepdims=True)
        ix = jnp.min(jnp.where(s == m, rid, jnp.int32(PAD_ID)), axis=0, keepdims=True)
        ref[...] = jnp.where(rid == ix, -jnp.inf, s)
        vals.append(m)
        idxs.append(ix)
    return vals, idxs


def _peer_route_body(x_ref, wq_ref, sk_ref, cid_ref, idx_ref, gate_ref, s_ref, cand_ref, v_ref, i_ref, et_ref, gt_ref):
    kk, nk = PEER_TOPK, PEER_N_KEYS
    f32 = jnp.float32
    q = jnp.dot(x_ref[...].astype(jnp.bfloat16), wq_ref[...], preferred_element_type=f32).astype(jnp.bfloat16)
    nt = (((1,), (1,)), ((), ()))
    row16 = lax.broadcasted_iota(jnp.int32, (kk, PEER_TT), 0)
    for h in range(PEER_HEADS):
        qh = q[:, h * PEER_KEY_DIM:(h + 1) * PEER_KEY_DIM]
        s_ref[...] = lax.dot_general(sk_ref[h], qh, nt, preferred_element_type=f32)
        for p in range(2):
            vals, idxs = _extract_top(s_ref.at[p * nk:(p + 1) * nk, :], kk)
            for i in range(kk):
                v_ref[p, i:i + 1, :] = vals[i]
                i_ref[p, i:i + 1, :] = idxs[i]
        off = 0
        for i in range(kk):
            n_j = kk // (i + 1)
            cand_ref[off:off + n_j, :] = v_ref[0, i:i + 1, :] + v_ref[1, 0:n_j, :]
            off += n_j
        cand_ref[off:, :] = jnp.full((cand_ref.shape[0] - off, PEER_TT), -jnp.inf, f32)
        vals, cis = _extract_top(cand_ref, kk, cid_ref[...])
        i1, i2 = i_ref[0], i_ref[1]
        es = [jnp.exp(v - vals[0]) for v in vals]
        den = es[0]
        for e in es[1:]:
            den = den + e
        for k in range(kk):
            ci = cis[k]
            e1 = jnp.sum(jnp.where(row16 == lax.shift_right_logical(ci, 4), i1, 0), axis=0, keepdims=True)
            e2 = jnp.sum(jnp.where(row16 == (ci & 15), i2, 0), axis=0, keepdims=True)
            et_ref[h * kk + k:h * kk + k + 1, :] = (e1 * nk + e2) * HALF_ROWS
            gt_ref[h * kk + k:h * kk + k + 1, :] = es[k] / den
    idx_ref[...] = lax.bitcast_convert_type(lax.bitcast_convert_type(et_ref[...], f32).T, jnp.int32)
    gate_ref[...] = gt_ref[...].T


def _pair_cells():
    kk = PEER_TOPK
    ids = [i * kk + j for i in range(kk) for j in range(kk // (i + 1))]
    return ids + [PAD_ID] * (-len(ids) % 8)


def peer_route(xn, wq, subkeys):
    n = xn.shape[0]
    tt = PEER_TT
    cells = _pair_cells()
    cid = jnp.broadcast_to(jnp.asarray(cells, jnp.int32)[:, None], (len(cells), tt))
    half = PEER_KEY_DIM // 2
    z = jnp.zeros((PEER_HEADS, PEER_N_KEYS, half), subkeys.dtype)
    skbd = jnp.concatenate([jnp.concatenate([subkeys[:, 0], z], axis=-1),
                            jnp.concatenate([z, subkeys[:, 1]], axis=-1)], axis=1).astype(jnp.bfloat16)
    return pl.pallas_call(
        _peer_route_body, grid=(n // tt,),
        in_specs=[pl.BlockSpec((tt, D_MODEL), lambda i: (i, 0)),
                  pl.BlockSpec((D_MODEL, PEER_HEADS * PEER_KEY_DIM), lambda i: (0, 0)),
                  pl.BlockSpec((PEER_HEADS, 2 * PEER_N_KEYS, PEER_KEY_DIM), lambda i: (0, 0, 0)),
                  pl.BlockSpec((len(cells), tt), lambda i: (0, 0))],
        out_specs=[pl.BlockSpec((tt, PEER_SLOTS), lambda i: (i, 0)), pl.BlockSpec((tt, PEER_SLOTS), lambda i: (i, 0))],
        out_shape=[jax.ShapeDtypeStruct((n, PEER_SLOTS), jnp.int32), jax.ShapeDtypeStruct((n, PEER_SLOTS), jnp.float32)],
        scratch_shapes=[pltpu.VMEM((2 * PEER_N_KEYS, tt), jnp.float32),
                        pltpu.VMEM((len(cells), tt), jnp.float32),
                        pltpu.VMEM((2, PEER_TOPK, tt), jnp.float32),
                        pltpu.VMEM((2, PEER_TOPK, tt), jnp.int32),
                        pltpu.VMEM((PEER_SLOTS, tt), jnp.int32),
                        pltpu.VMEM((PEER_SLOTS, tt), jnp.float32)],
        compiler_params=pltpu.CompilerParams(dimension_semantics=("arbitrary",), vmem_limit_bytes=32 * 1024 * 1024),
        name="peer_route")(xn, wq.astype(jnp.bfloat16), skbd, cid)


def peer_channel_mixer(xn, wq, subkeys, u_tab, v_tab):
    bsz, seq, d = xn.shape
    n_tok = bsz * seq
    x2 = xn.reshape(n_tok, d)
    idx, gates = peer_route(x2, wq, subkeys)
    coef = peer_u(idx, x2, gates, pack_table(u_tab))
    out = peer_v(idx, coef, pack_table(v_tab))
    return out.reshape(bsz, seq, d)


def _residual_body(h_ref, y_ref, gt_ref, o_ref):
    o_ref[...] = h_ref[...] + gt_ref[0] * y_ref[...]


def _final_norm_body(h_ref, y_ref, gt_ref, g_ref, o_ref):
    h = h_ref[...] + gt_ref[0] * y_ref[...]
    o_ref[...] = h * lax.rsqrt(jnp.mean(h * h, axis=-1, keepdims=True) + EPS) * g_ref[...]


def gated_residual(h2, y2, gt, seq, gain=None):
    n, d = h2.shape
    tm = min(1024, seq)
    per = seq // tm
    row = pl.BlockSpec((tm, d), lambda i: (i, 0))
    specs = [row, row, pl.BlockSpec((1, 1, d), lambda i: (i // per, 0, 0))]
    args = [h2, y2, gt]
    if gain is not None:
        specs.append(pl.BlockSpec((1, d), lambda i: (0, 0)))
        args.append(gain.astype(jnp.float32).reshape(1, d))
    return pl.pallas_call(
        _residual_body if gain is None else _final_norm_body, grid=(n // tm,),
        in_specs=specs, out_specs=row, out_shape=jax.ShapeDtypeStruct((n, d), h2.dtype),
        name="gated_residual" if gain is None else "final_norm")(*args)


def kernel(x, c, positions, w_ada, b_ada, w_in, conv_w, a_log, dt_bias, norm_b_w,
           w_pa, w_pb, w_o, peer_wq, peer_subkeys, peer_u, peer_v, final_norm_w):
    bsz, seq, d = x.shape
    h = x.reshape(bsz * seq, d)
    for layer in range(DEPTH):
        mod = ada_modulation(c, w_ada[layer], b_ada[layer])
        sh1, sc1, gt1, sh2, sc2, gt2 = [m.reshape(bsz, 1, d) for m in jnp.split(mod, 6, axis=-1)]
        proj = in_proj(h, 1.0 + sc1, sh1, permute_w_in(w_in[layer]), seq)
        o_a, o_b = token_mixers(proj, positions, conv_w[layer], a_log[layer], dt_bias[layer], norm_b_w[layer],
                                bsz, seq)
        h, n2 = merge(o_a, o_b.reshape(bsz * seq, B_V_WIDTH), proj, h, gt1,
                      1.0 + sc2, sh2, w_pa[layer], w_pb[layer], w_o[layer], seq)
        y2 = peer_channel_mixer(n2.reshape(bsz, seq, d), peer_wq[layer], peer_subkeys[layer], peer_u[layer],
                                peer_v[layer])
        last = layer == DEPTH - 1
        h = gated_residual(h, y2.reshape(bsz * seq, d), gt2, seq, final_norm_w if last else None)
    return h.reshape(bsz, seq, d)
```

```python
import functools

import jax, jax.numpy as jnp
from jax import lax
from jax.experimental import pallas as pl
from jax.experimental.pallas import tpu as pltpu

D_MODEL = 1024
DEPTH = 1

A_HEADS = 8
A_KV_HEADS = 2
A_HEAD_DIM = 64
IDX_HEADS = 16
IDX_DIM = 64
IDX_TOPK_MAX = 256
B_HEADS = 8
B_KEY_DIM = 64
B_VAL_DIM = 64
CONV_WIDTH = 4
CHUNK = 64
ROPE_THETA = 500000.0
ROPE_FRACTION_DEN = 4
PEER_HEADS = 8
PEER_KEY_DIM = 128
PEER_N_KEYS = 128
PEER_N_EXPERTS = PEER_N_KEYS * PEER_N_KEYS
PEER_TOPK = 16
EPS = 1e-6

A_WIDTH = A_HEADS * A_HEAD_DIM
KV_WIDTH = A_KV_HEADS * A_HEAD_DIM
B_QK_WIDTH = B_HEADS * B_KEY_DIM
B_V_WIDTH = B_HEADS * B_VAL_DIM
IN_SPLITS = (A_WIDTH, KV_WIDTH, KV_WIDTH, IDX_HEADS * IDX_DIM, IDX_DIM, IDX_HEADS,
             B_QK_WIDTH, B_QK_WIDTH, B_V_WIDTH, B_V_WIDTH, B_HEADS, B_HEADS, D_MODEL, D_MODEL)


DSA_TQ = 256
DSA_TK = 256
INT_MIN = -2**31
NEG_BIG = -1e30


def _dsa_body(topk, qit_ref, w_ref, qat_ref, ki_ref, ka_ref, vat_ref, o_ref,
              key_ref):
    tq, tk = DSA_TQ, DSA_TK
    qb = pl.program_id(1)
    n_kv = qb + 1
    t_glob = qb * tq + lax.broadcasted_iota(jnp.int32, (1, tq), 1)
    row = lax.broadcasted_iota(jnp.int32, (tk, 1), 0)
    f32 = jnp.float32

    def p1(j, carry):
        kt = ki_ref[0, j]
        score = jnp.zeros((tk, tq), f32)
        for h in range(IDX_HEADS):
            lt = jnp.dot(kt, qit_ref[0, h], preferred_element_type=f32)
            score = score + w_ref[0, h:h + 1, :] * jnp.maximum(lt, 0.0)
        bits = lax.bitcast_convert_type(score + 0.0, jnp.int32)
        skey = jnp.where(bits >= 0, bits, bits ^ jnp.int32(0x7FFFFFFF))
        skey = jnp.where(j * tk + row <= t_glob, skey, jnp.int32(INT_MIN))
        key_ref[j] = skey
        return carry
    lax.fori_loop(0, n_kv, p1, 0)

    def count(pred):
        def body(j, acc):
            hit = jnp.where(pred(key_ref[j], j * tk + row), 1.0, 0.0)
            return acc + jnp.sum(hit.reshape(tk // 8, 8, tq), axis=0)
        acc = lax.fori_loop(0, n_kv, body, jnp.zeros((8, tq), f32))
        return jnp.sum(acc, axis=0, keepdims=True)

    kf = jnp.float32(topk)

    def bit_step(i, ku):
        cand_u = ku | lax.shift_left(jnp.int32(1), 31 - i)
        cand = cand_u ^ jnp.int32(INT_MIN)
        c = count(lambda k, s: k >= cand)
        return jnp.where(c >= kf, cand_u, ku)
    ku = lax.fori_loop(0, 32, bit_step, jnp.zeros((1, tq), jnp.int32))
    kth = ku ^ jnp.int32(INT_MIN)
    c_gt = count(lambda k, s: k > kth)
    c_ge = count(lambda k, s: k >= kth)
    short = kth == jnp.int32(INT_MIN)
    x0 = jnp.where(short, jnp.int32(-1), jnp.int32(2**30))
    need = kf - c_gt
    has_tie = jnp.max(jnp.where(jnp.logical_and(c_ge > kf, jnp.logical_not(short)), 1.0, 0.0)) > 0.0

    def tie_search():
        def step(i, x):
            bit = lax.shift_left(jnp.int32(1), 11 - i)
            probe = x + bit - 1
            c = count(lambda k, s: jnp.logical_and(k == kth, s <= probe))
            return jnp.where(c < need, x + bit, x)
        x = lax.fori_loop(0, 12, step, jnp.zeros((1, tq), jnp.int32))
        return jnp.where(short, jnp.int32(-1), x)
    x_lim = lax.cond(has_tie, tie_search, lambda: x0)

    rep = A_HEADS // A_KV_HEADS
    hs = range(A_HEADS)

    def p3(j, carry):
        m, l, acc = carry
        skey = key_ref[j]
        s_idx = j * tk + row
        sel = jnp.logical_or(skey > kth, jnp.logical_and(skey == kth, s_idx <= x_lim))
        kt = [ka_ref[0, g, j] for g in range(A_KV_HEADS)]
        vt = [vat_ref[0, g, j] for g in range(A_KV_HEADS)]
        s = [jnp.where(sel, jnp.dot(kt[h // rep], qat_ref[0, h], preferred_element_type=f32), NEG_BIG) for h in hs]
        m_new = [jnp.maximum(m[h], jnp.max(s[h], axis=0, keepdims=True)) for h in hs]
        alpha = [jnp.exp(m[h] - m_new[h]) for h in hs]
        p = [jnp.exp(s[h] - m_new[h]) for h in hs]
        l_new = [alpha[h] * l[h] + jnp.sum(p[h], axis=0, keepdims=True) for h in hs]
        acc_new = [alpha[h] * acc[h] + jnp.dot(vt[h // rep], p[h].astype(jnp.bfloat16), preferred_element_type=f32)
                   for h in hs]
        return tuple(m_new), tuple(l_new), tuple(acc_new)

    init = (tuple(jnp.full((1, tq), NEG_BIG, f32) for _ in hs), tuple(jnp.zeros((1, tq), f32) for _ in hs),
            tuple(jnp.zeros((A_HEAD_DIM, tq), f32) for _ in hs))
    _, l_fin, acc_fin = lax.fori_loop(0, n_kv, p3, init)
    for h in hs:
        o_ref[0, h] = acc_fin[h] / l_fin[h]


ROPE_HALF = A_HEAD_DIM // ROPE_FRACTION_DEN // 2


def _rope_t(xt, n_heads, cos, sin, scale=None):
    outs = []
    for h in range(n_heads):
        b = h * A_HEAD_DIM
        x1, x2, rest = xt[b:b + ROPE_HALF], xt[b + ROPE_HALF:b + 2 * ROPE_HALF], xt[b + 2 * ROPE_HALF:b + A_HEAD_DIM]
        o = jnp.concatenate([x1 * cos - x2 * sin, x2 * cos + x1 * sin, rest], axis=0)
        outs.append(o if scale is None else o * scale)
    return outs


def _dsa_prep_body(qi_ref, qa_ref, ka_ref, va_ref, sm_ref, cos_ref, sin_ref,
                   qit_ref, qat_ref, w_ref, kit_ref, kat_ref, vat_ref, bg_ref):
    bf = jnp.bfloat16
    cos, sin = cos_ref[0], sin_ref[0]
    for h, o in enumerate(_rope_t(qi_ref[...].T, IDX_HEADS, cos, sin)):
        qit_ref[0, h] = o.astype(bf)
    for h, o in enumerate(_rope_t(qa_ref[...].T, A_HEADS, cos, sin, A_HEAD_DIM ** -0.5)):
        qat_ref[0, h] = o.astype(bf)
    kat = _rope_t(ka_ref[...].T, A_KV_HEADS, cos, sin)
    vt = va_ref[...].T
    for g in range(A_KV_HEADS):
        kat_ref[0, g, 0] = kat[g].T.astype(bf)
        vat_ref[0, g, 0] = vt[g * A_HEAD_DIM:(g + 1) * A_HEAD_DIM].astype(bf)
    smt = sm_ref[...].T
    kit_ref[0, 0] = _rope_t(smt[0:IDX_DIM], 1, cos, sin)[0].T.astype(bf)
    w_ref[0] = smt[IDX_DIM:IDX_DIM + IDX_HEADS] * ((IDX_HEADS ** -0.5) * (IDX_DIM ** -0.5))
    bg_ref[0] = smt[IDX_DIM + IDX_HEADS:IDX_DIM + IDX_HEADS + 2 * B_HEADS]


def dsa_prep(proj, positions, bsz, seq):
    t = DSA_TQ
    per = seq // t
    assert DSA_TK == t and PROJ_ORDER[-4:] == ("ki", "wi", "bb", "ab") and PROJ_DST["ki"][0] + LANES == PROJ_WIDTH
    rd = A_HEAD_DIM // ROPE_FRACTION_DEN
    inv_freq = jnp.power(jnp.float32(ROPE_THETA), -jnp.arange(ROPE_HALF, dtype=jnp.float32) * (2.0 / rd))
    ang = positions.astype(jnp.float32)[:, None, :] * inv_freq[None, :, None]

    def col(name, width=None):
        off, w = PROJ_DST[name]
        w = width or w
        return pl.BlockSpec((t, w), lambda i: (i, off // w))
    trig = pl.BlockSpec((1, ROPE_HALF, t), lambda i: (i // per, 0, i % per))
    bf, f32 = jnp.bfloat16, jnp.float32
    sd = jax.ShapeDtypeStruct
    n_kv = seq // DSA_TK
    return pl.pallas_call(
        _dsa_prep_body, grid=(bsz * per,),
        in_specs=[col("qi"), col("qa"), col("ka"), col("va"), col("ki", LANES), trig, trig],
        out_specs=[pl.BlockSpec((1, IDX_HEADS, IDX_DIM, t), lambda i: (i // per, 0, 0, i % per)),
                   pl.BlockSpec((1, A_HEADS, A_HEAD_DIM, t), lambda i: (i // per, 0, 0, i % per)),
                   pl.BlockSpec((1, IDX_HEADS, t), lambda i: (i // per, 0, i % per)),
                   pl.BlockSpec((1, 1, t, IDX_DIM), lambda i: (i // per, i % per, 0, 0)),
                   pl.BlockSpec((1, A_KV_HEADS, 1, t, A_HEAD_DIM), lambda i: (i // per, 0, i % per, 0, 0)),
                   pl.BlockSpec((1, A_KV_HEADS, 1, A_HEAD_DIM, t), lambda i: (i // per, 0, i % per, 0, 0)),
                   pl.BlockSpec((1, 2 * B_HEADS, t), lambda i: (i // per, 0, i % per))],
        out_shape=[sd((bsz, IDX_HEADS, IDX_DIM, seq), bf), sd((bsz, A_HEADS, A_HEAD_DIM, seq), bf),
                   sd((bsz, IDX_HEADS, seq), f32), sd((bsz, n_kv, DSA_TK, IDX_DIM), bf),
                   sd((bsz, A_KV_HEADS, n_kv, DSA_TK, A_HEAD_DIM), bf),
                   sd((bsz, A_KV_HEADS, n_kv, A_HEAD_DIM, DSA_TK), bf), sd((bsz, 2 * B_HEADS, seq), f32)],
        compiler_params=pltpu.CompilerParams(dimension_semantics=("arbitrary",), vmem_limit_bytes=32 * 1024 * 1024),
        name="dsa_prep")(proj, proj, proj, proj, proj, jnp.cos(ang), jnp.sin(ang))


def dsa_attention_pallas(qit, w, qat, kit, kat, vat):
    bsz, seq = qit.shape[0], qit.shape[-1]
    tq, tk = DSA_TQ, DSA_TK
    topk = min(IDX_TOPK_MAX, seq // 4)
    n_kv = seq // tk
    return pl.pallas_call(
        functools.partial(_dsa_body, topk),
        grid=(bsz, seq // tq),
        in_specs=[
            pl.BlockSpec((1, IDX_HEADS, IDX_DIM, tq), lambda b, q: (b, 0, 0, q)),
            pl.BlockSpec((1, IDX_HEADS, tq), lambda b, q: (b, 0, q)),
            pl.BlockSpec((1, A_HEADS, A_HEAD_DIM, tq), lambda b, q: (b, 0, 0, q)),
            pl.BlockSpec((1, n_kv, tk, IDX_DIM), lambda b, q: (b, 0, 0, 0)),
            pl.BlockSpec((1, A_KV_HEADS, n_kv, tk, A_HEAD_DIM), lambda b, q: (b, 0, 0, 0, 0)),
            pl.BlockSpec((1, A_KV_HEADS, n_kv, A_HEAD_DIM, tk), lambda b, q: (b, 0, 0, 0, 0)),
        ],
        out_specs=pl.BlockSpec((1, A_HEADS, A_HEAD_DIM, tq), lambda b, q: (b, 0, 0, q)),
        out_shape=jax.ShapeDtypeStruct((bsz, A_HEADS, A_HEAD_DIM, seq), jnp.float32),
        scratch_shapes=[
            pltpu.VMEM((n_kv, tk, tq), jnp.int32),
        ],
        compiler_params=pltpu.CompilerParams(dimension_semantics=("arbitrary", "arbitrary"),
                                             vmem_limit_bytes=48 * 1024 * 1024),
        name="dsa_attention",
    )(qit, w, qat, kit, kat, vat)


GDN_G = 8
GDN_BASE = 8
GDN_PAD = 8


def _gdn_body(q_ref, k_ref, v_ref, cw_ref, gc_ref, bt_ref, nw_ref, o_ref, s_ref, xs_ref, cs_ref):
    c_sz = CHUNK
    f32, bf = jnp.float32, jnp.bfloat16

    rows = GDN_G * c_sz
    halo = CONV_WIDTH - 1

    @pl.when(pl.program_id(1) == 0)
    def _():
        s_ref[...] = jnp.zeros(s_ref.shape, f32)
        xs_ref[:, 0:GDN_PAD, :] = jnp.zeros((3, GDN_PAD, xs_ref.shape[-1]), f32)

    for j, ref in enumerate((q_ref, k_ref, v_ref)):
        xs_ref[j, GDN_PAD:GDN_PAD + rows, :] = ref[0]
        acc = xs_ref[j, GDN_PAD - halo:GDN_PAD - halo + rows, :] * cw_ref[j, 0:1, :]
        for i in range(1, CONV_WIDTH):
            acc = acc + xs_ref[j, GDN_PAD - halo + i:GDN_PAD - halo + i + rows, :] * cw_ref[j, i:i + 1, :]
        cs_ref[j] = acc * _sigmoid(acc)
        xs_ref[j, GDN_PAD - halo:GDN_PAD, :] = xs_ref[j, GDN_PAD + rows - halo:GDN_PAD + rows, :]

    ri = lax.broadcasted_iota(jnp.int32, (c_sz, c_sz), 0)
    ci = lax.broadcasted_iota(jnp.int32, (c_sz, c_sz), 1)
    incl, strict = ri >= ci, ri > ci
    eye = jnp.where(ri == ci, 1.0, 0.0).astype(f32)
    blk = lambda w: (ri // w) == (ci // w)
    diag8 = blk(GDN_BASE)
    sub_blocks = []
    w = GDN_BASE
    while w < c_sz:
        sub_blocks.append(jnp.logical_and(blk(2 * w), jnp.logical_not(blk(w))))
        w *= 2
    nt = (((1,), (1,)), ((), ()))
    dot = lambda a, b: jnp.dot(a.astype(bf), b.astype(bf), preferred_element_type=f32)
    dot_nt = lambda a, b: lax.dot_general(a.astype(bf), b.astype(bf), nt, preferred_element_type=f32)

    def split(a):
        hi = a.astype(bf)
        return hi, (a - hi.astype(f32)).astype(bf)

    def hp(a, b):
        a_hi, a_lo = split(a)
        b_hi, b_lo = split(b)
        mm = lambda u, w: jnp.dot(u, w, preferred_element_type=f32)
        return mm(a_hi, b_hi) + (mm(a_hi, b_lo) + mm(a_lo, b_hi))

    hs = range(B_HEADS)

    def chunk_pair(i, carry):
        cs = [2 * i, 2 * i + 1]
        r0 = [pl.multiple_of(c * c_sz, c_sz) for c in cs]
        items = [(j, h) for j in range(2) for h in hs]
        hd = lambda a, j, h: cs_ref[a, pl.ds(r0[j], c_sz), h * B_KEY_DIM:(h + 1) * B_KEY_DIM]
        l2n = lambda t: t * lax.rsqrt(jnp.sum(t * t, axis=-1, keepdims=True) + EPS)
        q = [l2n(hd(0, j, h)) * (B_KEY_DIM ** -0.5) for j, h in items]
        k = [l2n(hd(1, j, h)) for j, h in items]
        v = [hd(2, j, h) for j, h in items]
        gc8 = [gc_ref[0, c] for c in cs]
        gct = [g8.T for g8 in gc8]
        btt = [bt_ref[0, c].T for c in cs]
        gcr = [gc8[j][h:h + 1, :] for j, h in items]
        n_it = range(len(items))
        gcc = [gct[j][:, h:h + 1] for j, h in items]
        beta = [btt[j][:, h:h + 1] for j, h in items]
        decay = [jnp.exp(jnp.where(incl, gcc[n] - gcr[n], -jnp.inf)) for n in n_it]
        kb = [k[n] * beta[n] for n in n_it]
        vb = [v[n] * beta[n] for n in n_it]
        low = [jnp.where(strict, dot_nt(kb[n], k[n]) * decay[n], 0.0) for n in n_it]
        dg = [jnp.where(diag8, low[n], 0.0) for n in n_it]
        t = [eye - dg[n] for n in n_it]
        p = [hp(dg[n], dg[n]) for n in n_it]
        t = [hp(t[n], eye + p[n]) for n in n_it]
        p = [hp(p[n], p[n]) for n in n_it]
        t = [hp(t[n], eye + p[n]) for n in n_it]
        for below in sub_blocks:
            lb = [jnp.where(below, low[n], 0.0) for n in n_it]
            lt = [hp(lb[n], t[n]) for n in n_it]
            t = [t[n] - hp(t[n], lt[n]) for n in n_it]
        u = [dot(t[n], vb[n]) for n in n_it]
        kcd = [dot(t[n], kb[n] * jnp.exp(gcc[n])) for n in n_it]
        intra = [dot_nt(q[n], k[n]) * decay[n] for n in n_it]
        qg = [q[n] * jnp.exp(gcc[n]) for n in n_it]
        glast = [g[:, c_sz - 1:c_sz] for g in gcr]
        kdt = [(k[n] * jnp.exp(glast[n] - gcc[n])).T for n in n_it]
        s = [s_ref[h] for h in hs]
        for j in range(2):
            ix = [j * B_HEADS + h for h in hs]
            v_new = [u[ix[h]] - dot(kcd[ix[h]], s[h]) for h in hs]
            out = [dot(qg[ix[h]], s[h]) + dot(intra[ix[h]], v_new[h]) for h in hs]
            s = [s[h] * jnp.exp(glast[ix[h]]) + dot(kdt[ix[h]], v_new[h]) for h in hs]
            for h in hs:
                o = out[h]
                o = o * lax.rsqrt(jnp.mean(o * o, axis=-1, keepdims=True) + EPS) * nw_ref[...]
                o_ref[0, pl.ds(r0[j], c_sz), h * B_VAL_DIM:(h + 1) * B_VAL_DIM] = o
        for h in hs:
            s_ref[h] = s[h]
        return carry
    lax.fori_loop(0, GDN_G // 2, chunk_pair, 0)


def gated_delta_rule_pallas(proj, conv_w, g, beta, norm_w, bsz, seq):
    nh, dk, dv = B_HEADS, B_KEY_DIM, B_VAL_DIM
    n = seq // CHUNK
    chunked = lambda t: jnp.moveaxis(t.reshape(bsz, nh, n, CHUNK), 1, 2)
    gc = chunked(jnp.cumsum(g.reshape(bsz, nh, n, CHUNK), axis=-1))
    side = pl.BlockSpec((1, GDN_G, nh, CHUNK), lambda b, c: (b, c, 0, 0))
    rows = GDN_G * CHUNK
    width = nh * dk

    def col(name):
        off, w = PROJ_DST[name]
        assert w == width
        return pl.BlockSpec((1, rows, w), lambda b, c: (b, c, off // w))
    cw = jnp.transpose(conv_w.astype(jnp.float32).reshape(CONV_WIDTH, 3, width), (1, 0, 2))
    proj3 = proj.reshape(bsz, seq, proj.shape[-1])
    return pl.pallas_call(
        _gdn_body, grid=(bsz, n // GDN_G),
        in_specs=[col("qb"), col("kb"), col("vb"),
                  pl.BlockSpec((3, CONV_WIDTH, width), lambda b, c: (0, 0, 0)),
                  side, side, pl.BlockSpec((1, dv), lambda b, c: (0, 0))],
        out_specs=pl.BlockSpec((1, rows, nh * dv), lambda b, c: (b, c, 0)),
        out_shape=jax.ShapeDtypeStruct((bsz, seq, nh * dv), jnp.float32),
        scratch_shapes=[pltpu.VMEM((nh, dk, dv), jnp.float32),
                        pltpu.VMEM((3, GDN_PAD + rows, width), jnp.float32),
                        pltpu.VMEM((3, rows, width), jnp.float32)],
        compiler_params=pltpu.CompilerParams(dimension_semantics=("arbitrary", "arbitrary"),
                                             vmem_limit_bytes=40 * 1024 * 1024),
        name="gated_delta_rule")(proj3, proj3, proj3, cw, gc, chunked(beta), norm_w.astype(jnp.float32).reshape(1, dv))


IN_NAMES = ("qa", "ka", "va", "qi", "ki", "wi", "qb", "kb", "vb", "zb", "bb", "ab", "gate_a", "gate_b")
PROJ_ORDER = ("gate_a", "gate_b", "qi", "qa", "qb", "kb", "vb", "zb", "ka", "va", "ki", "wi", "bb", "ab")
LANES = 128


def _proj_layout():
    src, off = {}, 0
    for name, w in zip(IN_NAMES, IN_SPLITS):
        src[name] = (off, w)
        off += w
    dst, off = {}, 0
    for name in PROJ_ORDER:
        dst[name] = (off, src[name][1])
        off += src[name][1]
    return src, dst, -(-off // LANES) * LANES


PROJ_SRC, PROJ_DST, PROJ_WIDTH = _proj_layout()
PROJ_TM = 256
PROJ_VMEM_LIMIT = 48 * 1024 * 1024


def permute_w_in(w_in):
    cols = [w_in[:, PROJ_SRC[n][0]:PROJ_SRC[n][0] + PROJ_SRC[n][1]] for n in PROJ_ORDER]
    cols.append(jnp.zeros((w_in.shape[0], PROJ_WIDTH - sum(c.shape[1] for c in cols)), w_in.dtype))
    return jnp.concatenate(cols, axis=1).astype(jnp.bfloat16)


def _modulated_norm(x, scale, shift):
    return x * lax.rsqrt(jnp.mean(x * x, axis=-1, keepdims=True) + EPS) * scale + shift


def _sigmoid(v):
    return 1.0 / (1.0 + jnp.exp(-v))


def _ada_body(c_ref, w_ref, b_ref, o_ref):
    c = c_ref[...]
    o_ref[...] = jnp.dot((c * _sigmoid(c)).astype(jnp.bfloat16), w_ref[...].astype(jnp.bfloat16),
                         preferred_element_type=jnp.float32) + b_ref[...]


def ada_modulation(c, w_ada, b_ada):
    bsz, d = c.shape
    n = w_ada.shape[1]
    tn = D_MODEL
    return pl.pallas_call(
        _ada_body, grid=(n // tn,),
        in_specs=[pl.BlockSpec((bsz, d), lambda j: (0, 0)), pl.BlockSpec((d, tn), lambda j: (0, j)),
                  pl.BlockSpec((1, tn), lambda j: (0, j))],
        out_specs=pl.BlockSpec((bsz, tn), lambda j: (0, j)),
        out_shape=jax.ShapeDtypeStruct((bsz, n), jnp.float32),
        name="ada_modulation")(c, w_ada, b_ada.reshape(1, n))


def _in_proj_body(x_ref, sc_ref, sh_ref, w_ref, o_ref):
    n1 = _modulated_norm(x_ref[...], sc_ref[0], sh_ref[0])
    o_ref[...] = jnp.dot(n1.astype(jnp.bfloat16), w_ref[...], preferred_element_type=jnp.float32)


def in_proj(x2, scale, shift, w_perm, seq):
    n, d = x2.shape
    tm = PROJ_TM
    per = seq // tm
    vec = pl.BlockSpec((1, 1, d), lambda i: (i // per, 0, 0))
    return pl.pallas_call(
        _in_proj_body, grid=(n // tm,),
        in_specs=[pl.BlockSpec((tm, d), lambda i: (i, 0)), vec, vec,
                  pl.BlockSpec((d, PROJ_WIDTH), lambda i: (0, 0), pipeline_mode=pl.Buffered(1))],
        out_specs=pl.BlockSpec((tm, PROJ_WIDTH), lambda i: (i, 0)),
        out_shape=jax.ShapeDtypeStruct((n, PROJ_WIDTH), jnp.float32),
        compiler_params=pltpu.CompilerParams(dimension_semantics=("arbitrary",), vmem_limit_bytes=PROJ_VMEM_LIMIT),
        name="in_proj")(x2, scale, shift, w_perm)


def _merge_body(oa_ref, ob_ref, z_ref, ga_ref, gb_ref, x_ref, gt_ref, sc_ref, sh_ref, wpa_ref, wpb_ref, wo_ref,
                h_ref, n2_ref):
    f32, bf = jnp.float32, jnp.bfloat16
    z = z_ref[...]
    ob = ob_ref[...] * (z * _sigmoid(z))
    oa = oa_ref[0].reshape(A_WIDTH, oa_ref.shape[-1]).T
    ya = jnp.dot(oa.astype(bf), wpa_ref[...], preferred_element_type=f32)
    yb = jnp.dot(ob.astype(bf), wpb_ref[...], preferred_element_type=f32)
    merged = _sigmoid(ga_ref[...]) * ya + _sigmoid(gb_ref[...]) * yb
    y1 = jnp.dot(merged.astype(bf), wo_ref[...], preferred_element_type=f32)
    h = x_ref[...] + gt_ref[0] * y1
    h_ref[...] = h
    n2_ref[...] = _modulated_norm(h, sc_ref[0], sh_ref[0])


def merge(o_a, o_b, proj, x2, gt1, scale2, shift2, w_pa, w_pb, w_o, seq):
    n, d = x2.shape
    tm = PROJ_TM
    per = seq // tm
    bf = jnp.bfloat16
    vec = pl.BlockSpec((1, 1, d), lambda i: (i // per, 0, 0))

    def col(name):
        off, w = PROJ_DST[name]
        return pl.BlockSpec((tm, w), lambda i: (i, off // w))
    row = lambda w: pl.BlockSpec((tm, w), lambda i: (i, 0))
    res = lambda a, b: pl.BlockSpec((a, b), lambda i: (0, 0))
    return pl.pallas_call(
        _merge_body, grid=(n // tm,),
        in_specs=[pl.BlockSpec((1, A_HEADS, A_HEAD_DIM, tm), lambda i: (i // per, 0, 0, i % per)), row(B_V_WIDTH), col("zb"), col("gate_a"), col("gate_b"), row(d), vec, vec, vec,
                  res(A_WIDTH, d), res(B_V_WIDTH, d), res(d, d)],
        out_specs=[row(d), row(d)],
        out_shape=[jax.ShapeDtypeStruct((n, d), jnp.float32), jax.ShapeDtypeStruct((n, d), jnp.float32)],
        compiler_params=pltpu.CompilerParams(dimension_semantics=("arbitrary",), vmem_limit_bytes=PROJ_VMEM_LIMIT),
        name="merge")(o_a, o_b, proj, proj, proj, x2, gt1, scale2, shift2,
                      w_pa.astype(bf), w_pb.astype(bf), w_o.astype(bf))


def token_mixers(proj, positions, conv_w, a_log, dt_bias, norm_b_w, bsz, seq):
    qit, qat, w, kit, kat, vat, bg = dsa_prep(proj, positions, bsz, seq)
    o_a = dsa_attention_pallas(qit, w, qat, kit, kat, vat)

    beta = jax.nn.sigmoid(bg[:, :B_HEADS])
    g = -jnp.exp(a_log.astype(jnp.float32))[:, None] * jax.nn.softplus(bg[:, B_HEADS:] + dt_bias.astype(jnp.float32)[:, None])
    o_b = gated_delta_rule_pallas(proj, conv_w, g, beta, norm_b_w, bsz, seq)
    return o_a, o_b


PEER_SLOTS = PEER_HEADS * PEER_TOPK
PEER_TB = 32
HALF_ROWS = 4
HI_MASK = -65536


def pack_table(tab):
    bits = lax.bitcast_convert_type(tab.astype(jnp.bfloat16), jnp.uint16).astype(jnp.uint32)
    half = tab.shape[1] // 2
    word = bits[:, :half] | (bits[:, half:] << 16)
    return lax.bitcast_convert_type(word, jnp.int32).reshape(tab.shape[0] * HALF_ROWS, 128)


def _table_row(tab_ref, rows, k):
    return _unpack(tab_ref[pl.ds(pl.multiple_of(rows[k], HALF_ROWS), HALF_ROWS), :])


def _unpack(w):
    lo = lax.bitcast_convert_type(lax.shift_left(w, 16), jnp.float32)
    hi = lax.bitcast_convert_type(w & jnp.int32(HI_MASK), jnp.float32)
    return lo, hi


def _peer_u_body(idx_ref, x_ref, gate_ref, tab_ref, o_ref, s_ref, sb_ref):
    def lane_sums(t):
        a = jnp.sum(sb_ref[t].T, axis=0, keepdims=True)
        o_ref[pl.ds(t, 1), :] = 0.5 * a * (1.0 + lax.erf(a * (2.0 ** -0.5))) * gate_ref[pl.ds(t, 1), :]

    sb_ref[0] = jnp.zeros(sb_ref.shape[1:], jnp.float32)

    def tok(t, carry):
        lane_sums(jnp.maximum(t - 1, 0))
        x8 = x_ref[pl.ds(t, 1), :].reshape(2 * HALF_ROWS, 128)
        xlo, xhi = x8[0:HALF_ROWS], x8[HALF_ROWS:2 * HALF_ROWS]
        rows = idx_ref.at[t]
        for k in range(PEER_SLOTS):
            lo, hi = _table_row(tab_ref, rows, k)
            s_ref[HALF_ROWS * k:HALF_ROWS * (k + 1), :] = lo * xlo + hi * xhi
        s4 = s_ref[pl.ds(0, PEER_SLOTS, stride=HALF_ROWS), :]
        for r in range(1, HALF_ROWS):
            s4 = s4 + s_ref[pl.ds(r, PEER_SLOTS, stride=HALF_ROWS), :]
        sb_ref[t] = s4
        return carry
    lax.fori_loop(0, PEER_TB, tok, 0, unroll=2)
    lane_sums(PEER_TB - 1)


def _peer_v_body(idx_ref, coef_ref, tab_ref, o_ref, cb_ref):
    def spread(t):
        return jnp.broadcast_to(coef_ref[pl.ds(t, 1), :], (PEER_SLOTS, 128)).T

    def tok(t, weights):
        nxt = spread(jnp.minimum(t + 1, PEER_TB - 1))
        cb_ref[...] = weights
        nacc = 2
        acc = [jnp.zeros((HALF_ROWS, 128), jnp.float32) for _ in range(2 * nacc)]
        rows = idx_ref.at[t]
        for k in range(PEER_SLOTS):
            lo, hi = _table_row(tab_ref, rows, k)
            c = jnp.broadcast_to(cb_ref[k:k + 1, :], (HALF_ROWS, 128))
            a = k % nacc
            acc[2 * a] = acc[2 * a] + c * lo
            acc[2 * a + 1] = acc[2 * a + 1] + c * hi
        half = HALF_ROWS * 128
        o_ref[pl.ds(t, 1), 0:half] = (acc[0] + acc[2]).reshape(1, half)
        o_ref[pl.ds(t, 1), half:2 * half] = (acc[1] + acc[3]).reshape(1, half)
        return nxt
    lax.fori_loop(0, PEER_TB, tok, spread(0))


def _table_spec():
    return pl.BlockSpec((PEER_N_EXPERTS * HALF_ROWS, 128), lambda i: (0, 0), pipeline_mode=pl.Buffered(1))


PEER_VMEM_LIMIT = 48 * 1024 * 1024


def peer_u(idx, x, gates, tab):
    n = idx.shape[0]
    tb = PEER_TB
    return pl.pallas_call(
        _peer_u_body, grid=(n // tb,),
        in_specs=[pl.BlockSpec((tb, PEER_SLOTS), lambda i: (i, 0), memory_space=pltpu.SMEM),
                  pl.BlockSpec((tb, x.shape[1]), lambda i: (i, 0)),
                  pl.BlockSpec((tb, PEER_SLOTS), lambda i: (i, 0)),
                  _table_spec()],
        out_specs=pl.BlockSpec((tb, PEER_SLOTS), lambda i: (i, 0)),
        out_shape=jax.ShapeDtypeStruct((n, PEER_SLOTS), jnp.float32),
        scratch_shapes=[pltpu.VMEM((HALF_ROWS * PEER_SLOTS, 128), jnp.float32),
                        pltpu.VMEM((tb, PEER_SLOTS, 128), jnp.float32)],
        compiler_params=pltpu.CompilerParams(dimension_semantics=("arbitrary",), vmem_limit_bytes=PEER_VMEM_LIMIT),
        name="peer_u")(idx, x, gates, tab)


def peer_v(idx, coef, tab):
    n = idx.shape[0]
    tb = PEER_TB
    return pl.pallas_call(
        _peer_v_body, grid=(n // tb,),
        in_specs=[pl.BlockSpec((tb, PEER_SLOTS), lambda i: (i, 0), memory_space=pltpu.SMEM),
                  pl.BlockSpec((tb, PEER_SLOTS), lambda i: (i, 0)),
                  _table_spec()],
        out_specs=pl.BlockSpec((tb, 2 * HALF_ROWS * 128), lambda i: (i, 0)),
        out_shape=jax.ShapeDtypeStruct((n, 2 * HALF_ROWS * 128), jnp.float32),
        scratch_shapes=[pltpu.VMEM((PEER_SLOTS, 128), jnp.float32)],
        compiler_params=pltpu.CompilerParams(dimension_semantics=("arbitrary",), vmem_limit_bytes=PEER_VMEM_LIMIT),
        name="peer_v")(idx, coef, tab)


PEER_TT = 256
PAD_ID = 2**30


def _extract_top(ref, n_out, rid=None):
    rows, t = ref.shape
    if rid is None:
        rid = lax.broadcasted_iota(jnp.int32, (rows, t), 0)
    vals, idxs = [], []
    for _ in range(n_out):
        s = ref[...]
        m = jnp.max(s, axis=0, keepdims=True)
        ix = jnp.min(jnp.where(s == m, rid, jnp.int32(PAD_ID)), axis=0, keepdims=True)
        ref[...] = jnp.where(rid == ix, -jnp.inf, s)
        vals.append(m)
        idxs.append(ix)
    return vals, idxs


def _peer_route_body(x_ref, wq_ref, sk_ref, cid_ref, idx_ref, gate_ref, s_ref, cand_ref, v_ref, i_ref, et_ref, gt_ref):
    kk, nk = PEER_TOPK, PEER_N_KEYS
    f32 = jnp.float32
    q = jnp.dot(x_ref[...].astype(jnp.bfloat16), wq_ref[...], preferred_element_type=f32).astype(jnp.bfloat16)
    nt = (((1,), (1,)), ((), ()))
    row16 = lax.broadcasted_iota(jnp.int32, (kk, PEER_TT), 0)
    for h in range(PEER_HEADS):
        qh = q[:, h * PEER_KEY_DIM:(h + 1) * PEER_KEY_DIM]
        s_ref[...] = lax.dot_general(sk_ref[h], qh, nt, preferred_element_type=f32)
        for p in range(2):
            vals, idxs = _extract_top(s_ref.at[p * nk:(p + 1) * nk, :], kk)
            for i in range(kk):
                v_ref[p, i:i + 1, :] = vals[i]
                i_ref[p, i:i + 1, :] = idxs[i]
        off = 0
        for i in range(kk):
            n_j = kk // (i + 1)
            cand_ref[off:off + n_j, :] = v_ref[0, i:i + 1, :] + v_ref[1, 0:n_j, :]
            off += n_j
        cand_ref[off:, :] = jnp.full((cand_ref.shape[0] - off, PEER_TT), -jnp.inf, f32)
        vals, cis = _extract_top(cand_ref, kk, cid_ref[...])
        i1, i2 = i_ref[0], i_ref[1]
        es = [jnp.exp(v - vals[0]) for v in vals]
        den = es[0]
        for e in es[1:]:
            den = den + e
        for k in range(kk):
            ci = cis[k]
            e1 = jnp.sum(jnp.where(row16 == lax.shift_right_logical(ci, 4), i1, 0), axis=0, keepdims=True)
            e2 = jnp.sum(jnp.where(row16 == (ci & 15), i2, 0), axis=0, keepdims=True)
            et_ref[h * kk + k:h * kk + k + 1, :] = (e1 * nk + e2) * HALF_ROWS
            gt_ref[h * kk + k:h * kk + k + 1, :] = es[k] / den
    idx_ref[...] = lax.bitcast_convert_type(lax.bitcast_convert_type(et_ref[...], f32).T, jnp.int32)
    gate_ref[...] = gt_ref[...].T


def _pair_cells():
    kk = PEER_TOPK
    ids = [i * kk + j for i in range(kk) for j in range(kk // (i + 1))]
    return ids + [PAD_ID] * (-len(ids) % 8)


def peer_route(xn, wq, subkeys):
    n = xn.shape[0]
    tt = PEER_TT
    cells = _pair_cells()
    cid = jnp.broadcast_to(jnp.asarray(cells, jnp.int32)[:, None], (len(cells), tt))
    half = PEER_KEY_DIM // 2
    z = jnp.zeros((PEER_HEADS, PEER_N_KEYS, half), subkeys.dtype)
    skbd = jnp.concatenate([jnp.concatenate([subkeys[:, 0], z], axis=-1),
                            jnp.concatenate([z, subkeys[:, 1]], axis=-1)], axis=1).astype(jnp.bfloat16)
    return pl.pallas_call(
        _peer_route_body, grid=(n // tt,),
        in_specs=[pl.BlockSpec((tt, D_MODEL), lambda i: (i, 0)),
                  pl.BlockSpec((D_MODEL, PEER_HEADS * PEER_KEY_DIM), lambda i: (0, 0)),
                  pl.BlockSpec((PEER_HEADS, 2 * PEER_N_KEYS, PEER_KEY_DIM), lambda i: (0, 0, 0)),
                  pl.BlockSpec((len(cells), tt), lambda i: (0, 0))],
        out_specs=[pl.BlockSpec((tt, PEER_SLOTS), lambda i: (i, 0)), pl.BlockSpec((tt, PEER_SLOTS), lambda i: (i, 0))],
        out_shape=[jax.ShapeDtypeStruct((n, PEER_SLOTS), jnp.int32), jax.ShapeDtypeStruct((n, PEER_SLOTS), jnp.float32)],
        scratch_shapes=[pltpu.VMEM((2 * PEER_N_KEYS, tt), jnp.float32),
                        pltpu.VMEM((len(cells), tt), jnp.float32),
                        pltpu.VMEM((2, PEER_TOPK, tt), jnp.float32),
                        pltpu.VMEM((2, PEER_TOPK, tt), jnp.int32),
                        pltpu.VMEM((PEER_SLOTS, tt), jnp.int32),
                        pltpu.VMEM((PEER_SLOTS, tt), jnp.float32)],
        compiler_params=pltpu.CompilerParams(dimension_semantics=("arbitrary",), vmem_limit_bytes=32 * 1024 * 1024),
        name="peer_route")(xn, wq.astype(jnp.bfloat16), skbd, cid)


def peer_channel_mixer(xn, wq, subkeys, u_tab, v_tab):
    bsz, seq, d = xn.shape
    n_tok = bsz * seq
    x2 = xn.reshape(n_tok, d)
    idx, gates = peer_route(x2, wq, subkeys)
    coef = peer_u(idx, x2, gates, pack_table(u_tab))
    out = peer_v(idx, coef, pack_table(v_tab))
    return out.reshape(bsz, seq, d)


def _residual_body(h_ref, y_ref, gt_ref, o_ref):
    o_ref[...] = h_ref[...] + gt_ref[0] * y_ref[...]


def _final_norm_body(h_ref, y_ref, gt_ref, g_ref, o_ref):
    h = h_ref[...] + gt_ref[0] * y_ref[...]
    o_ref[...] = h * lax.rsqrt(jnp.mean(h * h, axis=-1, keepdims=True) + EPS) * g_ref[...]


def gated_residual(h2, y2, gt, seq, gain=None):
    n, d = h2.shape
    tm = min(1024, seq)
    per = seq // tm
    row = pl.BlockSpec((tm, d), lambda i: (i, 0))
    specs = [row, row, pl.BlockSpec((1, 1, d), lambda i: (i // per, 0, 0))]
    args = [h2, y2, gt]
    if gain is not None:
        specs.append(pl.BlockSpec((1, d), lambda i: (0, 0)))
        args.append(gain.astype(jnp.float32).reshape(1, d))
    return pl.pallas_call(
        _residual_body if gain is None else _final_norm_body, grid=(n // tm,),
        in_specs=specs, out_specs=row, out_shape=jax.ShapeDtypeStruct((n, d), h2.dtype),
        name="gated_residual" if gain is None else "final_norm")(*args)


def kernel(x, c, positions, w_ada, b_ada, w_in, conv_w, a_log, dt_bias, norm_b_w,
           w_pa, w_pb, w_o, peer_wq, peer_subkeys, peer_u, peer_v, final_norm_w):
    bsz, seq, d = x.shape
    h = x.reshape(bsz * seq, d)
    for layer in range(DEPTH):
        mod = ada_modulation(c, w_ada[layer], b_ada[layer])
        sh1, sc1, gt1, sh2, sc2, gt2 = [m.reshape(bsz, 1, d) for m in jnp.split(mod, 6, axis=-1)]
        proj = in_proj(h, 1.0 + sc1, sh1, permute_w_in(w_in[layer]), seq)
        o_a, o_b = token_mixers(proj, positions, conv_w[layer], a_log[layer], dt_bias[layer], norm_b_w[layer],
                                bsz, seq)
        h, n2 = merge(o_a, o_b.reshape(bsz * seq, B_V_WIDTH), proj, h, gt1,
                      1.0 + sc2, sh2, w_pa[layer], w_pb[layer], w_o[layer], seq)
        y2 = peer_channel_mixer(n2.reshape(bsz, seq, d), peer_wq[layer], peer_subkeys[layer], peer_u[layer],
                                peer_v[layer])
        last = layer == DEPTH - 1
        h = gated_residual(h, y2.reshape(bsz * seq, d), gt2, seq, final_norm_w if last else None)
    return h.reshape(bsz, seq, d)
```

```python
import functools

import jax, jax.numpy as jnp
from jax import lax
from jax.experimental import pallas as pl
from jax.experimental.pallas import tpu as pltpu

D_MODEL = 1024
DEPTH = 1

LANES = 128
SUBLANES = 8
MIB = 1024 * 1024
DSA_VMEM_LIMIT = 48 * MIB
PREP_VMEM_LIMIT = 32 * MIB
GDN_VMEM_LIMIT = 40 * MIB
PROJ_VMEM_LIMIT = 48 * MIB
PEER_VMEM_LIMIT = 48 * MIB
ROUTE_VMEM_LIMIT = 32 * MIB

A_HEADS = 8
A_KV_HEADS = 2
A_HEAD_DIM = 64
IDX_HEADS = 16
IDX_DIM = 64
IDX_TOPK_MAX = 256
B_HEADS = 8
B_KEY_DIM = 64
B_VAL_DIM = 64
CONV_WIDTH = 4
CHUNK = 64
ROPE_THETA = 500000.0
ROPE_FRACTION_DEN = 4
PEER_HEADS = 8
PEER_KEY_DIM = 128
PEER_N_KEYS = 128
PEER_N_EXPERTS = PEER_N_KEYS * PEER_N_KEYS
PEER_TOPK = 16
EPS = 1e-6

A_WIDTH = A_HEADS * A_HEAD_DIM
KV_WIDTH = A_KV_HEADS * A_HEAD_DIM
B_QK_WIDTH = B_HEADS * B_KEY_DIM
B_V_WIDTH = B_HEADS * B_VAL_DIM
IN_SPLITS = (A_WIDTH, KV_WIDTH, KV_WIDTH, IDX_HEADS * IDX_DIM, IDX_DIM, IDX_HEADS,
             B_QK_WIDTH, B_QK_WIDTH, B_V_WIDTH, B_V_WIDTH, B_HEADS, B_HEADS, D_MODEL, D_MODEL)


DSA_TQ = 256
DSA_TK = 256
INT_MIN = -2**31
NEG_BIG = -1e30


def _dsa_body(topk, qit_ref, w_ref, qat_ref, ki_ref, ka_ref, vat_ref, o_ref,
              key_ref):
    tq, tk = DSA_TQ, DSA_TK
    qb = pl.program_id(1)
    n_kv = qb + 1
    t_glob = qb * tq + lax.broadcasted_iota(jnp.int32, (1, tq), 1)
    row = lax.broadcasted_iota(jnp.int32, (tk, 1), 0)
    f32 = jnp.float32

    def p1(j, carry):
        kt = ki_ref[0, j]
        score = jnp.zeros((tk, tq), f32)
        for h in range(IDX_HEADS):
            lt = jnp.dot(kt, qit_ref[0, h], preferred_element_type=f32)
            score = score + w_ref[0, h:h + 1, :] * jnp.maximum(lt, 0.0)
        bits = lax.bitcast_convert_type(score + 0.0, jnp.int32)
        skey = jnp.where(bits >= 0, bits, bits ^ jnp.int32(0x7FFFFFFF))
        skey = jnp.where(j * tk + row <= t_glob, skey, jnp.int32(INT_MIN))
        key_ref[j] = skey
        return carry
    lax.fori_loop(0, n_kv, p1, 0)

    def count(pred):
        def body(j, acc):
            hit = jnp.where(pred(key_ref[j], j * tk + row), 1.0, 0.0)
            return acc + jnp.sum(hit.reshape(tk // SUBLANES, SUBLANES, tq), axis=0)
        acc = lax.fori_loop(0, n_kv, body, jnp.zeros((SUBLANES, tq), f32))
        return jnp.sum(acc, axis=0, keepdims=True)

    kf = jnp.float32(topk)

    def bit_step(i, ku):
        cand_u = ku | lax.shift_left(jnp.int32(1), 31 - i)
        cand = cand_u ^ jnp.int32(INT_MIN)
        c = count(lambda k, s: k >= cand)
        return jnp.where(c >= kf, cand_u, ku)
    ku = lax.fori_loop(0, 32, bit_step, jnp.zeros((1, tq), jnp.int32))
    kth = ku ^ jnp.int32(INT_MIN)
    c_gt = count(lambda k, s: k > kth)
    c_ge = count(lambda k, s: k >= kth)
    short = kth == jnp.int32(INT_MIN)
    x0 = jnp.where(short, jnp.int32(-1), jnp.int32(2**30))
    need = kf - c_gt
    has_tie = jnp.max(jnp.where(jnp.logical_and(c_ge > kf, jnp.logical_not(short)), 1.0, 0.0)) > 0.0

    def tie_search():
        def step(i, x):
            bit = lax.shift_left(jnp.int32(1), 11 - i)
            probe = x + bit - 1
            c = count(lambda k, s: jnp.logical_and(k == kth, s <= probe))
            return jnp.where(c < need, x + bit, x)
        x = lax.fori_loop(0, 12, step, jnp.zeros((1, tq), jnp.int32))
        return jnp.where(short, jnp.int32(-1), x)
    x_lim = lax.cond(has_tie, tie_search, lambda: x0)

    rep = A_HEADS // A_KV_HEADS
    hs = range(A_HEADS)

    def p3(j, carry):
        m, l, acc = carry
        skey = key_ref[j]
        s_idx = j * tk + row
        sel = jnp.logical_or(skey > kth, jnp.logical_and(skey == kth, s_idx <= x_lim))
        kt = [ka_ref[0, g, j] for g in range(A_KV_HEADS)]
        vt = [vat_ref[0, g, j] for g in range(A_KV_HEADS)]
        s = [jnp.where(sel, jnp.dot(kt[h // rep], qat_ref[0, h], preferred_element_type=f32), NEG_BIG) for h in hs]
        m_new = [jnp.maximum(m[h], jnp.max(s[h], axis=0, keepdims=True)) for h in hs]
        alpha = [jnp.exp(m[h] - m_new[h]) for h in hs]
        p = [jnp.exp(s[h] - m_new[h]) for h in hs]
        l_new = [alpha[h] * l[h] + jnp.sum(p[h], axis=0, keepdims=True) for h in hs]
        acc_new = [alpha[h] * acc[h] + jnp.dot(vt[h // rep], p[h].astype(jnp.bfloat16), preferred_element_type=f32)
                   for h in hs]
        return tuple(m_new), tuple(l_new), tuple(acc_new)

    init = (tuple(jnp.full((1, tq), NEG_BIG, f32) for _ in hs), tuple(jnp.zeros((1, tq), f32) for _ in hs),
            tuple(jnp.zeros((A_HEAD_DIM, tq), f32) for _ in hs))
    _, l_fin, acc_fin = lax.fori_loop(0, n_kv, p3, init)
    for h in hs:
        o_ref[0, h] = acc_fin[h] / l_fin[h]


ROPE_HALF = A_HEAD_DIM // ROPE_FRACTION_DEN // 2


def _rope_t(xt, n_heads, cos, sin, scale=None):
    outs = []
    for h in range(n_heads):
        b = h * A_HEAD_DIM
        x1, x2, rest = xt[b:b + ROPE_HALF], xt[b + ROPE_HALF:b + 2 * ROPE_HALF], xt[b + 2 * ROPE_HALF:b + A_HEAD_DIM]
        o = jnp.concatenate([x1 * cos - x2 * sin, x2 * cos + x1 * sin, rest], axis=0)
        outs.append(o if scale is None else o * scale)
    return outs


def _dsa_prep_body(qi_ref, qa_ref, ka_ref, va_ref, sm_ref, cos_ref, sin_ref, dp_ref,
                   qit_ref, qat_ref, w_ref, kit_ref, kat_ref, vat_ref, bg_ref):
    bf = jnp.bfloat16
    cos, sin = cos_ref[0], sin_ref[0]
    for h, o in enumerate(_rope_t(qi_ref[...].T, IDX_HEADS, cos, sin)):
        qit_ref[0, h] = o.astype(bf)
    for h, o in enumerate(_rope_t(qa_ref[...].T, A_HEADS, cos, sin, A_HEAD_DIM ** -0.5)):
        qat_ref[0, h] = o.astype(bf)
    kat = _rope_t(ka_ref[...].T, A_KV_HEADS, cos, sin)
    vt = va_ref[...].T
    for g in range(A_KV_HEADS):
        kat_ref[0, g, 0] = kat[g].T.astype(bf)
        vat_ref[0, g, 0] = vt[g * A_HEAD_DIM:(g + 1) * A_HEAD_DIM].astype(bf)
    smt = sm_ref[...].T
    kit_ref[0, 0] = _rope_t(smt[0:IDX_DIM], 1, cos, sin)[0].T.astype(bf)
    w_ref[0] = smt[IDX_DIM:IDX_DIM + IDX_HEADS] * ((IDX_HEADS ** -0.5) * (IDX_DIM ** -0.5))
    b0 = IDX_DIM + IDX_HEADS
    bb, ab = smt[b0:b0 + B_HEADS], smt[b0 + B_HEADS:b0 + 2 * B_HEADS]
    a_log, dt_bias = dp_ref[:, 0:1], dp_ref[:, 1:2]
    bg_ref[0] = jnp.concatenate([_sigmoid(bb), -jnp.exp(a_log) * jax.nn.softplus(ab + dt_bias)], axis=0)


def dsa_prep(proj, positions, a_log, dt_bias, bsz, seq):
    t = DSA_TQ
    per = seq // t
    assert DSA_TK == t and PROJ_ORDER[-4:] == ("ki", "wi", "bb", "ab") and PROJ_DST["ki"][0] + LANES == PROJ_WIDTH
    rd = A_HEAD_DIM // ROPE_FRACTION_DEN
    inv_freq = jnp.power(jnp.float32(ROPE_THETA), -jnp.arange(ROPE_HALF, dtype=jnp.float32) * (2.0 / rd))
    ang = positions.astype(jnp.float32)[:, None, :] * inv_freq[None, :, None]

    def col(name, width=None):
        off, w = PROJ_DST[name]
        w = width or w
        return pl.BlockSpec((t, w), lambda i: (i, off // w))
    trig = pl.BlockSpec((1, ROPE_HALF, t), lambda i: (i // per, 0, i % per))
    bf, f32 = jnp.bfloat16, jnp.float32
    sd = jax.ShapeDtypeStruct
    n_kv = seq // DSA_TK
    return pl.pallas_call(
        _dsa_prep_body, grid=(bsz * per,),
        in_specs=[col("qi"), col("qa"), col("ka"), col("va"), col("ki", LANES), trig, trig,
                  pl.BlockSpec((B_HEADS, 2), lambda i: (0, 0))],
        out_specs=[pl.BlockSpec((1, IDX_HEADS, IDX_DIM, t), lambda i: (i // per, 0, 0, i % per)),
                   pl.BlockSpec((1, A_HEADS, A_HEAD_DIM, t), lambda i: (i // per, 0, 0, i % per)),
                   pl.BlockSpec((1, IDX_HEADS, t), lambda i: (i // per, 0, i % per)),
                   pl.BlockSpec((1, 1, t, IDX_DIM), lambda i: (i // per, i % per, 0, 0)),
                   pl.BlockSpec((1, A_KV_HEADS, 1, t, A_HEAD_DIM), lambda i: (i // per, 0, i % per, 0, 0)),
                   pl.BlockSpec((1, A_KV_HEADS, 1, A_HEAD_DIM, t), lambda i: (i // per, 0, i % per, 0, 0)),
                   pl.BlockSpec((1, 2 * B_HEADS, t), lambda i: (i // per, 0, i % per))],
        out_shape=[sd((bsz, IDX_HEADS, IDX_DIM, seq), bf), sd((bsz, A_HEADS, A_HEAD_DIM, seq), bf),
                   sd((bsz, IDX_HEADS, seq), f32), sd((bsz, n_kv, DSA_TK, IDX_DIM), bf),
                   sd((bsz, A_KV_HEADS, n_kv, DSA_TK, A_HEAD_DIM), bf),
                   sd((bsz, A_KV_HEADS, n_kv, A_HEAD_DIM, DSA_TK), bf), sd((bsz, 2 * B_HEADS, seq), f32)],
        compiler_params=pltpu.CompilerParams(dimension_semantics=("arbitrary",), vmem_limit_bytes=PREP_VMEM_LIMIT),
        name="dsa_prep")(proj, proj, proj, proj, proj, jnp.cos(ang), jnp.sin(ang),
                         jnp.stack([a_log, dt_bias], axis=-1).astype(f32))


def dsa_attention_pallas(qit, w, qat, kit, kat, vat):
    bsz, seq = qit.shape[0], qit.shape[-1]
    tq, tk = DSA_TQ, DSA_TK
    topk = min(IDX_TOPK_MAX, seq // 4)
    n_kv = seq // tk
    return pl.pallas_call(
        functools.partial(_dsa_body, topk),
        grid=(bsz, seq // tq),
        in_specs=[
            pl.BlockSpec((1, IDX_HEADS, IDX_DIM, tq), lambda b, q: (b, 0, 0, q)),
            pl.BlockSpec((1, IDX_HEADS, tq), lambda b, q: (b, 0, q)),
            pl.BlockSpec((1, A_HEADS, A_HEAD_DIM, tq), lambda b, q: (b, 0, 0, q)),
            pl.BlockSpec((1, n_kv, tk, IDX_DIM), lambda b, q: (b, 0, 0, 0)),
            pl.BlockSpec((1, A_KV_HEADS, n_kv, tk, A_HEAD_DIM), lambda b, q: (b, 0, 0, 0, 0)),
            pl.BlockSpec((1, A_KV_HEADS, n_kv, A_HEAD_DIM, tk), lambda b, q: (b, 0, 0, 0, 0)),
        ],
        out_specs=pl.BlockSpec((1, A_HEADS, A_HEAD_DIM, tq), lambda b, q: (b, 0, 0, q)),
        out_shape=jax.ShapeDtypeStruct((bsz, A_HEADS, A_HEAD_DIM, seq), jnp.float32),
        scratch_shapes=[
            pltpu.VMEM((n_kv, tk, tq), jnp.int32),
        ],
        compiler_params=pltpu.CompilerParams(dimension_semantics=("arbitrary", "arbitrary"),
                                             vmem_limit_bytes=DSA_VMEM_LIMIT),
        name="dsa_attention",
    )(qit, w, qat, kit, kat, vat)


GDN_G = 8
GDN_BASE = 8
GDN_PAD = 8


def _gdn_body(q_ref, k_ref, v_ref, cw_ref, gc_ref, bt_ref, nw_ref, o_ref, s_ref, xs_ref, cs_ref):
    c_sz = CHUNK
    f32, bf = jnp.float32, jnp.bfloat16

    rows = GDN_G * c_sz
    halo = CONV_WIDTH - 1

    @pl.when(pl.program_id(1) == 0)
    def _():
        s_ref[...] = jnp.zeros(s_ref.shape, f32)
        xs_ref[:, 0:GDN_PAD, :] = jnp.zeros((3, GDN_PAD, xs_ref.shape[-1]), f32)

    for j, ref in enumerate((q_ref, k_ref, v_ref)):
        xs_ref[j, GDN_PAD:GDN_PAD + rows, :] = ref[0]
        acc = xs_ref[j, GDN_PAD - halo:GDN_PAD - halo + rows, :] * cw_ref[j, 0:1, :]
        for i in range(1, CONV_WIDTH):
            acc = acc + xs_ref[j, GDN_PAD - halo + i:GDN_PAD - halo + i + rows, :] * cw_ref[j, i:i + 1, :]
        cs_ref[j] = acc * _sigmoid(acc)
        xs_ref[j, GDN_PAD - halo:GDN_PAD, :] = xs_ref[j, GDN_PAD + rows - halo:GDN_PAD + rows, :]

    ri = lax.broadcasted_iota(jnp.int32, (c_sz, c_sz), 0)
    ci = lax.broadcasted_iota(jnp.int32, (c_sz, c_sz), 1)
    incl, strict = ri >= ci, ri > ci
    eye = jnp.where(ri == ci, 1.0, 0.0).astype(f32)
    blk = lambda w: (ri // w) == (ci // w)
    diag8 = blk(GDN_BASE)
    sub_blocks = []
    w = GDN_BASE
    while w < c_sz:
        sub_blocks.append(jnp.logical_and(blk(2 * w), jnp.logical_not(blk(w))))
        w *= 2
    nt = (((1,), (1,)), ((), ()))
    dot = lambda a, b: jnp.dot(a.astype(bf), b.astype(bf), preferred_element_type=f32)
    dot_nt = lambda a, b: lax.dot_general(a.astype(bf), b.astype(bf), nt, preferred_element_type=f32)

    def split(a):
        hi = a.astype(bf)
        return hi, (a - hi.astype(f32)).astype(bf)

    def hp(a, b):
        a_hi, a_lo = split(a)
        b_hi, b_lo = split(b)
        mm = lambda u, w: jnp.dot(u, w, preferred_element_type=f32)
        return mm(a_hi, b_hi) + (mm(a_hi, b_lo) + mm(a_lo, b_hi))

    hs = range(B_HEADS)

    def chunk_pair(i, carry):
        cs = [2 * i, 2 * i + 1]
        r0 = [pl.multiple_of(c * c_sz, c_sz) for c in cs]
        items = [(j, h) for j in range(2) for h in hs]
        hd = lambda a, j, h: cs_ref[a, pl.ds(r0[j], c_sz), h * B_KEY_DIM:(h + 1) * B_KEY_DIM]
        l2n = lambda t: t * lax.rsqrt(jnp.sum(t * t, axis=-1, keepdims=True) + EPS)
        q = [l2n(hd(0, j, h)) * (B_KEY_DIM ** -0.5) for j, h in items]
        k = [l2n(hd(1, j, h)) for j, h in items]
        v = [hd(2, j, h) for j, h in items]
        gc8 = [gc_ref[0, c] for c in cs]
        gct = [g8.T for g8 in gc8]
        btt = [bt_ref[0, c].T for c in cs]
        gcr = [gc8[j][h:h + 1, :] for j, h in items]
        n_it = range(len(items))
        gcc = [gct[j][:, h:h + 1] for j, h in items]
        beta = [btt[j][:, h:h + 1] for j, h in items]
        decay = [jnp.exp(jnp.where(incl, gcc[n] - gcr[n], -jnp.inf)) for n in n_it]
        kb = [k[n] * beta[n] for n in n_it]
        vb = [v[n] * beta[n] for n in n_it]
        low = [jnp.where(strict, dot_nt(kb[n], k[n]) * decay[n], 0.0) for n in n_it]
        dg = [jnp.where(diag8, low[n], 0.0) for n in n_it]
        t = [eye - dg[n] for n in n_it]
        p = [hp(dg[n], dg[n]) for n in n_it]
        t = [hp(t[n], eye + p[n]) for n in n_it]
        p = [hp(p[n], p[n]) for n in n_it]
        t = [hp(t[n], eye + p[n]) for n in n_it]
        for below in sub_blocks:
            lb = [jnp.where(below, low[n], 0.0) for n in n_it]
            lt = [hp(lb[n], t[n]) for n in n_it]
            t = [t[n] - hp(t[n], lt[n]) for n in n_it]
        u = [dot(t[n], vb[n]) for n in n_it]
        kcd = [dot(t[n], kb[n] * jnp.exp(gcc[n])) for n in n_it]
        intra = [dot_nt(q[n], k[n]) * decay[n] for n in n_it]
        qg = [q[n] * jnp.exp(gcc[n]) for n in n_it]
        glast = [g[:, c_sz - 1:c_sz] for g in gcr]
        kdt = [(k[n] * jnp.exp(glast[n] - gcc[n])).T for n in n_it]
        s = [s_ref[h] for h in hs]
        for j in range(2):
            ix = [j * B_HEADS + h for h in hs]
            v_new = [u[ix[h]] - dot(kcd[ix[h]], s[h]) for h in hs]
            out = [dot(qg[ix[h]], s[h]) + dot(intra[ix[h]], v_new[h]) for h in hs]
            s = [s[h] * jnp.exp(glast[ix[h]]) + dot(kdt[ix[h]], v_new[h]) for h in hs]
            for h in hs:
                o = out[h]
                o = o * lax.rsqrt(jnp.mean(o * o, axis=-1, keepdims=True) + EPS) * nw_ref[...]
                o_ref[0, pl.ds(r0[j], c_sz), h * B_VAL_DIM:(h + 1) * B_VAL_DIM] = o
        for h in hs:
            s_ref[h] = s[h]
        return carry
    lax.fori_loop(0, GDN_G // 2, chunk_pair, 0)


def gated_delta_rule_pallas(proj, conv_w, g, beta, norm_w, bsz, seq):
    nh, dk, dv = B_HEADS, B_KEY_DIM, B_VAL_DIM
    n = seq // CHUNK
    chunked = lambda t: jnp.moveaxis(t.reshape(bsz, nh, n, CHUNK), 1, 2)
    gc = chunked(jnp.cumsum(g.reshape(bsz, nh, n, CHUNK), axis=-1))
    side = pl.BlockSpec((1, GDN_G, nh, CHUNK), lambda b, c: (b, c, 0, 0))
    rows = GDN_G * CHUNK
    width = nh * dk

    def col(name):
        off, w = PROJ_DST[name]
        assert w == width
        return pl.BlockSpec((1, rows, w), lambda b, c: (b, c, off // w))
    cw = jnp.transpose(conv_w.astype(jnp.float32).reshape(CONV_WIDTH, 3, width), (1, 0, 2))
    proj3 = proj.reshape(bsz, seq, proj.shape[-1])
    return pl.pallas_call(
        _gdn_body, grid=(bsz, n // GDN_G),
        in_specs=[col("qb"), col("kb"), col("vb"),
                  pl.BlockSpec((3, CONV_WIDTH, width), lambda b, c: (0, 0, 0)),
                  side, side, pl.BlockSpec((1, dv), lambda b, c: (0, 0))],
        out_specs=pl.BlockSpec((1, rows, nh * dv), lambda b, c: (b, c, 0)),
        out_shape=jax.ShapeDtypeStruct((bsz, seq, nh * dv), jnp.float32),
        scratch_shapes=[pltpu.VMEM((nh, dk, dv), jnp.float32),
                        pltpu.VMEM((3, GDN_PAD + rows, width), jnp.float32),
                        pltpu.VMEM((3, rows, width), jnp.float32)],
        compiler_params=pltpu.CompilerParams(dimension_semantics=("arbitrary", "arbitrary"),
                                             vmem_limit_bytes=GDN_VMEM_LIMIT),
        name="gated_delta_rule")(proj3, proj3, proj3, cw, gc, chunked(beta), norm_w.astype(jnp.float32).reshape(1, dv))


IN_NAMES = ("qa", "ka", "va", "qi", "ki", "wi", "qb", "kb", "vb", "zb", "bb", "ab", "gate_a", "gate_b")
PROJ_ORDER = ("gate_a", "gate_b", "qi", "qa", "qb", "kb", "vb", "zb", "ka", "va", "ki", "wi", "bb", "ab")


def _proj_layout():
    src, off = {}, 0
    for name, w in zip(IN_NAMES, IN_SPLITS):
        src[name] = (off, w)
        off += w
    dst, off = {}, 0
    for name in PROJ_ORDER:
        dst[name] = (off, src[name][1])
        off += src[name][1]
    return src, dst, -(-off // LANES) * LANES


PROJ_SRC, PROJ_DST, PROJ_WIDTH = _proj_layout()
PROJ_TM = 256


def permute_w_in(w_in):
    cols = [w_in[:, PROJ_SRC[n][0]:PROJ_SRC[n][0] + PROJ_SRC[n][1]] for n in PROJ_ORDER]
    cols.append(jnp.zeros((w_in.shape[0], PROJ_WIDTH - sum(c.shape[1] for c in cols)), w_in.dtype))
    return jnp.concatenate(cols, axis=1).astype(jnp.bfloat16)


def _modulated_norm(x, scale, shift):
    return x * lax.rsqrt(jnp.mean(x * x, axis=-1, keepdims=True) + EPS) * scale + shift


def _sigmoid(v):
    return 1.0 / (1.0 + jnp.exp(-v))


def _ada_body(c_ref, w_ref, b_ref, o_ref):
    c = c_ref[...]
    o_ref[...] = jnp.dot((c * _sigmoid(c)).astype(jnp.bfloat16), w_ref[...].astype(jnp.bfloat16),
                         preferred_element_type=jnp.float32) + b_ref[...]


def ada_modulation(c, w_ada, b_ada):
    bsz, d = c.shape
    n = w_ada.shape[1]
    tn = D_MODEL
    return pl.pallas_call(
        _ada_body, grid=(n // tn,),
        in_specs=[pl.BlockSpec((bsz, d), lambda j: (0, 0)), pl.BlockSpec((d, tn), lambda j: (0, j)),
                  pl.BlockSpec((1, tn), lambda j: (0, j))],
        out_specs=pl.BlockSpec((bsz, tn), lambda j: (0, j)),
        out_shape=jax.ShapeDtypeStruct((bsz, n), jnp.float32),
        name="ada_modulation")(c, w_ada, b_ada.reshape(1, n))


def _in_proj_body(x_ref, sc_ref, sh_ref, w_ref, o_ref):
    n1 = _modulated_norm(x_ref[...], sc_ref[0], sh_ref[0])
    o_ref[...] = jnp.dot(n1.astype(jnp.bfloat16), w_ref[...], preferred_element_type=jnp.float32)


def in_proj(x2, scale, shift, w_perm, seq):
    n, d = x2.shape
    tm = PROJ_TM
    per = seq // tm
    vec = pl.BlockSpec((1, 1, d), lambda i: (i // per, 0, 0))
    return pl.pallas_call(
        _in_proj_body, grid=(n // tm,),
        in_specs=[pl.BlockSpec((tm, d), lambda i: (i, 0)), vec, vec,
                  pl.BlockSpec((d, PROJ_WIDTH), lambda i: (0, 0), pipeline_mode=pl.Buffered(1))],
        out_specs=pl.BlockSpec((tm, PROJ_WIDTH), lambda i: (i, 0)),
        out_shape=jax.ShapeDtypeStruct((n, PROJ_WIDTH), jnp.float32),
        compiler_params=pltpu.CompilerParams(dimension_semantics=("arbitrary",), vmem_limit_bytes=PROJ_VMEM_LIMIT),
        name="in_proj")(x2, scale, shift, w_perm)


def _merge_body(oa_ref, ob_ref, z_ref, ga_ref, gb_ref, x_ref, gt_ref, sc_ref, sh_ref, wpa_ref, wpb_ref, wo_ref,
                h_ref, n2_ref):
    f32, bf = jnp.float32, jnp.bfloat16
    z = z_ref[...]
    ob = ob_ref[...] * (z * _sigmoid(z))
    oa = oa_ref[0].reshape(A_WIDTH, oa_ref.shape[-1]).T
    ya = jnp.dot(oa.astype(bf), wpa_ref[...], preferred_element_type=f32)
    yb = jnp.dot(ob.astype(bf), wpb_ref[...], preferred_element_type=f32)
    merged = _sigmoid(ga_ref[...]) * ya + _sigmoid(gb_ref[...]) * yb
    y1 = jnp.dot(merged.astype(bf), wo_ref[...], preferred_element_type=f32)
    h = x_ref[...] + gt_ref[0] * y1
    h_ref[...] = h
    n2_ref[...] = _modulated_norm(h, sc_ref[0], sh_ref[0])


def merge(o_a, o_b, proj, x2, gt1, scale2, shift2, w_pa, w_pb, w_o, seq):
    n, d = x2.shape
    tm = PROJ_TM
    per = seq // tm
    bf = jnp.bfloat16
    vec = pl.BlockSpec((1, 1, d), lambda i: (i // per, 0, 0))

    def col(name):
        off, w = PROJ_DST[name]
        return pl.BlockSpec((tm, w), lambda i: (i, off // w))
    row = lambda w: pl.BlockSpec((tm, w), lambda i: (i, 0))
    res = lambda a, b: pl.BlockSpec((a, b), lambda i: (0, 0))
    return pl.pallas_call(
        _merge_body, grid=(n // tm,),
        in_specs=[pl.BlockSpec((1, A_HEADS, A_HEAD_DIM, tm), lambda i: (i // per, 0, 0, i % per)), row(B_V_WIDTH), col("zb"), col("gate_a"), col("gate_b"), row(d), vec, vec, vec,
                  res(A_WIDTH, d), res(B_V_WIDTH, d), res(d, d)],
        out_specs=[row(d), row(d)],
        out_shape=[jax.ShapeDtypeStruct((n, d), jnp.float32), jax.ShapeDtypeStruct((n, d), jnp.float32)],
        compiler_params=pltpu.CompilerParams(dimension_semantics=("arbitrary",), vmem_limit_bytes=PROJ_VMEM_LIMIT),
        name="merge")(o_a, o_b, proj, proj, proj, x2, gt1, scale2, shift2,
                      w_pa.astype(bf), w_pb.astype(bf), w_o.astype(bf))


def token_mixers(proj, positions, conv_w, a_log, dt_bias, norm_b_w, bsz, seq):
    qit, qat, w, kit, kat, vat, bg = dsa_prep(proj, positions, a_log, dt_bias, bsz, seq)
    o_a = dsa_attention_pallas(qit, w, qat, kit, kat, vat)

    o_b = gated_delta_rule_pallas(proj, conv_w, bg[:, B_HEADS:], bg[:, :B_HEADS], norm_b_w, bsz, seq)
    return o_a, o_b


PEER_SLOTS = PEER_HEADS * PEER_TOPK
PEER_TB = 32
HALF_ROWS = 4
HI_MASK = -65536


def pack_table(tab):
    bits = lax.bitcast_convert_type(tab.astype(jnp.bfloat16), jnp.uint16).astype(jnp.uint32)
    half = tab.shape[1] // 2
    word = bits[:, :half] | (bits[:, half:] << 16)
    return lax.bitcast_convert_type(word, jnp.int32).reshape(tab.shape[0] * HALF_ROWS, LANES)


def _table_row(tab_ref, rows, k):
    return _unpack(tab_ref[pl.ds(pl.multiple_of(rows[k], HALF_ROWS), HALF_ROWS), :])


def _unpack(w):
    lo = lax.bitcast_convert_type(lax.shift_left(w, 16), jnp.float32)
    hi = lax.bitcast_convert_type(w & jnp.int32(HI_MASK), jnp.float32)
    return lo, hi


def _peer_u_body(idx_ref, x_ref, gate_ref, tab_ref, o_ref, s_ref, sb_ref):
    def lane_sums(t):
        a = jnp.sum(sb_ref[t].T, axis=0, keepdims=True)
        o_ref[pl.ds(t, 1), :] = 0.5 * a * (1.0 + lax.erf(a * (2.0 ** -0.5))) * gate_ref[pl.ds(t, 1), :]

    sb_ref[0] = jnp.zeros(sb_ref.shape[1:], jnp.float32)

    def tok(t, carry):
        lane_sums(jnp.maximum(t - 1, 0))
        x8 = x_ref[pl.ds(t, 1), :].reshape(2 * HALF_ROWS, LANES)
        xlo, xhi = x8[0:HALF_ROWS], x8[HALF_ROWS:2 * HALF_ROWS]
        rows = idx_ref.at[t]
        for k in range(PEER_SLOTS):
            lo, hi = _table_row(tab_ref, rows, k)
            s_ref[HALF_ROWS * k:HALF_ROWS * (k + 1), :] = lo * xlo + hi * xhi
        s4 = s_ref[pl.ds(0, PEER_SLOTS, stride=HALF_ROWS), :]
        for r in range(1, HALF_ROWS):
            s4 = s4 + s_ref[pl.ds(r, PEER_SLOTS, stride=HALF_ROWS), :]
        sb_ref[t] = s4
        return carry
    lax.fori_loop(0, PEER_TB, tok, 0, unroll=2)
    lane_sums(PEER_TB - 1)


def _peer_v_body(idx_ref, coef_ref, tab_ref, o_ref, cb_ref):
    def spread(t):
        return jnp.broadcast_to(coef_ref[pl.ds(t, 1), :], (PEER_SLOTS, LANES)).T

    def tok(t, weights):
        nxt = spread(jnp.minimum(t + 1, PEER_TB - 1))
        cb_ref[...] = weights
        nacc = 2
        acc = [jnp.zeros((HALF_ROWS, LANES), jnp.float32) for _ in range(2 * nacc)]
        rows = idx_ref.at[t]
        for k in range(PEER_SLOTS):
            lo, hi = _table_row(tab_ref, rows, k)
            c = jnp.broadcast_to(cb_ref[k:k + 1, :], (HALF_ROWS, LANES))
            a = k % nacc
            acc[2 * a] = acc[2 * a] + c * lo
            acc[2 * a + 1] = acc[2 * a + 1] + c * hi
        half = HALF_ROWS * LANES
        o_ref[pl.ds(t, 1), 0:half] = (acc[0] + acc[2]).reshape(1, half)
        o_ref[pl.ds(t, 1), half:2 * half] = (acc[1] + acc[3]).reshape(1, half)
        return nxt
    lax.fori_loop(0, PEER_TB, tok, spread(0))


def _table_spec():
    return pl.BlockSpec((PEER_N_EXPERTS * HALF_ROWS, LANES), lambda i: (0, 0), pipeline_mode=pl.Buffered(1))


def peer_u(idx, x, gates, tab):
    n = idx.shape[0]
    tb = PEER_TB
    return pl.pallas_call(
        _peer_u_body, grid=(n // tb,),
        in_specs=[pl.BlockSpec((tb, PEER_SLOTS), lambda i: (i, 0), memory_space=pltpu.SMEM),
                  pl.BlockSpec((tb, x.shape[1]), lambda i: (i, 0)),
                  pl.BlockSpec((tb, PEER_SLOTS), lambda i: (i, 0)),
                  _table_spec()],
        out_specs=pl.BlockSpec((tb, PEER_SLOTS), lambda i: (i, 0)),
        out_shape=jax.ShapeDtypeStruct((n, PEER_SLOTS), jnp.float32),
        scratch_shapes=[pltpu.VMEM((HALF_ROWS * PEER_SLOTS, LANES), jnp.float32),
                        pltpu.VMEM((tb, PEER_SLOTS, LANES), jnp.float32)],
        compiler_params=pltpu.CompilerParams(dimension_semantics=("arbitrary",), vmem_limit_bytes=PEER_VMEM_LIMIT),
        name="peer_u")(idx, x, gates, tab)


def peer_v(idx, coef, tab):
    n = idx.shape[0]
    tb = PEER_TB
    return pl.pallas_call(
        _peer_v_body, grid=(n // tb,),
        in_specs=[pl.BlockSpec((tb, PEER_SLOTS), lambda i: (i, 0), memory_space=pltpu.SMEM),
                  pl.BlockSpec((tb, PEER_SLOTS), lambda i: (i, 0)),
                  _table_spec()],
        out_specs=pl.BlockSpec((tb, 2 * HALF_ROWS * LANES), lambda i: (i, 0)),
        out_shape=jax.ShapeDtypeStruct((n, 2 * HALF_ROWS * LANES), jnp.float32),
        scratch_shapes=[pltpu.VMEM((PEER_SLOTS, LANES), jnp.float32)],
        compiler_params=pltpu.CompilerParams(dimension_semantics=("arbitrary",), vmem_limit_bytes=PEER_VMEM_LIMIT),
        name="peer_v")(idx, coef, tab)


PEER_TT = 256
PAD_ID = 2**30


def _extract_top(ref, n_out, rid=None):
    rows, t = ref.shape
    if rid is None:
        rid = lax.broadcasted_iota(jnp.int32, (rows, t), 0)
    vals, idxs = [], []
    for _ in range(n_out):
        s = ref[...]
        m = jnp.max(s, axis=0, keepdims=True)
        ix = jnp.min(jnp.where(s == m, rid, jnp.int32(PAD_ID)), axis=0, keepdims=True)
        ref[...] = jnp.where(rid == ix, -jnp.inf, s)
        vals.append(m)
        idxs.append(ix)
    return vals, idxs


def _peer_route_body(x_ref, wq_ref, sk_ref, cid_ref, idx_ref, gate_ref, s_ref, cand_ref, v_ref, i_ref, et_ref, gt_ref):
    kk, nk = PEER_TOPK, PEER_N_KEYS
    f32 = jnp.float32
    q = jnp.dot(x_ref[...].astype(jnp.bfloat16), wq_ref[...], preferred_element_type=f32).astype(jnp.bfloat16)
    nt = (((1,), (1,)), ((), ()))
    row16 = lax.broadcasted_iota(jnp.int32, (kk, PEER_TT), 0)
    for h in range(PEER_HEADS):
        qh = q[:, h * PEER_KEY_DIM:(h + 1) * PEER_KEY_DIM]
        s_ref[...] = lax.dot_general(sk_ref[h], qh, nt, preferred_element_type=f32)
        for p in range(2):
            vals, idxs = _extract_top(s_ref.at[p * nk:(p + 1) * nk, :], kk)
            for i in range(kk):
                v_ref[p, i:i + 1, :] = vals[i]
                i_ref[p, i:i + 1, :] = idxs[i]
        off = 0
        for i in range(kk):
            n_j = kk // (i + 1)
            cand_ref[off:off + n_j, :] = v_ref[0, i:i + 1, :] + v_ref[1, 0:n_j, :]
            off += n_j
        cand_ref[off:, :] = jnp.full((cand_ref.shape[0] - off, PEER_TT), -jnp.inf, f32)
        vals, cis = _extract_top(cand_ref, kk, cid_ref[...])
        i1, i2 = i_ref[0], i_ref[1]
        es = [jnp.exp(v - vals[0]) for v in vals]
        den = es[0]
        for e in es[1:]:
            den = den + e
        for k in range(kk):
            ci = cis[k]
            e1 = jnp.sum(jnp.where(row16 == lax.shift_right_logical(ci, 4), i1, 0), axis=0, keepdims=True)
            e2 = jnp.sum(jnp.where(row16 == (ci & 15), i2, 0), axis=0, keepdims=True)
            et_ref[h * kk + k:h * kk + k + 1, :] = (e1 * nk + e2) * HALF_ROWS
            gt_ref[h * kk + k:h * kk + k + 1, :] = es[k] / den
    idx_ref[...] = lax.bitcast_convert_type(lax.bitcast_convert_type(et_ref[...], f32).T, jnp.int32)
    gate_ref[...] = gt_ref[...].T


def _pair_cells():
    kk = PEER_TOPK
    ids = [i * kk + j for i in range(kk) for j in range(kk // (i + 1))]
    return ids + [PAD_ID] * (-len(ids) % SUBLANES)


def peer_route(xn, wq, subkeys):
    n = xn.shape[0]
    tt = PEER_TT
    cells = _pair_cells()
    cid = jnp.broadcast_to(jnp.asarray(cells, jnp.int32)[:, None], (len(cells), tt))
    half = PEER_KEY_DIM // 2
    z = jnp.zeros((PEER_HEADS, PEER_N_KEYS, half), subkeys.dtype)
    skbd = jnp.concatenate([jnp.concatenate([subkeys[:, 0], z], axis=-1),
                            jnp.concatenate([z, subkeys[:, 1]], axis=-1)], axis=1).astype(jnp.bfloat16)
    return pl.pallas_call(
        _peer_route_body, grid=(n // tt,),
        in_specs=[pl.BlockSpec((tt, D_MODEL), lambda i: (i, 0)),
                  pl.BlockSpec((D_MODEL, PEER_HEADS * PEER_KEY_DIM), lambda i: (0, 0)),
                  pl.BlockSpec((PEER_HEADS, 2 * PEER_N_KEYS, PEER_KEY_DIM), lambda i: (0, 0, 0)),
                  pl.BlockSpec((len(cells), tt), lambda i: (0, 0))],
        out_specs=[pl.BlockSpec((tt, PEER_SLOTS), lambda i: (i, 0)), pl.BlockSpec((tt, PEER_SLOTS), lambda i: (i, 0))],
        out_shape=[jax.ShapeDtypeStruct((n, PEER_SLOTS), jnp.int32), jax.ShapeDtypeStruct((n, PEER_SLOTS), jnp.float32)],
        scratch_shapes=[pltpu.VMEM((2 * PEER_N_KEYS, tt), jnp.float32),
                        pltpu.VMEM((len(cells), tt), jnp.float32),
                        pltpu.VMEM((2, PEER_TOPK, tt), jnp.float32),
                        pltpu.VMEM((2, PEER_TOPK, tt), jnp.int32),
                        pltpu.VMEM((PEER_SLOTS, tt), jnp.int32),
                        pltpu.VMEM((PEER_SLOTS, tt), jnp.float32)],
        compiler_params=pltpu.CompilerParams(dimension_semantics=("arbitrary",), vmem_limit_bytes=ROUTE_VMEM_LIMIT),
        name="peer_route")(xn, wq.astype(jnp.bfloat16), skbd, cid)


def peer_channel_mixer(xn, wq, subkeys, u_tab, v_tab):
    bsz, seq, d = xn.shape
    n_tok = bsz * seq
    x2 = xn.reshape(n_tok, d)
    idx, gates = peer_route(x2, wq, subkeys)
    coef = peer_u(idx, x2, gates, pack_table(u_tab))
    out = peer_v(idx, coef, pack_table(v_tab))
    return out.reshape(bsz, seq, d)


def _residual_body(h_ref, y_ref, gt_ref, o_ref):
    o_ref[...] = h_ref[...] + gt_ref[0] * y_ref[...]


def _final_norm_body(h_ref, y_ref, gt_ref, g_ref, o_ref):
    h = h_ref[...] + gt_ref[0] * y_ref[...]
    o_ref[...] = h * lax.rsqrt(jnp.mean(h * h, axis=-1, keepdims=True) + EPS) * g_ref[...]


def gated_residual(h2, y2, gt, seq, gain=None):
    n, d = h2.shape
    tm = min(1024, seq)
    per = seq // tm
    row = pl.BlockSpec((tm, d), lambda i: (i, 0))
    specs = [row, row, pl.BlockSpec((1, 1, d), lambda i: (i // per, 0, 0))]
    args = [h2, y2, gt]
    if gain is not None:
        specs.append(pl.BlockSpec((1, d), lambda i: (0, 0)))
        args.append(gain.astype(jnp.float32).reshape(1, d))
    return pl.pallas_call(
        _residual_body if gain is None else _final_norm_body, grid=(n // tm,),
        in_specs=specs, out_specs=row, out_shape=jax.ShapeDtypeStruct((n, d), h2.dtype),
        name="gated_residual" if gain is None else "final_norm")(*args)


def kernel(x, c, positions, w_ada, b_ada, w_in, conv_w, a_log, dt_bias, norm_b_w,
           w_pa, w_pb, w_o, peer_wq, peer_subkeys, peer_u, peer_v, final_norm_w):
    bsz, seq, d = x.shape
    h = x.reshape(bsz * seq, d)
    for layer in range(DEPTH):
        mod = ada_modulation(c, w_ada[layer], b_ada[layer])
        sh1, sc1, gt1, sh2, sc2, gt2 = [m.reshape(bsz, 1, d) for m in jnp.split(mod, 6, axis=-1)]
        proj = in_proj(h, 1.0 + sc1, sh1, permute_w_in(w_in[layer]), seq)
        o_a, o_b = token_mixers(proj, positions, conv_w[layer], a_log[layer], dt_bias[layer], norm_b_w[layer],
                                bsz, seq)
        h, n2 = merge(o_a, o_b.reshape(bsz * seq, B_V_WIDTH), proj, h, gt1,
                      1.0 + sc2, sh2, w_pa[layer], w_pb[layer], w_o[layer], seq)
        y2 = peer_channel_mixer(n2.reshape(bsz, seq, d), peer_wq[layer], peer_subkeys[layer], peer_u[layer],
                                peer_v[layer])
        last = layer == DEPTH - 1
        h = gated_residual(h, y2.reshape(bsz * seq, d), gt2, seq, final_norm_w if last else None)
    return h.reshape(bsz, seq, d)
```

```python
import functools

import jax, jax.numpy as jnp
from jax import lax
from jax.experimental import pallas as pl
from jax.experimental.pallas import tpu as pltpu

D_MODEL = 1024
DEPTH = 1

LANES = 128
SUBLANES = 8
MIB = 1024 * 1024
DSA_VMEM_LIMIT = 48 * MIB
PREP_VMEM_LIMIT = 32 * MIB
GDN_VMEM_LIMIT = 40 * MIB
PROJ_VMEM_LIMIT = 48 * MIB
PEER_VMEM_LIMIT = 48 * MIB
ROUTE_VMEM_LIMIT = 32 * MIB

A_HEADS = 8
A_KV_HEADS = 2
A_HEAD_DIM = 64
IDX_HEADS = 16
IDX_DIM = 64
IDX_TOPK_MAX = 256
B_HEADS = 8
B_KEY_DIM = 64
B_VAL_DIM = 64
CONV_WIDTH = 4
CHUNK = 64
ROPE_THETA = 500000.0
ROPE_FRACTION_DEN = 4
PEER_HEADS = 8
PEER_KEY_DIM = 128
PEER_N_KEYS = 128
PEER_N_EXPERTS = PEER_N_KEYS * PEER_N_KEYS
PEER_TOPK = 16
EPS = 1e-6

A_WIDTH = A_HEADS * A_HEAD_DIM
KV_WIDTH = A_KV_HEADS * A_HEAD_DIM
B_QK_WIDTH = B_HEADS * B_KEY_DIM
B_V_WIDTH = B_HEADS * B_VAL_DIM
IN_SPLITS = (A_WIDTH, KV_WIDTH, KV_WIDTH, IDX_HEADS * IDX_DIM, IDX_DIM, IDX_HEADS,
             B_QK_WIDTH, B_QK_WIDTH, B_V_WIDTH, B_V_WIDTH, B_HEADS, B_HEADS, D_MODEL, D_MODEL)


DSA_TQ = 256
DSA_TK = 256
INT_MIN = -2**31
NEG_BIG = -1e30


def _dsa_body(topk, qit_ref, w_ref, qat_ref, ki_ref, ka_ref, vat_ref, o_ref,
              key_ref):
    tq, tk = DSA_TQ, DSA_TK
    qb = pl.program_id(1)
    n_kv = qb + 1
    t_glob = qb * tq + lax.broadcasted_iota(jnp.int32, (1, tq), 1)
    row = lax.broadcasted_iota(jnp.int32, (tk, 1), 0)
    f32 = jnp.float32

    def p1(j, carry):
        kt = ki_ref[0, j]
        score = jnp.zeros((tk, tq), f32)
        for h in range(IDX_HEADS):
            lt = jnp.dot(kt, qit_ref[0, h], preferred_element_type=f32)
            score = score + w_ref[0, h:h + 1, :] * jnp.maximum(lt, 0.0)
        bits = lax.bitcast_convert_type(score + 0.0, jnp.int32)
        skey = jnp.where(bits >= 0, bits, bits ^ jnp.int32(0x7FFFFFFF))
        skey = jnp.where(j * tk + row <= t_glob, skey, jnp.int32(INT_MIN))
        key_ref[j] = skey
        return carry
    lax.fori_loop(0, n_kv, p1, 0)

    key_ref[n_kv] = jnp.full((tk, tq), INT_MIN, jnp.int32)

    def count(pred):
        def body(jj, acc):
            for j in (2 * jj, 2 * jj + 1):
                hit = jnp.where(pred(key_ref[j], j * tk + row), 1.0, 0.0)
                acc = acc + jnp.sum(hit.reshape(tk // SUBLANES, SUBLANES, tq), axis=0)
            return acc
        acc = lax.fori_loop(0, (n_kv + 1) // 2, body, jnp.zeros((SUBLANES, tq), f32))
        return jnp.sum(acc, axis=0, keepdims=True)

    kf = jnp.float32(topk)

    def bit_step(i, ku):
        cand_u = ku | lax.shift_left(jnp.int32(1), 31 - i)
        cand = cand_u ^ jnp.int32(INT_MIN)
        c = count(lambda k, s: k >= cand)
        return jnp.where(c >= kf, cand_u, ku)
    ku = lax.fori_loop(0, 32, bit_step, jnp.zeros((1, tq), jnp.int32))
    kth = ku ^ jnp.int32(INT_MIN)
    c_gt = count(lambda k, s: k > kth)
    c_ge = count(lambda k, s: k >= kth)
    short = kth == jnp.int32(INT_MIN)
    x0 = jnp.where(short, jnp.int32(-1), jnp.int32(2**30))
    need = kf - c_gt
    has_tie = jnp.max(jnp.where(jnp.logical_and(c_ge > kf, jnp.logical_not(short)), 1.0, 0.0)) > 0.0

    def tie_search():
        def step(i, x):
            bit = lax.shift_left(jnp.int32(1), 11 - i)
            probe = x + bit - 1
            c = count(lambda k, s: jnp.logical_and(k == kth, s <= probe))
            return jnp.where(c < need, x + bit, x)
        x = lax.fori_loop(0, 12, step, jnp.zeros((1, tq), jnp.int32))
        return jnp.where(short, jnp.int32(-1), x)
    x_lim = lax.cond(has_tie, tie_search, lambda: x0)

    rep = A_HEADS // A_KV_HEADS
    hs = range(A_HEADS)

    def p3(j, carry):
        m, l, acc = carry
        skey = key_ref[j]
        s_idx = j * tk + row
        sel = jnp.logical_or(skey > kth, jnp.logical_and(skey == kth, s_idx <= x_lim))
        kt = [ka_ref[0, g, j] for g in range(A_KV_HEADS)]
        vt = [vat_ref[0, g, j] for g in range(A_KV_HEADS)]
        s = [jnp.where(sel, jnp.dot(kt[h // rep], qat_ref[0, h], preferred_element_type=f32), NEG_BIG) for h in hs]
        m_new = [jnp.maximum(m[h], jnp.max(s[h], axis=0, keepdims=True)) for h in hs]
        alpha = [jnp.exp(m[h] - m_new[h]) for h in hs]
        p = [jnp.exp(s[h] - m_new[h]) for h in hs]
        l_new = [alpha[h] * l[h] + jnp.sum(p[h], axis=0, keepdims=True) for h in hs]
        acc_new = [alpha[h] * acc[h] + jnp.dot(vt[h // rep], p[h].astype(jnp.bfloat16), preferred_element_type=f32)
                   for h in hs]
        return tuple(m_new), tuple(l_new), tuple(acc_new)

    init = (tuple(jnp.full((1, tq), NEG_BIG, f32) for _ in hs), tuple(jnp.zeros((1, tq), f32) for _ in hs),
            tuple(jnp.zeros((A_HEAD_DIM, tq), f32) for _ in hs))
    _, l_fin, acc_fin = lax.fori_loop(0, n_kv, p3, init)
    for h in hs:
        o_ref[0, h] = acc_fin[h] / l_fin[h]


ROPE_HALF = A_HEAD_DIM // ROPE_FRACTION_DEN // 2


def _rope_t(xt, n_heads, cos, sin, scale=None):
    outs = []
    for h in range(n_heads):
        b = h * A_HEAD_DIM
        x1, x2, rest = xt[b:b + ROPE_HALF], xt[b + ROPE_HALF:b + 2 * ROPE_HALF], xt[b + 2 * ROPE_HALF:b + A_HEAD_DIM]
        o = jnp.concatenate([x1 * cos - x2 * sin, x2 * cos + x1 * sin, rest], axis=0)
        outs.append(o if scale is None else o * scale)
    return outs


def _dsa_prep_body(qi_ref, qa_ref, ka_ref, va_ref, sm_ref, cos_ref, sin_ref, dp_ref,
                   qit_ref, qat_ref, w_ref, kit_ref, kat_ref, vat_ref, bg_ref):
    bf = jnp.bfloat16
    cos, sin = cos_ref[0], sin_ref[0]
    for h, o in enumerate(_rope_t(qi_ref[...].T, IDX_HEADS, cos, sin)):
        qit_ref[0, h] = o.astype(bf)
    for h, o in enumerate(_rope_t(qa_ref[...].T, A_HEADS, cos, sin, A_HEAD_DIM ** -0.5)):
        qat_ref[0, h] = o.astype(bf)
    kat = _rope_t(ka_ref[...].T, A_KV_HEADS, cos, sin)
    vt = va_ref[...].T
    for g in range(A_KV_HEADS):
        kat_ref[0, g, 0] = kat[g].T.astype(bf)
        vat_ref[0, g, 0] = vt[g * A_HEAD_DIM:(g + 1) * A_HEAD_DIM].astype(bf)
    smt = sm_ref[...].T
    kit_ref[0, 0] = _rope_t(smt[0:IDX_DIM], 1, cos, sin)[0].T.astype(bf)
    w_ref[0] = smt[IDX_DIM:IDX_DIM + IDX_HEADS] * ((IDX_HEADS ** -0.5) * (IDX_DIM ** -0.5))
    b0 = IDX_DIM + IDX_HEADS
    bb, ab = smt[b0:b0 + B_HEADS], smt[b0 + B_HEADS:b0 + 2 * B_HEADS]
    a_log, dt_bias = dp_ref[:, 0:1], dp_ref[:, 1:2]
    bg_ref[0] = jnp.concatenate([_sigmoid(bb), -jnp.exp(a_log) * jax.nn.softplus(ab + dt_bias)], axis=0)


def dsa_prep(proj, positions, a_log, dt_bias, bsz, seq):
    t = DSA_TQ
    per = seq // t
    assert DSA_TK == t and PROJ_ORDER[-4:] == ("ki", "wi", "bb", "ab") and PROJ_DST["ki"][0] + LANES == PROJ_WIDTH
    rd = A_HEAD_DIM // ROPE_FRACTION_DEN
    inv_freq = jnp.power(jnp.float32(ROPE_THETA), -jnp.arange(ROPE_HALF, dtype=jnp.float32) * (2.0 / rd))
    ang = positions.astype(jnp.float32)[:, None, :] * inv_freq[None, :, None]

    def col(name, width=None):
        off, w = PROJ_DST[name]
        w = width or w
        return pl.BlockSpec((t, w), lambda i: (i, off // w))
    trig = pl.BlockSpec((1, ROPE_HALF, t), lambda i: (i // per, 0, i % per))
    bf, f32 = jnp.bfloat16, jnp.float32
    sd = jax.ShapeDtypeStruct
    n_kv = seq // DSA_TK
    return pl.pallas_call(
        _dsa_prep_body, grid=(bsz * per,),
        in_specs=[col("qi"), col("qa"), col("ka"), col("va"), col("ki", LANES), trig, trig,
                  pl.BlockSpec((B_HEADS, 2), lambda i: (0, 0))],
        out_specs=[pl.BlockSpec((1, IDX_HEADS, IDX_DIM, t), lambda i: (i // per, 0, 0, i % per)),
                   pl.BlockSpec((1, A_HEADS, A_HEAD_DIM, t), lambda i: (i // per, 0, 0, i % per)),
                   pl.BlockSpec((1, IDX_HEADS, t), lambda i: (i // per, 0, i % per)),
                   pl.BlockSpec((1, 1, t, IDX_DIM), lambda i: (i // per, i % per, 0, 0)),
                   pl.BlockSpec((1, A_KV_HEADS, 1, t, A_HEAD_DIM), lambda i: (i // per, 0, i % per, 0, 0)),
                   pl.BlockSpec((1, A_KV_HEADS, 1, A_HEAD_DIM, t), lambda i: (i // per, 0, i % per, 0, 0)),
                   pl.BlockSpec((1, 2 * B_HEADS, t), lambda i: (i // per, 0, i % per))],
        out_shape=[sd((bsz, IDX_HEADS, IDX_DIM, seq), bf), sd((bsz, A_HEADS, A_HEAD_DIM, seq), bf),
                   sd((bsz, IDX_HEADS, seq), f32), sd((bsz, n_kv, DSA_TK, IDX_DIM), bf),
                   sd((bsz, A_KV_HEADS, n_kv, DSA_TK, A_HEAD_DIM), bf),
                   sd((bsz, A_KV_HEADS, n_kv, A_HEAD_DIM, DSA_TK), bf), sd((bsz, 2 * B_HEADS, seq), f32)],
        compiler_params=pltpu.CompilerParams(dimension_semantics=("arbitrary",), vmem_limit_bytes=PREP_VMEM_LIMIT),
        name="dsa_prep")(proj, proj, proj, proj, proj, jnp.cos(ang), jnp.sin(ang),
                         jnp.stack([a_log, dt_bias], axis=-1).astype(f32))


def dsa_attention_pallas(qit, w, qat, kit, kat, vat):
    bsz, seq = qit.shape[0], qit.shape[-1]
    tq, tk = DSA_TQ, DSA_TK
    topk = min(IDX_TOPK_MAX, seq // 4)
    n_kv = seq // tk
    return pl.pallas_call(
        functools.partial(_dsa_body, topk),
        grid=(bsz, seq // tq),
        in_specs=[
            pl.BlockSpec((1, IDX_HEADS, IDX_DIM, tq), lambda b, q: (b, 0, 0, q)),
            pl.BlockSpec((1, IDX_HEADS, tq), lambda b, q: (b, 0, q)),
            pl.BlockSpec((1, A_HEADS, A_HEAD_DIM, tq), lambda b, q: (b, 0, 0, q)),
            pl.BlockSpec((1, n_kv, tk, IDX_DIM), lambda b, q: (b, 0, 0, 0)),
            pl.BlockSpec((1, A_KV_HEADS, n_kv, tk, A_HEAD_DIM), lambda b, q: (b, 0, 0, 0, 0)),
            pl.BlockSpec((1, A_KV_HEADS, n_kv, A_HEAD_DIM, tk), lambda b, q: (b, 0, 0, 0, 0)),
        ],
        out_specs=pl.BlockSpec((1, A_HEADS, A_HEAD_DIM, tq), lambda b, q: (b, 0, 0, q)),
        out_shape=jax.ShapeDtypeStruct((bsz, A_HEADS, A_HEAD_DIM, seq), jnp.float32),
        scratch_shapes=[
            pltpu.VMEM((n_kv + 1, tk, tq), jnp.int32),
        ],
        compiler_params=pltpu.CompilerParams(dimension_semantics=("arbitrary", "arbitrary"),
                                             vmem_limit_bytes=DSA_VMEM_LIMIT),
        name="dsa_attention",
    )(qit, w, qat, kit, kat, vat)


GDN_G = 8
GDN_BASE = 8
GDN_PAD = 8


def _gdn_body(q_ref, k_ref, v_ref, cw_ref, gc_ref, bt_ref, nw_ref, o_ref, s_ref, xs_ref, cs_ref):
    c_sz = CHUNK
    f32, bf = jnp.float32, jnp.bfloat16

    rows = GDN_G * c_sz
    halo = CONV_WIDTH - 1

    @pl.when(pl.program_id(1) == 0)
    def _():
        s_ref[...] = jnp.zeros(s_ref.shape, f32)
        xs_ref[:, 0:GDN_PAD, :] = jnp.zeros((3, GDN_PAD, xs_ref.shape[-1]), f32)

    for j, ref in enumerate((q_ref, k_ref, v_ref)):
        xs_ref[j, GDN_PAD:GDN_PAD + rows, :] = ref[0]
        acc = xs_ref[j, GDN_PAD - halo:GDN_PAD - halo + rows, :] * cw_ref[j, 0:1, :]
        for i in range(1, CONV_WIDTH):
            acc = acc + xs_ref[j, GDN_PAD - halo + i:GDN_PAD - halo + i + rows, :] * cw_ref[j, i:i + 1, :]
        cs_ref[j] = acc * _sigmoid(acc)
        xs_ref[j, GDN_PAD - halo:GDN_PAD, :] = xs_ref[j, GDN_PAD + rows - halo:GDN_PAD + rows, :]

    ri = lax.broadcasted_iota(jnp.int32, (c_sz, c_sz), 0)
    ci = lax.broadcasted_iota(jnp.int32, (c_sz, c_sz), 1)
    incl, strict = ri >= ci, ri > ci
    eye = jnp.where(ri == ci, 1.0, 0.0).astype(f32)
    blk = lambda w: (ri // w) == (ci // w)
    diag8 = blk(GDN_BASE)
    sub_blocks = []
    w = GDN_BASE
    while w < c_sz:
        sub_blocks.append(jnp.logical_and(blk(2 * w), jnp.logical_not(blk(w))))
        w *= 2
    nt = (((1,), (1,)), ((), ()))
    dot = lambda a, b: jnp.dot(a.astype(bf), b.astype(bf), preferred_element_type=f32)
    dot_nt = lambda a, b: lax.dot_general(a.astype(bf), b.astype(bf), nt, preferred_element_type=f32)

    def split(a):
        hi = a.astype(bf)
        return hi, (a - hi.astype(f32)).astype(bf)

    def hp(a, b):
        a_hi, a_lo = split(a)
        b_hi, b_lo = split(b)
        mm = lambda u, w: jnp.dot(u, w, preferred_element_type=f32)
        return mm(a_hi, b_hi) + (mm(a_hi, b_lo) + mm(a_lo, b_hi))

    hs = range(B_HEADS)

    def chunk_pair(i, carry):
        cs = [2 * i, 2 * i + 1]
        r0 = [pl.multiple_of(c * c_sz, c_sz) for c in cs]
        items = [(j, h) for j in range(2) for h in hs]
        hd = lambda a, j, h: cs_ref[a, pl.ds(r0[j], c_sz), h * B_KEY_DIM:(h + 1) * B_KEY_DIM]
        l2n = lambda t: t * lax.rsqrt(jnp.sum(t * t, axis=-1, keepdims=True) + EPS)
        q = [l2n(hd(0, j, h)) * (B_KEY_DIM ** -0.5) for j, h in items]
        k = [l2n(hd(1, j, h)) for j, h in items]
        v = [hd(2, j, h) for j, h in items]
        gc8 = [gc_ref[0, c] for c in cs]
        gct = [g8.T for g8 in gc8]
        btt = [bt_ref[0, c].T for c in cs]
        gcr = [gc8[j][h:h + 1, :] for j, h in items]
        n_it = range(len(items))
        gcc = [gct[j][:, h:h + 1] for j, h in items]
        beta = [btt[j][:, h:h + 1] for j, h in items]
        decay = [jnp.exp(jnp.where(incl, gcc[n] - gcr[n], -jnp.inf)) for n in n_it]
        kb = [k[n] * beta[n] for n in n_it]
        vb = [v[n] * beta[n] for n in n_it]
        low = [jnp.where(strict, dot_nt(kb[n], k[n]) * decay[n], 0.0) for n in n_it]
        dg = [jnp.where(diag8, low[n], 0.0) for n in n_it]
        t = [eye - dg[n] for n in n_it]
        p = [hp(dg[n], dg[n]) for n in n_it]
        t = [hp(t[n], eye + p[n]) for n in n_it]
        p = [hp(p[n], p[n]) for n in n_it]
        t = [hp(t[n], eye + p[n]) for n in n_it]
        for below in sub_blocks:
            lb = [jnp.where(below, low[n], 0.0) for n in n_it]
            lt = [hp(lb[n], t[n]) for n in n_it]
            t = [t[n] - hp(t[n], lt[n]) for n in n_it]
        u = [dot(t[n], vb[n]) for n in n_it]
        kcd = [dot(t[n], kb[n] * jnp.exp(gcc[n])) for n in n_it]
        intra = [dot_nt(q[n], k[n]) * decay[n] for n in n_it]
        qg = [q[n] * jnp.exp(gcc[n]) for n in n_it]
        glast = [g[:, c_sz - 1:c_sz] for g in gcr]
        kdt = [(k[n] * jnp.exp(glast[n] - gcc[n])).T for n in n_it]
        s = [s_ref[h] for h in hs]
        for j in range(2):
            ix = [j * B_HEADS + h for h in hs]
            v_new = [u[ix[h]] - dot(kcd[ix[h]], s[h]) for h in hs]
            out = [dot(qg[ix[h]], s[h]) + dot(intra[ix[h]], v_new[h]) for h in hs]
            s = [s[h] * jnp.exp(glast[ix[h]]) + dot(kdt[ix[h]], v_new[h]) for h in hs]
            for h in hs:
                o = out[h]
                o = o * lax.rsqrt(jnp.mean(o * o, axis=-1, keepdims=True) + EPS) * nw_ref[...]
                o_ref[0, pl.ds(r0[j], c_sz), h * B_VAL_DIM:(h + 1) * B_VAL_DIM] = o
        for h in hs:
            s_ref[h] = s[h]
        return carry
    lax.fori_loop(0, GDN_G // 2, chunk_pair, 0)


def gated_delta_rule_pallas(proj, conv_w, g, beta, norm_w, bsz, seq):
    nh, dk, dv = B_HEADS, B_KEY_DIM, B_VAL_DIM
    n = seq // CHUNK
    chunked = lambda t: jnp.moveaxis(t.reshape(bsz, nh, n, CHUNK), 1, 2)
    gc = chunked(jnp.cumsum(g.reshape(bsz, nh, n, CHUNK), axis=-1))
    side = pl.BlockSpec((1, GDN_G, nh, CHUNK), lambda b, c: (b, c, 0, 0))
    rows = GDN_G * CHUNK
    width = nh * dk

    def col(name):
        off, w = PROJ_DST[name]
        assert w == width
        return pl.BlockSpec((1, rows, w), lambda b, c: (b, c, off // w))
    cw = jnp.transpose(conv_w.astype(jnp.float32).reshape(CONV_WIDTH, 3, width), (1, 0, 2))
    proj3 = proj.reshape(bsz, seq, proj.shape[-1])
    return pl.pallas_call(
        _gdn_body, grid=(bsz, n // GDN_G),
        in_specs=[col("qb"), col("kb"), col("vb"),
                  pl.BlockSpec((3, CONV_WIDTH, width), lambda b, c: (0, 0, 0)),
                  side, side, pl.BlockSpec((1, dv), lambda b, c: (0, 0))],
        out_specs=pl.BlockSpec((1, rows, nh * dv), lambda b, c: (b, c, 0)),
        out_shape=jax.ShapeDtypeStruct((bsz, seq, nh * dv), jnp.float32),
        scratch_shapes=[pltpu.VMEM((nh, dk, dv), jnp.float32),
                        pltpu.VMEM((3, GDN_PAD + rows, width), jnp.float32),
                        pltpu.VMEM((3, rows, width), jnp.float32)],
        compiler_params=pltpu.CompilerParams(dimension_semantics=("arbitrary", "arbitrary"),
                                             vmem_limit_bytes=GDN_VMEM_LIMIT),
        name="gated_delta_rule")(proj3, proj3, proj3, cw, gc, chunked(beta), norm_w.astype(jnp.float32).reshape(1, dv))


IN_NAMES = ("qa", "ka", "va", "qi", "ki", "wi", "qb", "kb", "vb", "zb", "bb", "ab", "gate_a", "gate_b")
PROJ_ORDER = ("gate_a", "gate_b", "qi", "qa", "qb", "kb", "vb", "zb", "ka", "va", "ki", "wi", "bb", "ab")


def _proj_layout():
    src, off = {}, 0
    for name, w in zip(IN_NAMES, IN_SPLITS):
        src[name] = (off, w)
        off += w
    dst, off = {}, 0
    for name in PROJ_ORDER:
        dst[name] = (off, src[name][1])
        off += src[name][1]
    return src, dst, -(-off // LANES) * LANES


PROJ_SRC, PROJ_DST, PROJ_WIDTH = _proj_layout()
PROJ_TM = 256


def permute_w_in(w_in):
    cols = [w_in[:, PROJ_SRC[n][0]:PROJ_SRC[n][0] + PROJ_SRC[n][1]] for n in PROJ_ORDER]
    cols.append(jnp.zeros((w_in.shape[0], PROJ_WIDTH - sum(c.shape[1] for c in cols)), w_in.dtype))
    return jnp.concatenate(cols, axis=1).astype(jnp.bfloat16)


def _modulated_norm(x, scale, shift):
    return x * lax.rsqrt(jnp.mean(x * x, axis=-1, keepdims=True) + EPS) * scale + shift


def _sigmoid(v):
    return 1.0 / (1.0 + jnp.exp(-v))


def _ada_body(c_ref, w_ref, b_ref, o_ref):
    c = c_ref[...]
    o_ref[...] = jnp.dot((c * _sigmoid(c)).astype(jnp.bfloat16), w_ref[...].astype(jnp.bfloat16),
                         preferred_element_type=jnp.float32) + b_ref[...]


def ada_modulation(c, w_ada, b_ada):
    bsz, d = c.shape
    n = w_ada.shape[1]
    tn = D_MODEL
    return pl.pallas_call(
        _ada_body, grid=(n // tn,),
        in_specs=[pl.BlockSpec((bsz, d), lambda j: (0, 0)), pl.BlockSpec((d, tn), lambda j: (0, j)),
                  pl.BlockSpec((1, tn), lambda j: (0, j))],
        out_specs=pl.BlockSpec((bsz, tn), lambda j: (0, j)),
        out_shape=jax.ShapeDtypeStruct((bsz, n), jnp.float32),
        name="ada_modulation")(c, w_ada, b_ada.reshape(1, n))


def _in_proj_body(x_ref, sc_ref, sh_ref, w_ref, o_ref):
    n1 = _modulated_norm(x_ref[...], sc_ref[0], sh_ref[0])
    o_ref[...] = jnp.dot(n1.astype(jnp.bfloat16), w_ref[...], preferred_element_type=jnp.float32)


def in_proj(x2, scale, shift, w_perm, seq):
    n, d = x2.shape
    tm = PROJ_TM
    per = seq // tm
    vec = pl.BlockSpec((1, 1, d), lambda i: (i // per, 0, 0))
    return pl.pallas_call(
        _in_proj_body, grid=(n // tm,),
        in_specs=[pl.BlockSpec((tm, d), lambda i: (i, 0)), vec, vec,
                  pl.BlockSpec((d, PROJ_WIDTH), lambda i: (0, 0), pipeline_mode=pl.Buffered(1))],
        out_specs=pl.BlockSpec((tm, PROJ_WIDTH), lambda i: (i, 0)),
        out_shape=jax.ShapeDtypeStruct((n, PROJ_WIDTH), jnp.float32),
        compiler_params=pltpu.CompilerParams(dimension_semantics=("arbitrary",), vmem_limit_bytes=PROJ_VMEM_LIMIT),
        name="in_proj")(x2, scale, shift, w_perm)


def _merge_body(oa_ref, ob_ref, z_ref, ga_ref, gb_ref, x_ref, gt_ref, sc_ref, sh_ref, wpa_ref, wpb_ref, wo_ref,
                h_ref, n2_ref):
    f32, bf = jnp.float32, jnp.bfloat16
    z = z_ref[...]
    ob = ob_ref[...] * (z * _sigmoid(z))
    oa = oa_ref[0].reshape(A_WIDTH, oa_ref.shape[-1]).T
    ya = jnp.dot(oa.astype(bf), wpa_ref[...], preferred_element_type=f32)
    yb = jnp.dot(ob.astype(bf), wpb_ref[...], preferred_element_type=f32)
    merged = _sigmoid(ga_ref[...]) * ya + _sigmoid(gb_ref[...]) * yb
    y1 = jnp.dot(merged.astype(bf), wo_ref[...], preferred_element_type=f32)
    h = x_ref[...] + gt_ref[0] * y1
    h_ref[...] = h
    n2_ref[...] = _modulated_norm(h, sc_ref[0], sh_ref[0])


def merge(o_a, o_b, proj, x2, gt1, scale2, shift2, w_pa, w_pb, w_o, seq):
    n, d = x2.shape
    tm = PROJ_TM
    per = seq // tm
    bf = jnp.bfloat16
    vec = pl.BlockSpec((1, 1, d), lambda i: (i // per, 0, 0))

    def col(name):
        off, w = PROJ_DST[name]
        return pl.BlockSpec((tm, w), lambda i: (i, off // w))
    row = lambda w: pl.BlockSpec((tm, w), lambda i: (i, 0))
    res = lambda a, b: pl.BlockSpec((a, b), lambda i: (0, 0))
    return pl.pallas_call(
        _merge_body, grid=(n // tm,),
        in_specs=[pl.BlockSpec((1, A_HEADS, A_HEAD_DIM, tm), lambda i: (i // per, 0, 0, i % per)), row(B_V_WIDTH), col("zb"), col("gate_a"), col("gate_b"), row(d), vec, vec, vec,
                  res(A_WIDTH, d), res(B_V_WIDTH, d), res(d, d)],
        out_specs=[row(d), row(d)],
        out_shape=[jax.ShapeDtypeStruct((n, d), jnp.float32), jax.ShapeDtypeStruct((n, d), jnp.float32)],
        compiler_params=pltpu.CompilerParams(dimension_semantics=("arbitrary",), vmem_limit_bytes=PROJ_VMEM_LIMIT),
        name="merge")(o_a, o_b, proj, proj, proj, x2, gt1, scale2, shift2,
                      w_pa.astype(bf), w_pb.astype(bf), w_o.astype(bf))


def token_mixers(proj, positions, conv_w, a_log, dt_bias, norm_b_w, bsz, seq):
    qit, qat, w, kit, kat, vat, bg = dsa_prep(proj, positions, a_log, dt_bias, bsz, seq)
    o_a = dsa_attention_pallas(qit, w, qat, kit, kat, vat)

    o_b = gated_delta_rule_pallas(proj, conv_w, bg[:, B_HEADS:], bg[:, :B_HEADS], norm_b_w, bsz, seq)
    return o_a, o_b


PEER_SLOTS = PEER_HEADS * PEER_TOPK
PEER_TB = 32
HALF_ROWS = 4
HI_MASK = -65536


def pack_table(tab):
    bits = lax.bitcast_convert_type(tab.astype(jnp.bfloat16), jnp.uint16).astype(jnp.uint32)
    half = tab.shape[1] // 2
    word = bits[:, :half] | (bits[:, half:] << 16)
    return lax.bitcast_convert_type(word, jnp.int32).reshape(tab.shape[0] * HALF_ROWS, LANES)


def _table_row(tab_ref, rows, k):
    return _unpack(tab_ref[pl.ds(pl.multiple_of(rows[k], HALF_ROWS), HALF_ROWS), :])


def _unpack(w):
    lo = lax.bitcast_convert_type(lax.shift_left(w, 16), jnp.float32)
    hi = lax.bitcast_convert_type(w & jnp.int32(HI_MASK), jnp.float32)
    return lo, hi


def _peer_u_body(idx_ref, x_ref, gate_ref, tab_ref, o_ref, s_ref, sb_ref):
    def lane_sums(t):
        a = jnp.sum(sb_ref[t].T, axis=0, keepdims=True)
        o_ref[pl.ds(t, 1), :] = 0.5 * a * (1.0 + lax.erf(a * (2.0 ** -0.5))) * gate_ref[pl.ds(t, 1), :]

    sb_ref[0] = jnp.zeros(sb_ref.shape[1:], jnp.float32)

    def tok(t, carry):
        lane_sums(jnp.maximum(t - 1, 0))
        x8 = x_ref[pl.ds(t, 1), :].reshape(2 * HALF_ROWS, LANES)
        xlo, xhi = x8[0:HALF_ROWS], x8[HALF_ROWS:2 * HALF_ROWS]
        rows = idx_ref.at[t]
        for k in range(PEER_SLOTS):
            lo, hi = _table_row(tab_ref, rows, k)
            s_ref[HALF_ROWS * k:HALF_ROWS * (k + 1), :] = lo * xlo + hi * xhi
        s4 = s_ref[pl.ds(0, PEER_SLOTS, stride=HALF_ROWS), :]
        for r in range(1, HALF_ROWS):
            s4 = s4 + s_ref[pl.ds(r, PEER_SLOTS, stride=HALF_ROWS), :]
        sb_ref[t] = s4
        return carry
    lax.fori_loop(0, PEER_TB, tok, 0, unroll=2)
    lane_sums(PEER_TB - 1)


def _peer_v_body(idx_ref, coef_ref, tab_ref, o_ref, cb_ref):
    def spread(t):
        return jnp.broadcast_to(coef_ref[pl.ds(t, 1), :], (PEER_SLOTS, LANES)).T

    def tok(t, weights):
        nxt = spread(jnp.minimum(t + 1, PEER_TB - 1))
        cb_ref[...] = weights
        nacc = 2
        acc = [jnp.zeros((HALF_ROWS, LANES), jnp.float32) for _ in range(2 * nacc)]
        rows = idx_ref.at[t]
        for k in range(PEER_SLOTS):
            lo, hi = _table_row(tab_ref, rows, k)
            c = jnp.broadcast_to(cb_ref[k:k + 1, :], (HALF_ROWS, LANES))
            a = k % nacc
            acc[2 * a] = acc[2 * a] + c * lo
            acc[2 * a + 1] = acc[2 * a + 1] + c * hi
        half = HALF_ROWS * LANES
        o_ref[pl.ds(t, 1), 0:half] = (acc[0] + acc[2]).reshape(1, half)
        o_ref[pl.ds(t, 1), half:2 * half] = (acc[1] + acc[3]).reshape(1, half)
        return nxt
    lax.fori_loop(0, PEER_TB, tok, spread(0))


def _table_spec():
    return pl.BlockSpec((PEER_N_EXPERTS * HALF_ROWS, LANES), lambda i: (0, 0), pipeline_mode=pl.Buffered(1))


def peer_u(idx, x, gates, tab):
    n = idx.shape[0]
    tb = PEER_TB
    return pl.pallas_call(
        _peer_u_body, grid=(n // tb,),
        in_specs=[pl.BlockSpec((tb, PEER_SLOTS), lambda i: (i, 0), memory_space=pltpu.SMEM),
                  pl.BlockSpec((tb, x.shape[1]), lambda i: (i, 0)),
                  pl.BlockSpec((tb, PEER_SLOTS), lambda i: (i, 0)),
                  _table_spec()],
        out_specs=pl.BlockSpec((tb, PEER_SLOTS), lambda i: (i, 0)),
        out_shape=jax.ShapeDtypeStruct((n, PEER_SLOTS), jnp.float32),
        scratch_shapes=[pltpu.VMEM((HALF_ROWS * PEER_SLOTS, LANES), jnp.float32),
                        pltpu.VMEM((tb, PEER_SLOTS, LANES), jnp.float32)],
        compiler_params=pltpu.CompilerParams(dimension_semantics=("arbitrary",), vmem_limit_bytes=PEER_VMEM_LIMIT),
        name="peer_u")(idx, x, gates, tab)


def peer_v(idx, coef, tab):
    n = idx.shape[0]
    tb = PEER_TB
    return pl.pallas_call(
        _peer_v_body, grid=(n // tb,),
        in_specs=[pl.BlockSpec((tb, PEER_SLOTS), lambda i: (i, 0), memory_space=pltpu.SMEM),
                  pl.BlockSpec((tb, PEER_SLOTS), lambda i: (i, 0)),
                  _table_spec()],
        out_specs=pl.BlockSpec((tb, 2 * HALF_ROWS * LANES), lambda i: (i, 0)),
        out_shape=jax.ShapeDtypeStruct((n, 2 * HALF_ROWS * LANES), jnp.float32),
        scratch_shapes=[pltpu.VMEM((PEER_SLOTS, LANES), jnp.float32)],
        compiler_params=pltpu.CompilerParams(dimension_semantics=("arbitrary",), vmem_limit_bytes=PEER_VMEM_LIMIT),
        name="peer_v")(idx, coef, tab)


PEER_TT = 256
PAD_ID = 2**30


def _extract_top(ref, n_out, rid=None):
    rows, t = ref.shape
    if rid is None:
        rid = lax.broadcasted_iota(jnp.int32, (rows, t), 0)
    vals, idxs = [], []
    for _ in range(n_out):
        s = ref[...]
        m = jnp.max(s, axis=0, keepdims=True)
        ix = jnp.min(jnp.where(s == m, rid, jnp.int32(PAD_ID)), axis=0, keepdims=True)
        ref[...] = jnp.where(rid == ix, -jnp.inf, s)
        vals.append(m)
        idxs.append(ix)
    return vals, idxs


def _peer_route_body(x_ref, wq_ref, sk_ref, cid_ref, idx_ref, gate_ref, s_ref, cand_ref, v_ref, i_ref, et_ref, gt_ref):
    kk, nk = PEER_TOPK, PEER_N_KEYS
    f32 = jnp.float32
    q = jnp.dot(x_ref[...].astype(jnp.bfloat16), wq_ref[...], preferred_element_type=f32).astype(jnp.bfloat16)
    nt = (((1,), (1,)), ((), ()))
    row16 = lax.broadcasted_iota(jnp.int32, (kk, PEER_TT), 0)
    for h in range(PEER_HEADS):
        qh = q[:, h * PEER_KEY_DIM:(h + 1) * PEER_KEY_DIM]
        s_ref[...] = lax.dot_general(sk_ref[h], qh, nt, preferred_element_type=f32)
        for p in range(2):
            vals, idxs = _extract_top(s_ref.at[p * nk:(p + 1) * nk, :], kk)
            for i in range(kk):
                v_ref[p, i:i + 1, :] = vals[i]
                i_ref[p, i:i + 1, :] = idxs[i]
        off = 0
        for i in range(kk):
            n_j = kk // (i + 1)
            cand_ref[off:off + n_j, :] = v_ref[0, i:i + 1, :] + v_ref[1, 0:n_j, :]
            off += n_j
        cand_ref[off:, :] = jnp.full((cand_ref.shape[0] - off, PEER_TT), -jnp.inf, f32)
        vals, cis = _extract_top(cand_ref, kk, cid_ref[...])
        i1, i2 = i_ref[0], i_ref[1]
        es = [jnp.exp(v - vals[0]) for v in vals]
        den = es[0]
        for e in es[1:]:
            den = den + e
        for k in range(kk):
            ci = cis[k]
            e1 = jnp.sum(jnp.where(row16 == lax.shift_right_logical(ci, 4), i1, 0), axis=0, keepdims=True)
            e2 = jnp.sum(jnp.where(row16 == (ci & 15), i2, 0), axis=0, keepdims=True)
            et_ref[h * kk + k:h * kk + k + 1, :] = (e1 * nk + e2) * HALF_ROWS
            gt_ref[h * kk + k:h * kk + k + 1, :] = es[k] / den
    idx_ref[...] = lax.bitcast_convert_type(lax.bitcast_convert_type(et_ref[...], f32).T, jnp.int32)
    gate_ref[...] = gt_ref[...].T


def _pair_cells():
    kk = PEER_TOPK
    ids = [i * kk + j for i in range(kk) for j in range(kk // (i + 1))]
    return ids + [PAD_ID] * (-len(ids) % SUBLANES)


def peer_route(xn, wq, subkeys):
    n = xn.shape[0]
    tt = PEER_TT
    cells = _pair_cells()
    cid = jnp.broadcast_to(jnp.asarray(cells, jnp.int32)[:, None], (len(cells), tt))
    half = PEER_KEY_DIM // 2
    z = jnp.zeros((PEER_HEADS, PEER_N_KEYS, half), subkeys.dtype)
    skbd = jnp.concatenate([jnp.concatenate([subkeys[:, 0], z], axis=-1),
                            jnp.concatenate([z, subkeys[:, 1]], axis=-1)], axis=1).astype(jnp.bfloat16)
    return pl.pallas_call(
        _peer_route_body, grid=(n // tt,),
        in_specs=[pl.BlockSpec((tt, D_MODEL), lambda i: (i, 0)),
                  pl.BlockSpec((D_MODEL, PEER_HEADS * PEER_KEY_DIM), lambda i: (0, 0)),
                  pl.BlockSpec((PEER_HEADS, 2 * PEER_N_KEYS, PEER_KEY_DIM), lambda i: (0, 0, 0)),
                  pl.BlockSpec((len(cells), tt), lambda i: (0, 0))],
        out_specs=[pl.BlockSpec((tt, PEER_SLOTS), lambda i: (i, 0)), pl.BlockSpec((tt, PEER_SLOTS), lambda i: (i, 0))],
        out_shape=[jax.ShapeDtypeStruct((n, PEER_SLOTS), jnp.int32), jax.ShapeDtypeStruct((n, PEER_SLOTS), jnp.float32)],
        scratch_shapes=[pltpu.VMEM((2 * PEER_N_KEYS, tt), jnp.float32),
                        pltpu.VMEM((len(cells), tt), jnp.float32),
                        pltpu.VMEM((2, PEER_TOPK, tt), jnp.float32),
                        pltpu.VMEM((2, PEER_TOPK, tt), jnp.int32),
                        pltpu.VMEM((PEER_SLOTS, tt), jnp.int32),
                        pltpu.VMEM((PEER_SLOTS, tt), jnp.float32)],
        compiler_params=pltpu.CompilerParams(dimension_semantics=("arbitrary",), vmem_limit_bytes=ROUTE_VMEM_LIMIT),
        name="peer_route")(xn, wq.astype(jnp.bfloat16), skbd, cid)


def peer_channel_mixer(xn, wq, subkeys, u_tab, v_tab):
    bsz, seq, d = xn.shape
    n_tok = bsz * seq
    x2 = xn.reshape(n_tok, d)
    idx, gates = peer_route(x2, wq, subkeys)
    coef = peer_u(idx, x2, gates, pack_table(u_tab))
    out = peer_v(idx, coef, pack_table(v_tab))
    return out.reshape(bsz, seq, d)


def _residual_body(h_ref, y_ref, gt_ref, o_ref):
    o_ref[...] = h_ref[...] + gt_ref[0] * y_ref[...]


def _final_norm_body(h_ref, y_ref, gt_ref, g_ref, o_ref):
    h = h_ref[...] + gt_ref[0] * y_ref[...]
    o_ref[...] = h * lax.rsqrt(jnp.mean(h * h, axis=-1, keepdims=True) + EPS) * g_ref[...]


def gated_residual(h2, y2, gt, seq, gain=None):
    n, d = h2.shape
    tm = min(1024, seq)
    per = seq // tm
    row = pl.BlockSpec((tm, d), lambda i: (i, 0))
    specs = [row, row, pl.BlockSpec((1, 1, d), lambda i: (i // per, 0, 0))]
    args = [h2, y2, gt]
    if gain is not None:
        specs.append(pl.BlockSpec((1, d), lambda i: (0, 0)))
        args.append(gain.astype(jnp.float32).reshape(1, d))
    return pl.pallas_call(
        _residual_body if gain is None else _final_norm_body, grid=(n // tm,),
        in_specs=specs, out_specs=row, out_shape=jax.ShapeDtypeStruct((n, d), h2.dtype),
        name="gated_residual" if gain is None else "final_norm")(*args)


def kernel(x, c, positions, w_ada, b_ada, w_in, conv_w, a_log, dt_bias, norm_b_w,
           w_pa, w_pb, w_o, peer_wq, peer_subkeys, peer_u, peer_v, final_norm_w):
    bsz, seq, d = x.shape
    h = x.reshape(bsz * seq, d)
    for layer in range(DEPTH):
        mod = ada_modulation(c, w_ada[layer], b_ada[layer])
        sh1, sc1, gt1, sh2, sc2, gt2 = [m.reshape(bsz, 1, d) for m in jnp.split(mod, 6, axis=-1)]
        proj = in_proj(h, 1.0 + sc1, sh1, permute_w_in(w_in[layer]), seq)
        o_a, o_b = token_mixers(proj, positions, conv_w[layer], a_log[layer], dt_bias[layer], norm_b_w[layer],
                                bsz, seq)
        h, n2 = merge(o_a, o_b.reshape(bsz * seq, B_V_WIDTH), proj, h, gt1,
                      1.0 + sc2, sh2, w_pa[layer], w_pb[layer], w_o[layer], seq)
        y2 = peer_channel_mixer(n2.reshape(bsz, seq, d), peer_wq[layer], peer_subkeys[layer], peer_u[layer],
                                peer_v[layer])
        last = layer == DEPTH - 1
        h = gated_residual(h, y2.reshape(bsz * seq, d), gt2, seq, final_norm_w if last else None)
    return h.reshape(bsz, seq, d)
```

```python
import functools

import jax, jax.numpy as jnp
from jax import lax
from jax.experimental import pallas as pl
from jax.experimental.pallas import tpu as pltpu

D_MODEL = 1024
DEPTH = 1

LANES = 128
SUBLANES = 8
MIB = 1024 * 1024
DSA_VMEM_LIMIT = 48 * MIB
PREP_VMEM_LIMIT = 32 * MIB
GDN_VMEM_LIMIT = 40 * MIB
PROJ_VMEM_LIMIT = 48 * MIB
PEER_VMEM_LIMIT = 48 * MIB
ROUTE_VMEM_LIMIT = 32 * MIB

A_HEADS = 8
A_KV_HEADS = 2
A_HEAD_DIM = 64
IDX_HEADS = 16
IDX_DIM = 64
IDX_TOPK_MAX = 256
B_HEADS = 8
B_KEY_DIM = 64
B_VAL_DIM = 64
CONV_WIDTH = 4
CHUNK = 64
ROPE_THETA = 500000.0
ROPE_FRACTION_DEN = 4
PEER_HEADS = 8
PEER_KEY_DIM = 128
PEER_N_KEYS = 128
PEER_N_EXPERTS = PEER_N_KEYS * PEER_N_KEYS
PEER_TOPK = 16
EPS = 1e-6

A_WIDTH = A_HEADS * A_HEAD_DIM
KV_WIDTH = A_KV_HEADS * A_HEAD_DIM
B_QK_WIDTH = B_HEADS * B_KEY_DIM
B_V_WIDTH = B_HEADS * B_VAL_DIM
IN_SPLITS = (A_WIDTH, KV_WIDTH, KV_WIDTH, IDX_HEADS * IDX_DIM, IDX_DIM, IDX_HEADS,
             B_QK_WIDTH, B_QK_WIDTH, B_V_WIDTH, B_V_WIDTH, B_HEADS, B_HEADS, D_MODEL, D_MODEL)


DSA_TQ = 256
DSA_TK = 256
INT_MIN = -2**31
NEG_BIG = -1e30


def _dsa_body(topk, qit_ref, w_ref, qat_ref, ki_ref, ka_ref, vat_ref, o_ref,
              key_ref):
    tq, tk = DSA_TQ, DSA_TK
    qb = pl.program_id(1)
    n_kv = qb + 1
    t_glob = qb * tq + lax.broadcasted_iota(jnp.int32, (1, tq), 1)
    row = lax.broadcasted_iota(jnp.int32, (tk, 1), 0)
    f32 = jnp.float32

    def p1(j, carry):
        kt = ki_ref[0, j]
        score = jnp.zeros((tk, tq), f32)
        for h in range(IDX_HEADS):
            lt = jnp.dot(kt, qit_ref[0, h], preferred_element_type=f32)
            score = score + w_ref[0, h:h + 1, :] * jnp.maximum(lt, 0.0)
        bits = lax.bitcast_convert_type(score + 0.0, jnp.int32)
        skey = jnp.where(bits >= 0, bits, bits ^ jnp.int32(0x7FFFFFFF))
        skey = jnp.where(j * tk + row <= t_glob, skey, jnp.int32(INT_MIN))
        key_ref[j] = skey
        return carry
    lax.fori_loop(0, n_kv, p1, 0)

    key_ref[n_kv] = jnp.full((tk, tq), INT_MIN, jnp.int32)

    def count(pred):
        def body(jj, acc):
            for j in (2 * jj, 2 * jj + 1):
                hit = jnp.where(pred(key_ref[j], j * tk + row), 1.0, 0.0)
                acc = acc + jnp.sum(hit.reshape(tk // SUBLANES, SUBLANES, tq), axis=0)
            return acc
        acc = lax.fori_loop(0, (n_kv + 1) // 2, body, jnp.zeros((SUBLANES, tq), f32))
        return jnp.sum(acc, axis=0, keepdims=True)

    kf = jnp.float32(topk)

    def bit_step(i, ku):
        cand_u = ku | lax.shift_left(jnp.int32(1), 31 - i)
        cand = cand_u ^ jnp.int32(INT_MIN)
        c = count(lambda k, s: k >= cand)
        return jnp.where(c >= kf, cand_u, ku)
    ku = lax.fori_loop(0, 32, bit_step, jnp.zeros((1, tq), jnp.int32))
    kth = ku ^ jnp.int32(INT_MIN)
    c_gt = count(lambda k, s: k > kth)
    c_ge = count(lambda k, s: k >= kth)
    short = kth == jnp.int32(INT_MIN)
    x0 = jnp.where(short, jnp.int32(-1), jnp.int32(2**30))
    need = kf - c_gt
    has_tie = jnp.max(jnp.where(jnp.logical_and(c_ge > kf, jnp.logical_not(short)), 1.0, 0.0)) > 0.0

    def tie_search():
        def step(i, x):
            bit = lax.shift_left(jnp.int32(1), 11 - i)
            probe = x + bit - 1
            c = count(lambda k, s: jnp.logical_and(k == kth, s <= probe))
            return jnp.where(c < need, x + bit, x)
        x = lax.fori_loop(0, 12, step, jnp.zeros((1, tq), jnp.int32))
        return jnp.where(short, jnp.int32(-1), x)
    x_lim = lax.cond(has_tie, tie_search, lambda: x0)

    rep = A_HEADS // A_KV_HEADS
    hs = range(A_HEADS)

    def p3(j, carry):
        m, l, acc = carry
        skey = key_ref[j]
        s_idx = j * tk + row
        sel = jnp.logical_or(skey > kth, jnp.logical_and(skey == kth, s_idx <= x_lim))
        kt = [ka_ref[0, g, j] for g in range(A_KV_HEADS)]
        vt = [vat_ref[0, g, j] for g in range(A_KV_HEADS)]
        s = [jnp.where(sel, jnp.dot(kt[h // rep], qat_ref[0, h], preferred_element_type=f32), NEG_BIG) for h in hs]
        m_new = [jnp.maximum(m[h], jnp.max(s[h], axis=0, keepdims=True)) for h in hs]
        alpha = [jnp.exp(m[h] - m_new[h]) for h in hs]
        p = [jnp.exp(s[h] - m_new[h]) for h in hs]
        l_new = [alpha[h] * l[h] + jnp.sum(p[h], axis=0, keepdims=True) for h in hs]
        acc_new = [alpha[h] * acc[h] + jnp.dot(vt[h // rep], p[h].astype(jnp.bfloat16), preferred_element_type=f32)
                   for h in hs]
        return tuple(m_new), tuple(l_new), tuple(acc_new)

    init = (tuple(jnp.full((1, tq), NEG_BIG, f32) for _ in hs), tuple(jnp.zeros((1, tq), f32) for _ in hs),
            tuple(jnp.zeros((A_HEAD_DIM, tq), f32) for _ in hs))
    _, l_fin, acc_fin = lax.fori_loop(0, n_kv, p3, init)
    for h in hs:
        o_ref[0, h] = acc_fin[h] / l_fin[h]


ROPE_HALF = A_HEAD_DIM // ROPE_FRACTION_DEN // 2


def _rope_t(xt, n_heads, cos, sin, scale=None):
    outs = []
    for h in range(n_heads):
        b = h * A_HEAD_DIM
        x1, x2, rest = xt[b:b + ROPE_HALF], xt[b + ROPE_HALF:b + 2 * ROPE_HALF], xt[b + 2 * ROPE_HALF:b + A_HEAD_DIM]
        o = jnp.concatenate([x1 * cos - x2 * sin, x2 * cos + x1 * sin, rest], axis=0)
        outs.append(o if scale is None else o * scale)
    return outs


def _dsa_prep_body(qi_ref, qa_ref, ka_ref, va_ref, sm_ref, cos_ref, sin_ref, dp_ref,
                   qit_ref, qat_ref, w_ref, kit_ref, kat_ref, vat_ref, bg_ref):
    bf = jnp.bfloat16
    cos, sin = cos_ref[0], sin_ref[0]
    for h, o in enumerate(_rope_t(qi_ref[...].T, IDX_HEADS, cos, sin)):
        qit_ref[0, h] = o.astype(bf)
    for h, o in enumerate(_rope_t(qa_ref[...].T, A_HEADS, cos, sin, A_HEAD_DIM ** -0.5)):
        qat_ref[0, h] = o.astype(bf)
    kat = _rope_t(ka_ref[...].T, A_KV_HEADS, cos, sin)
    vt = va_ref[...].T
    for g in range(A_KV_HEADS):
        kat_ref[0, g, 0] = kat[g].T.astype(bf)
        vat_ref[0, g, 0] = vt[g * A_HEAD_DIM:(g + 1) * A_HEAD_DIM].astype(bf)
    smt = sm_ref[...].T
    kit_ref[0, 0] = _rope_t(smt[0:IDX_DIM], 1, cos, sin)[0].T.astype(bf)
    w_ref[0] = smt[IDX_DIM:IDX_DIM + IDX_HEADS] * ((IDX_HEADS ** -0.5) * (IDX_DIM ** -0.5))
    b0 = IDX_DIM + IDX_HEADS
    bb, ab = smt[b0:b0 + B_HEADS], smt[b0 + B_HEADS:b0 + 2 * B_HEADS]
    a_log, dt_bias = dp_ref[:, 0:1], dp_ref[:, 1:2]
    bg_ref[0] = jnp.concatenate([_sigmoid(bb), -jnp.exp(a_log) * jax.nn.softplus(ab + dt_bias)], axis=0)


def dsa_prep(proj, positions, a_log, dt_bias, bsz, seq):
    t = DSA_TQ
    per = seq // t
    assert DSA_TK == t and PROJ_ORDER[-4:] == ("ki", "wi", "bb", "ab") and PROJ_DST["ki"][0] + LANES == PROJ_WIDTH
    rd = A_HEAD_DIM // ROPE_FRACTION_DEN
    inv_freq = jnp.power(jnp.float32(ROPE_THETA), -jnp.arange(ROPE_HALF, dtype=jnp.float32) * (2.0 / rd))
    ang = positions.astype(jnp.float32)[:, None, :] * inv_freq[None, :, None]

    def col(name, width=None):
        off, w = PROJ_DST[name]
        w = width or w
        return pl.BlockSpec((t, w), lambda i: (i, off // w))
    trig = pl.BlockSpec((1, ROPE_HALF, t), lambda i: (i // per, 0, i % per))
    bf, f32 = jnp.bfloat16, jnp.float32
    sd = jax.ShapeDtypeStruct
    n_kv = seq // DSA_TK
    return pl.pallas_call(
        _dsa_prep_body, grid=(bsz * per,),
        in_specs=[col("qi"), col("qa"), col("ka"), col("va"), col("ki", LANES), trig, trig,
                  pl.BlockSpec((B_HEADS, 2), lambda i: (0, 0))],
        out_specs=[pl.BlockSpec((1, IDX_HEADS, IDX_DIM, t), lambda i: (i // per, 0, 0, i % per)),
                   pl.BlockSpec((1, A_HEADS, A_HEAD_DIM, t), lambda i: (i // per, 0, 0, i % per)),
                   pl.BlockSpec((1, IDX_HEADS, t), lambda i: (i // per, 0, i % per)),
                   pl.BlockSpec((1, 1, t, IDX_DIM), lambda i: (i // per, i % per, 0, 0)),
                   pl.BlockSpec((1, A_KV_HEADS, 1, t, A_HEAD_DIM), lambda i: (i // per, 0, i % per, 0, 0)),
                   pl.BlockSpec((1, A_KV_HEADS, 1, A_HEAD_DIM, t), lambda i: (i // per, 0, i % per, 0, 0)),
                   pl.BlockSpec((1, 2 * B_HEADS, t), lambda i: (i // per, 0, i % per))],
        out_shape=[sd((bsz, IDX_HEADS, IDX_DIM, seq), bf), sd((bsz, A_HEADS, A_HEAD_DIM, seq), bf),
                   sd((bsz, IDX_HEADS, seq), f32), sd((bsz, n_kv, DSA_TK, IDX_DIM), bf),
                   sd((bsz, A_KV_HEADS, n_kv, DSA_TK, A_HEAD_DIM), bf),
                   sd((bsz, A_KV_HEADS, n_kv, A_HEAD_DIM, DSA_TK), bf), sd((bsz, 2 * B_HEADS, seq), f32)],
        compiler_params=pltpu.CompilerParams(dimension_semantics=("arbitrary",), vmem_limit_bytes=PREP_VMEM_LIMIT),
        name="dsa_prep")(proj, proj, proj, proj, proj, jnp.cos(ang), jnp.sin(ang),
                         jnp.stack([a_log, dt_bias], axis=-1).astype(f32))


def dsa_attention_pallas(qit, w, qat, kit, kat, vat):
    bsz, seq = qit.shape[0], qit.shape[-1]
    tq, tk = DSA_TQ, DSA_TK
    topk = min(IDX_TOPK_MAX, seq // 4)
    n_kv = seq // tk
    return pl.pallas_call(
        functools.partial(_dsa_body, topk),
        grid=(bsz, seq // tq),
        in_specs=[
            pl.BlockSpec((1, IDX_HEADS, IDX_DIM, tq), lambda b, q: (b, 0, 0, q)),
            pl.BlockSpec((1, IDX_HEADS, tq), lambda b, q: (b, 0, q)),
            pl.BlockSpec((1, A_HEADS, A_HEAD_DIM, tq), lambda b, q: (b, 0, 0, q)),
            pl.BlockSpec((1, n_kv, tk, IDX_DIM), lambda b, q: (b, 0, 0, 0)),
            pl.BlockSpec((1, A_KV_HEADS, n_kv, tk, A_HEAD_DIM), lambda b, q: (b, 0, 0, 0, 0)),
            pl.BlockSpec((1, A_KV_HEADS, n_kv, A_HEAD_DIM, tk), lambda b, q: (b, 0, 0, 0, 0)),
        ],
        out_specs=pl.BlockSpec((1, A_HEADS, A_HEAD_DIM, tq), lambda b, q: (b, 0, 0, q)),
        out_shape=jax.ShapeDtypeStruct((bsz, A_HEADS, A_HEAD_DIM, seq), jnp.float32),
        scratch_shapes=[
            pltpu.VMEM((n_kv + 1, tk, tq), jnp.int32),
        ],
        compiler_params=pltpu.CompilerParams(dimension_semantics=("arbitrary", "arbitrary"),
                                             vmem_limit_bytes=DSA_VMEM_LIMIT),
        name="dsa_attention",
    )(qit, w, qat, kit, kat, vat)


GDN_G = 8
GDN_BASE = 8
GDN_PAD = 8


def _gdn_body(q_ref, k_ref, v_ref, cw_ref, gc_ref, bt_ref, nw_ref, o_ref, s_ref, xs_ref, cs_ref):
    c_sz = CHUNK
    f32, bf = jnp.float32, jnp.bfloat16

    rows = GDN_G * c_sz
    halo = CONV_WIDTH - 1

    @pl.when(pl.program_id(1) == 0)
    def _():
        s_ref[...] = jnp.zeros(s_ref.shape, f32)
        xs_ref[:, 0:GDN_PAD, :] = jnp.zeros((3, GDN_PAD, xs_ref.shape[-1]), f32)

    for j, ref in enumerate((q_ref, k_ref, v_ref)):
        xs_ref[j, GDN_PAD:GDN_PAD + rows, :] = ref[0]
        acc = xs_ref[j, GDN_PAD - halo:GDN_PAD - halo + rows, :] * cw_ref[j, 0:1, :]
        for i in range(1, CONV_WIDTH):
            acc = acc + xs_ref[j, GDN_PAD - halo + i:GDN_PAD - halo + i + rows, :] * cw_ref[j, i:i + 1, :]
        cs_ref[j] = acc * _sigmoid(acc)
        xs_ref[j, GDN_PAD - halo:GDN_PAD, :] = xs_ref[j, GDN_PAD + rows - halo:GDN_PAD + rows, :]

    ri = lax.broadcasted_iota(jnp.int32, (c_sz, c_sz), 0)
    ci = lax.broadcasted_iota(jnp.int32, (c_sz, c_sz), 1)
    incl, strict = ri >= ci, ri > ci
    eye = jnp.where(ri == ci, 1.0, 0.0).astype(f32)
    blk = lambda w: (ri // w) == (ci // w)
    diag8 = blk(GDN_BASE)
    sub_blocks = []
    w = GDN_BASE
    while w < c_sz:
        sub_blocks.append(jnp.logical_and(blk(2 * w), jnp.logical_not(blk(w))))
        w *= 2
    nt = (((1,), (1,)), ((), ()))
    dot = lambda a, b: jnp.dot(a.astype(bf), b.astype(bf), preferred_element_type=f32)
    dot_nt = lambda a, b: lax.dot_general(a.astype(bf), b.astype(bf), nt, preferred_element_type=f32)

    def split(a):
        hi = a.astype(bf)
        return hi, (a - hi.astype(f32)).astype(bf)

    def hp(a, b):
        a_hi, a_lo = split(a)
        b_hi, b_lo = split(b)
        mm = lambda u, w: jnp.dot(u, w, preferred_element_type=f32)
        return mm(a_hi, b_hi) + (mm(a_hi, b_lo) + mm(a_lo, b_hi))

    hs = range(B_HEADS)

    def chunk_pair(i, carry):
        cs = [2 * i, 2 * i + 1]
        r0 = [pl.multiple_of(c * c_sz, c_sz) for c in cs]
        items = [(j, h) for j in range(2) for h in hs]
        hd = lambda a, j, h: cs_ref[a, pl.ds(r0[j], c_sz), h * B_KEY_DIM:(h + 1) * B_KEY_DIM]
        l2n = lambda t: t * lax.rsqrt(jnp.sum(t * t, axis=-1, keepdims=True) + EPS)
        q = [l2n(hd(0, j, h)) * (B_KEY_DIM ** -0.5) for j, h in items]
        k = [l2n(hd(1, j, h)) for j, h in items]
        v = [hd(2, j, h) for j, h in items]
        gc8 = [gc_ref[0, c] for c in cs]
        gct = [g8.T for g8 in gc8]
        btt = [bt_ref[0, c].T for c in cs]
        gcr = [gc8[j][h:h + 1, :] for j, h in items]
        n_it = range(len(items))
        gcc = [gct[j][:, h:h + 1] for j, h in items]
        beta = [btt[j][:, h:h + 1] for j, h in items]
        decay = [jnp.exp(jnp.where(incl, gcc[n] - gcr[n], -jnp.inf)) for n in n_it]
        kb = [k[n] * beta[n] for n in n_it]
        vb = [v[n] * beta[n] for n in n_it]
        low = [jnp.where(strict, dot_nt(kb[n], k[n]) * decay[n], 0.0) for n in n_it]
        dg = [jnp.where(diag8, low[n], 0.0) for n in n_it]
        t = [eye - dg[n] for n in n_it]
        p = [hp(dg[n], dg[n]) for n in n_it]
        t = [hp(t[n], eye + p[n]) for n in n_it]
        p = [hp(p[n], p[n]) for n in n_it]
        t = [hp(t[n], eye + p[n]) for n in n_it]
        for below in sub_blocks:
            lb = [jnp.where(below, low[n], 0.0) for n in n_it]
            lt = [hp(lb[n], t[n]) for n in n_it]
            t = [t[n] - hp(t[n], lt[n]) for n in n_it]
        u = [dot(t[n], vb[n]) for n in n_it]
        kcd = [dot(t[n], kb[n] * jnp.exp(gcc[n])) for n in n_it]
        intra = [dot_nt(q[n], k[n]) * decay[n] for n in n_it]
        qg = [q[n] * jnp.exp(gcc[n]) for n in n_it]
        glast = [g[:, c_sz - 1:c_sz] for g in gcr]
        kdt = [(k[n] * jnp.exp(glast[n] - gcc[n])).T for n in n_it]
        s = [s_ref[h] for h in hs]
        for j in range(2):
            ix = [j * B_HEADS + h for h in hs]
            v_new = [u[ix[h]] - dot(kcd[ix[h]], s[h]) for h in hs]
            out = [dot(qg[ix[h]], s[h]) + dot(intra[ix[h]], v_new[h]) for h in hs]
            s = [s[h] * jnp.exp(glast[ix[h]]) + dot(kdt[ix[h]], v_new[h]) for h in hs]
            for h in hs:
                o = out[h]
                o = o * lax.rsqrt(jnp.mean(o * o, axis=-1, keepdims=True) + EPS) * nw_ref[...]
                o_ref[0, pl.ds(r0[j], c_sz), h * B_VAL_DIM:(h + 1) * B_VAL_DIM] = o
        for h in hs:
            s_ref[h] = s[h]
        return carry
    lax.fori_loop(0, GDN_G // 2, chunk_pair, 0)


def gated_delta_rule_pallas(proj, conv_w, g, beta, norm_w, bsz, seq):
    nh, dk, dv = B_HEADS, B_KEY_DIM, B_VAL_DIM
    n = seq // CHUNK
    chunked = lambda t: jnp.moveaxis(t.reshape(bsz, nh, n, CHUNK), 1, 2)
    gc = chunked(jnp.cumsum(g.reshape(bsz, nh, n, CHUNK), axis=-1))
    side = pl.BlockSpec((1, GDN_G, nh, CHUNK), lambda b, c: (b, c, 0, 0))
    rows = GDN_G * CHUNK
    width = nh * dk

    def col(name):
        off, w = PROJ_DST[name]
        assert w == width
        return pl.BlockSpec((1, rows, w), lambda b, c: (b, c, off // w))
    cw = jnp.transpose(conv_w.astype(jnp.float32).reshape(CONV_WIDTH, 3, width), (1, 0, 2))
    proj3 = proj.reshape(bsz, seq, proj.shape[-1])
    return pl.pallas_call(
        _gdn_body, grid=(bsz, n // GDN_G),
        in_specs=[col("qb"), col("kb"), col("vb"),
                  pl.BlockSpec((3, CONV_WIDTH, width), lambda b, c: (0, 0, 0)),
                  side, side, pl.BlockSpec((1, dv), lambda b, c: (0, 0))],
        out_specs=pl.BlockSpec((1, rows, nh * dv), lambda b, c: (b, c, 0)),
        out_shape=jax.ShapeDtypeStruct((bsz, seq, nh * dv), jnp.float32),
        scratch_shapes=[pltpu.VMEM((nh, dk, dv), jnp.float32),
                        pltpu.VMEM((3, GDN_PAD + rows, width), jnp.float32),
                        pltpu.VMEM((3, rows, width), jnp.float32)],
        compiler_params=pltpu.CompilerParams(dimension_semantics=("arbitrary", "arbitrary"),
                                             vmem_limit_bytes=GDN_VMEM_LIMIT),
        name="gated_delta_rule")(proj3, proj3, proj3, cw, gc, chunked(beta), norm_w.astype(jnp.float32).reshape(1, dv))


IN_NAMES = ("qa", "ka", "va", "qi", "ki", "wi", "qb", "kb", "vb", "zb", "bb", "ab", "gate_a", "gate_b")
PROJ_ORDER = ("gate_a", "gate_b", "qi", "qa", "qb", "kb", "vb", "zb", "ka", "va", "ki", "wi", "bb", "ab")


def _proj_layout():
    src, off = {}, 0
    for name, w in zip(IN_NAMES, IN_SPLITS):
        src[name] = (off, w)
        off += w
    dst, off = {}, 0
    for name in PROJ_ORDER:
        dst[name] = (off, src[name][1])
        off += src[name][1]
    return src, dst, -(-off // LANES) * LANES


PROJ_SRC, PROJ_DST, PROJ_WIDTH = _proj_layout()
PROJ_TM = 256


def permute_w_in(w_in):
    cols = [w_in[:, PROJ_SRC[n][0]:PROJ_SRC[n][0] + PROJ_SRC[n][1]] for n in PROJ_ORDER]
    cols.append(jnp.zeros((w_in.shape[0], PROJ_WIDTH - sum(c.shape[1] for c in cols)), w_in.dtype))
    return jnp.concatenate(cols, axis=1).astype(jnp.bfloat16)


def _modulated_norm(x, scale, shift):
    return x * lax.rsqrt(jnp.mean(x * x, axis=-1, keepdims=True) + EPS) * scale + shift


def _sigmoid(v):
    return 1.0 / (1.0 + jnp.exp(-v))


def _ada_body(c_ref, w_ref, b_ref, o_ref):
    c = c_ref[...]
    o_ref[...] = jnp.dot((c * _sigmoid(c)).astype(jnp.bfloat16), w_ref[...].astype(jnp.bfloat16),
                         preferred_element_type=jnp.float32) + b_ref[...]


def ada_modulation(c, w_ada, b_ada):
    bsz, d = c.shape
    n = w_ada.shape[1]
    tn = D_MODEL
    return pl.pallas_call(
        _ada_body, grid=(n // tn,),
        in_specs=[pl.BlockSpec((bsz, d), lambda j: (0, 0)), pl.BlockSpec((d, tn), lambda j: (0, j)),
                  pl.BlockSpec((1, tn), lambda j: (0, j))],
        out_specs=pl.BlockSpec((bsz, tn), lambda j: (0, j)),
        out_shape=jax.ShapeDtypeStruct((bsz, n), jnp.float32),
        name="ada_modulation")(c, w_ada, b_ada.reshape(1, n))


def _in_proj_body(x_ref, sc_ref, sh_ref, w_ref, o_ref):
    n1 = _modulated_norm(x_ref[...], sc_ref[0], sh_ref[0])
    o_ref[...] = jnp.dot(n1.astype(jnp.bfloat16), w_ref[...], preferred_element_type=jnp.float32)


def in_proj(x2, scale, shift, w_perm, seq):
    n, d = x2.shape
    tm = PROJ_TM
    per = seq // tm
    vec = pl.BlockSpec((1, 1, d), lambda i: (i // per, 0, 0))
    return pl.pallas_call(
        _in_proj_body, grid=(n // tm,),
        in_specs=[pl.BlockSpec((tm, d), lambda i: (i, 0)), vec, vec,
                  pl.BlockSpec((d, PROJ_WIDTH), lambda i: (0, 0), pipeline_mode=pl.Buffered(1))],
        out_specs=pl.BlockSpec((tm, PROJ_WIDTH), lambda i: (i, 0)),
        out_shape=jax.ShapeDtypeStruct((n, PROJ_WIDTH), jnp.float32),
        compiler_params=pltpu.CompilerParams(dimension_semantics=("arbitrary",), vmem_limit_bytes=PROJ_VMEM_LIMIT),
        name="in_proj")(x2, scale, shift, w_perm)


def _merge_body(oa_ref, ob_ref, z_ref, ga_ref, gb_ref, x_ref, gt_ref, sc_ref, sh_ref, wpa_ref, wpb_ref, wo_ref,
                h_ref, n2_ref):
    f32, bf = jnp.float32, jnp.bfloat16
    z = z_ref[...]
    ob = ob_ref[...] * (z * _sigmoid(z))
    oa = oa_ref[0].reshape(A_WIDTH, oa_ref.shape[-1]).T
    ya = jnp.dot(oa.astype(bf), wpa_ref[...], preferred_element_type=f32)
    yb = jnp.dot(ob.astype(bf), wpb_ref[...], preferred_element_type=f32)
    merged = _sigmoid(ga_ref[...]) * ya + _sigmoid(gb_ref[...]) * yb
    y1 = jnp.dot(merged.astype(bf), wo_ref[...], preferred_element_type=f32)
    h = x_ref[...] + gt_ref[0] * y1
    h_ref[...] = h
    n2_ref[...] = _modulated_norm(h, sc_ref[0], sh_ref[0])


def merge(o_a, o_b, proj, x2, gt1, scale2, shift2, w_pa, w_pb, w_o, seq):
    n, d = x2.shape
    tm = PROJ_TM
    per = seq // tm
    bf = jnp.bfloat16
    vec = pl.BlockSpec((1, 1, d), lambda i: (i // per, 0, 0))

    def col(name):
        off, w = PROJ_DST[name]
        return pl.BlockSpec((tm, w), lambda i: (i, off // w))
    row = lambda w: pl.BlockSpec((tm, w), lambda i: (i, 0))
    res = lambda a, b: pl.BlockSpec((a, b), lambda i: (0, 0))
    return pl.pallas_call(
        _merge_body, grid=(n // tm,),
        in_specs=[pl.BlockSpec((1, A_HEADS, A_HEAD_DIM, tm), lambda i: (i // per, 0, 0, i % per)), row(B_V_WIDTH), col("zb"), col("gate_a"), col("gate_b"), row(d), vec, vec, vec,
                  res(A_WIDTH, d), res(B_V_WIDTH, d), res(d, d)],
        out_specs=[row(d), row(d)],
        out_shape=[jax.ShapeDtypeStruct((n, d), jnp.float32), jax.ShapeDtypeStruct((n, d), jnp.float32)],
        compiler_params=pltpu.CompilerParams(dimension_semantics=("arbitrary",), vmem_limit_bytes=PROJ_VMEM_LIMIT),
        name="merge")(o_a, o_b, proj, proj, proj, x2, gt1, scale2, shift2,
                      w_pa.astype(bf), w_pb.astype(bf), w_o.astype(bf))


def token_mixers(proj, positions, conv_w, a_log, dt_bias, norm_b_w, bsz, seq):
    qit, qat, w, kit, kat, vat, bg = dsa_prep(proj, positions, a_log, dt_bias, bsz, seq)
    o_a = dsa_attention_pallas(qit, w, qat, kit, kat, vat)

    o_b = gated_delta_rule_pallas(proj, conv_w, bg[:, B_HEADS:], bg[:, :B_HEADS], norm_b_w, bsz, seq)
    return o_a, o_b


PEER_SLOTS = PEER_HEADS * PEER_TOPK
PEER_TB = 64
HALF_ROWS = 4
HI_MASK = -65536


def pack_table(tab):
    bits = lax.bitcast_convert_type(tab.astype(jnp.bfloat16), jnp.uint16).astype(jnp.uint32)
    half = tab.shape[1] // 2
    word = bits[:, :half] | (bits[:, half:] << 16)
    return lax.bitcast_convert_type(word, jnp.int32).reshape(tab.shape[0] * HALF_ROWS, LANES)


def _table_row(tab_ref, rows, k):
    return _unpack(tab_ref[pl.ds(pl.multiple_of(rows[k], HALF_ROWS), HALF_ROWS), :])


def _unpack(w):
    lo = lax.bitcast_convert_type(lax.shift_left(w, 16), jnp.float32)
    hi = lax.bitcast_convert_type(w & jnp.int32(HI_MASK), jnp.float32)
    return lo, hi


def _peer_u_body(idx_ref, x_ref, gate_ref, tab_ref, o_ref, s_ref, sb_ref):
    def lane_sums(t):
        a = jnp.sum(sb_ref[t].T, axis=0, keepdims=True)
        o_ref[pl.ds(t, 1), :] = 0.5 * a * (1.0 + lax.erf(a * (2.0 ** -0.5))) * gate_ref[pl.ds(t, 1), :]

    sb_ref[0] = jnp.zeros(sb_ref.shape[1:], jnp.float32)

    def tok(t, carry):
        lane_sums(jnp.maximum(t - 1, 0))
        x8 = x_ref[pl.ds(t, 1), :].reshape(2 * HALF_ROWS, LANES)
        xlo, xhi = x8[0:HALF_ROWS], x8[HALF_ROWS:2 * HALF_ROWS]
        rows = idx_ref.at[t]
        for k in range(PEER_SLOTS):
            lo, hi = _table_row(tab_ref, rows, k)
            s_ref[HALF_ROWS * k:HALF_ROWS * (k + 1), :] = lo * xlo + hi * xhi
        s4 = s_ref[pl.ds(0, PEER_SLOTS, stride=HALF_ROWS), :]
        for r in range(1, HALF_ROWS):
            s4 = s4 + s_ref[pl.ds(r, PEER_SLOTS, stride=HALF_ROWS), :]
        sb_ref[t] = s4
        return carry
    lax.fori_loop(0, PEER_TB, tok, 0, unroll=2)
    lane_sums(PEER_TB - 1)


def _peer_v_body(idx_ref, coef_ref, tab_ref, o_ref, cb_ref):
    def spread(t):
        return jnp.broadcast_to(coef_ref[pl.ds(t, 1), :], (PEER_SLOTS, LANES)).T

    def tok(t, weights):
        nxt = spread(jnp.minimum(t + 1, PEER_TB - 1))
        cb_ref[...] = weights
        nacc = 2
        acc = [jnp.zeros((HALF_ROWS, LANES), jnp.float32) for _ in range(2 * nacc)]
        rows = idx_ref.at[t]
        for k in range(PEER_SLOTS):
            lo, hi = _table_row(tab_ref, rows, k)
            c = jnp.broadcast_to(cb_ref[k:k + 1, :], (HALF_ROWS, LANES))
            a = k % nacc
            acc[2 * a] = acc[2 * a] + c * lo
            acc[2 * a + 1] = acc[2 * a + 1] + c * hi
        half = HALF_ROWS * LANES
        o_ref[pl.ds(t, 1), 0:half] = (acc[0] + acc[2]).reshape(1, half)
        o_ref[pl.ds(t, 1), half:2 * half] = (acc[1] + acc[3]).reshape(1, half)
        return nxt
    lax.fori_loop(0, PEER_TB, tok, spread(0))


def _table_spec():
    return pl.BlockSpec((PEER_N_EXPERTS * HALF_ROWS, LANES), lambda i: (0, 0), pipeline_mode=pl.Buffered(1))


def peer_u(idx, x, gates, tab):
    n = idx.shape[0]
    tb = PEER_TB
    return pl.pallas_call(
        _peer_u_body, grid=(n // tb,),
        in_specs=[pl.BlockSpec((tb, PEER_SLOTS), lambda i: (i, 0), memory_space=pltpu.SMEM),
                  pl.BlockSpec((tb, x.shape[1]), lambda i: (i, 0)),
                  pl.BlockSpec((tb, PEER_SLOTS), lambda i: (i, 0)),
                  _table_spec()],
        out_specs=pl.BlockSpec((tb, PEER_SLOTS), lambda i: (i, 0)),
        out_shape=jax.ShapeDtypeStruct((n, PEER_SLOTS), jnp.float32),
        scratch_shapes=[pltpu.VMEM((HALF_ROWS * PEER_SLOTS, LANES), jnp.float32),
                        pltpu.VMEM((tb, PEER_SLOTS, LANES), jnp.float32)],
        compiler_params=pltpu.CompilerParams(dimension_semantics=("arbitrary",), vmem_limit_bytes=PEER_VMEM_LIMIT),
        name="peer_u")(idx, x, gates, tab)


def peer_v(idx, coef, tab):
    n = idx.shape[0]
    tb = PEER_TB
    return pl.pallas_call(
        _peer_v_body, grid=(n // tb,),
        in_specs=[pl.BlockSpec((tb, PEER_SLOTS), lambda i: (i, 0), memory_space=pltpu.SMEM),
                  pl.BlockSpec((tb, PEER_SLOTS), lambda i: (i, 0)),
                  _table_spec()],
        out_specs=pl.BlockSpec((tb, 2 * HALF_ROWS * LANES), lambda i: (i, 0)),
        out_shape=jax.ShapeDtypeStruct((n, 2 * HALF_ROWS * LANES), jnp.float32),
        scratch_shapes=[pltpu.VMEM((PEER_SLOTS, LANES), jnp.float32)],
        compiler_params=pltpu.CompilerParams(dimension_semantics=("arbitrary",), vmem_limit_bytes=PEER_VMEM_LIMIT),
        name="peer_v")(idx, coef, tab)


PEER_TT = 256
PAD_ID = 2**30


def _extract_top(ref, n_out, rid=None):
    rows, t = ref.shape
    if rid is None:
        rid = lax.broadcasted_iota(jnp.int32, (rows, t), 0)
    vals, idxs = [], []
    for _ in range(n_out):
        s = ref[...]
        m = jnp.max(s, axis=0, keepdims=True)
        ix = jnp.min(jnp.where(s == m, rid, jnp.int32(PAD_ID)), axis=0, keepdims=True)
        ref[...] = jnp.where(rid == ix, -jnp.inf, s)
        vals.append(m)
        idxs.append(ix)
    return vals, idxs


def _peer_route_body(x_ref, wq_ref, sk_ref, cid_ref, idx_ref, gate_ref, s_ref, cand_ref, v_ref, i_ref, et_ref, gt_ref):
    kk, nk = PEER_TOPK, PEER_N_KEYS
    f32 = jnp.float32
    q = jnp.dot(x_ref[...].astype(jnp.bfloat16), wq_ref[...], preferred_element_type=f32).astype(jnp.bfloat16)
    nt = (((1,), (1,)), ((), ()))
    row16 = lax.broadcasted_iota(jnp.int32, (kk, PEER_TT), 0)
    for h in range(PEER_HEADS):
        qh = q[:, h * PEER_KEY_DIM:(h + 1) * PEER_KEY_DIM]
        s_ref[...] = lax.dot_general(sk_ref[h], qh, nt, preferred_element_type=f32)
        for p in range(2):
            vals, idxs = _extract_top(s_ref.at[p * nk:(p + 1) * nk, :], kk)
            for i in range(kk):
                v_ref[p, i:i + 1, :] = vals[i]
                i_ref[p, i:i + 1, :] = idxs[i]
        off = 0
        for i in range(kk):
            n_j = kk // (i + 1)
            cand_ref[off:off + n_j, :] = v_ref[0, i:i + 1, :] + v_ref[1, 0:n_j, :]
            off += n_j
        cand_ref[off:, :] = jnp.full((cand_ref.shape[0] - off, PEER_TT), -jnp.inf, f32)
        vals, cis = _extract_top(cand_ref, kk, cid_ref[...])
        i1, i2 = i_ref[0], i_ref[1]
        es = [jnp.exp(v - vals[0]) for v in vals]
        den = es[0]
        for e in es[1:]:
            den = den + e
        for k in range(kk):
            ci = cis[k]
            e1 = jnp.sum(jnp.where(row16 == lax.shift_right_logical(ci, 4), i1, 0), axis=0, keepdims=True)
            e2 = jnp.sum(jnp.where(row16 == (ci & 15), i2, 0), axis=0, keepdims=True)
            et_ref[h * kk + k:h * kk + k + 1, :] = (e1 * nk + e2) * HALF_ROWS
            gt_ref[h * kk + k:h * kk + k + 1, :] = es[k] / den
    idx_ref[...] = lax.bitcast_convert_type(lax.bitcast_convert_type(et_ref[...], f32).T, jnp.int32)
    gate_ref[...] = gt_ref[...].T


def _pair_cells():
    kk = PEER_TOPK
    ids = [i * kk + j for i in range(kk) for j in range(kk // (i + 1))]
    return ids + [PAD_ID] * (-len(ids) % SUBLANES)


def peer_route(xn, wq, subkeys):
    n = xn.shape[0]
    tt = PEER_TT
    cells = _pair_cells()
    cid = jnp.broadcast_to(jnp.asarray(cells, jnp.int32)[:, None], (len(cells), tt))
    half = PEER_KEY_DIM // 2
    z = jnp.zeros((PEER_HEADS, PEER_N_KEYS, half), subkeys.dtype)
    skbd = jnp.concatenate([jnp.concatenate([subkeys[:, 0], z], axis=-1),
                            jnp.concatenate([z, subkeys[:, 1]], axis=-1)], axis=1).astype(jnp.bfloat16)
    return pl.pallas_call(
        _peer_route_body, grid=(n // tt,),
        in_specs=[pl.BlockSpec((tt, D_MODEL), lambda i: (i, 0)),
                  pl.BlockSpec((D_MODEL, PEER_HEADS * PEER_KEY_DIM), lambda i: (0, 0)),
                  pl.BlockSpec((PEER_HEADS, 2 * PEER_N_KEYS, PEER_KEY_DIM), lambda i: (0, 0, 0)),
                  pl.BlockSpec((len(cells), tt), lambda i: (0, 0))],
        out_specs=[pl.BlockSpec((tt, PEER_SLOTS), lambda i: (i, 0)), pl.BlockSpec((tt, PEER_SLOTS), lambda i: (i, 0))],
        out_shape=[jax.ShapeDtypeStruct((n, PEER_SLOTS), jnp.int32), jax.ShapeDtypeStruct((n, PEER_SLOTS), jnp.float32)],
        scratch_shapes=[pltpu.VMEM((2 * PEER_N_KEYS, tt), jnp.float32),
                        pltpu.VMEM((len(cells), tt), jnp.float32),
                        pltpu.VMEM((2, PEER_TOPK, tt), jnp.float32),
                        pltpu.VMEM((2, PEER_TOPK, tt), jnp.int32),
                        pltpu.VMEM((PEER_SLOTS, tt), jnp.int32),
                        pltpu.VMEM((PEER_SLOTS, tt), jnp.float32)],
        compiler_params=pltpu.CompilerParams(dimension_semantics=("arbitrary",), vmem_limit_bytes=ROUTE_VMEM_LIMIT),
        name="peer_route")(xn, wq.astype(jnp.bfloat16), skbd, cid)


def peer_channel_mixer(xn, wq, subkeys, u_tab, v_tab):
    bsz, seq, d = xn.shape
    n_tok = bsz * seq
    x2 = xn.reshape(n_tok, d)
    idx, gates = peer_route(x2, wq, subkeys)
    coef = peer_u(idx, x2, gates, pack_table(u_tab))
    out = peer_v(idx, coef, pack_table(v_tab))
    return out.reshape(bsz, seq, d)


def _residual_body(h_ref, y_ref, gt_ref, o_ref):
    o_ref[...] = h_ref[...] + gt_ref[0] * y_ref[...]


def _final_norm_body(h_ref, y_ref, gt_ref, g_ref, o_ref):
    h = h_ref[...] + gt_ref[0] * y_ref[...]
    o_ref[...] = h * lax.rsqrt(jnp.mean(h * h, axis=-1, keepdims=True) + EPS) * g_ref[...]


def gated_residual(h2, y2, gt, seq, gain=None):
    n, d = h2.shape
    tm = min(1024, seq)
    per = seq // tm
    row = pl.BlockSpec((tm, d), lambda i: (i, 0))
    specs = [row, row, pl.BlockSpec((1, 1, d), lambda i: (i // per, 0, 0))]
    args = [h2, y2, gt]
    if gain is not None:
        specs.append(pl.BlockSpec((1, d), lambda i: (0, 0)))
        args.append(gain.astype(jnp.float32).reshape(1, d))
    return pl.pallas_call(
        _residual_body if gain is None else _final_norm_body, grid=(n // tm,),
        in_specs=specs, out_specs=row, out_shape=jax.ShapeDtypeStruct((n, d), h2.dtype),
        name="gated_residual" if gain is None else "final_norm")(*args)


def kernel(x, c, positions, w_ada, b_ada, w_in, conv_w, a_log, dt_bias, norm_b_w,
           w_pa, w_pb, w_o, peer_wq, peer_subkeys, peer_u, peer_v, final_norm_w):
    bsz, seq, d = x.shape
    h = x.reshape(bsz * seq, d)
    for layer in range(DEPTH):
        mod = ada_modulation(c, w_ada[layer], b_ada[layer])
        sh1, sc1, gt1, sh2, sc2, gt2 = [m.reshape(bsz, 1, d) for m in jnp.split(mod, 6, axis=-1)]
        proj = in_proj(h, 1.0 + sc1, sh1, permute_w_in(w_in[layer]), seq)
        o_a, o_b = token_mixers(proj, positions, conv_w[layer], a_log[layer], dt_bias[layer], norm_b_w[layer],
                                bsz, seq)
        h, n2 = merge(o_a, o_b.reshape(bsz * seq, B_V_WIDTH), proj, h, gt1,
                      1.0 + sc2, sh2, w_pa[layer], w_pb[layer], w_o[layer], seq)
        y2 = peer_channel_mixer(n2.reshape(bsz, seq, d), peer_wq[layer], peer_subkeys[layer], peer_u[layer],
                                peer_v[layer])
        last = layer == DEPTH - 1
        h = gated_residual(h, y2.reshape(bsz * seq, d), gt2, seq, final_norm_w if last else None)
    return h.reshape(bsz, seq, d)
```

```python
import functools

import jax, jax.numpy as jnp
from jax import lax
from jax.experimental import pallas as pl
from jax.experimental.pallas import tpu as pltpu

D_MODEL = 1024
DEPTH = 1

LANES = 128
SUBLANES = 8
MIB = 1024 * 1024
DSA_VMEM_LIMIT = 48 * MIB
PREP_VMEM_LIMIT = 32 * MIB
GDN_VMEM_LIMIT = 40 * MIB
PROJ_VMEM_LIMIT = 48 * MIB
PEER_VMEM_LIMIT = 48 * MIB
ROUTE_VMEM_LIMIT = 32 * MIB

A_HEADS = 8
A_KV_HEADS = 2
A_HEAD_DIM = 64
IDX_HEADS = 16
IDX_DIM = 64
IDX_TOPK_MAX = 256
B_HEADS = 8
B_KEY_DIM = 64
B_VAL_DIM = 64
CONV_WIDTH = 4
CHUNK = 64
ROPE_THETA = 500000.0
ROPE_FRACTION_DEN = 4
PEER_HEADS = 8
PEER_KEY_DIM = 128
PEER_N_KEYS = 128
PEER_N_EXPERTS = PEER_N_KEYS * PEER_N_KEYS
PEER_TOPK = 16
EPS = 1e-6

A_WIDTH = A_HEADS * A_HEAD_DIM
KV_WIDTH = A_KV_HEADS * A_HEAD_DIM
B_QK_WIDTH = B_HEADS * B_KEY_DIM
B_V_WIDTH = B_HEADS * B_VAL_DIM
IN_SPLITS = (A_WIDTH, KV_WIDTH, KV_WIDTH, IDX_HEADS * IDX_DIM, IDX_DIM, IDX_HEADS,
             B_QK_WIDTH, B_QK_WIDTH, B_V_WIDTH, B_V_WIDTH, B_HEADS, B_HEADS, D_MODEL, D_MODEL)


DSA_TQ = 256
DSA_TK = 256
INT_MIN = -2**31
NEG_BIG = -1e30


def _dsa_body(topk, qit_ref, w_ref, qat_ref, ki_ref, ka_ref, vat_ref, o_ref,
              key_ref):
    tq, tk = DSA_TQ, DSA_TK
    qb = pl.program_id(1)
    n_kv = qb + 1
    t_glob = qb * tq + lax.broadcasted_iota(jnp.int32, (1, tq), 1)
    row = lax.broadcasted_iota(jnp.int32, (tk, 1), 0)
    f32 = jnp.float32

    def p1(j, carry):
        kt = ki_ref[0, j]
        score = jnp.zeros((tk, tq), f32)
        for h in range(IDX_HEADS):
            lt = jnp.dot(kt, qit_ref[0, h], preferred_element_type=f32)
            score = score + w_ref[0, h:h + 1, :] * jnp.maximum(lt, 0.0)
        bits = lax.bitcast_convert_type(score + 0.0, jnp.int32)
        skey = jnp.where(bits >= 0, bits, bits ^ jnp.int32(0x7FFFFFFF))
        skey = jnp.where(j * tk + row <= t_glob, skey, jnp.int32(INT_MIN))
        key_ref[j] = skey
        return carry
    lax.fori_loop(0, n_kv, p1, 0)

    key_ref[n_kv] = jnp.full((tk, tq), INT_MIN, jnp.int32)

    def count(pred):
        def body(jj, acc):
            for j in (2 * jj, 2 * jj + 1):
                hit = jnp.where(pred(key_ref[j], j * tk + row), 1.0, 0.0)
                acc = acc + jnp.sum(hit.reshape(tk // SUBLANES, SUBLANES, tq), axis=0)
            return acc
        acc = lax.fori_loop(0, (n_kv + 1) // 2, body, jnp.zeros((SUBLANES, tq), f32))
        return jnp.sum(acc, axis=0, keepdims=True)

    kf = jnp.float32(topk)

    def bit_step(i, ku):
        cand_u = ku | lax.shift_left(jnp.int32(1), 31 - i)
        cand = cand_u ^ jnp.int32(INT_MIN)
        c = count(lambda k, s: k >= cand)
        return jnp.where(c >= kf, cand_u, ku)
    ku = lax.fori_loop(0, 32, bit_step, jnp.zeros((1, tq), jnp.int32))
    kth = ku ^ jnp.int32(INT_MIN)
    c_gt = count(lambda k, s: k > kth)
    c_ge = count(lambda k, s: k >= kth)
    short = kth == jnp.int32(INT_MIN)
    x0 = jnp.where(short, jnp.int32(-1), jnp.int32(2**30))
    need = kf - c_gt
    has_tie = jnp.max(jnp.where(jnp.logical_and(c_ge > kf, jnp.logical_not(short)), 1.0, 0.0)) > 0.0

    def tie_search():
        def step(i, x):
            bit = lax.shift_left(jnp.int32(1), 11 - i)
            probe = x + bit - 1
            c = count(lambda k, s: jnp.logical_and(k == kth, s <= probe))
            return jnp.where(c < need, x + bit, x)
        x = lax.fori_loop(0, 12, step, jnp.zeros((1, tq), jnp.int32))
        return jnp.where(short, jnp.int32(-1), x)
    x_lim = lax.cond(has_tie, tie_search, lambda: x0)

    rep = A_HEADS // A_KV_HEADS
    hs = range(A_HEADS)

    def p3(j, carry):
        m, l, acc = carry
        skey = key_ref[j]
        s_idx = j * tk + row
        sel = jnp.logical_or(skey > kth, jnp.logical_and(skey == kth, s_idx <= x_lim))
        kt = [ka_ref[0, g, j] for g in range(A_KV_HEADS)]
        vt = [vat_ref[0, g, j] for g in range(A_KV_HEADS)]
        s = [jnp.where(sel, jnp.dot(kt[h // rep], qat_ref[0, h], preferred_element_type=f32), NEG_BIG) for h in hs]
        m_new = [jnp.maximum(m[h], jnp.max(s[h], axis=0, keepdims=True)) for h in hs]
        alpha = [jnp.exp(m[h] - m_new[h]) for h in hs]
        p = [jnp.exp(s[h] - m_new[h]) for h in hs]
        l_new = [alpha[h] * l[h] + jnp.sum(p[h], axis=0, keepdims=True) for h in hs]
        acc_new = [alpha[h] * acc[h] + jnp.dot(vt[h // rep], p[h].astype(jnp.bfloat16), preferred_element_type=f32)
                   for h in hs]
        return tuple(m_new), tuple(l_new), tuple(acc_new)

    init = (tuple(jnp.full((1, tq), NEG_BIG, f32) for _ in hs), tuple(jnp.zeros((1, tq), f32) for _ in hs),
            tuple(jnp.zeros((A_HEAD_DIM, tq), f32) for _ in hs))
    _, l_fin, acc_fin = lax.fori_loop(0, n_kv, p3, init)
    for h in hs:
        o_ref[0, h] = acc_fin[h] / l_fin[h]


ROPE_HALF = A_HEAD_DIM // ROPE_FRACTION_DEN // 2


def _rope_t(xt, n_heads, cos, sin, scale=None):
    outs = []
    for h in range(n_heads):
        b = h * A_HEAD_DIM
        x1, x2, rest = xt[b:b + ROPE_HALF], xt[b + ROPE_HALF:b + 2 * ROPE_HALF], xt[b + 2 * ROPE_HALF:b + A_HEAD_DIM]
        o = jnp.concatenate([x1 * cos - x2 * sin, x2 * cos + x1 * sin, rest], axis=0)
        outs.append(o if scale is None else o * scale)
    return outs


def _dsa_prep_body(qi_ref, qa_ref, ka_ref, va_ref, sm_ref, cos_ref, sin_ref, dp_ref,
                   qit_ref, qat_ref, w_ref, kit_ref, kat_ref, vat_ref, bg_ref):
    bf = jnp.bfloat16
    cos, sin = cos_ref[0], sin_ref[0]
    for h, o in enumerate(_rope_t(qi_ref[...].T, IDX_HEADS, cos, sin)):
        qit_ref[0, h] = o.astype(bf)
    for h, o in enumerate(_rope_t(qa_ref[...].T, A_HEADS, cos, sin, A_HEAD_DIM ** -0.5)):
        qat_ref[0, h] = o.astype(bf)
    kat = _rope_t(ka_ref[...].T, A_KV_HEADS, cos, sin)
    vt = va_ref[...].T
    for g in range(A_KV_HEADS):
        kat_ref[0, g, 0] = kat[g].T.astype(bf)
        vat_ref[0, g, 0] = vt[g * A_HEAD_DIM:(g + 1) * A_HEAD_DIM].astype(bf)
    smt = sm_ref[...].T
    kit_ref[0, 0] = _rope_t(smt[0:IDX_DIM], 1, cos, sin)[0].T.astype(bf)
    w_ref[0] = smt[IDX_DIM:IDX_DIM + IDX_HEADS] * ((IDX_HEADS ** -0.5) * (IDX_DIM ** -0.5))
    b0 = IDX_DIM + IDX_HEADS
    bb, ab = smt[b0:b0 + B_HEADS], smt[b0 + B_HEADS:b0 + 2 * B_HEADS]
    a_log, dt_bias = dp_ref[:, 0:1], dp_ref[:, 1:2]
    bg_ref[0] = jnp.concatenate([_sigmoid(bb), -jnp.exp(a_log) * jax.nn.softplus(ab + dt_bias)], axis=0)


def dsa_prep(proj, positions, a_log, dt_bias, bsz, seq):
    t = DSA_TQ
    per = seq // t
    assert DSA_TK == t and PROJ_ORDER[-4:] == ("ki", "wi", "bb", "ab") and PROJ_DST["ki"][0] + LANES == PROJ_WIDTH
    rd = A_HEAD_DIM // ROPE_FRACTION_DEN
    inv_freq = jnp.power(jnp.float32(ROPE_THETA), -jnp.arange(ROPE_HALF, dtype=jnp.float32) * (2.0 / rd))
    ang = positions.astype(jnp.float32)[:, None, :] * inv_freq[None, :, None]

    def col(name, width=None):
        off, w = PROJ_DST[name]
        w = width or w
        return pl.BlockSpec((t, w), lambda i: (i, off // w))
    trig = pl.BlockSpec((1, ROPE_HALF, t), lambda i: (i // per, 0, i % per))
    bf, f32 = jnp.bfloat16, jnp.float32
    sd = jax.ShapeDtypeStruct
    n_kv = seq // DSA_TK
    return pl.pallas_call(
        _dsa_prep_body, grid=(bsz * per,),
        in_specs=[col("qi"), col("qa"), col("ka"), col("va"), col("ki", LANES), trig, trig,
                  pl.BlockSpec((B_HEADS, 2), lambda i: (0, 0))],
        out_specs=[pl.BlockSpec((1, IDX_HEADS, IDX_DIM, t), lambda i: (i // per, 0, 0, i % per)),
                   pl.BlockSpec((1, A_HEADS, A_HEAD_DIM, t), lambda i: (i // per, 0, 0, i % per)),
                   pl.BlockSpec((1, IDX_HEADS, t), lambda i: (i // per, 0, i % per)),
                   pl.BlockSpec((1, 1, t, IDX_DIM), lambda i: (i // per, i % per, 0, 0)),
                   pl.BlockSpec((1, A_KV_HEADS, 1, t, A_HEAD_DIM), lambda i: (i // per, 0, i % per, 0, 0)),
                   pl.BlockSpec((1, A_KV_HEADS, 1, A_HEAD_DIM, t), lambda i: (i // per, 0, i % per, 0, 0)),
                   pl.BlockSpec((1, 2 * B_HEADS, t), lambda i: (i // per, 0, i % per))],
        out_shape=[sd((bsz, IDX_HEADS, IDX_DIM, seq), bf), sd((bsz, A_HEADS, A_HEAD_DIM, seq), bf),
                   sd((bsz, IDX_HEADS, seq), f32), sd((bsz, n_kv, DSA_TK, IDX_DIM), bf),
                   sd((bsz, A_KV_HEADS, n_kv, DSA_TK, A_HEAD_DIM), bf),
                   sd((bsz, A_KV_HEADS, n_kv, A_HEAD_DIM, DSA_TK), bf), sd((bsz, 2 * B_HEADS, seq), f32)],
        compiler_params=pltpu.CompilerParams(dimension_semantics=("arbitrary",), vmem_limit_bytes=PREP_VMEM_LIMIT),
        name="dsa_prep")(proj, proj, proj, proj, proj, jnp.cos(ang), jnp.sin(ang),
                         jnp.stack([a_log, dt_bias], axis=-1).astype(f32))


def dsa_attention_pallas(qit, w, qat, kit, kat, vat):
    bsz, seq = qit.shape[0], qit.shape[-1]
    tq, tk = DSA_TQ, DSA_TK
    topk = min(IDX_TOPK_MAX, seq // 4)
    n_kv = seq // tk
    return pl.pallas_call(
        functools.partial(_dsa_body, topk),
        grid=(bsz, seq // tq),
        in_specs=[
            pl.BlockSpec((1, IDX_HEADS, IDX_DIM, tq), lambda b, q: (b, 0, 0, q)),
            pl.BlockSpec((1, IDX_HEADS, tq), lambda b, q: (b, 0, q)),
            pl.BlockSpec((1, A_HEADS, A_HEAD_DIM, tq), lambda b, q: (b, 0, 0, q)),
            pl.BlockSpec((1, n_kv, tk, IDX_DIM), lambda b, q: (b, 0, 0, 0)),
            pl.BlockSpec((1, A_KV_HEADS, n_kv, tk, A_HEAD_DIM), lambda b, q: (b, 0, 0, 0, 0)),
            pl.BlockSpec((1, A_KV_HEADS, n_kv, A_HEAD_DIM, tk), lambda b, q: (b, 0, 0, 0, 0)),
        ],
        out_specs=pl.BlockSpec((1, A_HEADS, A_HEAD_DIM, tq), lambda b, q: (b, 0, 0, q)),
        out_shape=jax.ShapeDtypeStruct((bsz, A_HEADS, A_HEAD_DIM, seq), jnp.float32),
        scratch_shapes=[
            pltpu.VMEM((n_kv + 1, tk, tq), jnp.int32),
        ],
        compiler_params=pltpu.CompilerParams(dimension_semantics=("arbitrary", "arbitrary"),
                                             vmem_limit_bytes=DSA_VMEM_LIMIT),
        name="dsa_attention",
    )(qit, w, qat, kit, kat, vat)


GDN_G = 8
GDN_BASE = 8
GDN_PAD = 8


def _gdn_body(q_ref, k_ref, v_ref, cw_ref, gc_ref, bt_ref, nw_ref, o_ref, s_ref, xs_ref, cs_ref):
    c_sz = CHUNK
    f32, bf = jnp.float32, jnp.bfloat16

    rows = GDN_G * c_sz
    halo = CONV_WIDTH - 1

    @pl.when(pl.program_id(1) == 0)
    def _():
        s_ref[...] = jnp.zeros(s_ref.shape, f32)
        xs_ref[:, 0:GDN_PAD, :] = jnp.zeros((3, GDN_PAD, xs_ref.shape[-1]), f32)

    for j, ref in enumerate((q_ref, k_ref, v_ref)):
        xs_ref[j, GDN_PAD:GDN_PAD + rows, :] = ref[0]
        acc = xs_ref[j, GDN_PAD - halo:GDN_PAD - halo + rows, :] * cw_ref[j, 0:1, :]
        for i in range(1, CONV_WIDTH):
            acc = acc + xs_ref[j, GDN_PAD - halo + i:GDN_PAD - halo + i + rows, :] * cw_ref[j, i:i + 1, :]
        cs_ref[j] = acc * _sigmoid(acc)
        xs_ref[j, GDN_PAD - halo:GDN_PAD, :] = xs_ref[j, GDN_PAD + rows - halo:GDN_PAD + rows, :]

    ri = lax.broadcasted_iota(jnp.int32, (c_sz, c_sz), 0)
    ci = lax.broadcasted_iota(jnp.int32, (c_sz, c_sz), 1)
    incl, strict = ri >= ci, ri > ci
    eye = jnp.where(ri == ci, 1.0, 0.0).astype(f32)
    blk = lambda w: (ri // w) == (ci // w)
    diag8 = blk(GDN_BASE)
    sub_blocks = []
    w = GDN_BASE
    while w < c_sz:
        sub_blocks.append(jnp.logical_and(blk(2 * w), jnp.logical_not(blk(w))))
        w *= 2
    nt = (((1,), (1,)), ((), ()))
    dot = lambda a, b: jnp.dot(a.astype(bf), b.astype(bf), preferred_element_type=f32)
    dot_nt = lambda a, b: lax.dot_general(a.astype(bf), b.astype(bf), nt, preferred_element_type=f32)

    def split(a):
        hi = a.astype(bf)
        return hi, (a - hi.astype(f32)).astype(bf)

    def hp(a, b):
        a_hi, a_lo = split(a)
        b_hi, b_lo = split(b)
        mm = lambda u, w: jnp.dot(u, w, preferred_element_type=f32)
        return mm(a_hi, b_hi) + (mm(a_hi, b_lo) + mm(a_lo, b_hi))

    hs = range(B_HEADS)

    def chunk_pair(i, carry):
        cs = [2 * i, 2 * i + 1]
        r0 = [pl.multiple_of(c * c_sz, c_sz) for c in cs]
        items = [(j, h) for j in range(2) for h in hs]
        hd = lambda a, j, h: cs_ref[a, pl.ds(r0[j], c_sz), h * B_KEY_DIM:(h + 1) * B_KEY_DIM]
        l2n = lambda t: t * lax.rsqrt(jnp.sum(t * t, axis=-1, keepdims=True) + EPS)
        q = [l2n(hd(0, j, h)) * (B_KEY_DIM ** -0.5) for j, h in items]
        k = [l2n(hd(1, j, h)) for j, h in items]
        v = [hd(2, j, h) for j, h in items]
        gc8 = [gc_ref[0, c] for c in cs]
        gct = [g8.T for g8 in gc8]
        btt = [bt_ref[0, c].T for c in cs]
        gcr = [gc8[j][h:h + 1, :] for j, h in items]
        n_it = range(len(items))
        gcc = [gct[j][:, h:h + 1] for j, h in items]
        beta = [btt[j][:, h:h + 1] for j, h in items]
        decay = [jnp.exp(jnp.where(incl, gcc[n] - gcr[n], -jnp.inf)) for n in n_it]
        kb = [k[n] * beta[n] for n in n_it]
        vb = [v[n] * beta[n] for n in n_it]
        low = [jnp.where(strict, dot_nt(kb[n], k[n]) * decay[n], 0.0) for n in n_it]
        dg = [jnp.where(diag8, low[n], 0.0) for n in n_it]
        t = [eye - dg[n] for n in n_it]
        p = [hp(dg[n], dg[n]) for n in n_it]
        t = [hp(t[n], eye + p[n]) for n in n_it]
        p = [hp(p[n], p[n]) for n in n_it]
        t = [hp(t[n], eye + p[n]) for n in n_it]
        for below in sub_blocks:
            lb = [jnp.where(below, low[n], 0.0) for n in n_it]
            lt = [hp(lb[n], t[n]) for n in n_it]
            t = [t[n] - hp(t[n], lt[n]) for n in n_it]
        u = [dot(t[n], vb[n]) for n in n_it]
        kcd = [dot(t[n], kb[n] * jnp.exp(gcc[n])) for n in n_it]
        intra = [dot_nt(q[n], k[n]) * decay[n] for n in n_it]
        qg = [q[n] * jnp.exp(gcc[n]) for n in n_it]
        glast = [g[:, c_sz - 1:c_sz] for g in gcr]
        kdt = [(k[n] * jnp.exp(glast[n] - gcc[n])).T for n in n_it]
        s = [s_ref[h] for h in hs]
        for j in range(2):
            ix = [j * B_HEADS + h for h in hs]
            v_new = [u[ix[h]] - dot(kcd[ix[h]], s[h]) for h in hs]
            out = [dot(qg[ix[h]], s[h]) + dot(intra[ix[h]], v_new[h]) for h in hs]
            s = [s[h] * jnp.exp(glast[ix[h]]) + dot(kdt[ix[h]], v_new[h]) for h in hs]
            for h in hs:
                o = out[h]
                o = o * lax.rsqrt(jnp.mean(o * o, axis=-1, keepdims=True) + EPS) * nw_ref[...]
                o_ref[0, pl.ds(r0[j], c_sz), h * B_VAL_DIM:(h + 1) * B_VAL_DIM] = o
        for h in hs:
            s_ref[h] = s[h]
        return carry
    lax.fori_loop(0, GDN_G // 2, chunk_pair, 0)


def gated_delta_rule_pallas(proj, conv_w, g, beta, norm_w, bsz, seq):
    nh, dk, dv = B_HEADS, B_KEY_DIM, B_VAL_DIM
    n = seq // CHUNK
    chunked = lambda t: jnp.moveaxis(t.reshape(bsz, nh, n, CHUNK), 1, 2)
    gc = chunked(jnp.cumsum(g.reshape(bsz, nh, n, CHUNK), axis=-1))
    side = pl.BlockSpec((1, GDN_G, nh, CHUNK), lambda b, c: (b, c, 0, 0))
    rows = GDN_G * CHUNK
    width = nh * dk

    def col(name):
        off, w = PROJ_DST[name]
        assert w == width
        return pl.BlockSpec((1, rows, w), lambda b, c: (b, c, off // w))
    cw = jnp.transpose(conv_w.astype(jnp.float32).reshape(CONV_WIDTH, 3, width), (1, 0, 2))
    proj3 = proj.reshape(bsz, seq, proj.shape[-1])
    return pl.pallas_call(
        _gdn_body, grid=(bsz, n // GDN_G),
        in_specs=[col("qb"), col("kb"), col("vb"),
                  pl.BlockSpec((3, CONV_WIDTH, width), lambda b, c: (0, 0, 0)),
                  side, side, pl.BlockSpec((1, dv), lambda b, c: (0, 0))],
        out_specs=pl.BlockSpec((1, rows, nh * dv), lambda b, c: (b, c, 0)),
        out_shape=jax.ShapeDtypeStruct((bsz, seq, nh * dv), jnp.float32),
        scratch_shapes=[pltpu.VMEM((nh, dk, dv), jnp.float32),
                        pltpu.VMEM((3, GDN_PAD + rows, width), jnp.float32),
                        pltpu.VMEM((3, rows, width), jnp.float32)],
        compiler_params=pltpu.CompilerParams(dimension_semantics=("arbitrary", "arbitrary"),
                                             vmem_limit_bytes=GDN_VMEM_LIMIT),
        name="gated_delta_rule")(proj3, proj3, proj3, cw, gc, chunked(beta), norm_w.astype(jnp.float32).reshape(1, dv))


IN_NAMES = ("qa", "ka", "va", "qi", "ki", "wi", "qb", "kb", "vb", "zb", "bb", "ab", "gate_a", "gate_b")
PROJ_ORDER = ("gate_a", "gate_b", "qi", "qa", "qb", "kb", "vb", "zb", "ka", "va", "ki", "wi", "bb", "ab")


def _proj_layout():
    src, off = {}, 0
    for name, w in zip(IN_NAMES, IN_SPLITS):
        src[name] = (off, w)
        off += w
    dst, off = {}, 0
    for name in PROJ_ORDER:
        dst[name] = (off, src[name][1])
        off += src[name][1]
    return src, dst, -(-off // LANES) * LANES


PROJ_SRC, PROJ_DST, PROJ_WIDTH = _proj_layout()
PROJ_TM = 512


def permute_w_in(w_in):
    cols = [w_in[:, PROJ_SRC[n][0]:PROJ_SRC[n][0] + PROJ_SRC[n][1]] for n in PROJ_ORDER]
    cols.append(jnp.zeros((w_in.shape[0], PROJ_WIDTH - sum(c.shape[1] for c in cols)), w_in.dtype))
    return jnp.concatenate(cols, axis=1).astype(jnp.bfloat16)


def _modulated_norm(x, scale, shift):
    return x * lax.rsqrt(jnp.mean(x * x, axis=-1, keepdims=True) + EPS) * scale + shift


def _sigmoid(v):
    return 1.0 / (1.0 + jnp.exp(-v))


def _ada_body(c_ref, w_ref, b_ref, o_ref):
    c = c_ref[...]
    o_ref[...] = jnp.dot((c * _sigmoid(c)).astype(jnp.bfloat16), w_ref[...].astype(jnp.bfloat16),
                         preferred_element_type=jnp.float32) + b_ref[...]


def ada_modulation(c, w_ada, b_ada):
    bsz, d = c.shape
    n = w_ada.shape[1]
    tn = D_MODEL
    return pl.pallas_call(
        _ada_body, grid=(n // tn,),
        in_specs=[pl.BlockSpec((bsz, d), lambda j: (0, 0)), pl.BlockSpec((d, tn), lambda j: (0, j)),
                  pl.BlockSpec((1, tn), lambda j: (0, j))],
        out_specs=pl.BlockSpec((bsz, tn), lambda j: (0, j)),
        out_shape=jax.ShapeDtypeStruct((bsz, n), jnp.float32),
        name="ada_modulation")(c, w_ada, b_ada.reshape(1, n))


def _in_proj_body(x_ref, sc_ref, sh_ref, w_ref, o_ref):
    n1 = _modulated_norm(x_ref[...], sc_ref[0], sh_ref[0])
    o_ref[...] = jnp.dot(n1.astype(jnp.bfloat16), w_ref[...], preferred_element_type=jnp.float32)


def in_proj(x2, scale, shift, w_perm, seq):
    n, d = x2.shape
    tm = PROJ_TM
    per = seq // tm
    vec = pl.BlockSpec((1, 1, d), lambda i: (i // per, 0, 0))
    return pl.pallas_call(
        _in_proj_body, grid=(n // tm,),
        in_specs=[pl.BlockSpec((tm, d), lambda i: (i, 0)), vec, vec,
                  pl.BlockSpec((d, PROJ_WIDTH), lambda i: (0, 0), pipeline_mode=pl.Buffered(1))],
        out_specs=pl.BlockSpec((tm, PROJ_WIDTH), lambda i: (i, 0)),
        out_shape=jax.ShapeDtypeStruct((n, PROJ_WIDTH), jnp.float32),
        compiler_params=pltpu.CompilerParams(dimension_semantics=("arbitrary",), vmem_limit_bytes=PROJ_VMEM_LIMIT),
        name="in_proj")(x2, scale, shift, w_perm)


def _merge_body(oa_ref, ob_ref, z_ref, ga_ref, gb_ref, x_ref, gt_ref, sc_ref, sh_ref, wpa_ref, wpb_ref, wo_ref,
                h_ref, n2_ref):
    f32, bf = jnp.float32, jnp.bfloat16
    z = z_ref[...]
    ob = ob_ref[...] * (z * _sigmoid(z))
    oa = oa_ref[0].reshape(A_WIDTH, oa_ref.shape[-1]).T
    ya = jnp.dot(oa.astype(bf), wpa_ref[...], preferred_element_type=f32)
    yb = jnp.dot(ob.astype(bf), wpb_ref[...], preferred_element_type=f32)
    merged = _sigmoid(ga_ref[...]) * ya + _sigmoid(gb_ref[...]) * yb
    y1 = jnp.dot(merged.astype(bf), wo_ref[...], preferred_element_type=f32)
    h = x_ref[...] + gt_ref[0] * y1
    h_ref[...] = h
    n2_ref[...] = _modulated_norm(h, sc_ref[0], sh_ref[0])


def merge(o_a, o_b, proj, x2, gt1, scale2, shift2, w_pa, w_pb, w_o, seq):
    n, d = x2.shape
    tm = PROJ_TM
    per = seq // tm
    bf = jnp.bfloat16
    vec = pl.BlockSpec((1, 1, d), lambda i: (i // per, 0, 0))

    def col(name):
        off, w = PROJ_DST[name]
        return pl.BlockSpec((tm, w), lambda i: (i, off // w))
    row = lambda w: pl.BlockSpec((tm, w), lambda i: (i, 0))
    res = lambda a, b: pl.BlockSpec((a, b), lambda i: (0, 0))
    return pl.pallas_call(
        _merge_body, grid=(n // tm,),
        in_specs=[pl.BlockSpec((1, A_HEADS, A_HEAD_DIM, tm), lambda i: (i // per, 0, 0, i % per)), row(B_V_WIDTH), col("zb"), col("gate_a"), col("gate_b"), row(d), vec, vec, vec,
                  res(A_WIDTH, d), res(B_V_WIDTH, d), res(d, d)],
        out_specs=[row(d), row(d)],
        out_shape=[jax.ShapeDtypeStruct((n, d), jnp.float32), jax.ShapeDtypeStruct((n, d), jnp.float32)],
        compiler_params=pltpu.CompilerParams(dimension_semantics=("arbitrary",), vmem_limit_bytes=PROJ_VMEM_LIMIT),
        name="merge")(o_a, o_b, proj, proj, proj, x2, gt1, scale2, shift2,
                      w_pa.astype(bf), w_pb.astype(bf), w_o.astype(bf))


def token_mixers(proj, positions, conv_w, a_log, dt_bias, norm_b_w, bsz, seq):
    qit, qat, w, kit, kat, vat, bg = dsa_prep(proj, positions, a_log, dt_bias, bsz, seq)
    o_a = dsa_attention_pallas(qit, w, qat, kit, kat, vat)

    o_b = gated_delta_rule_pallas(proj, conv_w, bg[:, B_HEADS:], bg[:, :B_HEADS], norm_b_w, bsz, seq)
    return o_a, o_b


PEER_SLOTS = PEER_HEADS * PEER_TOPK
PEER_TB = 128
HALF_ROWS = 4
HI_MASK = -65536


def pack_table(tab):
    bits = lax.bitcast_convert_type(tab.astype(jnp.bfloat16), jnp.uint16).astype(jnp.uint32)
    half = tab.shape[1] // 2
    word = bits[:, :half] | (bits[:, half:] << 16)
    return lax.bitcast_convert_type(word, jnp.int32).reshape(tab.shape[0] * HALF_ROWS, LANES)


def _table_row(tab_ref, rows, k):
    return _unpack(tab_ref[pl.ds(pl.multiple_of(rows[k], HALF_ROWS), HALF_ROWS), :])


def _unpack(w):
    lo = lax.bitcast_convert_type(lax.shift_left(w, 16), jnp.float32)
    hi = lax.bitcast_convert_type(w & jnp.int32(HI_MASK), jnp.float32)
    return lo, hi


def _peer_u_body(idx_ref, x_ref, gate_ref, tab_ref, o_ref, s_ref, sb_ref):
    def lane_sums(t):
        a = jnp.sum(sb_ref[t].T, axis=0, keepdims=True)
        o_ref[pl.ds(t, 1), :] = 0.5 * a * (1.0 + lax.erf(a * (2.0 ** -0.5))) * gate_ref[pl.ds(t, 1), :]

    sb_ref[0] = jnp.zeros(sb_ref.shape[1:], jnp.float32)

    def tok(t, carry):
        lane_sums(jnp.maximum(t - 1, 0))
        x8 = x_ref[pl.ds(t, 1), :].reshape(2 * HALF_ROWS, LANES)
        xlo, xhi = x8[0:HALF_ROWS], x8[HALF_ROWS:2 * HALF_ROWS]
        rows = idx_ref.at[t]
        for k in range(PEER_SLOTS):
            lo, hi = _table_row(tab_ref, rows, k)
            s_ref[HALF_ROWS * k:HALF_ROWS * (k + 1), :] = lo * xlo + hi * xhi
        s4 = s_ref[pl.ds(0, PEER_SLOTS, stride=HALF_ROWS), :]
        for r in range(1, HALF_ROWS):
            s4 = s4 + s_ref[pl.ds(r, PEER_SLOTS, stride=HALF_ROWS), :]
        sb_ref[t] = s4
        return carry
    lax.fori_loop(0, PEER_TB, tok, 0, unroll=2)
    lane_sums(PEER_TB - 1)


def _peer_v_body(idx_ref, coef_ref, tab_ref, o_ref, cb_ref):
    def spread(t):
        return jnp.broadcast_to(coef_ref[pl.ds(t, 1), :], (PEER_SLOTS, LANES)).T

    def tok(t, weights):
        nxt = spread(jnp.minimum(t + 1, PEER_TB - 1))
        cb_ref[...] = weights
        nacc = 2
        acc = [jnp.zeros((HALF_ROWS, LANES), jnp.float32) for _ in range(2 * nacc)]
        rows = idx_ref.at[t]
        for k in range(PEER_SLOTS):
            lo, hi = _table_row(tab_ref, rows, k)
            c = jnp.broadcast_to(cb_ref[k:k + 1, :], (HALF_ROWS, LANES))
            a = k % nacc
            acc[2 * a] = acc[2 * a] + c * lo
            acc[2 * a + 1] = acc[2 * a + 1] + c * hi
        half = HALF_ROWS * LANES
        o_ref[pl.ds(t, 1), 0:half] = (acc[0] + acc[2]).reshape(1, half)
        o_ref[pl.ds(t, 1), half:2 * half] = (acc[1] + acc[3]).reshape(1, half)
        return nxt
    lax.fori_loop(0, PEER_TB, tok, spread(0))


def _table_spec():
    return pl.BlockSpec((PEER_N_EXPERTS * HALF_ROWS, LANES), lambda i: (0, 0), pipeline_mode=pl.Buffered(1))


def peer_u(idx, x, gates, tab):
    n = idx.shape[0]
    tb = PEER_TB
    return pl.pallas_call(
        _peer_u_body, grid=(n // tb,),
        in_specs=[pl.BlockSpec((tb, PEER_SLOTS), lambda i: (i, 0), memory_space=pltpu.SMEM),
                  pl.BlockSpec((tb, x.shape[1]), lambda i: (i, 0)),
                  pl.BlockSpec((tb, PEER_SLOTS), lambda i: (i, 0)),
                  _table_spec()],
        out_specs=pl.BlockSpec((tb, PEER_SLOTS), lambda i: (i, 0)),
        out_shape=jax.ShapeDtypeStruct((n, PEER_SLOTS), jnp.float32),
        scratch_shapes=[pltpu.VMEM((HALF_ROWS * PEER_SLOTS, LANES), jnp.float32),
                        pltpu.VMEM((tb, PEER_SLOTS, LANES), jnp.float32)],
        compiler_params=pltpu.CompilerParams(dimension_semantics=("arbitrary",), vmem_limit_bytes=PEER_VMEM_LIMIT),
        name="peer_u")(idx, x, gates, tab)


def peer_v(idx, coef, tab):
    n = idx.shape[0]
    tb = PEER_TB
    return pl.pallas_call(
        _peer_v_body, grid=(n // tb,),
        in_specs=[pl.BlockSpec((tb, PEER_SLOTS), lambda i: (i, 0), memory_space=pltpu.SMEM),
                  pl.BlockSpec((tb, PEER_SLOTS), lambda i: (i, 0)),
                  _table_spec()],
        out_specs=pl.BlockSpec((tb, 2 * HALF_ROWS * LANES), lambda i: (i, 0)),
        out_shape=jax.ShapeDtypeStruct((n, 2 * HALF_ROWS * LANES), jnp.float32),
        scratch_shapes=[pltpu.VMEM((PEER_SLOTS, LANES), jnp.float32)],
        compiler_params=pltpu.CompilerParams(dimension_semantics=("arbitrary",), vmem_limit_bytes=PEER_VMEM_LIMIT),
        name="peer_v")(idx, coef, tab)


PEER_TT = 256
PAD_ID = 2**30


def _extract_top(ref, n_out, rid=None):
    rows, t = ref.shape
    if rid is None:
        rid = lax.broadcasted_iota(jnp.int32, (rows, t), 0)
    vals, idxs = [], []
    for _ in range(n_out):
        s = ref[...]
        m = jnp.max(s, axis=0, keepdims=True)
        ix = jnp.min(jnp.where(s == m, rid, jnp.int32(PAD_ID)), axis=0, keepdims=True)
        ref[...] = jnp.where(rid == ix, -jnp.inf, s)
        vals.append(m)
        idxs.append(ix)
    return vals, idxs


def _peer_route_body(x_ref, wq_ref, sk_ref, cid_ref, idx_ref, gate_ref, s_ref, cand_ref, v_ref, i_ref, et_ref, gt_ref):
    kk, nk = PEER_TOPK, PEER_N_KEYS
    f32 = jnp.float32
    q = jnp.dot(x_ref[...].astype(jnp.bfloat16), wq_ref[...], preferred_element_type=f32).astype(jnp.bfloat16)
    nt = (((1,), (1,)), ((), ()))
    row16 = lax.broadcasted_iota(jnp.int32, (kk, PEER_TT), 0)
    for h in range(PEER_HEADS):
        qh = q[:, h * PEER_KEY_DIM:(h + 1) * PEER_KEY_DIM]
        s_ref[...] = lax.dot_general(sk_ref[h], qh, nt, preferred_element_type=f32)
        for p in range(2):
            vals, idxs = _extract_top(s_ref.at[p * nk:(p + 1) * nk, :], kk)
            for i in range(kk):
                v_ref[p, i:i + 1, :] = vals[i]
                i_ref[p, i:i + 1, :] = idxs[i]
        off = 0
        for i in range(kk):
            n_j = kk // (i + 1)
            cand_ref[off:off + n_j, :] = v_ref[0, i:i + 1, :] + v_ref[1, 0:n_j, :]
            off += n_j
        cand_ref[off:, :] = jnp.full((cand_ref.shape[0] - off, PEER_TT), -jnp.inf, f32)
        vals, cis = _extract_top(cand_ref, kk, cid_ref[...])
        i1, i2 = i_ref[0], i_ref[1]
        es = [jnp.exp(v - vals[0]) for v in vals]
        den = es[0]
        for e in es[1:]:
            den = den + e
        for k in range(kk):
            ci = cis[k]
            e1 = jnp.sum(jnp.where(row16 == lax.shift_right_logical(ci, 4), i1, 0), axis=0, keepdims=True)
            e2 = jnp.sum(jnp.where(row16 == (ci & 15), i2, 0), axis=0, keepdims=True)
            et_ref[h * kk + k:h * kk + k + 1, :] = (e1 * nk + e2) * HALF_ROWS
            gt_ref[h * kk + k:h * kk + k + 1, :] = es[k] / den
    idx_ref[...] = lax.bitcast_convert_type(lax.bitcast_convert_type(et_ref[...], f32).T, jnp.int32)
    gate_ref[...] = gt_ref[...].T


def _pair_cells():
    kk = PEER_TOPK
    ids = [i * kk + j for i in range(kk) for j in range(kk // (i + 1))]
    return ids + [PAD_ID] * (-len(ids) % SUBLANES)


def peer_route(xn, wq, subkeys):
    n = xn.shape[0]
    tt = PEER_TT
    cells = _pair_cells()
    cid = jnp.broadcast_to(jnp.asarray(cells, jnp.int32)[:, None], (len(cells), tt))
    half = PEER_KEY_DIM // 2
    z = jnp.zeros((PEER_HEADS, PEER_N_KEYS, half), subkeys.dtype)
    skbd = jnp.concatenate([jnp.concatenate([subkeys[:, 0], z], axis=-1),
                            jnp.concatenate([z, subkeys[:, 1]], axis=-1)], axis=1).astype(jnp.bfloat16)
    return pl.pallas_call(
        _peer_route_body, grid=(n // tt,),
        in_specs=[pl.BlockSpec((tt, D_MODEL), lambda i: (i, 0)),
                  pl.BlockSpec((D_MODEL, PEER_HEADS * PEER_KEY_DIM), lambda i: (0, 0)),
                  pl.BlockSpec((PEER_HEADS, 2 * PEER_N_KEYS, PEER_KEY_DIM), lambda i: (0, 0, 0)),
                  pl.BlockSpec((len(cells), tt), lambda i: (0, 0))],
        out_specs=[pl.BlockSpec((tt, PEER_SLOTS), lambda i: (i, 0)), pl.BlockSpec((tt, PEER_SLOTS), lambda i: (i, 0))],
        out_shape=[jax.ShapeDtypeStruct((n, PEER_SLOTS), jnp.int32), jax.ShapeDtypeStruct((n, PEER_SLOTS), jnp.float32)],
        scratch_shapes=[pltpu.VMEM((2 * PEER_N_KEYS, tt), jnp.float32),
                        pltpu.VMEM((len(cells), tt), jnp.float32),
                        pltpu.VMEM((2, PEER_TOPK, tt), jnp.float32),
                        pltpu.VMEM((2, PEER_TOPK, tt), jnp.int32),
                        pltpu.VMEM((PEER_SLOTS, tt), jnp.int32),
                        pltpu.VMEM((PEER_SLOTS, tt), jnp.float32)],
        compiler_params=pltpu.CompilerParams(dimension_semantics=("arbitrary",), vmem_limit_bytes=ROUTE_VMEM_LIMIT),
        name="peer_route")(xn, wq.astype(jnp.bfloat16), skbd, cid)


def peer_channel_mixer(xn, wq, subkeys, u_tab, v_tab):
    bsz, seq, d = xn.shape
    n_tok = bsz * seq
    x2 = xn.reshape(n_tok, d)
    idx, gates = peer_route(x2, wq, subkeys)
    coef = peer_u(idx, x2, gates, pack_table(u_tab))
    out = peer_v(idx, coef, pack_table(v_tab))
    return out.reshape(bsz, seq, d)


def _residual_body(h_ref, y_ref, gt_ref, o_ref):
    o_ref[...] = h_ref[...] + gt_ref[0] * y_ref[...]


def _final_norm_body(h_ref, y_ref, gt_ref, g_ref, o_ref):
    h = h_ref[...] + gt_ref[0] * y_ref[...]
    o_ref[...] = h * lax.rsqrt(jnp.mean(h * h, axis=-1, keepdims=True) + EPS) * g_ref[...]


def gated_residual(h2, y2, gt, seq, gain=None):
    n, d = h2.shape
    tm = min(1024, seq)
    per = seq // tm
    row = pl.BlockSpec((tm, d), lambda i: (i, 0))
    specs = [row, row, pl.BlockSpec((1, 1, d), lambda i: (i // per, 0, 0))]
    args = [h2, y2, gt]
    if gain is not None:
        specs.append(pl.BlockSpec((1, d), lambda i: (0, 0)))
        args.append(gain.astype(jnp.float32).reshape(1, d))
    return pl.pallas_call(
        _residual_body if gain is None else _final_norm_body, grid=(n // tm,),
        in_specs=specs, out_specs=row, out_shape=jax.ShapeDtypeStruct((n, d), h2.dtype),
        name="gated_residual" if gain is None else "final_norm")(*args)


def kernel(x, c, positions, w_ada, b_ada, w_in, conv_w, a_log, dt_bias, norm_b_w,
           w_pa, w_pb, w_o, peer_wq, peer_subkeys, peer_u, peer_v, final_norm_w):
    bsz, seq, d = x.shape
    h = x.reshape(bsz * seq, d)
    for layer in range(DEPTH):
        mod = ada_modulation(c, w_ada[layer], b_ada[layer])
        sh1, sc1, gt1, sh2, sc2, gt2 = [m.reshape(bsz, 1, d) for m in jnp.split(mod, 6, axis=-1)]
        proj = in_proj(h, 1.0 + sc1, sh1, permute_w_in(w_in[layer]), seq)
        o_a, o_b = token_mixers(proj, positions, conv_w[layer], a_log[layer], dt_bias[layer], norm_b_w[layer],
                                bsz, seq)
        h, n2 = merge(o_a, o_b.reshape(bsz * seq, B_V_WIDTH), proj, h, gt1,
                      1.0 + sc2, sh2, w_pa[layer], w_pb[layer], w_o[layer], seq)
        y2 = peer_channel_mixer(n2.reshape(bsz, seq, d), peer_wq[layer], peer_subkeys[layer], peer_u[layer],
                                peer_v[layer])
        last = layer == DEPTH - 1
        h = gated_residual(h, y2.reshape(bsz * seq, d), gt2, seq, final_norm_w if last else None)
    return h.reshape(bsz, seq, d)
```

```python
import functools

import jax, jax.numpy as jnp
from jax import lax
from jax.experimental import pallas as pl
from jax.experimental.pallas import tpu as pltpu

D_MODEL = 1024
DEPTH = 1

LANES = 128
SUBLANES = 8
MIB = 1024 * 1024
DSA_VMEM_LIMIT = 48 * MIB
PREP_VMEM_LIMIT = 32 * MIB
GDN_VMEM_LIMIT = 40 * MIB
PROJ_VMEM_LIMIT = 48 * MIB
PEER_VMEM_LIMIT = 48 * MIB
ROUTE_VMEM_LIMIT = 32 * MIB

A_HEADS = 8
A_KV_HEADS = 2
A_HEAD_DIM = 64
IDX_HEADS = 16
IDX_DIM = 64
IDX_TOPK_MAX = 256
B_HEADS = 8
B_KEY_DIM = 64
B_VAL_DIM = 64
CONV_WIDTH = 4
CHUNK = 64
ROPE_THETA = 500000.0
ROPE_FRACTION_DEN = 4
PEER_HEADS = 8
PEER_KEY_DIM = 128
PEER_N_KEYS = 128
PEER_N_EXPERTS = PEER_N_KEYS * PEER_N_KEYS
PEER_TOPK = 16
EPS = 1e-6

A_WIDTH = A_HEADS * A_HEAD_DIM
KV_WIDTH = A_KV_HEADS * A_HEAD_DIM
B_QK_WIDTH = B_HEADS * B_KEY_DIM
B_V_WIDTH = B_HEADS * B_VAL_DIM
IN_SPLITS = (A_WIDTH, KV_WIDTH, KV_WIDTH, IDX_HEADS * IDX_DIM, IDX_DIM, IDX_HEADS,
             B_QK_WIDTH, B_QK_WIDTH, B_V_WIDTH, B_V_WIDTH, B_HEADS, B_HEADS, D_MODEL, D_MODEL)


DSA_TQ = 256
DSA_TK = 256
INT_MIN = -2**31
NEG_BIG = -1e30
PAD_ID = 2**30


def _dsa_body(topk, idx_bits, qit_ref, w_ref, qat_ref, ki_ref, ka_ref, vat_ref, o_ref, key_ref):
    tq, tk = DSA_TQ, DSA_TK
    qb = pl.program_id(1)
    n_kv = qb + 1
    t_glob = qb * tq + lax.broadcasted_iota(jnp.int32, (1, tq), 1)
    row = lax.broadcasted_iota(jnp.int32, (tk, 1), 0)
    f32 = jnp.float32

    def p1(j, carry):
        kt = ki_ref[0, j]
        score = jnp.zeros((tk, tq), f32)
        for h in range(IDX_HEADS):
            lt = jnp.dot(kt, qit_ref[0, h], preferred_element_type=f32)
            score = score + w_ref[0, h:h + 1, :] * jnp.maximum(lt, 0.0)
        bits = lax.bitcast_convert_type(score + 0.0, jnp.int32)
        skey = jnp.where(bits >= 0, bits, bits ^ jnp.int32(0x7FFFFFFF))
        skey = jnp.where(j * tk + row <= t_glob, skey, jnp.int32(INT_MIN))
        key_ref[j] = skey
        return carry
    lax.fori_loop(0, n_kv, p1, 0)

    key_ref[n_kv] = jnp.full((tk, tq), INT_MIN, jnp.int32)

    def count(pred):
        def body(jj, acc):
            for j in (2 * jj, 2 * jj + 1):
                hit = jnp.where(pred(key_ref[j], j * tk + row), 1.0, 0.0)
                acc = acc + jnp.sum(hit.reshape(tk // SUBLANES, SUBLANES, tq), axis=0)
            return acc
        acc = lax.fori_loop(0, (n_kv + 1) // 2, body, jnp.zeros((SUBLANES, tq), f32))
        return jnp.sum(acc, axis=0, keepdims=True)

    kf = jnp.float32(topk)

    def bit_step(i, ku):
        cand_u = ku | lax.shift_left(jnp.int32(1), 31 - i)
        cand = cand_u ^ jnp.int32(INT_MIN)
        c = count(lambda k, s: k >= cand)
        return jnp.where(c >= kf, cand_u, ku)
    ku = lax.fori_loop(0, 32, bit_step, jnp.zeros((1, tq), jnp.int32))
    kth = ku ^ jnp.int32(INT_MIN)
    c_gt = count(lambda k, s: k > kth)
    c_ge = count(lambda k, s: k >= kth)
    short = kth == jnp.int32(INT_MIN)
    x0 = jnp.where(short, jnp.int32(-1), jnp.int32(PAD_ID))
    need = kf - c_gt
    has_tie = jnp.max(jnp.where(jnp.logical_and(c_ge > kf, jnp.logical_not(short)), 1.0, 0.0)) > 0.0

    def tie_search():
        def step(i, x):
            bit = lax.shift_left(jnp.int32(1), idx_bits - 1 - i)
            probe = x + bit - 1
            c = count(lambda k, s: jnp.logical_and(k == kth, s <= probe))
            return jnp.where(c < need, x + bit, x)
        x = lax.fori_loop(0, idx_bits, step, jnp.zeros((1, tq), jnp.int32))
        return jnp.where(short, jnp.int32(-1), x)
    x_lim = lax.cond(has_tie, tie_search, lambda: x0)

    rep = A_HEADS // A_KV_HEADS
    hs = range(A_HEADS)

    def p3(j, carry):
        m, l, acc = carry
        skey = key_ref[j]
        s_idx = j * tk + row
        sel = jnp.logical_or(skey > kth, jnp.logical_and(skey == kth, s_idx <= x_lim))
        kt = [ka_ref[0, g, j] for g in range(A_KV_HEADS)]
        vt = [vat_ref[0, g, j] for g in range(A_KV_HEADS)]
        s = [jnp.where(sel, jnp.dot(kt[h // rep], qat_ref[0, h], preferred_element_type=f32), NEG_BIG) for h in hs]
        m_new = [jnp.maximum(m[h], jnp.max(s[h], axis=0, keepdims=True)) for h in hs]
        alpha = [jnp.exp(m[h] - m_new[h]) for h in hs]
        p = [jnp.exp(s[h] - m_new[h]) for h in hs]
        l_new = [alpha[h] * l[h] + jnp.sum(p[h], axis=0, keepdims=True) for h in hs]
        acc_new = [alpha[h] * acc[h] + jnp.dot(vt[h // rep], p[h].astype(jnp.bfloat16), preferred_element_type=f32)
                   for h in hs]
        return tuple(m_new), tuple(l_new), tuple(acc_new)

    init = (tuple(jnp.full((1, tq), NEG_BIG, f32) for _ in hs), tuple(jnp.zeros((1, tq), f32) for _ in hs),
            tuple(jnp.zeros((A_HEAD_DIM, tq), f32) for _ in hs))
    _, l_fin, acc_fin = lax.fori_loop(0, n_kv, p3, init)
    for h in hs:
        o_ref[0, h] = acc_fin[h] / l_fin[h]


ROPE_HALF = A_HEAD_DIM // ROPE_FRACTION_DEN // 2


def _rope_t(xt, n_heads, cos, sin, scale=None):
    outs = []
    for h in range(n_heads):
        b = h * A_HEAD_DIM
        x1, x2, rest = xt[b:b + ROPE_HALF], xt[b + ROPE_HALF:b + 2 * ROPE_HALF], xt[b + 2 * ROPE_HALF:b + A_HEAD_DIM]
        o = jnp.concatenate([x1 * cos - x2 * sin, x2 * cos + x1 * sin, rest], axis=0)
        outs.append(o if scale is None else o * scale)
    return outs


def _dsa_prep_body(qi_ref, qa_ref, ka_ref, va_ref, sm_ref, cos_ref, sin_ref, dp_ref,
                   qit_ref, qat_ref, w_ref, kit_ref, kat_ref, vat_ref, bg_ref):
    bf = jnp.bfloat16
    cos, sin = cos_ref[0], sin_ref[0]
    for h, o in enumerate(_rope_t(qi_ref[...].T, IDX_HEADS, cos, sin)):
        qit_ref[0, h] = o.astype(bf)
    for h, o in enumerate(_rope_t(qa_ref[...].T, A_HEADS, cos, sin, A_HEAD_DIM ** -0.5)):
        qat_ref[0, h] = o.astype(bf)
    kat = _rope_t(ka_ref[...].T, A_KV_HEADS, cos, sin)
    vt = va_ref[...].T
    for g in range(A_KV_HEADS):
        kat_ref[0, g, 0] = kat[g].T.astype(bf)
        vat_ref[0, g, 0] = vt[g * A_HEAD_DIM:(g + 1) * A_HEAD_DIM].astype(bf)
    smt = sm_ref[...].T
    kit_ref[0, 0] = _rope_t(smt[0:IDX_DIM], 1, cos, sin)[0].T.astype(bf)
    w_ref[0] = smt[IDX_DIM:IDX_DIM + IDX_HEADS] * ((IDX_HEADS ** -0.5) * (IDX_DIM ** -0.5))
    b0 = IDX_DIM + IDX_HEADS
    bb, ab = smt[b0:b0 + B_HEADS], smt[b0 + B_HEADS:b0 + 2 * B_HEADS]
    a_log, dt_bias = dp_ref[:, 0:1], dp_ref[:, 1:2]
    bg_ref[0] = jnp.concatenate([_sigmoid(bb), -jnp.exp(a_log) * jax.nn.softplus(ab + dt_bias)], axis=0)


def dsa_prep(proj, positions, a_log, dt_bias, bsz, seq):
    t = DSA_TQ
    per = seq // t
    assert DSA_TK == t and PROJ_ORDER[-4:] == ("ki", "wi", "bb", "ab") and PROJ_DST["ki"][0] + LANES == PROJ_WIDTH
    rd = A_HEAD_DIM // ROPE_FRACTION_DEN
    inv_freq = jnp.power(jnp.float32(ROPE_THETA), -jnp.arange(ROPE_HALF, dtype=jnp.float32) * (2.0 / rd))
    ang = positions.astype(jnp.float32)[:, None, :] * inv_freq[None, :, None]

    def col(name, width=None):
        off, w = PROJ_DST[name]
        w = width or w
        return pl.BlockSpec((t, w), lambda i: (i, off // w))
    trig = pl.BlockSpec((1, ROPE_HALF, t), lambda i: (i // per, 0, i % per))
    bf, f32 = jnp.bfloat16, jnp.float32
    sd = jax.ShapeDtypeStruct
    n_kv = seq // DSA_TK
    return pl.pallas_call(
        _dsa_prep_body, grid=(bsz * per,),
        in_specs=[col("qi"), col("qa"), col("ka"), col("va"), col("ki", LANES), trig, trig,
                  pl.BlockSpec((B_HEADS, 2), lambda i: (0, 0))],
        out_specs=[pl.BlockSpec((1, IDX_HEADS, IDX_DIM, t), lambda i: (i // per, 0, 0, i % per)),
                   pl.BlockSpec((1, A_HEADS, A_HEAD_DIM, t), lambda i: (i // per, 0, 0, i % per)),
                   pl.BlockSpec((1, IDX_HEADS, t), lambda i: (i // per, 0, i % per)),
                   pl.BlockSpec((1, 1, t, IDX_DIM), lambda i: (i // per, i % per, 0, 0)),
                   pl.BlockSpec((1, A_KV_HEADS, 1, t, A_HEAD_DIM), lambda i: (i // per, 0, i % per, 0, 0)),
                   pl.BlockSpec((1, A_KV_HEADS, 1, A_HEAD_DIM, t), lambda i: (i // per, 0, i % per, 0, 0)),
                   pl.BlockSpec((1, 2 * B_HEADS, t), lambda i: (i // per, 0, i % per))],
        out_shape=[sd((bsz, IDX_HEADS, IDX_DIM, seq), bf), sd((bsz, A_HEADS, A_HEAD_DIM, seq), bf),
                   sd((bsz, IDX_HEADS, seq), f32), sd((bsz, n_kv, DSA_TK, IDX_DIM), bf),
                   sd((bsz, A_KV_HEADS, n_kv, DSA_TK, A_HEAD_DIM), bf),
                   sd((bsz, A_KV_HEADS, n_kv, A_HEAD_DIM, DSA_TK), bf), sd((bsz, 2 * B_HEADS, seq), f32)],
        compiler_params=pltpu.CompilerParams(dimension_semantics=("arbitrary",), vmem_limit_bytes=PREP_VMEM_LIMIT),
        name="dsa_prep")(proj, proj, proj, proj, proj, jnp.cos(ang), jnp.sin(ang),
                         jnp.stack([a_log, dt_bias], axis=-1).astype(f32))


def dsa_attention_pallas(qit, w, qat, kit, kat, vat):
    bsz, seq = qit.shape[0], qit.shape[-1]
    tq, tk = DSA_TQ, DSA_TK
    topk = min(IDX_TOPK_MAX, seq // 4)
    n_kv = seq // tk
    return pl.pallas_call(
        functools.partial(_dsa_body, topk, (seq - 1).bit_length()),
        grid=(bsz, seq // tq),
        in_specs=[
            pl.BlockSpec((1, IDX_HEADS, IDX_DIM, tq), lambda b, q: (b, 0, 0, q)),
            pl.BlockSpec((1, IDX_HEADS, tq), lambda b, q: (b, 0, q)),
            pl.BlockSpec((1, A_HEADS, A_HEAD_DIM, tq), lambda b, q: (b, 0, 0, q)),
            pl.BlockSpec((1, n_kv, tk, IDX_DIM), lambda b, q: (b, 0, 0, 0)),
            pl.BlockSpec((1, A_KV_HEADS, n_kv, tk, A_HEAD_DIM), lambda b, q: (b, 0, 0, 0, 0)),
            pl.BlockSpec((1, A_KV_HEADS, n_kv, A_HEAD_DIM, tk), lambda b, q: (b, 0, 0, 0, 0)),
        ],
        out_specs=pl.BlockSpec((1, A_HEADS, A_HEAD_DIM, tq), lambda b, q: (b, 0, 0, q)),
        out_shape=jax.ShapeDtypeStruct((bsz, A_HEADS, A_HEAD_DIM, seq), jnp.float32),
        scratch_shapes=[
            pltpu.VMEM((n_kv + 1, tk, tq), jnp.int32),
        ],
        compiler_params=pltpu.CompilerParams(dimension_semantics=("arbitrary", "arbitrary"),
                                             vmem_limit_bytes=DSA_VMEM_LIMIT),
        name="dsa_attention",
    )(qit, w, qat, kit, kat, vat)


GDN_G = 8
GDN_BASE = 8
GDN_PAD = 8


def _gdn_body(q_ref, k_ref, v_ref, cw_ref, gc_ref, bt_ref, nw_ref, o_ref, s_ref, xs_ref, cs_ref):
    c_sz = CHUNK
    f32, bf = jnp.float32, jnp.bfloat16

    rows = GDN_G * c_sz
    halo = CONV_WIDTH - 1

    @pl.when(pl.program_id(1) == 0)
    def _():
        s_ref[...] = jnp.zeros(s_ref.shape, f32)
        xs_ref[:, 0:GDN_PAD, :] = jnp.zeros((3, GDN_PAD, xs_ref.shape[-1]), f32)

    for j, ref in enumerate((q_ref, k_ref, v_ref)):
        xs_ref[j, GDN_PAD:GDN_PAD + rows, :] = ref[0]
        acc = xs_ref[j, GDN_PAD - halo:GDN_PAD - halo + rows, :] * cw_ref[j, 0:1, :]
        for i in range(1, CONV_WIDTH):
            acc = acc + xs_ref[j, GDN_PAD - halo + i:GDN_PAD - halo + i + rows, :] * cw_ref[j, i:i + 1, :]
        cs_ref[j] = acc * _sigmoid(acc)
        xs_ref[j, GDN_PAD - halo:GDN_PAD, :] = xs_ref[j, GDN_PAD + rows - halo:GDN_PAD + rows, :]

    ri = lax.broadcasted_iota(jnp.int32, (c_sz, c_sz), 0)
    ci = lax.broadcasted_iota(jnp.int32, (c_sz, c_sz), 1)
    incl, strict = ri >= ci, ri > ci
    eye = jnp.where(ri == ci, 1.0, 0.0).astype(f32)
    blk = lambda w: (ri // w) == (ci // w)
    diag8 = blk(GDN_BASE)
    sub_blocks = []
    w = GDN_BASE
    while w < c_sz:
        sub_blocks.append(jnp.logical_and(blk(2 * w), jnp.logical_not(blk(w))))
        w *= 2
    nt = (((1,), (1,)), ((), ()))
    dot = lambda a, b: jnp.dot(a.astype(bf), b.astype(bf), preferred_element_type=f32)
    dot_nt = lambda a, b: lax.dot_general(a.astype(bf), b.astype(bf), nt, preferred_element_type=f32)

    def split(a):
        hi = a.astype(bf)
        return hi, (a - hi.astype(f32)).astype(bf)

    def hp(a, b):
        a_hi, a_lo = split(a)
        b_hi, b_lo = split(b)
        mm = lambda u, w: jnp.dot(u, w, preferred_element_type=f32)
        return mm(a_hi, b_hi) + (mm(a_hi, b_lo) + mm(a_lo, b_hi))

    hs = range(B_HEADS)

    def chunk_pair(i, carry):
        cs = [2 * i, 2 * i + 1]
        r0 = [pl.multiple_of(c * c_sz, c_sz) for c in cs]
        items = [(j, h) for j in range(2) for h in hs]
        hd = lambda a, j, h: cs_ref[a, pl.ds(r0[j], c_sz), h * B_KEY_DIM:(h + 1) * B_KEY_DIM]
        l2n = lambda t: t * lax.rsqrt(jnp.sum(t * t, axis=-1, keepdims=True) + EPS)
        q = [l2n(hd(0, j, h)) * (B_KEY_DIM ** -0.5) for j, h in items]
        k = [l2n(hd(1, j, h)) for j, h in items]
        v = [hd(2, j, h) for j, h in items]
        gc8 = [gc_ref[0, c] for c in cs]
        gct = [g8.T for g8 in gc8]
        btt = [bt_ref[0, c].T for c in cs]
        gcr = [gc8[j][h:h + 1, :] for j, h in items]
        n_it = range(len(items))
        gcc = [gct[j][:, h:h + 1] for j, h in items]
        beta = [btt[j][:, h:h + 1] for j, h in items]
        decay = [jnp.exp(jnp.where(incl, gcc[n] - gcr[n], -jnp.inf)) for n in n_it]
        kb = [k[n] * beta[n] for n in n_it]
        vb = [v[n] * beta[n] for n in n_it]
        low = [jnp.where(strict, dot_nt(kb[n], k[n]) * decay[n], 0.0) for n in n_it]
        dg = [jnp.where(diag8, low[n], 0.0) for n in n_it]
        t = [eye - dg[n] for n in n_it]
        p = [hp(dg[n], dg[n]) for n in n_it]
        t = [hp(t[n], eye + p[n]) for n in n_it]
        p = [hp(p[n], p[n]) for n in n_it]
        t = [hp(t[n], eye + p[n]) for n in n_it]
        for below in sub_blocks:
            lb = [jnp.where(below, low[n], 0.0) for n in n_it]
            lt = [hp(lb[n], t[n]) for n in n_it]
            t = [t[n] - hp(t[n], lt[n]) for n in n_it]
        u = [dot(t[n], vb[n]) for n in n_it]
        kcd = [dot(t[n], kb[n] * jnp.exp(gcc[n])) for n in n_it]
        intra = [dot_nt(q[n], k[n]) * decay[n] for n in n_it]
        qg = [q[n] * jnp.exp(gcc[n]) for n in n_it]
        glast = [g[:, c_sz - 1:c_sz] for g in gcr]
        kdt = [(k[n] * jnp.exp(glast[n] - gcc[n])).T for n in n_it]
        s = [s_ref[h] for h in hs]
        for j in range(2):
            ix = [j * B_HEADS + h for h in hs]
            v_new = [u[ix[h]] - dot(kcd[ix[h]], s[h]) for h in hs]
            out = [dot(qg[ix[h]], s[h]) + dot(intra[ix[h]], v_new[h]) for h in hs]
            s = [s[h] * jnp.exp(glast[ix[h]]) + dot(kdt[ix[h]], v_new[h]) for h in hs]
            for h in hs:
                o = out[h]
                o = o * lax.rsqrt(jnp.mean(o * o, axis=-1, keepdims=True) + EPS) * nw_ref[...]
                o_ref[0, pl.ds(r0[j], c_sz), h * B_VAL_DIM:(h + 1) * B_VAL_DIM] = o
        for h in hs:
            s_ref[h] = s[h]
        return carry
    lax.fori_loop(0, GDN_G // 2, chunk_pair, 0)


def gated_delta_rule_pallas(proj, conv_w, g, beta, norm_w, bsz, seq):
    nh, dk, dv = B_HEADS, B_KEY_DIM, B_VAL_DIM
    n = seq // CHUNK
    chunked = lambda t: jnp.moveaxis(t.reshape(bsz, nh, n, CHUNK), 1, 2)
    gc = chunked(jnp.cumsum(g.reshape(bsz, nh, n, CHUNK), axis=-1))
    side = pl.BlockSpec((1, GDN_G, nh, CHUNK), lambda b, c: (b, c, 0, 0))
    rows = GDN_G * CHUNK
    width = nh * dk

    def col(name):
        off, w = PROJ_DST[name]
        assert w == width
        return pl.BlockSpec((1, rows, w), lambda b, c: (b, c, off // w))
    cw = jnp.transpose(conv_w.astype(jnp.float32).reshape(CONV_WIDTH, 3, width), (1, 0, 2))
    proj3 = proj.reshape(bsz, seq, proj.shape[-1])
    return pl.pallas_call(
        _gdn_body, grid=(bsz, n // GDN_G),
        in_specs=[col("qb"), col("kb"), col("vb"),
                  pl.BlockSpec((3, CONV_WIDTH, width), lambda b, c: (0, 0, 0)),
                  side, side, pl.BlockSpec((1, dv), lambda b, c: (0, 0))],
        out_specs=pl.BlockSpec((1, rows, nh * dv), lambda b, c: (b, c, 0)),
        out_shape=jax.ShapeDtypeStruct((bsz, seq, nh * dv), jnp.float32),
        scratch_shapes=[pltpu.VMEM((nh, dk, dv), jnp.float32),
                        pltpu.VMEM((3, GDN_PAD + rows, width), jnp.float32),
                        pltpu.VMEM((3, rows, width), jnp.float32)],
        compiler_params=pltpu.CompilerParams(dimension_semantics=("arbitrary", "arbitrary"),
                                             vmem_limit_bytes=GDN_VMEM_LIMIT),
        name="gated_delta_rule")(proj3, proj3, proj3, cw, gc, chunked(beta), norm_w.astype(jnp.float32).reshape(1, dv))


IN_NAMES = ("qa", "ka", "va", "qi", "ki", "wi", "qb", "kb", "vb", "zb", "bb", "ab", "gate_a", "gate_b")
PROJ_ORDER = ("gate_a", "gate_b", "qi", "qa", "qb", "kb", "vb", "zb", "ka", "va", "ki", "wi", "bb", "ab")


def _proj_layout():
    src, off = {}, 0
    for name, w in zip(IN_NAMES, IN_SPLITS):
        src[name] = (off, w)
        off += w
    dst, off = {}, 0
    for name in PROJ_ORDER:
        dst[name] = (off, src[name][1])
        off += src[name][1]
    return src, dst, -(-off // LANES) * LANES


PROJ_SRC, PROJ_DST, PROJ_WIDTH = _proj_layout()
PROJ_TM = 512


def permute_w_in(w_in):
    cols = [w_in[:, PROJ_SRC[n][0]:PROJ_SRC[n][0] + PROJ_SRC[n][1]] for n in PROJ_ORDER]
    cols.append(jnp.zeros((w_in.shape[0], PROJ_WIDTH - sum(c.shape[1] for c in cols)), w_in.dtype))
    return jnp.concatenate(cols, axis=1).astype(jnp.bfloat16)


def _modulated_norm(x, scale, shift):
    return x * lax.rsqrt(jnp.mean(x * x, axis=-1, keepdims=True) + EPS) * scale + shift


def _sigmoid(v):
    return 1.0 / (1.0 + jnp.exp(-v))


def _ada_body(c_ref, w_ref, b_ref, o_ref):
    c = c_ref[...]
    o_ref[...] = jnp.dot((c * _sigmoid(c)).astype(jnp.bfloat16), w_ref[...].astype(jnp.bfloat16),
                         preferred_element_type=jnp.float32) + b_ref[...]


def ada_modulation(c, w_ada, b_ada):
    bsz, d = c.shape
    n = w_ada.shape[1]
    tn = D_MODEL
    return pl.pallas_call(
        _ada_body, grid=(n // tn,),
        in_specs=[pl.BlockSpec((bsz, d), lambda j: (0, 0)), pl.BlockSpec((d, tn), lambda j: (0, j)),
                  pl.BlockSpec((1, tn), lambda j: (0, j))],
        out_specs=pl.BlockSpec((bsz, tn), lambda j: (0, j)),
        out_shape=jax.ShapeDtypeStruct((bsz, n), jnp.float32),
        name="ada_modulation")(c, w_ada, b_ada.reshape(1, n))


def _in_proj_body(x_ref, sc_ref, sh_ref, w_ref, o_ref):
    n1 = _modulated_norm(x_ref[...], sc_ref[0], sh_ref[0])
    o_ref[...] = jnp.dot(n1.astype(jnp.bfloat16), w_ref[...], preferred_element_type=jnp.float32)


def in_proj(x2, scale, shift, w_perm, seq):
    n, d = x2.shape
    tm = PROJ_TM
    per = seq // tm
    vec = pl.BlockSpec((1, 1, d), lambda i: (i // per, 0, 0))
    return pl.pallas_call(
        _in_proj_body, grid=(n // tm,),
        in_specs=[pl.BlockSpec((tm, d), lambda i: (i, 0)), vec, vec,
                  pl.BlockSpec((d, PROJ_WIDTH), lambda i: (0, 0), pipeline_mode=pl.Buffered(1))],
        out_specs=pl.BlockSpec((tm, PROJ_WIDTH), lambda i: (i, 0)),
        out_shape=jax.ShapeDtypeStruct((n, PROJ_WIDTH), jnp.float32),
        compiler_params=pltpu.CompilerParams(dimension_semantics=("arbitrary",), vmem_limit_bytes=PROJ_VMEM_LIMIT),
        name="in_proj")(x2, scale, shift, w_perm)


def _merge_body(oa_ref, ob_ref, z_ref, ga_ref, gb_ref, x_ref, gt_ref, sc_ref, sh_ref, wpa_ref, wpb_ref, wo_ref,
                h_ref, n2_ref):
    f32, bf = jnp.float32, jnp.bfloat16
    z = z_ref[...]
    ob = ob_ref[...] * (z * _sigmoid(z))
    oa = oa_ref[0].reshape(A_WIDTH, oa_ref.shape[-1]).T
    ya = jnp.dot(oa.astype(bf), wpa_ref[...], preferred_element_type=f32)
    yb = jnp.dot(ob.astype(bf), wpb_ref[...], preferred_element_type=f32)
    merged = _sigmoid(ga_ref[...]) * ya + _sigmoid(gb_ref[...]) * yb
    y1 = jnp.dot(merged.astype(bf), wo_ref[...], preferred_element_type=f32)
    h = x_ref[...] + gt_ref[0] * y1
    h_ref[...] = h
    n2_ref[...] = _modulated_norm(h, sc_ref[0], sh_ref[0])


def merge(o_a, o_b, proj, x2, gt1, scale2, shift2, w_pa, w_pb, w_o, seq):
    n, d = x2.shape
    tm = PROJ_TM
    per = seq // tm
    bf = jnp.bfloat16
    vec = pl.BlockSpec((1, 1, d), lambda i: (i // per, 0, 0))

    def col(name):
        off, w = PROJ_DST[name]
        return pl.BlockSpec((tm, w), lambda i: (i, off // w))
    row = lambda w: pl.BlockSpec((tm, w), lambda i: (i, 0))
    res = lambda a, b: pl.BlockSpec((a, b), lambda i: (0, 0))
    return pl.pallas_call(
        _merge_body, grid=(n // tm,),
        in_specs=[pl.BlockSpec((1, A_HEADS, A_HEAD_DIM, tm), lambda i: (i // per, 0, 0, i % per)), row(B_V_WIDTH), col("zb"), col("gate_a"), col("gate_b"), row(d), vec, vec, vec,
                  res(A_WIDTH, d), res(B_V_WIDTH, d), res(d, d)],
        out_specs=[row(d), row(d)],
        out_shape=[jax.ShapeDtypeStruct((n, d), jnp.float32), jax.ShapeDtypeStruct((n, d), jnp.float32)],
        compiler_params=pltpu.CompilerParams(dimension_semantics=("arbitrary",), vmem_limit_bytes=PROJ_VMEM_LIMIT),
        name="merge")(o_a, o_b, proj, proj, proj, x2, gt1, scale2, shift2,
                      w_pa.astype(bf), w_pb.astype(bf), w_o.astype(bf))


def token_mixers(proj, positions, conv_w, a_log, dt_bias, norm_b_w, bsz, seq):
    qit, qat, w, kit, kat, vat, bg = dsa_prep(proj, positions, a_log, dt_bias, bsz, seq)
    o_a = dsa_attention_pallas(qit, w, qat, kit, kat, vat)

    o_b = gated_delta_rule_pallas(proj, conv_w, bg[:, B_HEADS:], bg[:, :B_HEADS], norm_b_w, bsz, seq)
    return o_a, o_b


PEER_SLOTS = PEER_HEADS * PEER_TOPK
PEER_TB = 128
HALF_ROWS = 4
HI_MASK = -65536


def pack_table(tab):
    bits = lax.bitcast_convert_type(tab.astype(jnp.bfloat16), jnp.uint16).astype(jnp.uint32)
    half = tab.shape[1] // 2
    word = bits[:, :half] | (bits[:, half:] << 16)
    return lax.bitcast_convert_type(word, jnp.int32).reshape(tab.shape[0] * HALF_ROWS, LANES)


def _table_row(tab_ref, rows, k):
    return _unpack(tab_ref[pl.ds(pl.multiple_of(rows[k], HALF_ROWS), HALF_ROWS), :])


def _unpack(w):
    lo = lax.bitcast_convert_type(lax.shift_left(w, 16), jnp.float32)
    hi = lax.bitcast_convert_type(w & jnp.int32(HI_MASK), jnp.float32)
    return lo, hi


def _peer_u_body(idx_ref, x_ref, gate_ref, tab_ref, o_ref, s_ref, sb_ref):
    def lane_sums(t):
        a = jnp.sum(sb_ref[t].T, axis=0, keepdims=True)
        o_ref[pl.ds(t, 1), :] = 0.5 * a * (1.0 + lax.erf(a * (2.0 ** -0.5))) * gate_ref[pl.ds(t, 1), :]

    sb_ref[0] = jnp.zeros(sb_ref.shape[1:], jnp.float32)

    def tok(t, carry):
        lane_sums(jnp.maximum(t - 1, 0))
        x8 = x_ref[pl.ds(t, 1), :].reshape(2 * HALF_ROWS, LANES)
        xlo, xhi = x8[0:HALF_ROWS], x8[HALF_ROWS:2 * HALF_ROWS]
        rows = idx_ref.at[t]
        for k in range(PEER_SLOTS):
            lo, hi = _table_row(tab_ref, rows, k)
            s_ref[HALF_ROWS * k:HALF_ROWS * (k + 1), :] = lo * xlo + hi * xhi
        s4 = s_ref[pl.ds(0, PEER_SLOTS, stride=HALF_ROWS), :]
        for r in range(1, HALF_ROWS):
            s4 = s4 + s_ref[pl.ds(r, PEER_SLOTS, stride=HALF_ROWS), :]
        sb_ref[t] = s4
        return carry
    lax.fori_loop(0, PEER_TB, tok, 0, unroll=2)
    lane_sums(PEER_TB - 1)


def _peer_v_body(idx_ref, coef_ref, tab_ref, o_ref, cb_ref):
    def spread(t):
        return jnp.broadcast_to(coef_ref[pl.ds(t, 1), :], (PEER_SLOTS, LANES)).T

    def tok(t, weights):
        nxt = spread(jnp.minimum(t + 1, PEER_TB - 1))
        cb_ref[...] = weights
        nacc = 2
        acc = [jnp.zeros((HALF_ROWS, LANES), jnp.float32) for _ in range(2 * nacc)]
        rows = idx_ref.at[t]
        for k in range(PEER_SLOTS):
            lo, hi = _table_row(tab_ref, rows, k)
            c = jnp.broadcast_to(cb_ref[k:k + 1, :], (HALF_ROWS, LANES))
            a = k % nacc
            acc[2 * a] = acc[2 * a] + c * lo
            acc[2 * a + 1] = acc[2 * a + 1] + c * hi
        half = HALF_ROWS * LANES
        o_ref[pl.ds(t, 1), 0:half] = (acc[0] + acc[2]).reshape(1, half)
        o_ref[pl.ds(t, 1), half:2 * half] = (acc[1] + acc[3]).reshape(1, half)
        return nxt
    lax.fori_loop(0, PEER_TB, tok, spread(0))


def _table_spec():
    return pl.BlockSpec((PEER_N_EXPERTS * HALF_ROWS, LANES), lambda i: (0, 0), pipeline_mode=pl.Buffered(1))


def peer_u(idx, x, gates, tab):
    n = idx.shape[0]
    tb = PEER_TB
    return pl.pallas_call(
        _peer_u_body, grid=(n // tb,),
        in_specs=[pl.BlockSpec((tb, PEER_SLOTS), lambda i: (i, 0), memory_space=pltpu.SMEM),
                  pl.BlockSpec((tb, x.shape[1]), lambda i: (i, 0)),
                  pl.BlockSpec((tb, PEER_SLOTS), lambda i: (i, 0)),
                  _table_spec()],
        out_specs=pl.BlockSpec((tb, PEER_SLOTS), lambda i: (i, 0)),
        out_shape=jax.ShapeDtypeStruct((n, PEER_SLOTS), jnp.float32),
        scratch_shapes=[pltpu.VMEM((HALF_ROWS * PEER_SLOTS, LANES), jnp.float32),
                        pltpu.VMEM((tb, PEER_SLOTS, LANES), jnp.float32)],
        compiler_params=pltpu.CompilerParams(dimension_semantics=("arbitrary",), vmem_limit_bytes=PEER_VMEM_LIMIT),
        name="peer_u")(idx, x, gates, tab)


def peer_v(idx, coef, tab):
    n = idx.shape[0]
    tb = PEER_TB
    return pl.pallas_call(
        _peer_v_body, grid=(n // tb,),
        in_specs=[pl.BlockSpec((tb, PEER_SLOTS), lambda i: (i, 0), memory_space=pltpu.SMEM),
                  pl.BlockSpec((tb, PEER_SLOTS), lambda i: (i, 0)),
                  _table_spec()],
        out_specs=pl.BlockSpec((tb, 2 * HALF_ROWS * LANES), lambda i: (i, 0)),
        out_shape=jax.ShapeDtypeStruct((n, 2 * HALF_ROWS * LANES), jnp.float32),
        scratch_shapes=[pltpu.VMEM((PEER_SLOTS, LANES), jnp.float32)],
        compiler_params=pltpu.CompilerParams(dimension_semantics=("arbitrary",), vmem_limit_bytes=PEER_VMEM_LIMIT),
        name="peer_v")(idx, coef, tab)


PEER_TT = 256


def _extract_top(ref, n_out, rid=None):
    rows, t = ref.shape
    if rid is None:
        rid = lax.broadcasted_iota(jnp.int32, (rows, t), 0)
    vals, idxs = [], []
    for _ in range(n_out):
        s = ref[...]
        m = jnp.max(s, axis=0, keepdims=True)
        ix = jnp.min(jnp.where(s == m, rid, jnp.int32(PAD_ID)), axis=0, keepdims=True)
        ref[...] = jnp.where(rid == ix, -jnp.inf, s)
        vals.append(m)
        idxs.append(ix)
    return vals, idxs


def _peer_route_body(x_ref, wq_ref, sk_ref, cid_ref, idx_ref, gate_ref, s_ref, cand_ref, v_ref, i_ref, et_ref, gt_ref):
    kk, nk = PEER_TOPK, PEER_N_KEYS
    f32 = jnp.float32
    q = jnp.dot(x_ref[...].astype(jnp.bfloat16), wq_ref[...], preferred_element_type=f32).astype(jnp.bfloat16)
    nt = (((1,), (1,)), ((), ()))
    row16 = lax.broadcasted_iota(jnp.int32, (kk, PEER_TT), 0)
    for h in range(PEER_HEADS):
        qh = q[:, h * PEER_KEY_DIM:(h + 1) * PEER_KEY_DIM]
        s_ref[...] = lax.dot_general(sk_ref[h], qh, nt, preferred_element_type=f32)
        for p in range(2):
            vals, idxs = _extract_top(s_ref.at[p * nk:(p + 1) * nk, :], kk)
            for i in range(kk):
                v_ref[p, i:i + 1, :] = vals[i]
                i_ref[p, i:i + 1, :] = idxs[i]
        off = 0
        for i in range(kk):
            n_j = kk // (i + 1)
            cand_ref[off:off + n_j, :] = v_ref[0, i:i + 1, :] + v_ref[1, 0:n_j, :]
            off += n_j
        cand_ref[off:, :] = jnp.full((cand_ref.shape[0] - off, PEER_TT), -jnp.inf, f32)
        vals, cis = _extract_top(cand_ref, kk, cid_ref[...])
        i1, i2 = i_ref[0], i_ref[1]
        es = [jnp.exp(v - vals[0]) for v in vals]
        den = es[0]
        for e in es[1:]:
            den = den + e
        for k in range(kk):
            ci = cis[k]
            e1 = jnp.sum(jnp.where(row16 == lax.shift_right_logical(ci, kk.bit_length() - 1), i1, 0), axis=0, keepdims=True)
            e2 = jnp.sum(jnp.where(row16 == (ci & (kk - 1)), i2, 0), axis=0, keepdims=True)
            et_ref[h * kk + k:h * kk + k + 1, :] = (e1 * nk + e2) * HALF_ROWS
            gt_ref[h * kk + k:h * kk + k + 1, :] = es[k] / den
    idx_ref[...] = lax.bitcast_convert_type(lax.bitcast_convert_type(et_ref[...], f32).T, jnp.int32)
    gate_ref[...] = gt_ref[...].T


def _pair_cells():
    kk = PEER_TOPK
    ids = [i * kk + j for i in range(kk) for j in range(kk // (i + 1))]
    return ids + [PAD_ID] * (-len(ids) % SUBLANES)


def peer_route(xn, wq, subkeys):
    n = xn.shape[0]
    tt = PEER_TT
    cells = _pair_cells()
    cid = jnp.broadcast_to(jnp.asarray(cells, jnp.int32)[:, None], (len(cells), tt))
    half = PEER_KEY_DIM // 2
    z = jnp.zeros((PEER_HEADS, PEER_N_KEYS, half), subkeys.dtype)
    skbd = jnp.concatenate([jnp.concatenate([subkeys[:, 0], z], axis=-1),
                            jnp.concatenate([z, subkeys[:, 1]], axis=-1)], axis=1).astype(jnp.bfloat16)
    return pl.pallas_call(
        _peer_route_body, grid=(n // tt,),
        in_specs=[pl.BlockSpec((tt, D_MODEL), lambda i: (i, 0)),
                  pl.BlockSpec((D_MODEL, PEER_HEADS * PEER_KEY_DIM), lambda i: (0, 0)),
                  pl.BlockSpec((PEER_HEADS, 2 * PEER_N_KEYS, PEER_KEY_DIM), lambda i: (0, 0, 0)),
                  pl.BlockSpec((len(cells), tt), lambda i: (0, 0))],
        out_specs=[pl.BlockSpec((tt, PEER_SLOTS), lambda i: (i, 0)), pl.BlockSpec((tt, PEER_SLOTS), lambda i: (i, 0))],
        out_shape=[jax.ShapeDtypeStruct((n, PEER_SLOTS), jnp.int32), jax.ShapeDtypeStruct((n, PEER_SLOTS), jnp.float32)],
        scratch_shapes=[pltpu.VMEM((2 * PEER_N_KEYS, tt), jnp.float32),
                        pltpu.VMEM((len(cells), tt), jnp.float32),
                        pltpu.VMEM((2, PEER_TOPK, tt), jnp.float32),
                        pltpu.VMEM((2, PEER_TOPK, tt), jnp.int32),
                        pltpu.VMEM((PEER_SLOTS, tt), jnp.int32),
                        pltpu.VMEM((PEER_SLOTS, tt), jnp.float32)],
        compiler_params=pltpu.CompilerParams(dimension_semantics=("arbitrary",), vmem_limit_bytes=ROUTE_VMEM_LIMIT),
        name="peer_route")(xn, wq.astype(jnp.bfloat16), skbd, cid)


def peer_channel_mixer(xn, wq, subkeys, u_tab, v_tab):
    bsz, seq, d = xn.shape
    n_tok = bsz * seq
    x2 = xn.reshape(n_tok, d)
    idx, gates = peer_route(x2, wq, subkeys)
    coef = peer_u(idx, x2, gates, pack_table(u_tab))
    out = peer_v(idx, coef, pack_table(v_tab))
    return out.reshape(bsz, seq, d)


def _residual_body(h_ref, y_ref, gt_ref, o_ref):
    o_ref[...] = h_ref[...] + gt_ref[0] * y_ref[...]


def _final_norm_body(h_ref, y_ref, gt_ref, g_ref, o_ref):
    h = h_ref[...] + gt_ref[0] * y_ref[...]
    o_ref[...] = h * lax.rsqrt(jnp.mean(h * h, axis=-1, keepdims=True) + EPS) * g_ref[...]


def gated_residual(h2, y2, gt, seq, gain=None):
    n, d = h2.shape
    tm = min(1024, seq)
    per = seq // tm
    row = pl.BlockSpec((tm, d), lambda i: (i, 0))
    specs = [row, row, pl.BlockSpec((1, 1, d), lambda i: (i // per, 0, 0))]
    args = [h2, y2, gt]
    if gain is not None:
        specs.append(pl.BlockSpec((1, d), lambda i: (0, 0)))
        args.append(gain.astype(jnp.float32).reshape(1, d))
    return pl.pallas_call(
        _residual_body if gain is None else _final_norm_body, grid=(n // tm,),
        in_specs=specs, out_specs=row, out_shape=jax.ShapeDtypeStruct((n, d), h2.dtype),
        name="gated_residual" if gain is None else "final_norm")(*args)


def kernel(x, c, positions, w_ada, b_ada, w_in, conv_w, a_log, dt_bias, norm_b_w,
           w_pa, w_pb, w_o, peer_wq, peer_subkeys, peer_u, peer_v, final_norm_w):
    bsz, seq, d = x.shape
    h = x.reshape(bsz * seq, d)
    for layer in range(DEPTH):
        mod = ada_modulation(c, w_ada[layer], b_ada[layer])
        sh1, sc1, gt1, sh2, sc2, gt2 = [m.reshape(bsz, 1, d) for m in jnp.split(mod, 6, axis=-1)]
        proj = in_proj(h, 1.0 + sc1, sh1, permute_w_in(w_in[layer]), seq)
        o_a, o_b = token_mixers(proj, positions, conv_w[layer], a_log[layer], dt_bias[layer], norm_b_w[layer],
                                bsz, seq)
        h, n2 = merge(o_a, o_b.reshape(bsz * seq, B_V_WIDTH), proj, h, gt1,
                      1.0 + sc2, sh2, w_pa[layer], w_pb[layer], w_o[layer], seq)
        y2 = peer_channel_mixer(n2.reshape(bsz, seq, d), peer_wq[layer], peer_subkeys[layer], peer_u[layer],
                                peer_v[layer])
        last = layer == DEPTH - 1
        h = gated_residual(h, y2.reshape(bsz * seq, d), gt2, seq, final_norm_w if last else None)
    return h.reshape(bsz, seq, d)
```

```python
import functools

import jax, jax.numpy as jnp
from jax import lax
from jax.experimental import pallas as pl
from jax.experimental.pallas import tpu as pltpu

D_MODEL = 1024
DEPTH = 1

LANES = 128
SUBLANES = 8
MIB = 1024 * 1024
DSA_VMEM_LIMIT = 48 * MIB
PREP_VMEM_LIMIT = 32 * MIB
GDN_VMEM_LIMIT = 40 * MIB
PROJ_VMEM_LIMIT = 48 * MIB
PEER_VMEM_LIMIT = 48 * MIB
ROUTE_VMEM_LIMIT = 32 * MIB

A_HEADS = 8
A_KV_HEADS = 2
A_HEAD_DIM = 64
IDX_HEADS = 16
IDX_DIM = 64
IDX_TOPK_MAX = 256
B_HEADS = 8
B_KEY_DIM = 64
B_VAL_DIM = 64
CONV_WIDTH = 4
CHUNK = 64
ROPE_THETA = 500000.0
ROPE_FRACTION_DEN = 4
PEER_HEADS = 8
PEER_KEY_DIM = 128
PEER_N_KEYS = 128
PEER_N_EXPERTS = PEER_N_KEYS * PEER_N_KEYS
PEER_TOPK = 16
EPS = 1e-6

A_WIDTH = A_HEADS * A_HEAD_DIM
KV_WIDTH = A_KV_HEADS * A_HEAD_DIM
B_QK_WIDTH = B_HEADS * B_KEY_DIM
B_V_WIDTH = B_HEADS * B_VAL_DIM
IN_SPLITS = (A_WIDTH, KV_WIDTH, KV_WIDTH, IDX_HEADS * IDX_DIM, IDX_DIM, IDX_HEADS,
             B_QK_WIDTH, B_QK_WIDTH, B_V_WIDTH, B_V_WIDTH, B_HEADS, B_HEADS, D_MODEL, D_MODEL)


DSA_TQ = 256
DSA_TK = 256
INT_MIN = -2**31
I16_MIN = -2**15
NEG_BIG = -1e30
PAD_ID = 2**30


def _dsa_body(topk, idx_bits, qit_ref, w_ref, qat_ref, ki_ref, ka_ref, vat_ref, o_ref, key_ref, hi_ref, lo_ref):
    tq, tk = DSA_TQ, DSA_TK
    qb = pl.program_id(1)
    n_kv = qb + 1
    t_glob = qb * tq + lax.broadcasted_iota(jnp.int32, (1, tq), 1)
    row = lax.broadcasted_iota(jnp.int32, (tk, 1), 0)
    f32 = jnp.float32

    def p1(j, carry):
        kt = ki_ref[0, j]
        score = jnp.zeros((tk, tq), f32)
        for h in range(IDX_HEADS):
            lt = jnp.dot(kt, qit_ref[0, h], preferred_element_type=f32)
            score = score + w_ref[0, h:h + 1, :] * jnp.maximum(lt, 0.0)
        bits = lax.bitcast_convert_type(score + 0.0, jnp.int32)
        skey = jnp.where(bits >= 0, bits, bits ^ jnp.int32(0x7FFFFFFF))
        skey = jnp.where(j * tk + row <= t_glob, skey, jnp.int32(INT_MIN))
        key_ref[j] = skey
        hi_ref[j] = lax.shift_right_arithmetic(skey, 16).astype(jnp.int16)
        return carry
    lax.fori_loop(0, n_kv, p1, 0)

    key_ref[n_kv] = jnp.full((tk, tq), INT_MIN, jnp.int32)
    hi_ref[n_kv] = jnp.full((tk, tq), I16_MIN, jnp.int16)
    lo_ref[n_kv] = jnp.full((tk, tq), I16_MIN, jnp.int16)

    def count(pred):
        def body(jj, acc):
            for j in (2 * jj, 2 * jj + 1):
                hit = jnp.where(pred(key_ref[j], j * tk + row), 1.0, 0.0)
                acc = acc + jnp.sum(hit.reshape(tk // SUBLANES, SUBLANES, tq), axis=0)
            return acc
        acc = lax.fori_loop(0, (n_kv + 1) // 2, body, jnp.zeros((SUBLANES, tq), f32))
        return jnp.sum(acc, axis=0, keepdims=True)

    kf = jnp.float32(topk)
    half = 16
    pack = 2 * SUBLANES

    def count16(ref, pred):
        def body(jj, acc):
            for j in (2 * jj, 2 * jj + 1):
                hit = jnp.where(pred(ref[j]), jnp.bfloat16(1), jnp.bfloat16(0)).reshape(tk // pack, pack, tq)
                part = hit[0]
                for g in range(1, tk // pack):
                    part = part + hit[g]
                acc = acc + part.astype(f32)
            return acc
        acc = lax.fori_loop(0, (n_kv + 1) // 2, body, jnp.zeros((pack, tq), f32))
        return jnp.sum(acc, axis=0, keepdims=True)

    def search16(ref, base):
        def step(i, u):
            cand_u = u | lax.shift_left(jnp.int32(1), half - 1 - i)
            cand = (cand_u + I16_MIN).astype(jnp.int16)
            c = base + count16(ref, lambda v: v >= cand)
            return jnp.where(c >= kf, cand_u, u)
        return lax.fori_loop(0, half, step, jnp.zeros((1, tq), jnp.int32))

    hi_u = search16(hi_ref, jnp.zeros((1, tq), f32))
    hi_s = hi_u + I16_MIN
    hi_s16 = hi_s.astype(jnp.int16)
    above = count16(hi_ref, lambda v: v > hi_s16)

    def lows(j, carry):
        lo = ((key_ref[j] & 0xFFFF) + I16_MIN).astype(jnp.int16)
        lo_ref[j] = jnp.where(hi_ref[j] == hi_s16, lo, jnp.int16(I16_MIN))
        return carry
    lax.fori_loop(0, n_kv, lows, 0)
    lo_u = search16(lo_ref, above)
    kth = hi_s * 65536 + lo_u
    c_gt = count(lambda k, s: k > kth)
    c_ge = count(lambda k, s: k >= kth)
    short = kth == jnp.int32(INT_MIN)
    x0 = jnp.where(short, jnp.int32(-1), jnp.int32(PAD_ID))
    need = kf - c_gt
    has_tie = jnp.max(jnp.where(jnp.logical_and(c_ge > kf, jnp.logical_not(short)), 1.0, 0.0)) > 0.0

    def tie_search():
        def step(i, x):
            bit = lax.shift_left(jnp.int32(1), idx_bits - 1 - i)
            probe = x + bit - 1
            c = count(lambda k, s: jnp.logical_and(k == kth, s <= probe))
            return jnp.where(c < need, x + bit, x)
        x = lax.fori_loop(0, idx_bits, step, jnp.zeros((1, tq), jnp.int32))
        return jnp.where(short, jnp.int32(-1), x)
    x_lim = lax.cond(has_tie, tie_search, lambda: x0)

    rep = A_HEADS // A_KV_HEADS
    hs = range(A_HEADS)

    def p3(j, carry):
        m, l, acc = carry
        skey = key_ref[j]
        s_idx = j * tk + row
        sel = jnp.logical_or(skey > kth, jnp.logical_and(skey == kth, s_idx <= x_lim))
        kt = [ka_ref[0, g, j] for g in range(A_KV_HEADS)]
        vt = [vat_ref[0, g, j] for g in range(A_KV_HEADS)]
        s = [jnp.where(sel, jnp.dot(kt[h // rep], qat_ref[0, h], preferred_element_type=f32), NEG_BIG) for h in hs]
        m_new = [jnp.maximum(m[h], jnp.max(s[h], axis=0, keepdims=True)) for h in hs]
        alpha = [jnp.exp(m[h] - m_new[h]) for h in hs]
        p = [jnp.exp(s[h] - m_new[h]) for h in hs]
        l_new = [alpha[h] * l[h] + jnp.sum(p[h], axis=0, keepdims=True) for h in hs]
        acc_new = [alpha[h] * acc[h] + jnp.dot(vt[h // rep], p[h].astype(jnp.bfloat16), preferred_element_type=f32)
                   for h in hs]
        return tuple(m_new), tuple(l_new), tuple(acc_new)

    init = (tuple(jnp.full((1, tq), NEG_BIG, f32) for _ in hs), tuple(jnp.zeros((1, tq), f32) for _ in hs),
            tuple(jnp.zeros((A_HEAD_DIM, tq), f32) for _ in hs))
    _, l_fin, acc_fin = lax.fori_loop(0, n_kv, p3, init)
    for h in hs:
        o_ref[0, h] = acc_fin[h] / l_fin[h]


ROPE_HALF = A_HEAD_DIM // ROPE_FRACTION_DEN // 2


def _rope_t(xt, n_heads, cos, sin, scale=None):
    outs = []
    for h in range(n_heads):
        b = h * A_HEAD_DIM
        x1, x2, rest = xt[b:b + ROPE_HALF], xt[b + ROPE_HALF:b + 2 * ROPE_HALF], xt[b + 2 * ROPE_HALF:b + A_HEAD_DIM]
        o = jnp.concatenate([x1 * cos - x2 * sin, x2 * cos + x1 * sin, rest], axis=0)
        outs.append(o if scale is None else o * scale)
    return outs


def _dsa_prep_body(qi_ref, qa_ref, ka_ref, va_ref, sm_ref, cos_ref, sin_ref, dp_ref,
                   qit_ref, qat_ref, w_ref, kit_ref, kat_ref, vat_ref, bg_ref):
    bf = jnp.bfloat16
    cos, sin = cos_ref[0], sin_ref[0]
    for h, o in enumerate(_rope_t(qi_ref[...].T, IDX_HEADS, cos, sin)):
        qit_ref[0, h] = o.astype(bf)
    for h, o in enumerate(_rope_t(qa_ref[...].T, A_HEADS, cos, sin, A_HEAD_DIM ** -0.5)):
        qat_ref[0, h] = o.astype(bf)
    kat = _rope_t(ka_ref[...].T, A_KV_HEADS, cos, sin)
    vt = va_ref[...].T
    for g in range(A_KV_HEADS):
        kat_ref[0, g, 0] = kat[g].T.astype(bf)
        vat_ref[0, g, 0] = vt[g * A_HEAD_DIM:(g + 1) * A_HEAD_DIM].astype(bf)
    smt = sm_ref[...].T
    kit_ref[0, 0] = _rope_t(smt[0:IDX_DIM], 1, cos, sin)[0].T.astype(bf)
    w_ref[0] = smt[IDX_DIM:IDX_DIM + IDX_HEADS] * ((IDX_HEADS ** -0.5) * (IDX_DIM ** -0.5))
    b0 = IDX_DIM + IDX_HEADS
    bb, ab = smt[b0:b0 + B_HEADS], smt[b0 + B_HEADS:b0 + 2 * B_HEADS]
    a_log, dt_bias = dp_ref[:, 0:1], dp_ref[:, 1:2]
    bg_ref[0] = jnp.concatenate([_sigmoid(bb), -jnp.exp(a_log) * jax.nn.softplus(ab + dt_bias)], axis=0)


def dsa_prep(proj, positions, a_log, dt_bias, bsz, seq):
    t = DSA_TQ
    per = seq // t
    assert DSA_TK == t and PROJ_ORDER[-4:] == ("ki", "wi", "bb", "ab") and PROJ_DST["ki"][0] + LANES == PROJ_WIDTH
    rd = A_HEAD_DIM // ROPE_FRACTION_DEN
    inv_freq = jnp.power(jnp.float32(ROPE_THETA), -jnp.arange(ROPE_HALF, dtype=jnp.float32) * (2.0 / rd))
    ang = positions.astype(jnp.float32)[:, None, :] * inv_freq[None, :, None]

    def col(name, width=None):
        off, w = PROJ_DST[name]
        w = width or w
        return pl.BlockSpec((t, w), lambda i: (i, off // w))
    trig = pl.BlockSpec((1, ROPE_HALF, t), lambda i: (i // per, 0, i % per))
    bf, f32 = jnp.bfloat16, jnp.float32
    sd = jax.ShapeDtypeStruct
    n_kv = seq // DSA_TK
    return pl.pallas_call(
        _dsa_prep_body, grid=(bsz * per,),
        in_specs=[col("qi"), col("qa"), col("ka"), col("va"), col("ki", LANES), trig, trig,
                  pl.BlockSpec((B_HEADS, 2), lambda i: (0, 0))],
        out_specs=[pl.BlockSpec((1, IDX_HEADS, IDX_DIM, t), lambda i: (i // per, 0, 0, i % per)),
                   pl.BlockSpec((1, A_HEADS, A_HEAD_DIM, t), lambda i: (i // per, 0, 0, i % per)),
                   pl.BlockSpec((1, IDX_HEADS, t), lambda i: (i // per, 0, i % per)),
                   pl.BlockSpec((1, 1, t, IDX_DIM), lambda i: (i // per, i % per, 0, 0)),
                   pl.BlockSpec((1, A_KV_HEADS, 1, t, A_HEAD_DIM), lambda i: (i // per, 0, i % per, 0, 0)),
                   pl.BlockSpec((1, A_KV_HEADS, 1, A_HEAD_DIM, t), lambda i: (i // per, 0, i % per, 0, 0)),
                   pl.BlockSpec((1, 2 * B_HEADS, t), lambda i: (i // per, 0, i % per))],
        out_shape=[sd((bsz, IDX_HEADS, IDX_DIM, seq), bf), sd((bsz, A_HEADS, A_HEAD_DIM, seq), bf),
                   sd((bsz, IDX_HEADS, seq), f32), sd((bsz, n_kv, DSA_TK, IDX_DIM), bf),
                   sd((bsz, A_KV_HEADS, n_kv, DSA_TK, A_HEAD_DIM), bf),
                   sd((bsz, A_KV_HEADS, n_kv, A_HEAD_DIM, DSA_TK), bf), sd((bsz, 2 * B_HEADS, seq), f32)],
        compiler_params=pltpu.CompilerParams(dimension_semantics=("arbitrary",), vmem_limit_bytes=PREP_VMEM_LIMIT),
        name="dsa_prep")(proj, proj, proj, proj, proj, jnp.cos(ang), jnp.sin(ang),
                         jnp.stack([a_log, dt_bias], axis=-1).astype(f32))


def dsa_attention_pallas(qit, w, qat, kit, kat, vat):
    bsz, seq = qit.shape[0], qit.shape[-1]
    tq, tk = DSA_TQ, DSA_TK
    topk = min(IDX_TOPK_MAX, seq // 4)
    n_kv = seq // tk
    return pl.pallas_call(
        functools.partial(_dsa_body, topk, (seq - 1).bit_length()),
        grid=(bsz, seq // tq),
        in_specs=[
            pl.BlockSpec((1, IDX_HEADS, IDX_DIM, tq), lambda b, q: (b, 0, 0, q)),
            pl.BlockSpec((1, IDX_HEADS, tq), lambda b, q: (b, 0, q)),
            pl.BlockSpec((1, A_HEADS, A_HEAD_DIM, tq), lambda b, q: (b, 0, 0, q)),
            pl.BlockSpec((1, n_kv, tk, IDX_DIM), lambda b, q: (b, 0, 0, 0)),
            pl.BlockSpec((1, A_KV_HEADS, n_kv, tk, A_HEAD_DIM), lambda b, q: (b, 0, 0, 0, 0)),
            pl.BlockSpec((1, A_KV_HEADS, n_kv, A_HEAD_DIM, tk), lambda b, q: (b, 0, 0, 0, 0)),
        ],
        out_specs=pl.BlockSpec((1, A_HEADS, A_HEAD_DIM, tq), lambda b, q: (b, 0, 0, q)),
        out_shape=jax.ShapeDtypeStruct((bsz, A_HEADS, A_HEAD_DIM, seq), jnp.float32),
        scratch_shapes=[
            pltpu.VMEM((n_kv + 1, tk, tq), jnp.int32),
            pltpu.VMEM((n_kv + 1, tk, tq), jnp.int16),
            pltpu.VMEM((n_kv + 1, tk, tq), jnp.int16),
        ],
        compiler_params=pltpu.CompilerParams(dimension_semantics=("arbitrary", "arbitrary"),
                                             vmem_limit_bytes=DSA_VMEM_LIMIT),
        name="dsa_attention",
    )(qit, w, qat, kit, kat, vat)


GDN_G = 8
GDN_BASE = 8
GDN_PAD = 8


def _gdn_body(q_ref, k_ref, v_ref, cw_ref, gc_ref, bt_ref, nw_ref, o_ref, s_ref, xs_ref, cs_ref):
    c_sz = CHUNK
    f32, bf = jnp.float32, jnp.bfloat16

    rows = GDN_G * c_sz
    halo = CONV_WIDTH - 1

    @pl.when(pl.program_id(1) == 0)
    def _():
        s_ref[...] = jnp.zeros(s_ref.shape, f32)
        xs_ref[:, 0:GDN_PAD, :] = jnp.zeros((3, GDN_PAD, xs_ref.shape[-1]), f32)

    for j, ref in enumerate((q_ref, k_ref, v_ref)):
        xs_ref[j, GDN_PAD:GDN_PAD + rows, :] = ref[0]
        acc = xs_ref[j, GDN_PAD - halo:GDN_PAD - halo + rows, :] * cw_ref[j, 0:1, :]
        for i in range(1, CONV_WIDTH):
            acc = acc + xs_ref[j, GDN_PAD - halo + i:GDN_PAD - halo + i + rows, :] * cw_ref[j, i:i + 1, :]
        cs_ref[j] = acc * _sigmoid(acc)
        xs_ref[j, GDN_PAD - halo:GDN_PAD, :] = xs_ref[j, GDN_PAD + rows - halo:GDN_PAD + rows, :]

    ri = lax.broadcasted_iota(jnp.int32, (c_sz, c_sz), 0)
    ci = lax.broadcasted_iota(jnp.int32, (c_sz, c_sz), 1)
    incl, strict = ri >= ci, ri > ci
    eye = jnp.where(ri == ci, 1.0, 0.0).astype(f32)
    blk = lambda w: (ri // w) == (ci // w)
    diag8 = blk(GDN_BASE)
    sub_blocks = []
    w = GDN_BASE
    while w < c_sz:
        sub_blocks.append(jnp.logical_and(blk(2 * w), jnp.logical_not(blk(w))))
        w *= 2
    nt = (((1,), (1,)), ((), ()))
    dot = lambda a, b: jnp.dot(a.astype(bf), b.astype(bf), preferred_element_type=f32)
    dot_nt = lambda a, b: lax.dot_general(a.astype(bf), b.astype(bf), nt, preferred_element_type=f32)

    def split(a):
        hi = a.astype(bf)
        return hi, (a - hi.astype(f32)).astype(bf)

    def hp(a, b):
        a_hi, a_lo = split(a)
        b_hi, b_lo = split(b)
        mm = lambda u, w: jnp.dot(u, w, preferred_element_type=f32)
        return mm(a_hi, b_hi) + (mm(a_hi, b_lo) + mm(a_lo, b_hi))

    hs = range(B_HEADS)

    def chunk_pair(i, carry):
        cs = [2 * i, 2 * i + 1]
        r0 = [pl.multiple_of(c * c_sz, c_sz) for c in cs]
        items = [(j, h) for j in range(2) for h in hs]
        hd = lambda a, j, h: cs_ref[a, pl.ds(r0[j], c_sz), h * B_KEY_DIM:(h + 1) * B_KEY_DIM]
        l2n = lambda t: t * lax.rsqrt(jnp.sum(t * t, axis=-1, keepdims=True) + EPS)
        q = [l2n(hd(0, j, h)) * (B_KEY_DIM ** -0.5) for j, h in items]
        k = [l2n(hd(1, j, h)) for j, h in items]
        v = [hd(2, j, h) for j, h in items]
        gc8 = [gc_ref[0, c] for c in cs]
        gct = [g8.T for g8 in gc8]
        btt = [bt_ref[0, c].T for c in cs]
        gcr = [gc8[j][h:h + 1, :] for j, h in items]
        n_it = range(len(items))
        gcc = [gct[j][:, h:h + 1] for j, h in items]
        beta = [btt[j][:, h:h + 1] for j, h in items]
        decay = [jnp.exp(jnp.where(incl, gcc[n] - gcr[n], -jnp.inf)) for n in n_it]
        kb = [k[n] * beta[n] for n in n_it]
        vb = [v[n] * beta[n] for n in n_it]
        low = [jnp.where(strict, dot_nt(kb[n], k[n]) * decay[n], 0.0) for n in n_it]
        dg = [jnp.where(diag8, low[n], 0.0) for n in n_it]
        t = [eye - dg[n] for n in n_it]
        p = [hp(dg[n], dg[n]) for n in n_it]
        t = [hp(t[n], eye + p[n]) for n in n_it]
        p = [hp(p[n], p[n]) for n in n_it]
        t = [hp(t[n], eye + p[n]) for n in n_it]
        for below in sub_blocks:
            lb = [jnp.where(below, low[n], 0.0) for n in n_it]
            lt = [hp(lb[n], t[n]) for n in n_it]
            t = [t[n] - hp(t[n], lt[n]) for n in n_it]
        u = [dot(t[n], vb[n]) for n in n_it]
        kcd = [dot(t[n], kb[n] * jnp.exp(gcc[n])) for n in n_it]
        intra = [dot_nt(q[n], k[n]) * decay[n] for n in n_it]
        qg = [q[n] * jnp.exp(gcc[n]) for n in n_it]
        glast = [g[:, c_sz - 1:c_sz] for g in gcr]
        kdt = [(k[n] * jnp.exp(glast[n] - gcc[n])).T for n in n_it]
        s = [s_ref[h] for h in hs]
        for j in range(2):
            ix = [j * B_HEADS + h for h in hs]
            v_new = [u[ix[h]] - dot(kcd[ix[h]], s[h]) for h in hs]
            out = [dot(qg[ix[h]], s[h]) + dot(intra[ix[h]], v_new[h]) for h in hs]
            s = [s[h] * jnp.exp(glast[ix[h]]) + dot(kdt[ix[h]], v_new[h]) for h in hs]
            for h in hs:
                o = out[h]
                o = o * lax.rsqrt(jnp.mean(o * o, axis=-1, keepdims=True) + EPS) * nw_ref[...]
                o_ref[0, pl.ds(r0[j], c_sz), h * B_VAL_DIM:(h + 1) * B_VAL_DIM] = o
        for h in hs:
            s_ref[h] = s[h]
        return carry
    lax.fori_loop(0, GDN_G // 2, chunk_pair, 0)


def gated_delta_rule_pallas(proj, conv_w, g, beta, norm_w, bsz, seq):
    nh, dk, dv = B_HEADS, B_KEY_DIM, B_VAL_DIM
    n = seq // CHUNK
    chunked = lambda t: jnp.moveaxis(t.reshape(bsz, nh, n, CHUNK), 1, 2)
    gc = chunked(jnp.cumsum(g.reshape(bsz, nh, n, CHUNK), axis=-1))
    side = pl.BlockSpec((1, GDN_G, nh, CHUNK), lambda b, c: (b, c, 0, 0))
    rows = GDN_G * CHUNK
    width = nh * dk

    def col(name):
        off, w = PROJ_DST[name]
        assert w == width
        return pl.BlockSpec((1, rows, w), lambda b, c: (b, c, off // w))
    cw = jnp.transpose(conv_w.astype(jnp.float32).reshape(CONV_WIDTH, 3, width), (1, 0, 2))
    proj3 = proj.reshape(bsz, seq, proj.shape[-1])
    return pl.pallas_call(
        _gdn_body, grid=(bsz, n // GDN_G),
        in_specs=[col("qb"), col("kb"), col("vb"),
                  pl.BlockSpec((3, CONV_WIDTH, width), lambda b, c: (0, 0, 0)),
                  side, side, pl.BlockSpec((1, dv), lambda b, c: (0, 0))],
        out_specs=pl.BlockSpec((1, rows, nh * dv), lambda b, c: (b, c, 0)),
        out_shape=jax.ShapeDtypeStruct((bsz, seq, nh * dv), jnp.float32),
        scratch_shapes=[pltpu.VMEM((nh, dk, dv), jnp.float32),
                        pltpu.VMEM((3, GDN_PAD + rows, width), jnp.float32),
                        pltpu.VMEM((3, rows, width), jnp.float32)],
        compiler_params=pltpu.CompilerParams(dimension_semantics=("arbitrary", "arbitrary"),
                                             vmem_limit_bytes=GDN_VMEM_LIMIT),
        name="gated_delta_rule")(proj3, proj3, proj3, cw, gc, chunked(beta), norm_w.astype(jnp.float32).reshape(1, dv))


IN_NAMES = ("qa", "ka", "va", "qi", "ki", "wi", "qb", "kb", "vb", "zb", "bb", "ab", "gate_a", "gate_b")
PROJ_ORDER = ("gate_a", "gate_b", "qi", "qa", "qb", "kb", "vb", "zb", "ka", "va", "ki", "wi", "bb", "ab")


def _proj_layout():
    src, off = {}, 0
    for name, w in zip(IN_NAMES, IN_SPLITS):
        src[name] = (off, w)
        off += w
    dst, off = {}, 0
    for name in PROJ_ORDER:
        dst[name] = (off, src[name][1])
        off += src[name][1]
    return src, dst, -(-off // LANES) * LANES


PROJ_SRC, PROJ_DST, PROJ_WIDTH = _proj_layout()
PROJ_TM = 512


def permute_w_in(w_in):
    cols = [w_in[:, PROJ_SRC[n][0]:PROJ_SRC[n][0] + PROJ_SRC[n][1]] for n in PROJ_ORDER]
    cols.append(jnp.zeros((w_in.shape[0], PROJ_WIDTH - sum(c.shape[1] for c in cols)), w_in.dtype))
    return jnp.concatenate(cols, axis=1).astype(jnp.bfloat16)


def _modulated_norm(x, scale, shift):
    return x * lax.rsqrt(jnp.mean(x * x, axis=-1, keepdims=True) + EPS) * scale + shift


def _sigmoid(v):
    return 1.0 / (1.0 + jnp.exp(-v))


def _ada_body(c_ref, w_ref, b_ref, o_ref):
    c = c_ref[...]
    o_ref[...] = jnp.dot((c * _sigmoid(c)).astype(jnp.bfloat16), w_ref[...].astype(jnp.bfloat16),
                         preferred_element_type=jnp.float32) + b_ref[...]


def ada_modulation(c, w_ada, b_ada):
    bsz, d = c.shape
    n = w_ada.shape[1]
    tn = D_MODEL
    return pl.pallas_call(
        _ada_body, grid=(n // tn,),
        in_specs=[pl.BlockSpec((bsz, d), lambda j: (0, 0)), pl.BlockSpec((d, tn), lambda j: (0, j)),
                  pl.BlockSpec((1, tn), lambda j: (0, j))],
        out_specs=pl.BlockSpec((bsz, tn), lambda j: (0, j)),
        out_shape=jax.ShapeDtypeStruct((bsz, n), jnp.float32),
        name="ada_modulation")(c, w_ada, b_ada.reshape(1, n))


def _in_proj_body(x_ref, sc_ref, sh_ref, w_ref, o_ref):
    n1 = _modulated_norm(x_ref[...], sc_ref[0], sh_ref[0])
    o_ref[...] = jnp.dot(n1.astype(jnp.bfloat16), w_ref[...], preferred_element_type=jnp.float32)


def in_proj(x2, scale, shift, w_perm, seq):
    n, d = x2.shape
    tm = PROJ_TM
    per = seq // tm
    vec = pl.BlockSpec((1, 1, d), lambda i: (i // per, 0, 0))
    return pl.pallas_call(
        _in_proj_body, grid=(n // tm,),
        in_specs=[pl.BlockSpec((tm, d), lambda i: (i, 0)), vec, vec,
                  pl.BlockSpec((d, PROJ_WIDTH), lambda i: (0, 0), pipeline_mode=pl.Buffered(1))],
        out_specs=pl.BlockSpec((tm, PROJ_WIDTH), lambda i: (i, 0)),
        out_shape=jax.ShapeDtypeStruct((n, PROJ_WIDTH), jnp.float32),
        compiler_params=pltpu.CompilerParams(dimension_semantics=("arbitrary",), vmem_limit_bytes=PROJ_VMEM_LIMIT),
        name="in_proj")(x2, scale, shift, w_perm)


def _merge_body(oa_ref, ob_ref, z_ref, ga_ref, gb_ref, x_ref, gt_ref, sc_ref, sh_ref, wpa_ref, wpb_ref, wo_ref,
                h_ref, n2_ref):
    f32, bf = jnp.float32, jnp.bfloat16
    z = z_ref[...]
    ob = ob_ref[...] * (z * _sigmoid(z))
    oa = oa_ref[0].reshape(A_WIDTH, oa_ref.shape[-1]).T
    ya = jnp.dot(oa.astype(bf), wpa_ref[...], preferred_element_type=f32)
    yb = jnp.dot(ob.astype(bf), wpb_ref[...], preferred_element_type=f32)
    merged = _sigmoid(ga_ref[...]) * ya + _sigmoid(gb_ref[...]) * yb
    y1 = jnp.dot(merged.astype(bf), wo_ref[...], preferred_element_type=f32)
    h = x_ref[...] + gt_ref[0] * y1
    h_ref[...] = h
    n2_ref[...] = _modulated_norm(h, sc_ref[0], sh_ref[0])


def merge(o_a, o_b, proj, x2, gt1, scale2, shift2, w_pa, w_pb, w_o, seq):
    n, d = x2.shape
    tm = PROJ_TM
    per = seq // tm
    bf = jnp.bfloat16
    vec = pl.BlockSpec((1, 1, d), lambda i: (i // per, 0, 0))

    def col(name):
        off, w = PROJ_DST[name]
        return pl.BlockSpec((tm, w), lambda i: (i, off // w))
    row = lambda w: pl.BlockSpec((tm, w), lambda i: (i, 0))
    res = lambda a, b: pl.BlockSpec((a, b), lambda i: (0, 0))
    return pl.pallas_call(
        _merge_body, grid=(n // tm,),
        in_specs=[pl.BlockSpec((1, A_HEADS, A_HEAD_DIM, tm), lambda i: (i // per, 0, 0, i % per)), row(B_V_WIDTH), col("zb"), col("gate_a"), col("gate_b"), row(d), vec, vec, vec,
                  res(A_WIDTH, d), res(B_V_WIDTH, d), res(d, d)],
        out_specs=[row(d), row(d)],
        out_shape=[jax.ShapeDtypeStruct((n, d), jnp.float32), jax.ShapeDtypeStruct((n, d), jnp.float32)],
        compiler_params=pltpu.CompilerParams(dimension_semantics=("arbitrary",), vmem_limit_bytes=PROJ_VMEM_LIMIT),
        name="merge")(o_a, o_b, proj, proj, proj, x2, gt1, scale2, shift2,
                      w_pa.astype(bf), w_pb.astype(bf), w_o.astype(bf))


def token_mixers(proj, positions, conv_w, a_log, dt_bias, norm_b_w, bsz, seq):
    qit, qat, w, kit, kat, vat, bg = dsa_prep(proj, positions, a_log, dt_bias, bsz, seq)
    o_a = dsa_attention_pallas(qit, w, qat, kit, kat, vat)

    o_b = gated_delta_rule_pallas(proj, conv_w, bg[:, B_HEADS:], bg[:, :B_HEADS], norm_b_w, bsz, seq)
    return o_a, o_b


PEER_SLOTS = PEER_HEADS * PEER_TOPK
PEER_TB = 128
HALF_ROWS = 4
HI_MASK = -65536


def pack_table(tab):
    bits = lax.bitcast_convert_type(tab.astype(jnp.bfloat16), jnp.uint16).astype(jnp.uint32)
    half = tab.shape[1] // 2
    word = bits[:, :half] | (bits[:, half:] << 16)
    return lax.bitcast_convert_type(word, jnp.int32).reshape(tab.shape[0] * HALF_ROWS, LANES)


def _table_row(tab_ref, rows, k):
    return _unpack(tab_ref[pl.ds(pl.multiple_of(rows[k], HALF_ROWS), HALF_ROWS), :])


def _unpack(w):
    lo = lax.bitcast_convert_type(lax.shift_left(w, 16), jnp.float32)
    hi = lax.bitcast_convert_type(w & jnp.int32(HI_MASK), jnp.float32)
    return lo, hi


def _peer_u_body(idx_ref, x_ref, gate_ref, tab_ref, o_ref, s_ref, sb_ref):
    def lane_sums(t):
        a = jnp.sum(sb_ref[t].T, axis=0, keepdims=True)
        o_ref[pl.ds(t, 1), :] = 0.5 * a * (1.0 + lax.erf(a * (2.0 ** -0.5))) * gate_ref[pl.ds(t, 1), :]

    sb_ref[0] = jnp.zeros(sb_ref.shape[1:], jnp.float32)

    def tok(t, carry):
        lane_sums(jnp.maximum(t - 1, 0))
        x8 = x_ref[pl.ds(t, 1), :].reshape(2 * HALF_ROWS, LANES)
        xlo, xhi = x8[0:HALF_ROWS], x8[HALF_ROWS:2 * HALF_ROWS]
        rows = idx_ref.at[t]
        for k in range(PEER_SLOTS):
            lo, hi = _table_row(tab_ref, rows, k)
            s_ref[HALF_ROWS * k:HALF_ROWS * (k + 1), :] = lo * xlo + hi * xhi
        s4 = s_ref[pl.ds(0, PEER_SLOTS, stride=HALF_ROWS), :]
        for r in range(1, HALF_ROWS):
            s4 = s4 + s_ref[pl.ds(r, PEER_SLOTS, stride=HALF_ROWS), :]
        sb_ref[t] = s4
        return carry
    lax.fori_loop(0, PEER_TB, tok, 0, unroll=2)
    lane_sums(PEER_TB - 1)


def _peer_v_body(idx_ref, coef_ref, tab_ref, o_ref, cb_ref):
    def spread(t):
        return jnp.broadcast_to(coef_ref[pl.ds(t, 1), :], (PEER_SLOTS, LANES)).T

    def tok(t, weights):
        nxt = spread(jnp.minimum(t + 1, PEER_TB - 1))
        cb_ref[...] = weights
        nacc = 2
        acc = [jnp.zeros((HALF_ROWS, LANES), jnp.float32) for _ in range(2 * nacc)]
        rows = idx_ref.at[t]
        for k in range(PEER_SLOTS):
            lo, hi = _table_row(tab_ref, rows, k)
            c = jnp.broadcast_to(cb_ref[k:k + 1, :], (HALF_ROWS, LANES))
            a = k % nacc
            acc[2 * a] = acc[2 * a] + c * lo
            acc[2 * a + 1] = acc[2 * a + 1] + c * hi
        half = HALF_ROWS * LANES
        o_ref[pl.ds(t, 1), 0:half] = (acc[0] + acc[2]).reshape(1, half)
        o_ref[pl.ds(t, 1), half:2 * half] = (acc[1] + acc[3]).reshape(1, half)
        return nxt
    lax.fori_loop(0, PEER_TB, tok, spread(0))


def _table_spec():
    return pl.BlockSpec((PEER_N_EXPERTS * HALF_ROWS, LANES), lambda i: (0, 0), pipeline_mode=pl.Buffered(1))


def peer_u(idx, x, gates, tab):
    n = idx.shape[0]
    tb = PEER_TB
    return pl.pallas_call(
        _peer_u_body, grid=(n // tb,),
        in_specs=[pl.BlockSpec((tb, PEER_SLOTS), lambda i: (i, 0), memory_space=pltpu.SMEM),
                  pl.BlockSpec((tb, x.shape[1]), lambda i: (i, 0)),
                  pl.BlockSpec((tb, PEER_SLOTS), lambda i: (i, 0)),
                  _table_spec()],
        out_specs=pl.BlockSpec((tb, PEER_SLOTS), lambda i: (i, 0)),
        out_shape=jax.ShapeDtypeStruct((n, PEER_SLOTS), jnp.float32),
        scratch_shapes=[pltpu.VMEM((HALF_ROWS * PEER_SLOTS, LANES), jnp.float32),
                        pltpu.VMEM((tb, PEER_SLOTS, LANES), jnp.float32)],
        compiler_params=pltpu.CompilerParams(dimension_semantics=("arbitrary",), vmem_limit_bytes=PEER_VMEM_LIMIT),
        name="peer_u")(idx, x, gates, tab)


def peer_v(idx, coef, tab):
    n = idx.shape[0]
    tb = PEER_TB
    return pl.pallas_call(
        _peer_v_body, grid=(n // tb,),
        in_specs=[pl.BlockSpec((tb, PEER_SLOTS), lambda i: (i, 0), memory_space=pltpu.SMEM),
                  pl.BlockSpec((tb, PEER_SLOTS), lambda i: (i, 0)),
                  _table_spec()],
        out_specs=pl.BlockSpec((tb, 2 * HALF_ROWS * LANES), lambda i: (i, 0)),
        out_shape=jax.ShapeDtypeStruct((n, 2 * HALF_ROWS * LANES), jnp.float32),
        scratch_shapes=[pltpu.VMEM((PEER_SLOTS, LANES), jnp.float32)],
        compiler_params=pltpu.CompilerParams(dimension_semantics=("arbitrary",), vmem_limit_bytes=PEER_VMEM_LIMIT),
        name="peer_v")(idx, coef, tab)


PEER_TT = 256


def _extract_top(ref, n_out, rid=None):
    rows, t = ref.shape
    if rid is None:
        rid = lax.broadcasted_iota(jnp.int32, (rows, t), 0)
    vals, idxs = [], []
    for _ in range(n_out):
        s = ref[...]
        m = jnp.max(s, axis=0, keepdims=True)
        ix = jnp.min(jnp.where(s == m, rid, jnp.int32(PAD_ID)), axis=0, keepdims=True)
        ref[...] = jnp.where(rid == ix, -jnp.inf, s)
        vals.append(m)
        idxs.append(ix)
    return vals, idxs


def _peer_route_body(x_ref, wq_ref, sk_ref, cid_ref, idx_ref, gate_ref, s_ref, cand_ref, v_ref, i_ref, et_ref, gt_ref):
    kk, nk = PEER_TOPK, PEER_N_KEYS
    f32 = jnp.float32
    q = jnp.dot(x_ref[...].astype(jnp.bfloat16), wq_ref[...], preferred_element_type=f32).astype(jnp.bfloat16)
    nt = (((1,), (1,)), ((), ()))
    row16 = lax.broadcasted_iota(jnp.int32, (kk, PEER_TT), 0)
    for h in range(PEER_HEADS):
        qh = q[:, h * PEER_KEY_DIM:(h + 1) * PEER_KEY_DIM]
        s_ref[...] = lax.dot_general(sk_ref[h], qh, nt, preferred_element_type=f32)
        for p in range(2):
            vals, idxs = _extract_top(s_ref.at[p * nk:(p + 1) * nk, :], kk)
            for i in range(kk):
                v_ref[p, i:i + 1, :] = vals[i]
                i_ref[p, i:i + 1, :] = idxs[i]
        off = 0
        for i in range(kk):
            n_j = kk // (i + 1)
            cand_ref[off:off + n_j, :] = v_ref[0, i:i + 1, :] + v_ref[1, 0:n_j, :]
            off += n_j
        cand_ref[off:, :] = jnp.full((cand_ref.shape[0] - off, PEER_TT), -jnp.inf, f32)
        vals, cis = _extract_top(cand_ref, kk, cid_ref[...])
        i1, i2 = i_ref[0], i_ref[1]
        es = [jnp.exp(v - vals[0]) for v in vals]
        den = es[0]
        for e in es[1:]:
            den = den + e
        for k in range(kk):
            ci = cis[k]
            e1 = jnp.sum(jnp.where(row16 == lax.shift_right_logical(ci, kk.bit_length() - 1), i1, 0), axis=0, keepdims=True)
            e2 = jnp.sum(jnp.where(row16 == (ci & (kk - 1)), i2, 0), axis=0, keepdims=True)
            et_ref[h * kk + k:h * kk + k + 1, :] = (e1 * nk + e2) * HALF_ROWS
            gt_ref[h * kk + k:h * kk + k + 1, :] = es[k] / den
    idx_ref[...] = lax.bitcast_convert_type(lax.bitcast_convert_type(et_ref[...], f32).T, jnp.int32)
    gate_ref[...] = gt_ref[...].T


def _pair_cells():
    kk = PEER_TOPK
    ids = [i * kk + j for i in range(kk) for j in range(kk // (i + 1))]
    return ids + [PAD_ID] * (-len(ids) % SUBLANES)


def peer_route(xn, wq, subkeys):
    n = xn.shape[0]
    tt = PEER_TT
    cells = _pair_cells()
    cid = jnp.broadcast_to(jnp.asarray(cells, jnp.int32)[:, None], (len(cells), tt))
    half = PEER_KEY_DIM // 2
    z = jnp.zeros((PEER_HEADS, PEER_N_KEYS, half), subkeys.dtype)
    skbd = jnp.concatenate([jnp.concatenate([subkeys[:, 0], z], axis=-1),
                            jnp.concatenate([z, subkeys[:, 1]], axis=-1)], axis=1).astype(jnp.bfloat16)
    return pl.pallas_call(
        _peer_route_body, grid=(n // tt,),
        in_specs=[pl.BlockSpec((tt, D_MODEL), lambda i: (i, 0)),
                  pl.BlockSpec((D_MODEL, PEER_HEADS * PEER_KEY_DIM), lambda i: (0, 0)),
                  pl.BlockSpec((PEER_HEADS, 2 * PEER_N_KEYS, PEER_KEY_DIM), lambda i: (0, 0, 0)),
                  pl.BlockSpec((len(cells), tt), lambda i: (0, 0))],
        out_specs=[pl.BlockSpec((tt, PEER_SLOTS), lambda i: (i, 0)), pl.BlockSpec((tt, PEER_SLOTS), lambda i: (i, 0))],
        out_shape=[jax.ShapeDtypeStruct((n, PEER_SLOTS), jnp.int32), jax.ShapeDtypeStruct((n, PEER_SLOTS), jnp.float32)],
        scratch_shapes=[pltpu.VMEM((2 * PEER_N_KEYS, tt), jnp.float32),
                        pltpu.VMEM((len(cells), tt), jnp.float32),
                        pltpu.VMEM((2, PEER_TOPK, tt), jnp.float32),
                        pltpu.VMEM((2, PEER_TOPK, tt), jnp.int32),
                        pltpu.VMEM((PEER_SLOTS, tt), jnp.int32),
                        pltpu.VMEM((PEER_SLOTS, tt), jnp.float32)],
        compiler_params=pltpu.CompilerParams(dimension_semantics=("arbitrary",), vmem_limit_bytes=ROUTE_VMEM_LIMIT),
        name="peer_route")(xn, wq.astype(jnp.bfloat16), skbd, cid)


def peer_channel_mixer(xn, wq, subkeys, u_tab, v_tab):
    bsz, seq, d = xn.shape
    n_tok = bsz * seq
    x2 = xn.reshape(n_tok, d)
    idx, gates = peer_route(x2, wq, subkeys)
    coef = peer_u(idx, x2, gates, pack_table(u_tab))
    out = peer_v(idx, coef, pack_table(v_tab))
    return out.reshape(bsz, seq, d)


def _residual_body(h_ref, y_ref, gt_ref, o_ref):
    o_ref[...] = h_ref[...] + gt_ref[0] * y_ref[...]


def _final_norm_body(h_ref, y_ref, gt_ref, g_ref, o_ref):
    h = h_ref[...] + gt_ref[0] * y_ref[...]
    o_ref[...] = h * lax.rsqrt(jnp.mean(h * h, axis=-1, keepdims=True) + EPS) * g_ref[...]


def gated_residual(h2, y2, gt, seq, gain=None):
    n, d = h2.shape
    tm = min(1024, seq)
    per = seq // tm
    row = pl.BlockSpec((tm, d), lambda i: (i, 0))
    specs = [row, row, pl.BlockSpec((1, 1, d), lambda i: (i // per, 0, 0))]
    args = [h2, y2, gt]
    if gain is not None:
        specs.append(pl.BlockSpec((1, d), lambda i: (0, 0)))
        args.append(gain.astype(jnp.float32).reshape(1, d))
    return pl.pallas_call(
        _residual_body if gain is None else _final_norm_body, grid=(n // tm,),
        in_specs=specs, out_specs=row, out_shape=jax.ShapeDtypeStruct((n, d), h2.dtype),
        name="gated_residual" if gain is None else "final_norm")(*args)


def kernel(x, c, positions, w_ada, b_ada, w_in, conv_w, a_log, dt_bias, norm_b_w,
           w_pa, w_pb, w_o, peer_wq, peer_subkeys, peer_u, peer_v, final_norm_w):
    bsz, seq, d = x.shape
    h = x.reshape(bsz * seq, d)
    for layer in range(DEPTH):
        mod = ada_modulation(c, w_ada[layer], b_ada[layer])
        sh1, sc1, gt1, sh2, sc2, gt2 = [m.reshape(bsz, 1, d) for m in jnp.split(mod, 6, axis=-1)]
        proj = in_proj(h, 1.0 + sc1, sh1, permute_w_in(w_in[layer]), seq)
        o_a, o_b = token_mixers(proj, positions, conv_w[layer], a_log[layer], dt_bias[layer], norm_b_w[layer],
                                bsz, seq)
        h, n2 = merge(o_a, o_b.reshape(bsz * seq, B_V_WIDTH), proj, h, gt1,
                      1.0 + sc2, sh2, w_pa[layer], w_pb[layer], w_o[layer], seq)
        y2 = peer_channel_mixer(n2.reshape(bsz, seq, d), peer_wq[layer], peer_subkeys[layer], peer_u[layer],
                                peer_v[layer])
        last = layer == DEPTH - 1
        h = gated_residual(h, y2.reshape(bsz * seq, d), gt2, seq, final_norm_w if last else None)
    return h.reshape(bsz, seq, d)
```

```python
import functools

import jax, jax.numpy as jnp
from jax import lax
from jax.experimental import pallas as pl
from jax.experimental.pallas import tpu as pltpu

D_MODEL = 1024
DEPTH = 1

LANES = 128
SUBLANES = 8
MIB = 1024 * 1024
DSA_VMEM_LIMIT = 48 * MIB
GDN_VMEM_LIMIT = 40 * MIB
PROJ_VMEM_LIMIT = 48 * MIB
PEER_VMEM_LIMIT = 48 * MIB
ROUTE_VMEM_LIMIT = 32 * MIB

A_HEADS = 8
A_KV_HEADS = 2
A_HEAD_DIM = 64
IDX_HEADS = 16
IDX_DIM = 64
IDX_TOPK_MAX = 256
B_HEADS = 8
B_KEY_DIM = 64
B_VAL_DIM = 64
CONV_WIDTH = 4
CHUNK = 64
ROPE_THETA = 500000.0
ROPE_FRACTION_DEN = 4
PEER_HEADS = 8
PEER_KEY_DIM = 128
PEER_N_KEYS = 128
PEER_N_EXPERTS = PEER_N_KEYS * PEER_N_KEYS
PEER_TOPK = 16
EPS = 1e-6

A_WIDTH = A_HEADS * A_HEAD_DIM
KV_WIDTH = A_KV_HEADS * A_HEAD_DIM
B_QK_WIDTH = B_HEADS * B_KEY_DIM
B_V_WIDTH = B_HEADS * B_VAL_DIM
IN_SPLITS = (A_WIDTH, KV_WIDTH, KV_WIDTH, IDX_HEADS * IDX_DIM, IDX_DIM, IDX_HEADS,
             B_QK_WIDTH, B_QK_WIDTH, B_V_WIDTH, B_V_WIDTH, B_HEADS, B_HEADS, D_MODEL, D_MODEL)


DSA_TQ = 256
DSA_TK = 256
INT_MIN = -2**31
I16_MIN = -2**15
NEG_BIG = -1e30
PAD_ID = 2**30


def _dsa_body(topk, idx_bits, qit_ref, w_ref, qat_ref, ki_ref, ka_ref, vat_ref, o_ref, key_ref, hi_ref, lo_ref):
    tq, tk = DSA_TQ, DSA_TK
    qb = pl.program_id(1)
    n_kv = qb + 1
    t_glob = qb * tq + lax.broadcasted_iota(jnp.int32, (1, tq), 1)
    row = lax.broadcasted_iota(jnp.int32, (tk, 1), 0)
    f32 = jnp.float32

    def p1(j, carry):
        kt = ki_ref[0, j]
        score = jnp.zeros((tk, tq), f32)
        for h in range(IDX_HEADS):
            lt = jnp.dot(kt, qit_ref[0, h], preferred_element_type=f32)
            score = score + w_ref[0, h:h + 1, :] * jnp.maximum(lt, 0.0)
        bits = lax.bitcast_convert_type(score + 0.0, jnp.int32)
        skey = jnp.where(bits >= 0, bits, bits ^ jnp.int32(0x7FFFFFFF))
        skey = jnp.where(j * tk + row <= t_glob, skey, jnp.int32(INT_MIN))
        key_ref[j] = skey
        hi_ref[j] = lax.shift_right_arithmetic(skey, 16).astype(jnp.int16)
        return carry
    lax.fori_loop(0, n_kv, p1, 0)

    key_ref[n_kv] = jnp.full((tk, tq), INT_MIN, jnp.int32)
    hi_ref[n_kv] = jnp.full((tk, tq), I16_MIN, jnp.int16)
    lo_ref[n_kv] = jnp.full((tk, tq), I16_MIN, jnp.int16)

    def count(pred):
        def body(jj, acc):
            for j in (2 * jj, 2 * jj + 1):
                hit = jnp.where(pred(key_ref[j], j * tk + row), 1.0, 0.0)
                acc = acc + jnp.sum(hit.reshape(tk // SUBLANES, SUBLANES, tq), axis=0)
            return acc
        acc = lax.fori_loop(0, (n_kv + 1) // 2, body, jnp.zeros((SUBLANES, tq), f32))
        return jnp.sum(acc, axis=0, keepdims=True)

    kf = jnp.float32(topk)
    half = 16
    pack = 2 * SUBLANES

    def count16(ref, pred):
        def body(jj, acc):
            for j in (2 * jj, 2 * jj + 1):
                hit = jnp.where(pred(ref[j]), jnp.bfloat16(1), jnp.bfloat16(0)).reshape(tk // pack, pack, tq)
                part = hit[0]
                for g in range(1, tk // pack):
                    part = part + hit[g]
                acc = acc + part.astype(f32)
            return acc
        acc = lax.fori_loop(0, (n_kv + 1) // 2, body, jnp.zeros((pack, tq), f32))
        return jnp.sum(acc, axis=0, keepdims=True)

    def search16(ref, base):
        def step(i, u):
            cand_u = u | lax.shift_left(jnp.int32(1), half - 1 - i)
            cand = (cand_u + I16_MIN).astype(jnp.int16)
            c = base + count16(ref, lambda v: v >= cand)
            return jnp.where(c >= kf, cand_u, u)
        return lax.fori_loop(0, half, step, jnp.zeros((1, tq), jnp.int32))

    hi_u = search16(hi_ref, jnp.zeros((1, tq), f32))
    hi_s = hi_u + I16_MIN
    hi_s16 = hi_s.astype(jnp.int16)
    above = count16(hi_ref, lambda v: v > hi_s16)

    def lows(j, carry):
        lo = ((key_ref[j] & 0xFFFF) + I16_MIN).astype(jnp.int16)
        lo_ref[j] = jnp.where(hi_ref[j] == hi_s16, lo, jnp.int16(I16_MIN))
        return carry
    lax.fori_loop(0, n_kv, lows, 0)
    lo_u = search16(lo_ref, above)
    kth = hi_s * 65536 + lo_u
    c_gt = count(lambda k, s: k > kth)
    c_ge = count(lambda k, s: k >= kth)
    short = kth == jnp.int32(INT_MIN)
    x0 = jnp.where(short, jnp.int32(-1), jnp.int32(PAD_ID))
    need = kf - c_gt
    has_tie = jnp.max(jnp.where(jnp.logical_and(c_ge > kf, jnp.logical_not(short)), 1.0, 0.0)) > 0.0

    def tie_search():
        def step(i, x):
            bit = lax.shift_left(jnp.int32(1), idx_bits - 1 - i)
            probe = x + bit - 1
            c = count(lambda k, s: jnp.logical_and(k == kth, s <= probe))
            return jnp.where(c < need, x + bit, x)
        x = lax.fori_loop(0, idx_bits, step, jnp.zeros((1, tq), jnp.int32))
        return jnp.where(short, jnp.int32(-1), x)
    x_lim = lax.cond(has_tie, tie_search, lambda: x0)

    rep = A_HEADS // A_KV_HEADS
    hs = range(A_HEADS)

    def p3(j, carry):
        m, l, acc = carry
        skey = key_ref[j]
        s_idx = j * tk + row
        sel = jnp.logical_or(skey > kth, jnp.logical_and(skey == kth, s_idx <= x_lim))
        kt = [ka_ref[0, g, j] for g in range(A_KV_HEADS)]
        vt = [vat_ref[0, g, j] for g in range(A_KV_HEADS)]
        s = [jnp.where(sel, jnp.dot(kt[h // rep], qat_ref[0, h], preferred_element_type=f32), NEG_BIG) for h in hs]
        m_new = [jnp.maximum(m[h], jnp.max(s[h], axis=0, keepdims=True)) for h in hs]
        alpha = [jnp.exp(m[h] - m_new[h]) for h in hs]
        p = [jnp.exp(s[h] - m_new[h]) for h in hs]
        l_new = [alpha[h] * l[h] + jnp.sum(p[h], axis=0, keepdims=True) for h in hs]
        acc_new = [alpha[h] * acc[h] + jnp.dot(vt[h // rep], p[h].astype(jnp.bfloat16), preferred_element_type=f32)
                   for h in hs]
        return tuple(m_new), tuple(l_new), tuple(acc_new)

    init = (tuple(jnp.full((1, tq), NEG_BIG, f32) for _ in hs), tuple(jnp.zeros((1, tq), f32) for _ in hs),
            tuple(jnp.zeros((A_HEAD_DIM, tq), f32) for _ in hs))
    _, l_fin, acc_fin = lax.fori_loop(0, n_kv, p3, init)
    for h in hs:
        o_ref[0, h] = acc_fin[h] / l_fin[h]


ROPE_HALF = A_HEAD_DIM // ROPE_FRACTION_DEN // 2


def _rope_t(xt, n_heads, cos, sin, scale=None):
    outs = []
    for h in range(n_heads):
        b = h * A_HEAD_DIM
        x1, x2, rest = xt[b:b + ROPE_HALF], xt[b + ROPE_HALF:b + 2 * ROPE_HALF], xt[b + 2 * ROPE_HALF:b + A_HEAD_DIM]
        o = jnp.concatenate([x1 * cos - x2 * sin, x2 * cos + x1 * sin, rest], axis=0)
        outs.append(o if scale is None else o * scale)
    return outs


def dsa_attention_pallas(qit, w, qat, kit, kat, vat):
    bsz, seq = qit.shape[0], qit.shape[-1]
    tq, tk = DSA_TQ, DSA_TK
    topk = min(IDX_TOPK_MAX, seq // 4)
    n_kv = seq // tk
    return pl.pallas_call(
        functools.partial(_dsa_body, topk, (seq - 1).bit_length()),
        grid=(bsz, seq // tq),
        in_specs=[
            pl.BlockSpec((1, IDX_HEADS, IDX_DIM, tq), lambda b, q: (b, 0, 0, q)),
            pl.BlockSpec((1, IDX_HEADS, tq), lambda b, q: (b, 0, q)),
            pl.BlockSpec((1, A_HEADS, A_HEAD_DIM, tq), lambda b, q: (b, 0, 0, q)),
            pl.BlockSpec((1, n_kv, tk, IDX_DIM), lambda b, q: (b, 0, 0, 0)),
            pl.BlockSpec((1, A_KV_HEADS, n_kv, tk, A_HEAD_DIM), lambda b, q: (b, 0, 0, 0, 0)),
            pl.BlockSpec((1, A_KV_HEADS, n_kv, A_HEAD_DIM, tk), lambda b, q: (b, 0, 0, 0, 0)),
        ],
        out_specs=pl.BlockSpec((1, A_HEADS, A_HEAD_DIM, tq), lambda b, q: (b, 0, 0, q)),
        out_shape=jax.ShapeDtypeStruct((bsz, A_HEADS, A_HEAD_DIM, seq), jnp.float32),
        scratch_shapes=[
            pltpu.VMEM((n_kv + 1, tk, tq), jnp.int32),
            pltpu.VMEM((n_kv + 1, tk, tq), jnp.int16),
            pltpu.VMEM((n_kv + 1, tk, tq), jnp.int16),
        ],
        compiler_params=pltpu.CompilerParams(dimension_semantics=("arbitrary", "arbitrary"),
                                             vmem_limit_bytes=DSA_VMEM_LIMIT),
        name="dsa_attention",
    )(qit, w, qat, kit, kat, vat)


GDN_G = 8
GDN_BASE = 8
GDN_PAD = 8


def _gdn_body(q_ref, k_ref, v_ref, cw_ref, gc_ref, bt_ref, nw_ref, o_ref, s_ref, xs_ref, cs_ref):
    c_sz = CHUNK
    f32, bf = jnp.float32, jnp.bfloat16

    rows = GDN_G * c_sz
    halo = CONV_WIDTH - 1

    @pl.when(pl.program_id(1) == 0)
    def _():
        s_ref[...] = jnp.zeros(s_ref.shape, f32)
        xs_ref[:, 0:GDN_PAD, :] = jnp.zeros((3, GDN_PAD, xs_ref.shape[-1]), f32)

    for j, ref in enumerate((q_ref, k_ref, v_ref)):
        xs_ref[j, GDN_PAD:GDN_PAD + rows, :] = ref[0]
        acc = xs_ref[j, GDN_PAD - halo:GDN_PAD - halo + rows, :] * cw_ref[j, 0:1, :]
        for i in range(1, CONV_WIDTH):
            acc = acc + xs_ref[j, GDN_PAD - halo + i:GDN_PAD - halo + i + rows, :] * cw_ref[j, i:i + 1, :]
        cs_ref[j] = acc * _sigmoid(acc)
        xs_ref[j, GDN_PAD - halo:GDN_PAD, :] = xs_ref[j, GDN_PAD + rows - halo:GDN_PAD + rows, :]

    ri = lax.broadcasted_iota(jnp.int32, (c_sz, c_sz), 0)
    ci = lax.broadcasted_iota(jnp.int32, (c_sz, c_sz), 1)
    incl, strict = ri >= ci, ri > ci
    eye = jnp.where(ri == ci, 1.0, 0.0).astype(f32)
    blk = lambda w: (ri // w) == (ci // w)
    diag8 = blk(GDN_BASE)
    sub_blocks = []
    w = GDN_BASE
    while w < c_sz:
        sub_blocks.append(jnp.logical_and(blk(2 * w), jnp.logical_not(blk(w))))
        w *= 2
    nt = (((1,), (1,)), ((), ()))
    dot = lambda a, b: jnp.dot(a.astype(bf), b.astype(bf), preferred_element_type=f32)
    dot_nt = lambda a, b: lax.dot_general(a.astype(bf), b.astype(bf), nt, preferred_element_type=f32)

    def split(a):
        hi = a.astype(bf)
        return hi, (a - hi.astype(f32)).astype(bf)

    def hp(a, b):
        a_hi, a_lo = split(a)
        b_hi, b_lo = split(b)
        mm = lambda u, w: jnp.dot(u, w, preferred_element_type=f32)
        return mm(a_hi, b_hi) + (mm(a_hi, b_lo) + mm(a_lo, b_hi))

    hs = range(B_HEADS)

    def chunk_pair(i, carry):
        cs = [2 * i, 2 * i + 1]
        r0 = [pl.multiple_of(c * c_sz, c_sz) for c in cs]
        items = [(j, h) for j in range(2) for h in hs]
        hd = lambda a, j, h: cs_ref[a, pl.ds(r0[j], c_sz), h * B_KEY_DIM:(h + 1) * B_KEY_DIM]
        l2n = lambda t: t * lax.rsqrt(jnp.sum(t * t, axis=-1, keepdims=True) + EPS)
        q = [l2n(hd(0, j, h)) * (B_KEY_DIM ** -0.5) for j, h in items]
        k = [l2n(hd(1, j, h)) for j, h in items]
        v = [hd(2, j, h) for j, h in items]
        gc8 = [gc_ref[0, c] for c in cs]
        gct = [g8.T for g8 in gc8]
        btt = [bt_ref[0, c].T for c in cs]
        gcr = [gc8[j][h:h + 1, :] for j, h in items]
        n_it = range(len(items))
        gcc = [gct[j][:, h:h + 1] for j, h in items]
        beta = [btt[j][:, h:h + 1] for j, h in items]
        decay = [jnp.exp(jnp.where(incl, gcc[n] - gcr[n], -jnp.inf)) for n in n_it]
        kb = [k[n] * beta[n] for n in n_it]
        vb = [v[n] * beta[n] for n in n_it]
        low = [jnp.where(strict, dot_nt(kb[n], k[n]) * decay[n], 0.0) for n in n_it]
        dg = [jnp.where(diag8, low[n], 0.0) for n in n_it]
        t = [eye - dg[n] for n in n_it]
        p = [hp(dg[n], dg[n]) for n in n_it]
        t = [hp(t[n], eye + p[n]) for n in n_it]
        p = [hp(p[n], p[n]) for n in n_it]
        t = [hp(t[n], eye + p[n]) for n in n_it]
        for below in sub_blocks:
            lb = [jnp.where(below, low[n], 0.0) for n in n_it]
            lt = [hp(lb[n], t[n]) for n in n_it]
            t = [t[n] - hp(t[n], lt[n]) for n in n_it]
        u = [dot(t[n], vb[n]) for n in n_it]
        kcd = [dot(t[n], kb[n] * jnp.exp(gcc[n])) for n in n_it]
        intra = [dot_nt(q[n], k[n]) * decay[n] for n in n_it]
        qg = [q[n] * jnp.exp(gcc[n]) for n in n_it]
        glast = [g[:, c_sz - 1:c_sz] for g in gcr]
        kdt = [(k[n] * jnp.exp(glast[n] - gcc[n])).T for n in n_it]
        s = [s_ref[h] for h in hs]
        for j in range(2):
            ix = [j * B_HEADS + h for h in hs]
            v_new = [u[ix[h]] - dot(kcd[ix[h]], s[h]) for h in hs]
            out = [dot(qg[ix[h]], s[h]) + dot(intra[ix[h]], v_new[h]) for h in hs]
            s = [s[h] * jnp.exp(glast[ix[h]]) + dot(kdt[ix[h]], v_new[h]) for h in hs]
            for h in hs:
                o = out[h]
                o = o * lax.rsqrt(jnp.mean(o * o, axis=-1, keepdims=True) + EPS) * nw_ref[...]
                o_ref[0, pl.ds(r0[j], c_sz), h * B_VAL_DIM:(h + 1) * B_VAL_DIM] = o
        for h in hs:
            s_ref[h] = s[h]
        return carry
    lax.fori_loop(0, GDN_G // 2, chunk_pair, 0)


def gated_delta_rule_pallas(proj, conv_w, g, beta, norm_w, bsz, seq):
    nh, dk, dv = B_HEADS, B_KEY_DIM, B_VAL_DIM
    n = seq // CHUNK
    chunked = lambda t: jnp.moveaxis(t.reshape(bsz, nh, n, CHUNK), 1, 2)
    gc = chunked(jnp.cumsum(g.reshape(bsz, nh, n, CHUNK), axis=-1))
    side = pl.BlockSpec((1, GDN_G, nh, CHUNK), lambda b, c: (b, c, 0, 0))
    rows = GDN_G * CHUNK
    width = nh * dk

    def col(name):
        off, w = PROJ_DST[name]
        assert w == width
        return pl.BlockSpec((1, rows, w), lambda b, c: (b, c, off // w))
    cw = jnp.transpose(conv_w.astype(jnp.float32).reshape(CONV_WIDTH, 3, width), (1, 0, 2))
    proj3 = proj.reshape(bsz, seq, proj.shape[-1])
    return pl.pallas_call(
        _gdn_body, grid=(bsz, n // GDN_G),
        in_specs=[col("qb"), col("kb"), col("vb"),
                  pl.BlockSpec((3, CONV_WIDTH, width), lambda b, c: (0, 0, 0)),
                  side, side, pl.BlockSpec((1, dv), lambda b, c: (0, 0))],
        out_specs=pl.BlockSpec((1, rows, nh * dv), lambda b, c: (b, c, 0)),
        out_shape=jax.ShapeDtypeStruct((bsz, seq, nh * dv), jnp.float32),
        scratch_shapes=[pltpu.VMEM((nh, dk, dv), jnp.float32),
                        pltpu.VMEM((3, GDN_PAD + rows, width), jnp.float32),
                        pltpu.VMEM((3, rows, width), jnp.float32)],
        compiler_params=pltpu.CompilerParams(dimension_semantics=("arbitrary", "arbitrary"),
                                             vmem_limit_bytes=GDN_VMEM_LIMIT),
        name="gated_delta_rule")(proj3, proj3, proj3, cw, gc, chunked(beta), norm_w.astype(jnp.float32).reshape(1, dv))


IN_NAMES = ("qa", "ka", "va", "qi", "ki", "wi", "qb", "kb", "vb", "zb", "bb", "ab", "gate_a", "gate_b")
PROJ_ORDER = ("gate_a", "gate_b", "qb", "kb", "vb", "zb", "qi", "qa", "ka", "va", "ki", "wi", "bb", "ab")


def _proj_layout():
    src, off = {}, 0
    for name, w in zip(IN_NAMES, IN_SPLITS):
        src[name] = (off, w)
        off += w
    dst, off = {}, 0
    for name in PROJ_ORDER:
        dst[name] = (off, src[name][1])
        off += src[name][1]
    return src, dst, -(-off // LANES) * LANES


PROJ_SRC, PROJ_DST, PROJ_WIDTH = _proj_layout()
MAIN_WIDTH = PROJ_DST["qi"][0]
PROJ_TM = 512


def permute_w_in(w_in):
    cols = [w_in[:, PROJ_SRC[n][0]:PROJ_SRC[n][0] + PROJ_SRC[n][1]] for n in PROJ_ORDER]
    cols.append(jnp.zeros((w_in.shape[0], PROJ_WIDTH - sum(c.shape[1] for c in cols)), w_in.dtype))
    return jnp.concatenate(cols, axis=1).astype(jnp.bfloat16)


def _modulated_norm(x, scale, shift):
    return x * lax.rsqrt(jnp.mean(x * x, axis=-1, keepdims=True) + EPS) * scale + shift


def _sigmoid(v):
    return 1.0 / (1.0 + jnp.exp(-v))


def _ada_body(c_ref, w_ref, b_ref, o_ref):
    c = c_ref[...]
    o_ref[...] = jnp.dot((c * _sigmoid(c)).astype(jnp.bfloat16), w_ref[...].astype(jnp.bfloat16),
                         preferred_element_type=jnp.float32) + b_ref[...]


def ada_modulation(c, w_ada, b_ada):
    bsz, d = c.shape
    n = w_ada.shape[1]
    tn = D_MODEL
    return pl.pallas_call(
        _ada_body, grid=(n // tn,),
        in_specs=[pl.BlockSpec((bsz, d), lambda j: (0, 0)), pl.BlockSpec((d, tn), lambda j: (0, j)),
                  pl.BlockSpec((1, tn), lambda j: (0, j))],
        out_specs=pl.BlockSpec((bsz, tn), lambda j: (0, j)),
        out_shape=jax.ShapeDtypeStruct((bsz, n), jnp.float32),
        name="ada_modulation")(c, w_ada, b_ada.reshape(1, n))


def _in_proj_body(x_ref, sc_ref, sh_ref, w_ref, cos_ref, sin_ref, dp_ref,
                  o_ref, qit_ref, qat_ref, wt_ref, kit_ref, kat_ref, vat_ref, bg_ref):
    bf, f32 = jnp.bfloat16, jnp.float32
    n1 = _modulated_norm(x_ref[...], sc_ref[0], sh_ref[0]).astype(bf)
    o_ref[...] = jnp.dot(n1, w_ref[:, 0:MAIN_WIDTH], preferred_element_type=f32)
    pa = jnp.dot(n1, w_ref[:, MAIN_WIDTH:PROJ_WIDTH], preferred_element_type=f32)

    def piece(name, width=None):
        off = PROJ_DST[name][0] - MAIN_WIDTH
        return pa[:, off:off + (width or PROJ_DST[name][1])]
    cos, sin = cos_ref[0], sin_ref[0]
    for h, o in enumerate(_rope_t(piece("qi").T, IDX_HEADS, cos, sin)):
        qit_ref[0, h] = o.astype(bf)
    for h, o in enumerate(_rope_t(piece("qa").T, A_HEADS, cos, sin, A_HEAD_DIM ** -0.5)):
        qat_ref[0, h] = o.astype(bf)
    kat = _rope_t(piece("ka").T, A_KV_HEADS, cos, sin)
    vt = piece("va").T
    smt = piece("ki", LANES).T
    kit = _rope_t(smt[0:IDX_DIM], 1, cos, sin)[0]
    for u in range(pa.shape[0] // DSA_TK):
        cols = slice(u * DSA_TK, (u + 1) * DSA_TK)
        for g in range(A_KV_HEADS):
            kat_ref[0, g, u] = kat[g][:, cols].T.astype(bf)
            vat_ref[0, g, u] = vt[g * A_HEAD_DIM:(g + 1) * A_HEAD_DIM, cols].astype(bf)
        kit_ref[0, u] = kit[:, cols].T.astype(bf)
    wt_ref[0] = smt[IDX_DIM:IDX_DIM + IDX_HEADS] * ((IDX_HEADS ** -0.5) * (IDX_DIM ** -0.5))
    b0 = IDX_DIM + IDX_HEADS
    bb, ab = smt[b0:b0 + B_HEADS], smt[b0 + B_HEADS:b0 + 2 * B_HEADS]
    a_log, dt_bias = dp_ref[:, 0:1], dp_ref[:, 1:2]
    bg_ref[0] = jnp.concatenate([_sigmoid(bb), -jnp.exp(a_log) * jax.nn.softplus(ab + dt_bias)], axis=0)


def in_proj(x2, scale, shift, w_perm, positions, a_log, dt_bias, bsz, seq):
    n, d = x2.shape
    tm = PROJ_TM
    per = seq // tm
    sub = tm // DSA_TK
    assert PROJ_ORDER[-4:] == ("ki", "wi", "bb", "ab") and PROJ_DST["ki"][0] + LANES == PROJ_WIDTH
    vec = pl.BlockSpec((1, 1, d), lambda i: (i // per, 0, 0))
    rd = A_HEAD_DIM // ROPE_FRACTION_DEN
    inv_freq = jnp.power(jnp.float32(ROPE_THETA), -jnp.arange(ROPE_HALF, dtype=jnp.float32) * (2.0 / rd))
    ang = positions.astype(jnp.float32)[:, None, :] * inv_freq[None, :, None]
    trig = pl.BlockSpec((1, ROPE_HALF, tm), lambda i: (i // per, 0, i % per))
    bf, f32 = jnp.bfloat16, jnp.float32
    sd = jax.ShapeDtypeStruct
    n_kv = seq // DSA_TK
    return pl.pallas_call(
        _in_proj_body, grid=(n // tm,),
        in_specs=[pl.BlockSpec((tm, d), lambda i: (i, 0)), vec, vec,
                  pl.BlockSpec((d, PROJ_WIDTH), lambda i: (0, 0), pipeline_mode=pl.Buffered(1)),
                  trig, trig, pl.BlockSpec((B_HEADS, 2), lambda i: (0, 0))],
        out_specs=[pl.BlockSpec((tm, MAIN_WIDTH), lambda i: (i, 0)),
                   pl.BlockSpec((1, IDX_HEADS, IDX_DIM, tm), lambda i: (i // per, 0, 0, i % per)),
                   pl.BlockSpec((1, A_HEADS, A_HEAD_DIM, tm), lambda i: (i // per, 0, 0, i % per)),
                   pl.BlockSpec((1, IDX_HEADS, tm), lambda i: (i // per, 0, i % per)),
                   pl.BlockSpec((1, sub, DSA_TK, IDX_DIM), lambda i: (i // per, i % per, 0, 0)),
                   pl.BlockSpec((1, A_KV_HEADS, sub, DSA_TK, A_HEAD_DIM), lambda i: (i // per, 0, i % per, 0, 0)),
                   pl.BlockSpec((1, A_KV_HEADS, sub, A_HEAD_DIM, DSA_TK), lambda i: (i // per, 0, i % per, 0, 0)),
                   pl.BlockSpec((1, 2 * B_HEADS, tm), lambda i: (i // per, 0, i % per))],
        out_shape=[sd((n, MAIN_WIDTH), f32), sd((bsz, IDX_HEADS, IDX_DIM, seq), bf),
                   sd((bsz, A_HEADS, A_HEAD_DIM, seq), bf), sd((bsz, IDX_HEADS, seq), f32),
                   sd((bsz, n_kv, DSA_TK, IDX_DIM), bf), sd((bsz, A_KV_HEADS, n_kv, DSA_TK, A_HEAD_DIM), bf),
                   sd((bsz, A_KV_HEADS, n_kv, A_HEAD_DIM, DSA_TK), bf), sd((bsz, 2 * B_HEADS, seq), f32)],
        compiler_params=pltpu.CompilerParams(dimension_semantics=("arbitrary",), vmem_limit_bytes=PROJ_VMEM_LIMIT),
        name="in_proj")(x2, scale, shift, w_perm, jnp.cos(ang), jnp.sin(ang),
                        jnp.stack([a_log, dt_bias], axis=-1).astype(f32))


def _merge_body(oa_ref, ob_ref, z_ref, ga_ref, gb_ref, x_ref, gt_ref, sc_ref, sh_ref, wpa_ref, wpb_ref, wo_ref,
                h_ref, n2_ref):
    f32, bf = jnp.float32, jnp.bfloat16
    z = z_ref[...]
    ob = ob_ref[...] * (z * _sigmoid(z))
    oa = oa_ref[0].reshape(A_WIDTH, oa_ref.shape[-1]).T
    ya = jnp.dot(oa.astype(bf), wpa_ref[...], preferred_element_type=f32)
    yb = jnp.dot(ob.astype(bf), wpb_ref[...], preferred_element_type=f32)
    merged = _sigmoid(ga_ref[...]) * ya + _sigmoid(gb_ref[...]) * yb
    y1 = jnp.dot(merged.astype(bf), wo_ref[...], preferred_element_type=f32)
    h = x_ref[...] + gt_ref[0] * y1
    h_ref[...] = h
    n2_ref[...] = _modulated_norm(h, sc_ref[0], sh_ref[0])


def merge(o_a, o_b, proj, x2, gt1, scale2, shift2, w_pa, w_pb, w_o, seq):
    n, d = x2.shape
    tm = PROJ_TM
    per = seq // tm
    bf = jnp.bfloat16
    vec = pl.BlockSpec((1, 1, d), lambda i: (i // per, 0, 0))

    def col(name):
        off, w = PROJ_DST[name]
        return pl.BlockSpec((tm, w), lambda i: (i, off // w))
    row = lambda w: pl.BlockSpec((tm, w), lambda i: (i, 0))
    res = lambda a, b: pl.BlockSpec((a, b), lambda i: (0, 0))
    return pl.pallas_call(
        _merge_body, grid=(n // tm,),
        in_specs=[pl.BlockSpec((1, A_HEADS, A_HEAD_DIM, tm), lambda i: (i // per, 0, 0, i % per)), row(B_V_WIDTH), col("zb"), col("gate_a"), col("gate_b"), row(d), vec, vec, vec,
                  res(A_WIDTH, d), res(B_V_WIDTH, d), res(d, d)],
        out_specs=[row(d), row(d)],
        out_shape=[jax.ShapeDtypeStruct((n, d), jnp.float32), jax.ShapeDtypeStruct((n, d), jnp.float32)],
        compiler_params=pltpu.CompilerParams(dimension_semantics=("arbitrary",), vmem_limit_bytes=PROJ_VMEM_LIMIT),
        name="merge")(o_a, o_b, proj, proj, proj, x2, gt1, scale2, shift2,
                      w_pa.astype(bf), w_pb.astype(bf), w_o.astype(bf))


PEER_SLOTS = PEER_HEADS * PEER_TOPK
PEER_TB = 128
HALF_ROWS = 4
HI_MASK = -65536


def pack_table(tab):
    bits = lax.bitcast_convert_type(tab.astype(jnp.bfloat16), jnp.uint16).astype(jnp.uint32)
    half = tab.shape[1] // 2
    word = bits[:, :half] | (bits[:, half:] << 16)
    return lax.bitcast_convert_type(word, jnp.int32).reshape(tab.shape[0] * HALF_ROWS, LANES)


def _table_row(tab_ref, rows, k):
    return _unpack(tab_ref[pl.ds(pl.multiple_of(rows[k], HALF_ROWS), HALF_ROWS), :])


def _unpack(w):
    lo = lax.bitcast_convert_type(lax.shift_left(w, 16), jnp.float32)
    hi = lax.bitcast_convert_type(w & jnp.int32(HI_MASK), jnp.float32)
    return lo, hi


def _peer_u_body(idx_ref, x_ref, gate_ref, tab_ref, o_ref, s_ref, sb_ref):
    def lane_sums(t):
        a = jnp.sum(sb_ref[t].T, axis=0, keepdims=True)
        o_ref[pl.ds(t, 1), :] = 0.5 * a * (1.0 + lax.erf(a * (2.0 ** -0.5))) * gate_ref[pl.ds(t, 1), :]

    sb_ref[0] = jnp.zeros(sb_ref.shape[1:], jnp.float32)

    def tok(t, carry):
        lane_sums(jnp.maximum(t - 1, 0))
        x8 = x_ref[pl.ds(t, 1), :].reshape(2 * HALF_ROWS, LANES)
        xlo, xhi = x8[0:HALF_ROWS], x8[HALF_ROWS:2 * HALF_ROWS]
        rows = idx_ref.at[t]
        for k in range(PEER_SLOTS):
            lo, hi = _table_row(tab_ref, rows, k)
            s_ref[HALF_ROWS * k:HALF_ROWS * (k + 1), :] = lo * xlo + hi * xhi
        s4 = s_ref[pl.ds(0, PEER_SLOTS, stride=HALF_ROWS), :]
        for r in range(1, HALF_ROWS):
            s4 = s4 + s_ref[pl.ds(r, PEER_SLOTS, stride=HALF_ROWS), :]
        sb_ref[t] = s4
        return carry
    lax.fori_loop(0, PEER_TB, tok, 0, unroll=2)
    lane_sums(PEER_TB - 1)


def _peer_v_body(idx_ref, coef_ref, tab_ref, o_ref, cb_ref):
    def spread(t):
        return jnp.broadcast_to(coef_ref[pl.ds(t, 1), :], (PEER_SLOTS, LANES)).T

    def tok(t, weights):
        nxt = spread(jnp.minimum(t + 1, PEER_TB - 1))
        cb_ref[...] = weights
        nacc = 2
        acc = [jnp.zeros((HALF_ROWS, LANES), jnp.float32) for _ in range(2 * nacc)]
        rows = idx_ref.at[t]
        for k in range(PEER_SLOTS):
            lo, hi = _table_row(tab_ref, rows, k)
            c = jnp.broadcast_to(cb_ref[k:k + 1, :], (HALF_ROWS, LANES))
            a = k % nacc
            acc[2 * a] = acc[2 * a] + c * lo
            acc[2 * a + 1] = acc[2 * a + 1] + c * hi
        half = HALF_ROWS * LANES
        o_ref[pl.ds(t, 1), 0:half] = (acc[0] + acc[2]).reshape(1, half)
        o_ref[pl.ds(t, 1), half:2 * half] = (acc[1] + acc[3]).reshape(1, half)
        return nxt
    lax.fori_loop(0, PEER_TB, tok, spread(0))


def _table_spec():
    return pl.BlockSpec((PEER_N_EXPERTS * HALF_ROWS, LANES), lambda i: (0, 0), pipeline_mode=pl.Buffered(1))


def peer_u(idx, x, gates, tab):
    n = idx.shape[0]
    tb = PEER_TB
    return pl.pallas_call(
        _peer_u_body, grid=(n // tb,),
        in_specs=[pl.BlockSpec((tb, PEER_SLOTS), lambda i: (i, 0), memory_space=pltpu.SMEM),
                  pl.BlockSpec((tb, x.shape[1]), lambda i: (i, 0)),
                  pl.BlockSpec((tb, PEER_SLOTS), lambda i: (i, 0)),
                  _table_spec()],
        out_specs=pl.BlockSpec((tb, PEER_SLOTS), lambda i: (i, 0)),
        out_shape=jax.ShapeDtypeStruct((n, PEER_SLOTS), jnp.float32),
        scratch_shapes=[pltpu.VMEM((HALF_ROWS * PEER_SLOTS, LANES), jnp.float32),
                        pltpu.VMEM((tb, PEER_SLOTS, LANES), jnp.float32)],
        compiler_params=pltpu.CompilerParams(dimension_semantics=("arbitrary",), vmem_limit_bytes=PEER_VMEM_LIMIT),
        name="peer_u")(idx, x, gates, tab)


def peer_v(idx, coef, tab):
    n = idx.shape[0]
    tb = PEER_TB
    return pl.pallas_call(
        _peer_v_body, grid=(n // tb,),
        in_specs=[pl.BlockSpec((tb, PEER_SLOTS), lambda i: (i, 0), memory_space=pltpu.SMEM),
                  pl.BlockSpec((tb, PEER_SLOTS), lambda i: (i, 0)),
                  _table_spec()],
        out_specs=pl.BlockSpec((tb, 2 * HALF_ROWS * LANES), lambda i: (i, 0)),
        out_shape=jax.ShapeDtypeStruct((n, 2 * HALF_ROWS * LANES), jnp.float32),
        scratch_shapes=[pltpu.VMEM((PEER_SLOTS, LANES), jnp.float32)],
        compiler_params=pltpu.CompilerParams(dimension_semantics=("arbitrary",), vmem_limit_bytes=PEER_VMEM_LIMIT),
        name="peer_v")(idx, coef, tab)


PEER_TT = 256


def _extract_top(ref, n_out, rid=None):
    rows, t = ref.shape
    if rid is None:
        rid = lax.broadcasted_iota(jnp.int32, (rows, t), 0)
    vals, idxs = [], []
    for _ in range(n_out):
        s = ref[...]
        m = jnp.max(s, axis=0, keepdims=True)
        ix = jnp.min(jnp.where(s == m, rid, jnp.int32(PAD_ID)), axis=0, keepdims=True)
        ref[...] = jnp.where(rid == ix, -jnp.inf, s)
        vals.append(m)
        idxs.append(ix)
    return vals, idxs


def _peer_route_body(x_ref, wq_ref, sk_ref, cid_ref, idx_ref, gate_ref, s_ref, cand_ref, v_ref, i_ref, et_ref, gt_ref):
    kk, nk = PEER_TOPK, PEER_N_KEYS
    f32 = jnp.float32
    q = jnp.dot(x_ref[...].astype(jnp.bfloat16), wq_ref[...], preferred_element_type=f32).astype(jnp.bfloat16)
    nt = (((1,), (1,)), ((), ()))
    row16 = lax.broadcasted_iota(jnp.int32, (kk, PEER_TT), 0)
    for h in range(PEER_HEADS):
        qh = q[:, h * PEER_KEY_DIM:(h + 1) * PEER_KEY_DIM]
        s_ref[...] = lax.dot_general(sk_ref[h], qh, nt, preferred_element_type=f32)
        for p in range(2):
            vals, idxs = _extract_top(s_ref.at[p * nk:(p + 1) * nk, :], kk)
            for i in range(kk):
                v_ref[p, i:i + 1, :] = vals[i]
                i_ref[p, i:i + 1, :] = idxs[i]
        off = 0
        for i in range(kk):
            n_j = kk // (i + 1)
            cand_ref[off:off + n_j, :] = v_ref[0, i:i + 1, :] + v_ref[1, 0:n_j, :]
            off += n_j
        cand_ref[off:, :] = jnp.full((cand_ref.shape[0] - off, PEER_TT), -jnp.inf, f32)
        vals, cis = _extract_top(cand_ref, kk, cid_ref[...])
        i1, i2 = i_ref[0], i_ref[1]
        es = [jnp.exp(v - vals[0]) for v in vals]
        den = es[0]
        for e in es[1:]:
            den = den + e
        for k in range(kk):
            ci = cis[k]
            e1 = jnp.sum(jnp.where(row16 == lax.shift_right_logical(ci, kk.bit_length() - 1), i1, 0), axis=0, keepdims=True)
            e2 = jnp.sum(jnp.where(row16 == (ci & (kk - 1)), i2, 0), axis=0, keepdims=True)
            et_ref[h * kk + k:h * kk + k + 1, :] = (e1 * nk + e2) * HALF_ROWS
            gt_ref[h * kk + k:h * kk + k + 1, :] = es[k] / den
    idx_ref[...] = lax.bitcast_convert_type(lax.bitcast_convert_type(et_ref[...], f32).T, jnp.int32)
    gate_ref[...] = gt_ref[...].T


def _pair_cells():
    kk = PEER_TOPK
    ids = [i * kk + j for i in range(kk) for j in range(kk // (i + 1))]
    return ids + [PAD_ID] * (-len(ids) % SUBLANES)


def peer_route(xn, wq, subkeys):
    n = xn.shape[0]
    tt = PEER_TT
    cells = _pair_cells()
    cid = jnp.broadcast_to(jnp.asarray(cells, jnp.int32)[:, None], (len(cells), tt))
    half = PEER_KEY_DIM // 2
    z = jnp.zeros((PEER_HEADS, PEER_N_KEYS, half), subkeys.dtype)
    skbd = jnp.concatenate([jnp.concatenate([subkeys[:, 0], z], axis=-1),
                            jnp.concatenate([z, subkeys[:, 1]], axis=-1)], axis=1).astype(jnp.bfloat16)
    return pl.pallas_call(
        _peer_route_body, grid=(n // tt,),
        in_specs=[pl.BlockSpec((tt, D_MODEL), lambda i: (i, 0)),
                  pl.BlockSpec((D_MODEL, PEER_HEADS * PEER_KEY_DIM), lambda i: (0, 0)),
                  pl.BlockSpec((PEER_HEADS, 2 * PEER_N_KEYS, PEER_KEY_DIM), lambda i: (0, 0, 0)),
                  pl.BlockSpec((len(cells), tt), lambda i: (0, 0))],
        out_specs=[pl.BlockSpec((tt, PEER_SLOTS), lambda i: (i, 0)), pl.BlockSpec((tt, PEER_SLOTS), lambda i: (i, 0))],
        out_shape=[jax.ShapeDtypeStruct((n, PEER_SLOTS), jnp.int32), jax.ShapeDtypeStruct((n, PEER_SLOTS), jnp.float32)],
        scratch_shapes=[pltpu.VMEM((2 * PEER_N_KEYS, tt), jnp.float32),
                        pltpu.VMEM((len(cells), tt), jnp.float32),
                        pltpu.VMEM((2, PEER_TOPK, tt), jnp.float32),
                        pltpu.VMEM((2, PEER_TOPK, tt), jnp.int32),
                        pltpu.VMEM((PEER_SLOTS, tt), jnp.int32),
                        pltpu.VMEM((PEER_SLOTS, tt), jnp.float32)],
        compiler_params=pltpu.CompilerParams(dimension_semantics=("arbitrary",), vmem_limit_bytes=ROUTE_VMEM_LIMIT),
        name="peer_route")(xn, wq.astype(jnp.bfloat16), skbd, cid)


def peer_channel_mixer(xn, wq, subkeys, u_tab, v_tab):
    bsz, seq, d = xn.shape
    n_tok = bsz * seq
    x2 = xn.reshape(n_tok, d)
    idx, gates = peer_route(x2, wq, subkeys)
    coef = peer_u(idx, x2, gates, pack_table(u_tab))
    out = peer_v(idx, coef, pack_table(v_tab))
    return out.reshape(bsz, seq, d)


def _residual_body(h_ref, y_ref, gt_ref, o_ref):
    o_ref[...] = h_ref[...] + gt_ref[0] * y_ref[...]


def _final_norm_body(h_ref, y_ref, gt_ref, g_ref, o_ref):
    h = h_ref[...] + gt_ref[0] * y_ref[...]
    o_ref[...] = h * lax.rsqrt(jnp.mean(h * h, axis=-1, keepdims=True) + EPS) * g_ref[...]


def gated_residual(h2, y2, gt, seq, gain=None):
    n, d = h2.shape
    tm = min(1024, seq)
    per = seq // tm
    row = pl.BlockSpec((tm, d), lambda i: (i, 0))
    specs = [row, row, pl.BlockSpec((1, 1, d), lambda i: (i // per, 0, 0))]
    args = [h2, y2, gt]
    if gain is not None:
        specs.append(pl.BlockSpec((1, d), lambda i: (0, 0)))
        args.append(gain.astype(jnp.float32).reshape(1, d))
    return pl.pallas_call(
        _residual_body if gain is None else _final_norm_body, grid=(n // tm,),
        in_specs=specs, out_specs=row, out_shape=jax.ShapeDtypeStruct((n, d), h2.dtype),
        name="gated_residual" if gain is None else "final_norm")(*args)


def kernel(x, c, positions, w_ada, b_ada, w_in, conv_w, a_log, dt_bias, norm_b_w,
           w_pa, w_pb, w_o, peer_wq, peer_subkeys, peer_u, peer_v, final_norm_w):
    bsz, seq, d = x.shape
    h = x.reshape(bsz * seq, d)
    for layer in range(DEPTH):
        mod = ada_modulation(c, w_ada[layer], b_ada[layer])
        sh1, sc1, gt1, sh2, sc2, gt2 = [m.reshape(bsz, 1, d) for m in jnp.split(mod, 6, axis=-1)]
        proj, qit, qat, w, kit, kat, vat, bg = in_proj(h, 1.0 + sc1, sh1, permute_w_in(w_in[layer]), positions,
                                                       a_log[layer], dt_bias[layer], bsz, seq)
        o_a = dsa_attention_pallas(qit, w, qat, kit, kat, vat)
        o_b = gated_delta_rule_pallas(proj, conv_w[layer], bg[:, B_HEADS:], bg[:, :B_HEADS], norm_b_w[layer],
                                      bsz, seq)
        h, n2 = merge(o_a, o_b.reshape(bsz * seq, B_V_WIDTH), proj, h, gt1,
                      1.0 + sc2, sh2, w_pa[layer], w_pb[layer], w_o[layer], seq)
        y2 = peer_channel_mixer(n2.reshape(bsz, seq, d), peer_wq[layer], peer_subkeys[layer], peer_u[layer],
                                peer_v[layer])
        last = layer == DEPTH - 1
        h = gated_residual(h, y2.reshape(bsz * seq, d), gt2, seq, final_norm_w if last else None)
    return h.reshape(bsz, seq, d)
```

```python
import functools

import jax, jax.numpy as jnp
from jax import lax
from jax.experimental import pallas as pl
from jax.experimental.pallas import tpu as pltpu

D_MODEL = 1024
DEPTH = 1

LANES = 128
SUBLANES = 8
MIB = 1024 * 1024
DSA_VMEM_LIMIT = 48 * MIB
GDN_VMEM_LIMIT = 40 * MIB
PROJ_VMEM_LIMIT = 48 * MIB
PEER_VMEM_LIMIT = 48 * MIB
ROUTE_VMEM_LIMIT = 32 * MIB

A_HEADS = 8
A_KV_HEADS = 2
A_HEAD_DIM = 64
IDX_HEADS = 16
IDX_DIM = 64
IDX_TOPK_MAX = 256
B_HEADS = 8
B_KEY_DIM = 64
B_VAL_DIM = 64
CONV_WIDTH = 4
CHUNK = 64
ROPE_THETA = 500000.0
ROPE_FRACTION_DEN = 4
PEER_HEADS = 8
PEER_KEY_DIM = 128
PEER_N_KEYS = 128
PEER_N_EXPERTS = PEER_N_KEYS * PEER_N_KEYS
PEER_TOPK = 16
EPS = 1e-6

A_WIDTH = A_HEADS * A_HEAD_DIM
KV_WIDTH = A_KV_HEADS * A_HEAD_DIM
B_QK_WIDTH = B_HEADS * B_KEY_DIM
B_V_WIDTH = B_HEADS * B_VAL_DIM
IN_SPLITS = (A_WIDTH, KV_WIDTH, KV_WIDTH, IDX_HEADS * IDX_DIM, IDX_DIM, IDX_HEADS,
             B_QK_WIDTH, B_QK_WIDTH, B_V_WIDTH, B_V_WIDTH, B_HEADS, B_HEADS, D_MODEL, D_MODEL)


DSA_TQ = 256
DSA_TK = 256
INT_MIN = -2**31
I16_MIN = -2**15
NEG_BIG = -1e30
PAD_ID = 2**30


def _dsa_body(topk, idx_bits, qit_ref, w_ref, qat_ref, ki_ref, ka_ref, vat_ref, o_ref, key_ref, hi_ref, lo_ref):
    tq, tk = DSA_TQ, DSA_TK
    qb = pl.program_id(1)
    n_kv = qb + 1
    t_glob = qb * tq + lax.broadcasted_iota(jnp.int32, (1, tq), 1)
    row = lax.broadcasted_iota(jnp.int32, (tk, 1), 0)
    f32 = jnp.float32

    def p1(j, carry):
        kt = ki_ref[0, j]
        score = jnp.zeros((tk, tq), f32)
        for h in range(IDX_HEADS):
            lt = jnp.dot(kt, qit_ref[0, h], preferred_element_type=f32)
            score = score + w_ref[0, h:h + 1, :] * jnp.maximum(lt, 0.0)
        bits = lax.bitcast_convert_type(score + 0.0, jnp.int32)
        skey = jnp.where(bits >= 0, bits, bits ^ jnp.int32(0x7FFFFFFF))
        skey = jnp.where(j * tk + row <= t_glob, skey, jnp.int32(INT_MIN))
        key_ref[j] = skey
        hi_ref[j] = lax.shift_right_arithmetic(skey, 16).astype(jnp.int16)
        return carry
    lax.fori_loop(0, n_kv, p1, 0)

    key_ref[n_kv] = jnp.full((tk, tq), INT_MIN, jnp.int32)
    hi_ref[n_kv] = jnp.full((tk, tq), I16_MIN, jnp.int16)
    lo_ref[n_kv] = jnp.full((tk, tq), I16_MIN, jnp.int16)

    def count(pred):
        def body(jj, acc):
            for j in (2 * jj, 2 * jj + 1):
                hit = jnp.where(pred(key_ref[j], j * tk + row), 1.0, 0.0)
                acc = acc + jnp.sum(hit.reshape(tk // SUBLANES, SUBLANES, tq), axis=0)
            return acc
        acc = lax.fori_loop(0, (n_kv + 1) // 2, body, jnp.zeros((SUBLANES, tq), f32))
        return jnp.sum(acc, axis=0, keepdims=True)

    kf = jnp.float32(topk)
    half = 16
    pack = 2 * SUBLANES

    def count16(ref, pred):
        def body(jj, acc):
            for j in (2 * jj, 2 * jj + 1):
                hit = jnp.where(pred(ref[j]), jnp.bfloat16(1), jnp.bfloat16(0)).reshape(tk // pack, pack, tq)
                part = hit[0]
                for g in range(1, tk // pack):
                    part = part + hit[g]
                acc = acc + part.astype(f32)
            return acc
        acc = lax.fori_loop(0, (n_kv + 1) // 2, body, jnp.zeros((pack, tq), f32))
        return jnp.sum(acc, axis=0, keepdims=True)

    def search16(ref, base):
        def step(i, u):
            cand_u = u | lax.shift_left(jnp.int32(1), half - 1 - i)
            cand = (cand_u + I16_MIN).astype(jnp.int16)
            c = base + count16(ref, lambda v: v >= cand)
            return jnp.where(c >= kf, cand_u, u)
        return lax.fori_loop(0, half, step, jnp.zeros((1, tq), jnp.int32))

    hi_u = search16(hi_ref, jnp.zeros((1, tq), f32))
    hi_s = hi_u + I16_MIN
    hi_s16 = hi_s.astype(jnp.int16)
    above = count16(hi_ref, lambda v: v > hi_s16)

    def lows(j, carry):
        lo = ((key_ref[j] & 0xFFFF) + I16_MIN).astype(jnp.int16)
        lo_ref[j] = jnp.where(hi_ref[j] == hi_s16, lo, jnp.int16(I16_MIN))
        return carry
    lax.fori_loop(0, n_kv, lows, 0)
    lo_u = search16(lo_ref, above)
    kth = hi_s * 65536 + lo_u
    c_gt = count(lambda k, s: k > kth)
    c_ge = count(lambda k, s: k >= kth)
    short = kth == jnp.int32(INT_MIN)
    x0 = jnp.where(short, jnp.int32(-1), jnp.int32(PAD_ID))
    need = kf - c_gt
    has_tie = jnp.max(jnp.where(jnp.logical_and(c_ge > kf, jnp.logical_not(short)), 1.0, 0.0)) > 0.0

    def tie_search():
        def step(i, x):
            bit = lax.shift_left(jnp.int32(1), idx_bits - 1 - i)
            probe = x + bit - 1
            c = count(lambda k, s: jnp.logical_and(k == kth, s <= probe))
            return jnp.where(c < need, x + bit, x)
        x = lax.fori_loop(0, idx_bits, step, jnp.zeros((1, tq), jnp.int32))
        return jnp.where(short, jnp.int32(-1), x)
    x_lim = lax.cond(has_tie, tie_search, lambda: x0)

    rep = A_HEADS // A_KV_HEADS
    hs = range(A_HEADS)

    def p3(j, carry):
        m, l, acc = carry
        skey = key_ref[j]
        s_idx = j * tk + row
        sel = jnp.logical_or(skey > kth, jnp.logical_and(skey == kth, s_idx <= x_lim))
        kt = [ka_ref[0, g, j] for g in range(A_KV_HEADS)]
        vt = [vat_ref[0, g, j] for g in range(A_KV_HEADS)]
        s = [jnp.where(sel, jnp.dot(kt[h // rep], qat_ref[0, h], preferred_element_type=f32), NEG_BIG) for h in hs]
        m_new = [jnp.maximum(m[h], jnp.max(s[h], axis=0, keepdims=True)) for h in hs]
        alpha = [jnp.exp(m[h] - m_new[h]) for h in hs]
        p = [jnp.exp(s[h] - m_new[h]) for h in hs]
        l_new = [alpha[h] * l[h] + jnp.sum(p[h], axis=0, keepdims=True) for h in hs]
        acc_new = [alpha[h] * acc[h] + jnp.dot(vt[h // rep], p[h].astype(jnp.bfloat16), preferred_element_type=f32)
                   for h in hs]
        return tuple(m_new), tuple(l_new), tuple(acc_new)

    init = (tuple(jnp.full((1, tq), NEG_BIG, f32) for _ in hs), tuple(jnp.zeros((1, tq), f32) for _ in hs),
            tuple(jnp.zeros((A_HEAD_DIM, tq), f32) for _ in hs))
    _, l_fin, acc_fin = lax.fori_loop(0, n_kv, p3, init)
    for h in hs:
        o_ref[0, h] = acc_fin[h] / l_fin[h]


ROPE_HALF = A_HEAD_DIM // ROPE_FRACTION_DEN // 2


def _rope_t(xt, n_heads, cos, sin, scale=None):
    outs = []
    for h in range(n_heads):
        b = h * A_HEAD_DIM
        x1, x2, rest = xt[b:b + ROPE_HALF], xt[b + ROPE_HALF:b + 2 * ROPE_HALF], xt[b + 2 * ROPE_HALF:b + A_HEAD_DIM]
        o = jnp.concatenate([x1 * cos - x2 * sin, x2 * cos + x1 * sin, rest], axis=0)
        outs.append(o if scale is None else o * scale)
    return outs


def dsa_attention_pallas(qit, w, qat, kit, kat, vat):
    bsz, seq = qit.shape[0], qit.shape[-1]
    tq, tk = DSA_TQ, DSA_TK
    topk = min(IDX_TOPK_MAX, seq // 4)
    n_kv = seq // tk
    return pl.pallas_call(
        functools.partial(_dsa_body, topk, (seq - 1).bit_length()),
        grid=(bsz, seq // tq),
        in_specs=[
            pl.BlockSpec((1, IDX_HEADS, IDX_DIM, tq), lambda b, q: (b, 0, 0, q)),
            pl.BlockSpec((1, IDX_HEADS, tq), lambda b, q: (b, 0, q)),
            pl.BlockSpec((1, A_HEADS, A_HEAD_DIM, tq), lambda b, q: (b, 0, 0, q)),
            pl.BlockSpec((1, n_kv, tk, IDX_DIM), lambda b, q: (b, 0, 0, 0)),
            pl.BlockSpec((1, A_KV_HEADS, n_kv, tk, A_HEAD_DIM), lambda b, q: (b, 0, 0, 0, 0)),
            pl.BlockSpec((1, A_KV_HEADS, n_kv, A_HEAD_DIM, tk), lambda b, q: (b, 0, 0, 0, 0)),
        ],
        out_specs=pl.BlockSpec((1, A_HEADS, A_HEAD_DIM, tq), lambda b, q: (b, 0, 0, q)),
        out_shape=jax.ShapeDtypeStruct((bsz, A_HEADS, A_HEAD_DIM, seq), jnp.float32),
        scratch_shapes=[
            pltpu.VMEM((n_kv + 1, tk, tq), jnp.int32),
            pltpu.VMEM((n_kv + 1, tk, tq), jnp.int16),
            pltpu.VMEM((n_kv + 1, tk, tq), jnp.int16),
        ],
        compiler_params=pltpu.CompilerParams(dimension_semantics=("arbitrary", "arbitrary"),
                                             vmem_limit_bytes=DSA_VMEM_LIMIT),
        name="dsa_attention",
    )(qit, w, qat, kit, kat, vat)


GDN_G = 8
GDN_BASE = 8
GDN_PAD = 8


def _gdn_body(q_ref, k_ref, v_ref, cw_ref, gc_ref, bt_ref, nw_ref, o_ref, s_ref, xs_ref, cs_ref):
    c_sz = CHUNK
    f32, bf = jnp.float32, jnp.bfloat16

    rows = GDN_G * c_sz
    halo = CONV_WIDTH - 1

    @pl.when(pl.program_id(1) == 0)
    def _():
        s_ref[...] = jnp.zeros(s_ref.shape, f32)
        xs_ref[:, 0:GDN_PAD, :] = jnp.zeros((3, GDN_PAD, xs_ref.shape[-1]), f32)

    for j, ref in enumerate((q_ref, k_ref, v_ref)):
        xs_ref[j, GDN_PAD:GDN_PAD + rows, :] = ref[0]
        acc = xs_ref[j, GDN_PAD - halo:GDN_PAD - halo + rows, :] * cw_ref[j, 0:1, :]
        for i in range(1, CONV_WIDTH):
            acc = acc + xs_ref[j, GDN_PAD - halo + i:GDN_PAD - halo + i + rows, :] * cw_ref[j, i:i + 1, :]
        cs_ref[j] = acc * _sigmoid(acc)
        xs_ref[j, GDN_PAD - halo:GDN_PAD, :] = xs_ref[j, GDN_PAD + rows - halo:GDN_PAD + rows, :]

    ri = lax.broadcasted_iota(jnp.int32, (c_sz, c_sz), 0)
    ci = lax.broadcasted_iota(jnp.int32, (c_sz, c_sz), 1)
    incl, strict = ri >= ci, ri > ci
    eye = jnp.where(ri == ci, 1.0, 0.0).astype(f32)
    blk = lambda w: (ri // w) == (ci // w)
    diag8 = blk(GDN_BASE)
    sub_blocks = []
    w = GDN_BASE
    while w < c_sz:
        sub_blocks.append(jnp.logical_and(blk(2 * w), jnp.logical_not(blk(w))))
        w *= 2
    nt = (((1,), (1,)), ((), ()))
    dot = lambda a, b: jnp.dot(a.astype(bf), b.astype(bf), preferred_element_type=f32)
    dot_nt = lambda a, b: lax.dot_general(a.astype(bf), b.astype(bf), nt, preferred_element_type=f32)

    def split(a):
        hi = a.astype(bf)
        return hi, (a - hi.astype(f32)).astype(bf)

    def hp(a, b):
        a_hi, a_lo = split(a)
        b_hi, b_lo = split(b)
        mm = lambda u, w: jnp.dot(u, w, preferred_element_type=f32)
        return mm(a_hi, b_hi) + (mm(a_hi, b_lo) + mm(a_lo, b_hi))

    hs = range(B_HEADS)

    def chunk_pair(i, carry):
        cs = [2 * i, 2 * i + 1]
        r0 = [pl.multiple_of(c * c_sz, c_sz) for c in cs]
        items = [(j, h) for j in range(2) for h in hs]
        hd = lambda a, j, h: cs_ref[a, pl.ds(r0[j], c_sz), h * B_KEY_DIM:(h + 1) * B_KEY_DIM]
        l2n = lambda t: t * lax.rsqrt(jnp.sum(t * t, axis=-1, keepdims=True) + EPS)
        q = [l2n(hd(0, j, h)) * (B_KEY_DIM ** -0.5) for j, h in items]
        k = [l2n(hd(1, j, h)) for j, h in items]
        v = [hd(2, j, h) for j, h in items]
        gc8 = [gc_ref[0, c] for c in cs]
        gct = [g8.T for g8 in gc8]
        btt = [bt_ref[0, c].T for c in cs]
        gcr = [gc8[j][h:h + 1, :] for j, h in items]
        n_it = range(len(items))
        gcc = [gct[j][:, h:h + 1] for j, h in items]
        beta = [btt[j][:, h:h + 1] for j, h in items]
        decay = [jnp.exp(jnp.where(incl, gcc[n] - gcr[n], -jnp.inf)) for n in n_it]
        kb = [k[n] * beta[n] for n in n_it]
        vb = [v[n] * beta[n] for n in n_it]
        low = [jnp.where(strict, dot_nt(kb[n], k[n]) * decay[n], 0.0) for n in n_it]
        dg = [jnp.where(diag8, low[n], 0.0) for n in n_it]
        t = [eye - dg[n] for n in n_it]
        p = [hp(dg[n], dg[n]) for n in n_it]
        t = [hp(t[n], eye + p[n]) for n in n_it]
        p = [hp(p[n], p[n]) for n in n_it]
        t = [hp(t[n], eye + p[n]) for n in n_it]
        for below in sub_blocks:
            lb = [jnp.where(below, low[n], 0.0) for n in n_it]
            lt = [hp(lb[n], t[n]) for n in n_it]
            t = [t[n] - hp(t[n], lt[n]) for n in n_it]
        u = [dot(t[n], vb[n]) for n in n_it]
        kcd = [dot(t[n], kb[n] * jnp.exp(gcc[n])) for n in n_it]
        intra = [dot_nt(q[n], k[n]) * decay[n] for n in n_it]
        qg = [q[n] * jnp.exp(gcc[n]) for n in n_it]
        glast = [g[:, c_sz - 1:c_sz] for g in gcr]
        kdt = [(k[n] * jnp.exp(glast[n] - gcc[n])).T for n in n_it]
        s = [s_ref[h] for h in hs]
        for j in range(2):
            ix = [j * B_HEADS + h for h in hs]
            v_new = [u[ix[h]] - dot(kcd[ix[h]], s[h]) for h in hs]
            out = [dot(qg[ix[h]], s[h]) + dot(intra[ix[h]], v_new[h]) for h in hs]
            s = [s[h] * jnp.exp(glast[ix[h]]) + dot(kdt[ix[h]], v_new[h]) for h in hs]
            for h in hs:
                o = out[h]
                o = o * lax.rsqrt(jnp.mean(o * o, axis=-1, keepdims=True) + EPS) * nw_ref[...]
                o_ref[0, pl.ds(r0[j], c_sz), h * B_VAL_DIM:(h + 1) * B_VAL_DIM] = o
        for h in hs:
            s_ref[h] = s[h]
        return carry
    lax.fori_loop(0, GDN_G // 2, chunk_pair, 0)


def gated_delta_rule_pallas(proj, conv_w, g, beta, norm_w, bsz, seq):
    nh, dk, dv = B_HEADS, B_KEY_DIM, B_VAL_DIM
    n = seq // CHUNK
    chunked = lambda t: jnp.moveaxis(t.reshape(bsz, nh, n, CHUNK), 1, 2)
    gc = chunked(jnp.cumsum(g.reshape(bsz, nh, n, CHUNK), axis=-1))
    side = pl.BlockSpec((1, GDN_G, nh, CHUNK), lambda b, c: (b, c, 0, 0))
    rows = GDN_G * CHUNK
    width = nh * dk

    def col(name):
        off, w = PROJ_DST[name]
        assert w == width
        return pl.BlockSpec((1, rows, w), lambda b, c: (b, c, off // w))
    cw = jnp.transpose(conv_w.astype(jnp.float32).reshape(CONV_WIDTH, 3, width), (1, 0, 2))
    proj3 = proj.reshape(bsz, seq, proj.shape[-1])
    return pl.pallas_call(
        _gdn_body, grid=(bsz, n // GDN_G),
        in_specs=[col("qb"), col("kb"), col("vb"),
                  pl.BlockSpec((3, CONV_WIDTH, width), lambda b, c: (0, 0, 0)),
                  side, side, pl.BlockSpec((1, dv), lambda b, c: (0, 0))],
        out_specs=pl.BlockSpec((1, rows, nh * dv), lambda b, c: (b, c, 0)),
        out_shape=jax.ShapeDtypeStruct((bsz, seq, nh * dv), jnp.float32),
        scratch_shapes=[pltpu.VMEM((nh, dk, dv), jnp.float32),
                        pltpu.VMEM((3, GDN_PAD + rows, width), jnp.float32),
                        pltpu.VMEM((3, rows, width), jnp.float32)],
        compiler_params=pltpu.CompilerParams(dimension_semantics=("arbitrary", "arbitrary"),
                                             vmem_limit_bytes=GDN_VMEM_LIMIT),
        name="gated_delta_rule")(proj3, proj3, proj3, cw, gc, chunked(beta), norm_w.astype(jnp.float32).reshape(1, dv))


IN_NAMES = ("qa", "ka", "va", "qi", "ki", "wi", "qb", "kb", "vb", "zb", "bb", "ab", "gate_a", "gate_b")
PROJ_ORDER = ("gate_a", "gate_b", "qb", "kb", "vb", "zb", "qi", "qa", "ka", "va", "ki", "wi", "bb", "ab")


def _proj_layout():
    src, off = {}, 0
    for name, w in zip(IN_NAMES, IN_SPLITS):
        src[name] = (off, w)
        off += w
    dst, off = {}, 0
    for name in PROJ_ORDER:
        dst[name] = (off, src[name][1])
        off += src[name][1]
    return src, dst, -(-off // LANES) * LANES


PROJ_SRC, PROJ_DST, PROJ_WIDTH = _proj_layout()
MAIN_WIDTH = PROJ_DST["qi"][0]
PROJ_TM = 512


def permute_w_in(w_in):
    cols = [w_in[:, PROJ_SRC[n][0]:PROJ_SRC[n][0] + PROJ_SRC[n][1]] for n in PROJ_ORDER]
    cols.append(jnp.zeros((w_in.shape[0], PROJ_WIDTH - sum(c.shape[1] for c in cols)), w_in.dtype))
    return jnp.concatenate(cols, axis=1).astype(jnp.bfloat16)


def _modulated_norm(x, scale, shift):
    return x * lax.rsqrt(jnp.mean(x * x, axis=-1, keepdims=True) + EPS) * scale + shift


def _sigmoid(v):
    return 1.0 / (1.0 + jnp.exp(-v))


def _ada_body(c_ref, w_ref, b_ref, o_ref):
    c = c_ref[...]
    o_ref[...] = jnp.dot((c * _sigmoid(c)).astype(jnp.bfloat16), w_ref[...].astype(jnp.bfloat16),
                         preferred_element_type=jnp.float32) + b_ref[...]


def ada_modulation(c, w_ada, b_ada):
    bsz, d = c.shape
    n = w_ada.shape[1]
    tn = D_MODEL
    return pl.pallas_call(
        _ada_body, grid=(n // tn,),
        in_specs=[pl.BlockSpec((bsz, d), lambda j: (0, 0)), pl.BlockSpec((d, tn), lambda j: (0, j)),
                  pl.BlockSpec((1, tn), lambda j: (0, j))],
        out_specs=pl.BlockSpec((bsz, tn), lambda j: (0, j)),
        out_shape=jax.ShapeDtypeStruct((bsz, n), jnp.float32),
        name="ada_modulation")(c, w_ada, b_ada.reshape(1, n))


def _in_proj_body(x_ref, sc_ref, sh_ref, w_ref, cos_ref, sin_ref, dp_ref,
                  o_ref, qit_ref, qat_ref, wt_ref, kit_ref, kat_ref, vat_ref, bg_ref):
    bf, f32 = jnp.bfloat16, jnp.float32
    n1 = _modulated_norm(x_ref[...], sc_ref[0], sh_ref[0]).astype(bf)
    o_ref[...] = jnp.dot(n1, w_ref[:, 0:MAIN_WIDTH], preferred_element_type=f32)
    pa = jnp.dot(n1, w_ref[:, MAIN_WIDTH:PROJ_WIDTH], preferred_element_type=f32)

    def piece(name, width=None):
        off = PROJ_DST[name][0] - MAIN_WIDTH
        return pa[:, off:off + (width or PROJ_DST[name][1])]
    cos, sin = cos_ref[0], sin_ref[0]
    for h, o in enumerate(_rope_t(piece("qi").T, IDX_HEADS, cos, sin)):
        qit_ref[0, h] = o.astype(bf)
    for h, o in enumerate(_rope_t(piece("qa").T, A_HEADS, cos, sin, A_HEAD_DIM ** -0.5)):
        qat_ref[0, h] = o.astype(bf)
    kat = _rope_t(piece("ka").T, A_KV_HEADS, cos, sin)
    vt = piece("va").T
    smt = piece("ki", LANES).T
    kit = _rope_t(smt[0:IDX_DIM], 1, cos, sin)[0]
    for u in range(pa.shape[0] // DSA_TK):
        cols = slice(u * DSA_TK, (u + 1) * DSA_TK)
        for g in range(A_KV_HEADS):
            kat_ref[0, g, u] = kat[g][:, cols].T.astype(bf)
            vat_ref[0, g, u] = vt[g * A_HEAD_DIM:(g + 1) * A_HEAD_DIM, cols].astype(bf)
        kit_ref[0, u] = kit[:, cols].T.astype(bf)
    wt_ref[0] = smt[IDX_DIM:IDX_DIM + IDX_HEADS] * ((IDX_HEADS ** -0.5) * (IDX_DIM ** -0.5))
    b0 = IDX_DIM + IDX_HEADS
    bb, ab = smt[b0:b0 + B_HEADS], smt[b0 + B_HEADS:b0 + 2 * B_HEADS]
    a_log, dt_bias = dp_ref[:, 0:1], dp_ref[:, 1:2]
    bg_ref[0] = jnp.concatenate([_sigmoid(bb), -jnp.exp(a_log) * jax.nn.softplus(ab + dt_bias)], axis=0)


def in_proj(x2, scale, shift, w_perm, positions, a_log, dt_bias, bsz, seq):
    n, d = x2.shape
    tm = PROJ_TM
    per = seq // tm
    sub = tm // DSA_TK
    assert PROJ_ORDER[-4:] == ("ki", "wi", "bb", "ab") and PROJ_DST["ki"][0] + LANES == PROJ_WIDTH
    vec = pl.BlockSpec((1, 1, d), lambda i: (i // per, 0, 0))
    rd = A_HEAD_DIM // ROPE_FRACTION_DEN
    inv_freq = jnp.power(jnp.float32(ROPE_THETA), -jnp.arange(ROPE_HALF, dtype=jnp.float32) * (2.0 / rd))
    ang = positions.astype(jnp.float32)[:, None, :] * inv_freq[None, :, None]
    trig = pl.BlockSpec((1, ROPE_HALF, tm), lambda i: (i // per, 0, i % per))
    bf, f32 = jnp.bfloat16, jnp.float32
    sd = jax.ShapeDtypeStruct
    n_kv = seq // DSA_TK
    return pl.pallas_call(
        _in_proj_body, grid=(n // tm,),
        in_specs=[pl.BlockSpec((tm, d), lambda i: (i, 0)), vec, vec,
                  pl.BlockSpec((d, PROJ_WIDTH), lambda i: (0, 0), pipeline_mode=pl.Buffered(1)),
                  trig, trig, pl.BlockSpec((B_HEADS, 2), lambda i: (0, 0))],
        out_specs=[pl.BlockSpec((tm, MAIN_WIDTH), lambda i: (i, 0)),
                   pl.BlockSpec((1, IDX_HEADS, IDX_DIM, tm), lambda i: (i // per, 0, 0, i % per)),
                   pl.BlockSpec((1, A_HEADS, A_HEAD_DIM, tm), lambda i: (i // per, 0, 0, i % per)),
                   pl.BlockSpec((1, IDX_HEADS, tm), lambda i: (i // per, 0, i % per)),
                   pl.BlockSpec((1, sub, DSA_TK, IDX_DIM), lambda i: (i // per, i % per, 0, 0)),
                   pl.BlockSpec((1, A_KV_HEADS, sub, DSA_TK, A_HEAD_DIM), lambda i: (i // per, 0, i % per, 0, 0)),
                   pl.BlockSpec((1, A_KV_HEADS, sub, A_HEAD_DIM, DSA_TK), lambda i: (i // per, 0, i % per, 0, 0)),
                   pl.BlockSpec((1, 2 * B_HEADS, tm), lambda i: (i // per, 0, i % per))],
        out_shape=[sd((n, MAIN_WIDTH), f32), sd((bsz, IDX_HEADS, IDX_DIM, seq), bf),
                   sd((bsz, A_HEADS, A_HEAD_DIM, seq), bf), sd((bsz, IDX_HEADS, seq), f32),
                   sd((bsz, n_kv, DSA_TK, IDX_DIM), bf), sd((bsz, A_KV_HEADS, n_kv, DSA_TK, A_HEAD_DIM), bf),
                   sd((bsz, A_KV_HEADS, n_kv, A_HEAD_DIM, DSA_TK), bf), sd((bsz, 2 * B_HEADS, seq), f32)],
        compiler_params=pltpu.CompilerParams(dimension_semantics=("arbitrary",), vmem_limit_bytes=PROJ_VMEM_LIMIT),
        name="in_proj")(x2, scale, shift, w_perm, jnp.cos(ang), jnp.sin(ang),
                        jnp.stack([a_log, dt_bias], axis=-1).astype(f32))


def _merge_body(oa_ref, ob_ref, z_ref, ga_ref, gb_ref, x_ref, gt_ref, sc_ref, sh_ref, wpa_ref, wpb_ref, wo_ref,
                h_ref, n2_ref):
    f32, bf = jnp.float32, jnp.bfloat16
    z = z_ref[...]
    ob = ob_ref[...] * (z * _sigmoid(z))
    oa = oa_ref[0].reshape(A_WIDTH, oa_ref.shape[-1]).T
    ya = jnp.dot(oa.astype(bf), wpa_ref[...], preferred_element_type=f32)
    yb = jnp.dot(ob.astype(bf), wpb_ref[...], preferred_element_type=f32)
    merged = _sigmoid(ga_ref[...]) * ya + _sigmoid(gb_ref[...]) * yb
    y1 = jnp.dot(merged.astype(bf), wo_ref[...], preferred_element_type=f32)
    h = x_ref[...] + gt_ref[0] * y1
    h_ref[...] = h
    n2_ref[...] = _modulated_norm(h, sc_ref[0], sh_ref[0])


def merge(o_a, o_b, proj, x2, gt1, scale2, shift2, w_pa, w_pb, w_o, seq):
    n, d = x2.shape
    tm = PROJ_TM
    per = seq // tm
    bf = jnp.bfloat16
    vec = pl.BlockSpec((1, 1, d), lambda i: (i // per, 0, 0))

    def col(name):
        off, w = PROJ_DST[name]
        return pl.BlockSpec((tm, w), lambda i: (i, off // w))
    row = lambda w: pl.BlockSpec((tm, w), lambda i: (i, 0))
    res = lambda a, b: pl.BlockSpec((a, b), lambda i: (0, 0))
    return pl.pallas_call(
        _merge_body, grid=(n // tm,),
        in_specs=[pl.BlockSpec((1, A_HEADS, A_HEAD_DIM, tm), lambda i: (i // per, 0, 0, i % per)), row(B_V_WIDTH), col("zb"), col("gate_a"), col("gate_b"), row(d), vec, vec, vec,
                  res(A_WIDTH, d), res(B_V_WIDTH, d), res(d, d)],
        out_specs=[row(d), row(d)],
        out_shape=[jax.ShapeDtypeStruct((n, d), jnp.float32), jax.ShapeDtypeStruct((n, d), jnp.float32)],
        compiler_params=pltpu.CompilerParams(dimension_semantics=("arbitrary",), vmem_limit_bytes=PROJ_VMEM_LIMIT),
        name="merge")(o_a, o_b, proj, proj, proj, x2, gt1, scale2, shift2,
                      w_pa.astype(bf), w_pb.astype(bf), w_o.astype(bf))


PEER_SLOTS = PEER_HEADS * PEER_TOPK
PEER_TB = 128
HALF_ROWS = 4
HI_MASK = -65536


def pack_table(tab):
    bits = lax.bitcast_convert_type(tab.astype(jnp.bfloat16), jnp.uint16).astype(jnp.uint32)
    half = tab.shape[1] // 2
    word = bits[:, :half] | (bits[:, half:] << 16)
    return lax.bitcast_convert_type(word, jnp.int32).reshape(tab.shape[0] * HALF_ROWS, LANES)


def _table_row(tab_ref, rows, k):
    return _unpack(tab_ref[pl.ds(pl.multiple_of(rows[k], HALF_ROWS), HALF_ROWS), :])


def _unpack(w):
    lo = lax.bitcast_convert_type(lax.shift_left(w, 16), jnp.float32)
    hi = lax.bitcast_convert_type(w & jnp.int32(HI_MASK), jnp.float32)
    return lo, hi


def _peer_u_body(idx_ref, x_ref, gate_ref, tab_ref, o_ref, s_ref, sb_ref):
    def lane_sums(t):
        a = jnp.sum(sb_ref[t].T, axis=0, keepdims=True)
        o_ref[pl.ds(t, 1), :] = 0.5 * a * (1.0 + lax.erf(a * (2.0 ** -0.5))) * gate_ref[pl.ds(t, 1), :]

    sb_ref[0] = jnp.zeros(sb_ref.shape[1:], jnp.float32)

    def tok(t, carry):
        lane_sums(jnp.maximum(t - 1, 0))
        x8 = x_ref[pl.ds(t, 1), :].reshape(2 * HALF_ROWS, LANES)
        xlo, xhi = x8[0:HALF_ROWS], x8[HALF_ROWS:2 * HALF_ROWS]
        rows = idx_ref.at[t]
        for k in range(PEER_SLOTS):
            lo, hi = _table_row(tab_ref, rows, k)
            s_ref[HALF_ROWS * k:HALF_ROWS * (k + 1), :] = lo * xlo + hi * xhi
        s4 = s_ref[pl.ds(0, PEER_SLOTS, stride=HALF_ROWS), :]
        for r in range(1, HALF_ROWS):
            s4 = s4 + s_ref[pl.ds(r, PEER_SLOTS, stride=HALF_ROWS), :]
        sb_ref[t] = s4
        return carry
    lax.fori_loop(0, PEER_TB, tok, 0, unroll=2)
    lane_sums(PEER_TB - 1)


def _peer_v_body(normed, idx_ref, coef_ref, h_ref, gt_ref, gain_ref, tab_ref, o_ref, cb_ref):
    span = 2 * HALF_ROWS
    gt8 = gt_ref[0].reshape(span, LANES)
    gain8 = gain_ref[...].reshape(span, LANES)

    def spread(t):
        return jnp.broadcast_to(coef_ref[pl.ds(t, 1), :], (PEER_SLOTS, LANES)).T

    def finish(t, y8):
        z = h_ref[pl.ds(t, 1), :].reshape(span, LANES) + gt8 * y8
        if normed:
            ss = jnp.sum(jnp.sum(z * z, axis=1, keepdims=True), axis=0, keepdims=True)
            z = z * lax.rsqrt(ss * (1.0 / (span * LANES)) + EPS) * gain8
        o_ref[pl.ds(t, 1), :] = z.reshape(1, span * LANES)

    def tok(t, carry):
        weights, y_prev = carry
        finish(jnp.maximum(t - 1, 0), y_prev)
        nxt = spread(jnp.minimum(t + 1, PEER_TB - 1))
        cb_ref[...] = weights
        nacc = 2
        acc = [jnp.zeros((HALF_ROWS, LANES), jnp.float32) for _ in range(2 * nacc)]
        rows = idx_ref.at[t]
        for k in range(PEER_SLOTS):
            lo, hi = _table_row(tab_ref, rows, k)
            c = jnp.broadcast_to(cb_ref[k:k + 1, :], (HALF_ROWS, LANES))
            a = k % nacc
            acc[2 * a] = acc[2 * a] + c * lo
            acc[2 * a + 1] = acc[2 * a + 1] + c * hi
        return nxt, jnp.concatenate([acc[0] + acc[2], acc[1] + acc[3]], axis=0)
    _, y_last = lax.fori_loop(0, PEER_TB, tok, (spread(0), jnp.zeros((span, LANES), jnp.float32)))
    finish(PEER_TB - 1, y_last)


def _table_spec():
    return pl.BlockSpec((PEER_N_EXPERTS * HALF_ROWS, LANES), lambda i: (0, 0), pipeline_mode=pl.Buffered(1))


def peer_u(idx, x, gates, tab):
    n = idx.shape[0]
    tb = PEER_TB
    return pl.pallas_call(
        _peer_u_body, grid=(n // tb,),
        in_specs=[pl.BlockSpec((tb, PEER_SLOTS), lambda i: (i, 0), memory_space=pltpu.SMEM),
                  pl.BlockSpec((tb, x.shape[1]), lambda i: (i, 0)),
                  pl.BlockSpec((tb, PEER_SLOTS), lambda i: (i, 0)),
                  _table_spec()],
        out_specs=pl.BlockSpec((tb, PEER_SLOTS), lambda i: (i, 0)),
        out_shape=jax.ShapeDtypeStruct((n, PEER_SLOTS), jnp.float32),
        scratch_shapes=[pltpu.VMEM((HALF_ROWS * PEER_SLOTS, LANES), jnp.float32),
                        pltpu.VMEM((tb, PEER_SLOTS, LANES), jnp.float32)],
        compiler_params=pltpu.CompilerParams(dimension_semantics=("arbitrary",), vmem_limit_bytes=PEER_VMEM_LIMIT),
        name="peer_u")(idx, x, gates, tab)


def peer_v(idx, coef, tab, h2, gt, seq, gain=None):
    n, d = h2.shape
    tb = PEER_TB
    per = seq // tb
    row = pl.BlockSpec((tb, d), lambda i: (i, 0))
    gain2 = jnp.ones((1, d), jnp.float32) if gain is None else gain.astype(jnp.float32).reshape(1, d)
    return pl.pallas_call(
        functools.partial(_peer_v_body, gain is not None), grid=(n // tb,),
        in_specs=[pl.BlockSpec((tb, PEER_SLOTS), lambda i: (i, 0), memory_space=pltpu.SMEM),
                  pl.BlockSpec((tb, PEER_SLOTS), lambda i: (i, 0)),
                  row, pl.BlockSpec((1, 1, d), lambda i: (i // per, 0, 0)), pl.BlockSpec((1, d), lambda i: (0, 0)),
                  _table_spec()],
        out_specs=row,
        out_shape=jax.ShapeDtypeStruct((n, d), jnp.float32),
        scratch_shapes=[pltpu.VMEM((PEER_SLOTS, LANES), jnp.float32)],
        compiler_params=pltpu.CompilerParams(dimension_semantics=("arbitrary",), vmem_limit_bytes=PEER_VMEM_LIMIT),
        name="peer_v")(idx, coef, h2, gt, gain2, tab)


PEER_TT = 256


def _extract_top(ref, n_out, rid=None):
    rows, t = ref.shape
    if rid is None:
        rid = lax.broadcasted_iota(jnp.int32, (rows, t), 0)
    vals, idxs = [], []
    for _ in range(n_out):
        s = ref[...]
        m = jnp.max(s, axis=0, keepdims=True)
        ix = jnp.min(jnp.where(s == m, rid, jnp.int32(PAD_ID)), axis=0, keepdims=True)
        ref[...] = jnp.where(rid == ix, -jnp.inf, s)
        vals.append(m)
        idxs.append(ix)
    return vals, idxs


def _peer_route_body(x_ref, wq_ref, sk_ref, cid_ref, idx_ref, gate_ref, s_ref, cand_ref, v_ref, i_ref, et_ref, gt_ref):
    kk, nk = PEER_TOPK, PEER_N_KEYS
    f32 = jnp.float32
    q = jnp.dot(x_ref[...].astype(jnp.bfloat16), wq_ref[...], preferred_element_type=f32).astype(jnp.bfloat16)
    nt = (((1,), (1,)), ((), ()))
    row16 = lax.broadcasted_iota(jnp.int32, (kk, PEER_TT), 0)
    for h in range(PEER_HEADS):
        qh = q[:, h * PEER_KEY_DIM:(h + 1) * PEER_KEY_DIM]
        s_ref[...] = lax.dot_general(sk_ref[h], qh, nt, preferred_element_type=f32)
        for p in range(2):
            vals, idxs = _extract_top(s_ref.at[p * nk:(p + 1) * nk, :], kk)
            for i in range(kk):
                v_ref[p, i:i + 1, :] = vals[i]
                i_ref[p, i:i + 1, :] = idxs[i]
        off = 0
        for i in range(kk):
            n_j = kk // (i + 1)
            cand_ref[off:off + n_j, :] = v_ref[0, i:i + 1, :] + v_ref[1, 0:n_j, :]
            off += n_j
        cand_ref[off:, :] = jnp.full((cand_ref.shape[0] - off, PEER_TT), -jnp.inf, f32)
        vals, cis = _extract_top(cand_ref, kk, cid_ref[...])
        i1, i2 = i_ref[0], i_ref[1]
        es = [jnp.exp(v - vals[0]) for v in vals]
        den = es[0]
        for e in es[1:]:
            den = den + e
        for k in range(kk):
            ci = cis[k]
            e1 = jnp.sum(jnp.where(row16 == lax.shift_right_logical(ci, kk.bit_length() - 1), i1, 0), axis=0, keepdims=True)
            e2 = jnp.sum(jnp.where(row16 == (ci & (kk - 1)), i2, 0), axis=0, keepdims=True)
            et_ref[h * kk + k:h * kk + k + 1, :] = (e1 * nk + e2) * HALF_ROWS
            gt_ref[h * kk + k:h * kk + k + 1, :] = es[k] / den
    idx_ref[...] = lax.bitcast_convert_type(lax.bitcast_convert_type(et_ref[...], f32).T, jnp.int32)
    gate_ref[...] = gt_ref[...].T


def _pair_cells():
    kk = PEER_TOPK
    ids = [i * kk + j for i in range(kk) for j in range(kk // (i + 1))]
    return ids + [PAD_ID] * (-len(ids) % SUBLANES)


def peer_route(xn, wq, subkeys):
    n = xn.shape[0]
    tt = PEER_TT
    cells = _pair_cells()
    cid = jnp.broadcast_to(jnp.asarray(cells, jnp.int32)[:, None], (len(cells), tt))
    half = PEER_KEY_DIM // 2
    z = jnp.zeros((PEER_HEADS, PEER_N_KEYS, half), subkeys.dtype)
    skbd = jnp.concatenate([jnp.concatenate([subkeys[:, 0], z], axis=-1),
                            jnp.concatenate([z, subkeys[:, 1]], axis=-1)], axis=1).astype(jnp.bfloat16)
    return pl.pallas_call(
        _peer_route_body, grid=(n // tt,),
        in_specs=[pl.BlockSpec((tt, D_MODEL), lambda i: (i, 0)),
                  pl.BlockSpec((D_MODEL, PEER_HEADS * PEER_KEY_DIM), lambda i: (0, 0)),
                  pl.BlockSpec((PEER_HEADS, 2 * PEER_N_KEYS, PEER_KEY_DIM), lambda i: (0, 0, 0)),
                  pl.BlockSpec((len(cells), tt), lambda i: (0, 0))],
        out_specs=[pl.BlockSpec((tt, PEER_SLOTS), lambda i: (i, 0)), pl.BlockSpec((tt, PEER_SLOTS), lambda i: (i, 0))],
        out_shape=[jax.ShapeDtypeStruct((n, PEER_SLOTS), jnp.int32), jax.ShapeDtypeStruct((n, PEER_SLOTS), jnp.float32)],
        scratch_shapes=[pltpu.VMEM((2 * PEER_N_KEYS, tt), jnp.float32),
                        pltpu.VMEM((len(cells), tt), jnp.float32),
                        pltpu.VMEM((2, PEER_TOPK, tt), jnp.float32),
                        pltpu.VMEM((2, PEER_TOPK, tt), jnp.int32),
                        pltpu.VMEM((PEER_SLOTS, tt), jnp.int32),
                        pltpu.VMEM((PEER_SLOTS, tt), jnp.float32)],
        compiler_params=pltpu.CompilerParams(dimension_semantics=("arbitrary",), vmem_limit_bytes=ROUTE_VMEM_LIMIT),
        name="peer_route")(xn, wq.astype(jnp.bfloat16), skbd, cid)


def peer_channel_mixer(xn, h2, gt, wq, subkeys, u_tab, v_tab, seq, gain=None):
    idx, gates = peer_route(xn, wq, subkeys)
    coef = peer_u(idx, xn, gates, pack_table(u_tab))
    return peer_v(idx, coef, pack_table(v_tab), h2, gt, seq, gain)


def kernel(x, c, positions, w_ada, b_ada, w_in, conv_w, a_log, dt_bias, norm_b_w,
           w_pa, w_pb, w_o, peer_wq, peer_subkeys, peer_u, peer_v, final_norm_w):
    bsz, seq, d = x.shape
    h = x.reshape(bsz * seq, d)
    for layer in range(DEPTH):
        mod = ada_modulation(c, w_ada[layer], b_ada[layer])
        sh1, sc1, gt1, sh2, sc2, gt2 = [m.reshape(bsz, 1, d) for m in jnp.split(mod, 6, axis=-1)]
        proj, qit, qat, w, kit, kat, vat, bg = in_proj(h, 1.0 + sc1, sh1, permute_w_in(w_in[layer]), positions,
                                                       a_log[layer], dt_bias[layer], bsz, seq)
        o_a = dsa_attention_pallas(qit, w, qat, kit, kat, vat)
        o_b = gated_delta_rule_pallas(proj, conv_w[layer], bg[:, B_HEADS:], bg[:, :B_HEADS], norm_b_w[layer],
                                      bsz, seq)
        h, n2 = merge(o_a, o_b.reshape(bsz * seq, B_V_WIDTH), proj, h, gt1,
                      1.0 + sc2, sh2, w_pa[layer], w_pb[layer], w_o[layer], seq)
        last = layer == DEPTH - 1
        h = peer_channel_mixer(n2, h, gt2, peer_wq[layer], peer_subkeys[layer], peer_u[layer], peer_v[layer], seq,
                               final_norm_w if last else None)
    return h.reshape(bsz, seq, d)
```

```python
import functools

import jax, jax.numpy as jnp
from jax import lax
from jax.experimental import pallas as pl
from jax.experimental.pallas import tpu as pltpu

D_MODEL = 1024
DEPTH = 1

LANES = 128
SUBLANES = 8
MIB = 1024 * 1024
DSA_VMEM_LIMIT = 48 * MIB
GDN_VMEM_LIMIT = 40 * MIB
PROJ_VMEM_LIMIT = 48 * MIB
PEER_VMEM_LIMIT = 48 * MIB
ROUTE_VMEM_LIMIT = 32 * MIB

A_HEADS = 8
A_KV_HEADS = 2
A_HEAD_DIM = 64
IDX_HEADS = 16
IDX_DIM = 64
IDX_TOPK_MAX = 256
B_HEADS = 8
B_KEY_DIM = 64
B_VAL_DIM = 64
CONV_WIDTH = 4
CHUNK = 64
ROPE_THETA = 500000.0
ROPE_FRACTION_DEN = 4
PEER_HEADS = 8
PEER_KEY_DIM = 128
PEER_N_KEYS = 128
PEER_TOPK = 16
EPS = 1e-6

A_WIDTH = A_HEADS * A_HEAD_DIM
KV_WIDTH = A_KV_HEADS * A_HEAD_DIM
B_QK_WIDTH = B_HEADS * B_KEY_DIM
B_V_WIDTH = B_HEADS * B_VAL_DIM
IN_SPLITS = (A_WIDTH, KV_WIDTH, KV_WIDTH, IDX_HEADS * IDX_DIM, IDX_DIM, IDX_HEADS,
             B_QK_WIDTH, B_QK_WIDTH, B_V_WIDTH, B_V_WIDTH, B_HEADS, B_HEADS, D_MODEL, D_MODEL)


DSA_TQ = 256
DSA_TK = 256
INT_MIN = -2**31
I16_MIN = -2**15
NEG_BIG = -1e30
PAD_ID = 2**30


def _dsa_body(topk, idx_bits, qit_ref, w_ref, qat_ref, ki_ref, ka_ref, vat_ref, o_ref, key_ref, hi_ref, lo_ref):
    tq, tk = DSA_TQ, DSA_TK
    qb = pl.program_id(1)
    n_kv = qb + 1
    t_glob = qb * tq + lax.broadcasted_iota(jnp.int32, (1, tq), 1)
    row = lax.broadcasted_iota(jnp.int32, (tk, 1), 0)
    f32 = jnp.float32

    def p1(j, carry):
        kt = ki_ref[0, j]
        score = jnp.zeros((tk, tq), f32)
        for h in range(IDX_HEADS):
            lt = jnp.dot(kt, qit_ref[0, h], preferred_element_type=f32)
            score = score + w_ref[0, h:h + 1, :] * jnp.maximum(lt, 0.0)
        bits = lax.bitcast_convert_type(score + 0.0, jnp.int32)
        skey = jnp.where(bits >= 0, bits, bits ^ jnp.int32(0x7FFFFFFF))
        skey = jnp.where(j * tk + row <= t_glob, skey, jnp.int32(INT_MIN))
        key_ref[j] = skey
        hi_ref[j] = lax.shift_right_arithmetic(skey, 16).astype(jnp.int16)
        return carry
    lax.fori_loop(0, n_kv, p1, 0)

    key_ref[n_kv] = jnp.full((tk, tq), INT_MIN, jnp.int32)
    hi_ref[n_kv] = jnp.full((tk, tq), I16_MIN, jnp.int16)
    lo_ref[n_kv] = jnp.full((tk, tq), I16_MIN, jnp.int16)

    def count(pred):
        def body(jj, acc):
            for j in (2 * jj, 2 * jj + 1):
                hit = jnp.where(pred(key_ref[j], j * tk + row), 1.0, 0.0)
                acc = acc + jnp.sum(hit.reshape(tk // SUBLANES, SUBLANES, tq), axis=0)
            return acc
        acc = lax.fori_loop(0, (n_kv + 1) // 2, body, jnp.zeros((SUBLANES, tq), f32))
        return jnp.sum(acc, axis=0, keepdims=True)

    kf = jnp.float32(topk)
    half = 16
    pack = 2 * SUBLANES

    def count16(ref, pred):
        def body(jj, acc):
            for j in (2 * jj, 2 * jj + 1):
                hit = jnp.where(pred(ref[j]), jnp.bfloat16(1), jnp.bfloat16(0)).reshape(tk // pack, pack, tq)
                part = hit[0]
                for g in range(1, tk // pack):
                    part = part + hit[g]
                acc = acc + part.astype(f32)
            return acc
        acc = lax.fori_loop(0, (n_kv + 1) // 2, body, jnp.zeros((pack, tq), f32))
        return jnp.sum(acc, axis=0, keepdims=True)

    def search16(ref, base):
        def step(i, u):
            cand_u = u | lax.shift_left(jnp.int32(1), half - 1 - i)
            cand = (cand_u + I16_MIN).astype(jnp.int16)
            c = base + count16(ref, lambda v: v >= cand)
            return jnp.where(c >= kf, cand_u, u)
        return lax.fori_loop(0, half, step, jnp.zeros((1, tq), jnp.int32))

    hi_u = search16(hi_ref, jnp.zeros((1, tq), f32))
    hi_s = hi_u + I16_MIN
    hi_s16 = hi_s.astype(jnp.int16)
    above = count16(hi_ref, lambda v: v > hi_s16)

    def lows(j, carry):
        lo = ((key_ref[j] & 0xFFFF) + I16_MIN).astype(jnp.int16)
        lo_ref[j] = jnp.where(hi_ref[j] == hi_s16, lo, jnp.int16(I16_MIN))
        return carry
    lax.fori_loop(0, n_kv, lows, 0)
    lo_u = search16(lo_ref, above)
    kth = hi_s * 65536 + lo_u
    c_gt = count(lambda k, s: k > kth)
    c_ge = count(lambda k, s: k >= kth)
    short = kth == jnp.int32(INT_MIN)
    x0 = jnp.where(short, jnp.int32(-1), jnp.int32(PAD_ID))
    need = kf - c_gt
    has_tie = jnp.max(jnp.where(jnp.logical_and(c_ge > kf, jnp.logical_not(short)), 1.0, 0.0)) > 0.0

    def tie_search():
        def step(i, x):
            bit = lax.shift_left(jnp.int32(1), idx_bits - 1 - i)
            probe = x + bit - 1
            c = count(lambda k, s: jnp.logical_and(k == kth, s <= probe))
            return jnp.where(c < need, x + bit, x)
        x = lax.fori_loop(0, idx_bits, step, jnp.zeros((1, tq), jnp.int32))
        return jnp.where(short, jnp.int32(-1), x)
    x_lim = lax.cond(has_tie, tie_search, lambda: x0)

    rep = A_HEADS // A_KV_HEADS
    hs = range(A_HEADS)

    def p3(j, carry):
        m, l, acc = carry
        skey = key_ref[j]
        s_idx = j * tk + row
        sel = jnp.logical_or(skey > kth, jnp.logical_and(skey == kth, s_idx <= x_lim))
        kt = [ka_ref[0, g, j] for g in range(A_KV_HEADS)]
        vt = [vat_ref[0, g, j] for g in range(A_KV_HEADS)]
        s = [jnp.where(sel, jnp.dot(kt[h // rep], qat_ref[0, h], preferred_element_type=f32), NEG_BIG) for h in hs]
        m_new = [jnp.maximum(m[h], jnp.max(s[h], axis=0, keepdims=True)) for h in hs]
        alpha = [jnp.exp(m[h] - m_new[h]) for h in hs]
        p = [jnp.exp(s[h] - m_new[h]) for h in hs]
        l_new = [alpha[h] * l[h] + jnp.sum(p[h], axis=0, keepdims=True) for h in hs]
        acc_new = [alpha[h] * acc[h] + jnp.dot(vt[h // rep], p[h].astype(jnp.bfloat16), preferred_element_type=f32)
                   for h in hs]
        return tuple(m_new), tuple(l_new), tuple(acc_new)

    init = (tuple(jnp.full((1, tq), NEG_BIG, f32) for _ in hs), tuple(jnp.zeros((1, tq), f32) for _ in hs),
            tuple(jnp.zeros((A_HEAD_DIM, tq), f32) for _ in hs))
    _, l_fin, acc_fin = lax.fori_loop(0, n_kv, p3, init)
    for h in hs:
        o_ref[0, h] = acc_fin[h] / l_fin[h]


ROPE_HALF = A_HEAD_DIM // ROPE_FRACTION_DEN // 2


def _rope_t(xt, n_heads, cos, sin, scale=None):
    outs = []
    for h in range(n_heads):
        b = h * A_HEAD_DIM
        x1, x2, rest = xt[b:b + ROPE_HALF], xt[b + ROPE_HALF:b + 2 * ROPE_HALF], xt[b + 2 * ROPE_HALF:b + A_HEAD_DIM]
        o = jnp.concatenate([x1 * cos - x2 * sin, x2 * cos + x1 * sin, rest], axis=0)
        outs.append(o if scale is None else o * scale)
    return outs


def dsa_attention_pallas(qit, w, qat, kit, kat, vat):
    bsz, seq = qit.shape[0], qit.shape[-1]
    tq, tk = DSA_TQ, DSA_TK
    topk = min(IDX_TOPK_MAX, seq // 4)
    n_kv = seq // tk
    return pl.pallas_call(
        functools.partial(_dsa_body, topk, (seq - 1).bit_length()),
        grid=(bsz, seq // tq),
        in_specs=[
            pl.BlockSpec((1, IDX_HEADS, IDX_DIM, tq), lambda b, q: (b, 0, 0, q)),
            pl.BlockSpec((1, IDX_HEADS, tq), lambda b, q: (b, 0, q)),
            pl.BlockSpec((1, A_HEADS, A_HEAD_DIM, tq), lambda b, q: (b, 0, 0, q)),
            pl.BlockSpec((1, n_kv, tk, IDX_DIM), lambda b, q: (b, 0, 0, 0)),
            pl.BlockSpec((1, A_KV_HEADS, n_kv, tk, A_HEAD_DIM), lambda b, q: (b, 0, 0, 0, 0)),
            pl.BlockSpec((1, A_KV_HEADS, n_kv, A_HEAD_DIM, tk), lambda b, q: (b, 0, 0, 0, 0)),
        ],
        out_specs=pl.BlockSpec((1, A_HEADS, A_HEAD_DIM, tq), lambda b, q: (b, 0, 0, q)),
        out_shape=jax.ShapeDtypeStruct((bsz, A_HEADS, A_HEAD_DIM, seq), jnp.float32),
        scratch_shapes=[
            pltpu.VMEM((n_kv + 1, tk, tq), jnp.int32),
            pltpu.VMEM((n_kv + 1, tk, tq), jnp.int16),
            pltpu.VMEM((n_kv + 1, tk, tq), jnp.int16),
        ],
        compiler_params=pltpu.CompilerParams(dimension_semantics=("arbitrary", "arbitrary"),
                                             vmem_limit_bytes=DSA_VMEM_LIMIT),
        name="dsa_attention",
    )(qit, w, qat, kit, kat, vat)


GDN_G = 8
GDN_BASE = 8
GDN_PAD = 8


def _gdn_body(q_ref, k_ref, v_ref, cw_ref, gc_ref, bt_ref, nw_ref, o_ref, s_ref, xs_ref, cs_ref):
    c_sz = CHUNK
    f32, bf = jnp.float32, jnp.bfloat16

    rows = GDN_G * c_sz
    halo = CONV_WIDTH - 1

    @pl.when(pl.program_id(1) == 0)
    def _():
        s_ref[...] = jnp.zeros(s_ref.shape, f32)
        xs_ref[:, 0:GDN_PAD, :] = jnp.zeros((3, GDN_PAD, xs_ref.shape[-1]), f32)

    for j, ref in enumerate((q_ref, k_ref, v_ref)):
        xs_ref[j, GDN_PAD:GDN_PAD + rows, :] = ref[0]
        acc = xs_ref[j, GDN_PAD - halo:GDN_PAD - halo + rows, :] * cw_ref[j, 0:1, :]
        for i in range(1, CONV_WIDTH):
            acc = acc + xs_ref[j, GDN_PAD - halo + i:GDN_PAD - halo + i + rows, :] * cw_ref[j, i:i + 1, :]
        cs_ref[j] = acc * _sigmoid(acc)
        xs_ref[j, GDN_PAD - halo:GDN_PAD, :] = xs_ref[j, GDN_PAD + rows - halo:GDN_PAD + rows, :]

    ri = lax.broadcasted_iota(jnp.int32, (c_sz, c_sz), 0)
    ci = lax.broadcasted_iota(jnp.int32, (c_sz, c_sz), 1)
    incl, strict = ri >= ci, ri > ci
    eye = jnp.where(ri == ci, 1.0, 0.0).astype(f32)
    blk = lambda w: (ri // w) == (ci // w)
    diag8 = blk(GDN_BASE)
    sub_blocks = []
    w = GDN_BASE
    while w < c_sz:
        sub_blocks.append(jnp.logical_and(blk(2 * w), jnp.logical_not(blk(w))))
        w *= 2
    nt = (((1,), (1,)), ((), ()))
    dot = lambda a, b: jnp.dot(a.astype(bf), b.astype(bf), preferred_element_type=f32)
    dot_nt = lambda a, b: lax.dot_general(a.astype(bf), b.astype(bf), nt, preferred_element_type=f32)

    def split(a):
        hi = a.astype(bf)
        return hi, (a - hi.astype(f32)).astype(bf)

    def hp(a, b):
        a_hi, a_lo = split(a)
        b_hi, b_lo = split(b)
        mm = lambda u, w: jnp.dot(u, w, preferred_element_type=f32)
        return mm(a_hi, b_hi) + (mm(a_hi, b_lo) + mm(a_lo, b_hi))

    hs = range(B_HEADS)

    def chunk_pair(i, carry):
        cs = [2 * i, 2 * i + 1]
        r0 = [pl.multiple_of(c * c_sz, c_sz) for c in cs]
        items = [(j, h) for j in range(2) for h in hs]
        hd = lambda a, j, h: cs_ref[a, pl.ds(r0[j], c_sz), h * B_KEY_DIM:(h + 1) * B_KEY_DIM]
        l2n = lambda t: t * lax.rsqrt(jnp.sum(t * t, axis=-1, keepdims=True) + EPS)
        q = [l2n(hd(0, j, h)) * (B_KEY_DIM ** -0.5) for j, h in items]
        k = [l2n(hd(1, j, h)) for j, h in items]
        v = [hd(2, j, h) for j, h in items]
        gc8 = [gc_ref[0, c] for c in cs]
        gct = [g8.T for g8 in gc8]
        btt = [bt_ref[0, c].T for c in cs]
        gcr = [gc8[j][h:h + 1, :] for j, h in items]
        n_it = range(len(items))
        gcc = [gct[j][:, h:h + 1] for j, h in items]
        beta = [btt[j][:, h:h + 1] for j, h in items]
        decay = [jnp.exp(jnp.where(incl, gcc[n] - gcr[n], -jnp.inf)) for n in n_it]
        kb = [k[n] * beta[n] for n in n_it]
        vb = [v[n] * beta[n] for n in n_it]
        low = [jnp.where(strict, dot_nt(kb[n], k[n]) * decay[n], 0.0) for n in n_it]
        dg = [jnp.where(diag8, low[n], 0.0) for n in n_it]
        t = [eye - dg[n] for n in n_it]
        p = [hp(dg[n], dg[n]) for n in n_it]
        t = [hp(t[n], eye + p[n]) for n in n_it]
        p = [hp(p[n], p[n]) for n in n_it]
        t = [hp(t[n], eye + p[n]) for n in n_it]
        for below in sub_blocks:
            lb = [jnp.where(below, low[n], 0.0) for n in n_it]
            lt = [hp(lb[n], t[n]) for n in n_it]
            t = [t[n] - hp(t[n], lt[n]) for n in n_it]
        u = [dot(t[n], vb[n]) for n in n_it]
        kcd = [dot(t[n], kb[n] * jnp.exp(gcc[n])) for n in n_it]
        intra = [dot_nt(q[n], k[n]) * decay[n] for n in n_it]
        qg = [q[n] * jnp.exp(gcc[n]) for n in n_it]
        glast = [g[:, c_sz - 1:c_sz] for g in gcr]
        kdt = [(k[n] * jnp.exp(glast[n] - gcc[n])).T for n in n_it]
        s = [s_ref[h] for h in hs]
        for j in range(2):
            ix = [j * B_HEADS + h for h in hs]
            v_new = [u[ix[h]] - dot(kcd[ix[h]], s[h]) for h in hs]
            out = [dot(qg[ix[h]], s[h]) + dot(intra[ix[h]], v_new[h]) for h in hs]
            s = [s[h] * jnp.exp(glast[ix[h]]) + dot(kdt[ix[h]], v_new[h]) for h in hs]
            for h in hs:
                o = out[h]
                o = o * lax.rsqrt(jnp.mean(o * o, axis=-1, keepdims=True) + EPS) * nw_ref[...]
                o_ref[0, pl.ds(r0[j], c_sz), h * B_VAL_DIM:(h + 1) * B_VAL_DIM] = o
        for h in hs:
            s_ref[h] = s[h]
        return carry
    lax.fori_loop(0, GDN_G // 2, chunk_pair, 0)


def gated_delta_rule_pallas(proj, conv_w, g, beta, norm_w, bsz, seq):
    nh, dk, dv = B_HEADS, B_KEY_DIM, B_VAL_DIM
    n = seq // CHUNK
    chunked = lambda t: jnp.moveaxis(t.reshape(bsz, nh, n, CHUNK), 1, 2)
    gc = chunked(jnp.cumsum(g.reshape(bsz, nh, n, CHUNK), axis=-1))
    side = pl.BlockSpec((1, GDN_G, nh, CHUNK), lambda b, c: (b, c, 0, 0))
    rows = GDN_G * CHUNK
    width = nh * dk

    def col(name):
        off, w = PROJ_DST[name]
        assert w == width
        return pl.BlockSpec((1, rows, w), lambda b, c: (b, c, off // w))
    cw = jnp.transpose(conv_w.astype(jnp.float32).reshape(CONV_WIDTH, 3, width), (1, 0, 2))
    proj3 = proj.reshape(bsz, seq, proj.shape[-1])
    return pl.pallas_call(
        _gdn_body, grid=(bsz, n // GDN_G),
        in_specs=[col("qb"), col("kb"), col("vb"),
                  pl.BlockSpec((3, CONV_WIDTH, width), lambda b, c: (0, 0, 0)),
                  side, side, pl.BlockSpec((1, dv), lambda b, c: (0, 0))],
        out_specs=pl.BlockSpec((1, rows, nh * dv), lambda b, c: (b, c, 0)),
        out_shape=jax.ShapeDtypeStruct((bsz, seq, nh * dv), jnp.float32),
        scratch_shapes=[pltpu.VMEM((nh, dk, dv), jnp.float32),
                        pltpu.VMEM((3, GDN_PAD + rows, width), jnp.float32),
                        pltpu.VMEM((3, rows, width), jnp.float32)],
        compiler_params=pltpu.CompilerParams(dimension_semantics=("arbitrary", "arbitrary"),
                                             vmem_limit_bytes=GDN_VMEM_LIMIT),
        name="gated_delta_rule")(proj3, proj3, proj3, cw, gc, chunked(beta), norm_w.astype(jnp.float32).reshape(1, dv))


IN_NAMES = ("qa", "ka", "va", "qi", "ki", "wi", "qb", "kb", "vb", "zb", "bb", "ab", "gate_a", "gate_b")
PROJ_ORDER = ("gate_a", "gate_b", "qb", "kb", "vb", "zb", "qi", "qa", "ka", "va", "ki", "wi", "bb", "ab")


def _proj_layout():
    src, off = {}, 0
    for name, w in zip(IN_NAMES, IN_SPLITS):
        src[name] = (off, w)
        off += w
    dst, off = {}, 0
    for name in PROJ_ORDER:
        dst[name] = (off, src[name][1])
        off += src[name][1]
    return src, dst, -(-off // LANES) * LANES


PROJ_SRC, PROJ_DST, PROJ_WIDTH = _proj_layout()
MAIN_WIDTH = PROJ_DST["qi"][0]
PROJ_TM = 512


def permute_w_in(w_in):
    cols = [w_in[:, PROJ_SRC[n][0]:PROJ_SRC[n][0] + PROJ_SRC[n][1]] for n in PROJ_ORDER]
    cols.append(jnp.zeros((w_in.shape[0], PROJ_WIDTH - sum(c.shape[1] for c in cols)), w_in.dtype))
    return jnp.concatenate(cols, axis=1).astype(jnp.bfloat16)


def _modulated_norm(x, scale, shift):
    return x * lax.rsqrt(jnp.mean(x * x, axis=-1, keepdims=True) + EPS) * scale + shift


def _sigmoid(v):
    return 1.0 / (1.0 + jnp.exp(-v))


def _ada_body(c_ref, w_ref, b_ref, o_ref):
    c = c_ref[...]
    o_ref[...] = jnp.dot((c * _sigmoid(c)).astype(jnp.bfloat16), w_ref[...].astype(jnp.bfloat16),
                         preferred_element_type=jnp.float32) + b_ref[...]


def ada_modulation(c, w_ada, b_ada):
    bsz, d = c.shape
    n = w_ada.shape[1]
    tn = D_MODEL
    return pl.pallas_call(
        _ada_body, grid=(n // tn,),
        in_specs=[pl.BlockSpec((bsz, d), lambda j: (0, 0)), pl.BlockSpec((d, tn), lambda j: (0, j)),
                  pl.BlockSpec((1, tn), lambda j: (0, j))],
        out_specs=pl.BlockSpec((bsz, tn), lambda j: (0, j)),
        out_shape=jax.ShapeDtypeStruct((bsz, n), jnp.float32),
        name="ada_modulation")(c, w_ada, b_ada.reshape(1, n))


def _in_proj_body(x_ref, sc_ref, sh_ref, w_ref, cos_ref, sin_ref, dp_ref,
                  o_ref, qit_ref, qat_ref, wt_ref, kit_ref, kat_ref, vat_ref, bg_ref):
    bf, f32 = jnp.bfloat16, jnp.float32
    n1 = _modulated_norm(x_ref[...], sc_ref[0], sh_ref[0]).astype(bf)
    o_ref[...] = jnp.dot(n1, w_ref[:, 0:MAIN_WIDTH], preferred_element_type=f32)
    pa = jnp.dot(n1, w_ref[:, MAIN_WIDTH:PROJ_WIDTH], preferred_element_type=f32)

    def piece(name, width=None):
        off = PROJ_DST[name][0] - MAIN_WIDTH
        return pa[:, off:off + (width or PROJ_DST[name][1])]
    cos, sin = cos_ref[0], sin_ref[0]
    for h, o in enumerate(_rope_t(piece("qi").T, IDX_HEADS, cos, sin)):
        qit_ref[0, h] = o.astype(bf)
    for h, o in enumerate(_rope_t(piece("qa").T, A_HEADS, cos, sin, A_HEAD_DIM ** -0.5)):
        qat_ref[0, h] = o.astype(bf)
    kat = _rope_t(piece("ka").T, A_KV_HEADS, cos, sin)
    vt = piece("va").T
    smt = piece("ki", LANES).T
    kit = _rope_t(smt[0:IDX_DIM], 1, cos, sin)[0]
    for u in range(pa.shape[0] // DSA_TK):
        cols = slice(u * DSA_TK, (u + 1) * DSA_TK)
        for g in range(A_KV_HEADS):
            kat_ref[0, g, u] = kat[g][:, cols].T.astype(bf)
            vat_ref[0, g, u] = vt[g * A_HEAD_DIM:(g + 1) * A_HEAD_DIM, cols].astype(bf)
        kit_ref[0, u] = kit[:, cols].T.astype(bf)
    wt_ref[0] = smt[IDX_DIM:IDX_DIM + IDX_HEADS] * ((IDX_HEADS ** -0.5) * (IDX_DIM ** -0.5))
    b0 = IDX_DIM + IDX_HEADS
    bb, ab = smt[b0:b0 + B_HEADS], smt[b0 + B_HEADS:b0 + 2 * B_HEADS]
    a_log, dt_bias = dp_ref[:, 0:1], dp_ref[:, 1:2]
    bg_ref[0] = jnp.concatenate([_sigmoid(bb), -jnp.exp(a_log) * jax.nn.softplus(ab + dt_bias)], axis=0)


def in_proj(x2, scale, shift, w_perm, positions, a_log, dt_bias, bsz, seq):
    n, d = x2.shape
    tm = PROJ_TM
    per = seq // tm
    sub = tm // DSA_TK
    assert PROJ_ORDER[-4:] == ("ki", "wi", "bb", "ab") and PROJ_DST["ki"][0] + LANES == PROJ_WIDTH
    vec = pl.BlockSpec((1, 1, d), lambda i: (i // per, 0, 0))
    rd = A_HEAD_DIM // ROPE_FRACTION_DEN
    inv_freq = jnp.power(jnp.float32(ROPE_THETA), -jnp.arange(ROPE_HALF, dtype=jnp.float32) * (2.0 / rd))
    ang = positions.astype(jnp.float32)[:, None, :] * inv_freq[None, :, None]
    trig = pl.BlockSpec((1, ROPE_HALF, tm), lambda i: (i // per, 0, i % per))
    bf, f32 = jnp.bfloat16, jnp.float32
    sd = jax.ShapeDtypeStruct
    n_kv = seq // DSA_TK
    return pl.pallas_call(
        _in_proj_body, grid=(n // tm,),
        in_specs=[pl.BlockSpec((tm, d), lambda i: (i, 0)), vec, vec,
                  pl.BlockSpec((d, PROJ_WIDTH), lambda i: (0, 0), pipeline_mode=pl.Buffered(1)),
                  trig, trig, pl.BlockSpec((B_HEADS, 2), lambda i: (0, 0))],
        out_specs=[pl.BlockSpec((tm, MAIN_WIDTH), lambda i: (i, 0)),
                   pl.BlockSpec((1, IDX_HEADS, IDX_DIM, tm), lambda i: (i // per, 0, 0, i % per)),
                   pl.BlockSpec((1, A_HEADS, A_HEAD_DIM, tm), lambda i: (i // per, 0, 0, i % per)),
                   pl.BlockSpec((1, IDX_HEADS, tm), lambda i: (i // per, 0, i % per)),
                   pl.BlockSpec((1, sub, DSA_TK, IDX_DIM), lambda i: (i // per, i % per, 0, 0)),
                   pl.BlockSpec((1, A_KV_HEADS, sub, DSA_TK, A_HEAD_DIM), lambda i: (i // per, 0, i % per, 0, 0)),
                   pl.BlockSpec((1, A_KV_HEADS, sub, A_HEAD_DIM, DSA_TK), lambda i: (i // per, 0, i % per, 0, 0)),
                   pl.BlockSpec((1, 2 * B_HEADS, tm), lambda i: (i // per, 0, i % per))],
        out_shape=[sd((n, MAIN_WIDTH), f32), sd((bsz, IDX_HEADS, IDX_DIM, seq), bf),
                   sd((bsz, A_HEADS, A_HEAD_DIM, seq), bf), sd((bsz, IDX_HEADS, seq), f32),
                   sd((bsz, n_kv, DSA_TK, IDX_DIM), bf), sd((bsz, A_KV_HEADS, n_kv, DSA_TK, A_HEAD_DIM), bf),
                   sd((bsz, A_KV_HEADS, n_kv, A_HEAD_DIM, DSA_TK), bf), sd((bsz, 2 * B_HEADS, seq), f32)],
        compiler_params=pltpu.CompilerParams(dimension_semantics=("arbitrary",), vmem_limit_bytes=PROJ_VMEM_LIMIT),
        name="in_proj")(x2, scale, shift, w_perm, jnp.cos(ang), jnp.sin(ang),
                        jnp.stack([a_log, dt_bias], axis=-1).astype(f32))


def _merge_body(oa_ref, ob_ref, z_ref, ga_ref, gb_ref, x_ref, gt_ref, sc_ref, sh_ref, wpa_ref, wpb_ref, wo_ref,
                h_ref, n2_ref):
    f32, bf = jnp.float32, jnp.bfloat16
    z = z_ref[...]
    ob = ob_ref[...] * (z * _sigmoid(z))
    oa = oa_ref[0].reshape(A_WIDTH, oa_ref.shape[-1]).T
    ya = jnp.dot(oa.astype(bf), wpa_ref[...], preferred_element_type=f32)
    yb = jnp.dot(ob.astype(bf), wpb_ref[...], preferred_element_type=f32)
    merged = _sigmoid(ga_ref[...]) * ya + _sigmoid(gb_ref[...]) * yb
    y1 = jnp.dot(merged.astype(bf), wo_ref[...], preferred_element_type=f32)
    h = x_ref[...] + gt_ref[0] * y1
    h_ref[...] = h
    n2_ref[...] = _modulated_norm(h, sc_ref[0], sh_ref[0])


def merge(o_a, o_b, proj, x2, gt1, scale2, shift2, w_pa, w_pb, w_o, seq):
    n, d = x2.shape
    tm = PROJ_TM
    per = seq // tm
    bf = jnp.bfloat16
    vec = pl.BlockSpec((1, 1, d), lambda i: (i // per, 0, 0))

    def col(name):
        off, w = PROJ_DST[name]
        return pl.BlockSpec((tm, w), lambda i: (i, off // w))
    row = lambda w: pl.BlockSpec((tm, w), lambda i: (i, 0))
    res = lambda a, b: pl.BlockSpec((a, b), lambda i: (0, 0))
    return pl.pallas_call(
        _merge_body, grid=(n // tm,),
        in_specs=[pl.BlockSpec((1, A_HEADS, A_HEAD_DIM, tm), lambda i: (i // per, 0, 0, i % per)), row(B_V_WIDTH), col("zb"), col("gate_a"), col("gate_b"), row(d), vec, vec, vec,
                  res(A_WIDTH, d), res(B_V_WIDTH, d), res(d, d)],
        out_specs=[row(d), row(d)],
        out_shape=[jax.ShapeDtypeStruct((n, d), jnp.float32), jax.ShapeDtypeStruct((n, d), jnp.float32)],
        compiler_params=pltpu.CompilerParams(dimension_semantics=("arbitrary",), vmem_limit_bytes=PROJ_VMEM_LIMIT),
        name="merge")(o_a, o_b, proj, proj, proj, x2, gt1, scale2, shift2,
                      w_pa.astype(bf), w_pb.astype(bf), w_o.astype(bf))


PEER_SLOTS = PEER_HEADS * PEER_TOPK
PEER_TB = 128
HALF_ROWS = 4
HI_MASK = -65536


def pack_table(tab, pad=False):
    bits = lax.bitcast_convert_type(tab.astype(jnp.bfloat16), jnp.uint16).astype(jnp.uint32)
    half = tab.shape[1] // 2
    word = bits[:, :half] | (bits[:, half:] << 16)
    rows = lax.bitcast_convert_type(word, jnp.int32).reshape(tab.shape[0] * HALF_ROWS, LANES)
    return jnp.pad(rows, ((HALF_ROWS, HALF_ROWS), (0, 0))) if pad else rows


def pair_load_starts(idx):
    even = (1 - jnp.arange(idx.shape[1], dtype=jnp.int32) % 2)[None, :]
    return idx + HALF_ROWS * even


def _table_rows(tab_ref, starts, k):
    span = 2 * HALF_ROWS
    a = tab_ref[pl.ds(pl.multiple_of(starts[k], HALF_ROWS), span), :]
    b = tab_ref[pl.ds(pl.multiple_of(starts[k + 1], HALF_ROWS), span), :]
    sub = lax.broadcasted_iota(jnp.int32, (span, LANES), 0)
    return _unpack(jnp.where(sub < HALF_ROWS, a, b))


def _table_row(tab_ref, rows, k):
    return _unpack(tab_ref[pl.ds(pl.multiple_of(rows[k], HALF_ROWS), HALF_ROWS), :])


def _unpack(w):
    lo = lax.bitcast_convert_type(lax.shift_left(w, 16), jnp.float32)
    hi = lax.bitcast_convert_type(w & jnp.int32(HI_MASK), jnp.float32)
    return lo, hi


def _peer_u_body(idx_ref, x_ref, gate_ref, tab_ref, o_ref, s_ref, sb_ref):
    def lane_sums(t):
        a = jnp.sum(sb_ref[t].T, axis=0, keepdims=True)
        o_ref[pl.ds(t, 1), :] = 0.5 * a * (1.0 + lax.erf(a * (2.0 ** -0.5))) * gate_ref[pl.ds(t, 1), :]

    sb_ref[0] = jnp.zeros(sb_ref.shape[1:], jnp.float32)

    def tok(t, carry):
        lane_sums(jnp.maximum(t - 1, 0))
        x8 = x_ref[pl.ds(t, 1), :].reshape(2 * HALF_ROWS, LANES)
        xlo = jnp.concatenate([x8[0:HALF_ROWS]] * 2, axis=0)
        xhi = jnp.concatenate([x8[HALF_ROWS:2 * HALF_ROWS]] * 2, axis=0)
        starts = idx_ref.at[t]
        for k in range(0, PEER_SLOTS, 2):
            lo, hi = _table_rows(tab_ref, starts, k)
            s_ref[HALF_ROWS * k:HALF_ROWS * (k + 2), :] = lo * xlo + hi * xhi
        s4 = s_ref[pl.ds(0, PEER_SLOTS, stride=HALF_ROWS), :]
        for r in range(1, HALF_ROWS):
            s4 = s4 + s_ref[pl.ds(r, PEER_SLOTS, stride=HALF_ROWS), :]
        sb_ref[t] = s4
        return carry
    lax.fori_loop(0, PEER_TB, tok, 0, unroll=2)
    lane_sums(PEER_TB - 1)


def _peer_v_body(idx_ref, coef_ref, tab_ref, o_ref, cb_ref):
    def spread(t):
        return jnp.broadcast_to(coef_ref[pl.ds(t, 1), :], (PEER_SLOTS, LANES)).T

    def tok(t, weights):
        nxt = spread(jnp.minimum(t + 1, PEER_TB - 1))
        cb_ref[...] = weights
        nacc = 2
        acc = [jnp.zeros((HALF_ROWS, LANES), jnp.float32) for _ in range(2 * nacc)]
        rows = idx_ref.at[t]
        for k in range(PEER_SLOTS):
            lo, hi = _table_row(tab_ref, rows, k)
            c = jnp.broadcast_to(cb_ref[k:k + 1, :], (HALF_ROWS, LANES))
            a = k % nacc
            acc[2 * a] = acc[2 * a] + c * lo
            acc[2 * a + 1] = acc[2 * a + 1] + c * hi
        half = HALF_ROWS * LANES
        o_ref[pl.ds(t, 1), 0:half] = (acc[0] + acc[2]).reshape(1, half)
        o_ref[pl.ds(t, 1), half:2 * half] = (acc[1] + acc[3]).reshape(1, half)
        return nxt
    lax.fori_loop(0, PEER_TB, tok, spread(0))


def _table_spec(tab):
    return pl.BlockSpec(tab.shape, lambda i: (0, 0), pipeline_mode=pl.Buffered(1))


def peer_u(idx, x, gates, tab):
    n = idx.shape[0]
    tb = PEER_TB
    return pl.pallas_call(
        _peer_u_body, grid=(n // tb,),
        in_specs=[pl.BlockSpec((tb, PEER_SLOTS), lambda i: (i, 0), memory_space=pltpu.SMEM),
                  pl.BlockSpec((tb, x.shape[1]), lambda i: (i, 0)),
                  pl.BlockSpec((tb, PEER_SLOTS), lambda i: (i, 0)),
                  _table_spec(tab)],
        out_specs=pl.BlockSpec((tb, PEER_SLOTS), lambda i: (i, 0)),
        out_shape=jax.ShapeDtypeStruct((n, PEER_SLOTS), jnp.float32),
        scratch_shapes=[pltpu.VMEM((HALF_ROWS * PEER_SLOTS, LANES), jnp.float32),
                        pltpu.VMEM((tb, PEER_SLOTS, LANES), jnp.float32)],
        compiler_params=pltpu.CompilerParams(dimension_semantics=("arbitrary",), vmem_limit_bytes=PEER_VMEM_LIMIT),
        name="peer_u")(idx, x, gates, tab)


def peer_v(idx, coef, tab):
    n = idx.shape[0]
    tb = PEER_TB
    return pl.pallas_call(
        _peer_v_body, grid=(n // tb,),
        in_specs=[pl.BlockSpec((tb, PEER_SLOTS), lambda i: (i, 0), memory_space=pltpu.SMEM),
                  pl.BlockSpec((tb, PEER_SLOTS), lambda i: (i, 0)),
                  _table_spec(tab)],
        out_specs=pl.BlockSpec((tb, 2 * HALF_ROWS * LANES), lambda i: (i, 0)),
        out_shape=jax.ShapeDtypeStruct((n, 2 * HALF_ROWS * LANES), jnp.float32),
        scratch_shapes=[pltpu.VMEM((PEER_SLOTS, LANES), jnp.float32)],
        compiler_params=pltpu.CompilerParams(dimension_semantics=("arbitrary",), vmem_limit_bytes=PEER_VMEM_LIMIT),
        name="peer_v")(idx, coef, tab)


PEER_TT = 256


def _extract_top(ref, n_out, rid=None):
    rows, t = ref.shape
    if rid is None:
        rid = lax.broadcasted_iota(jnp.int32, (rows, t), 0)
    vals, idxs = [], []
    for _ in range(n_out):
        s = ref[...]
        m = jnp.max(s, axis=0, keepdims=True)
        ix = jnp.min(jnp.where(s == m, rid, jnp.int32(PAD_ID)), axis=0, keepdims=True)
        ref[...] = jnp.where(rid == ix, -jnp.inf, s)
        vals.append(m)
        idxs.append(ix)
    return vals, idxs


def _peer_route_body(x_ref, wq_ref, sk_ref, cid_ref, idx_ref, gate_ref, s_ref, cand_ref, v_ref, i_ref, et_ref, gt_ref):
    kk, nk = PEER_TOPK, PEER_N_KEYS
    f32 = jnp.float32
    q = jnp.dot(x_ref[...].astype(jnp.bfloat16), wq_ref[...], preferred_element_type=f32).astype(jnp.bfloat16)
    nt = (((1,), (1,)), ((), ()))
    row16 = lax.broadcasted_iota(jnp.int32, (kk, PEER_TT), 0)
    for h in range(PEER_HEADS):
        qh = q[:, h * PEER_KEY_DIM:(h + 1) * PEER_KEY_DIM]
        s_ref[...] = lax.dot_general(sk_ref[h], qh, nt, preferred_element_type=f32)
        for p in range(2):
            vals, idxs = _extract_top(s_ref.at[p * nk:(p + 1) * nk, :], kk)
            for i in range(kk):
                v_ref[p, i:i + 1, :] = vals[i]
                i_ref[p, i:i + 1, :] = idxs[i]
        off = 0
        for i in range(kk):
            n_j = kk // (i + 1)
            cand_ref[off:off + n_j, :] = v_ref[0, i:i + 1, :] + v_ref[1, 0:n_j, :]
            off += n_j
        cand_ref[off:, :] = jnp.full((cand_ref.shape[0] - off, PEER_TT), -jnp.inf, f32)
        vals, cis = _extract_top(cand_ref, kk, cid_ref[...])
        i1, i2 = i_ref[0], i_ref[1]
        es = [jnp.exp(v - vals[0]) for v in vals]
        den = es[0]
        for e in es[1:]:
            den = den + e
        for k in range(kk):
            ci = cis[k]
            e1 = jnp.sum(jnp.where(row16 == lax.shift_right_logical(ci, kk.bit_length() - 1), i1, 0), axis=0, keepdims=True)
            e2 = jnp.sum(jnp.where(row16 == (ci & (kk - 1)), i2, 0), axis=0, keepdims=True)
            et_ref[h * kk + k:h * kk + k + 1, :] = (e1 * nk + e2) * HALF_ROWS
            gt_ref[h * kk + k:h * kk + k + 1, :] = es[k] / den
    idx_ref[...] = lax.bitcast_convert_type(lax.bitcast_convert_type(et_ref[...], f32).T, jnp.int32)
    gate_ref[...] = gt_ref[...].T


def _pair_cells():
    kk = PEER_TOPK
    ids = [i * kk + j for i in range(kk) for j in range(kk // (i + 1))]
    return ids + [PAD_ID] * (-len(ids) % SUBLANES)


def peer_route(xn, wq, subkeys):
    n = xn.shape[0]
    tt = PEER_TT
    cells = _pair_cells()
    cid = jnp.broadcast_to(jnp.asarray(cells, jnp.int32)[:, None], (len(cells), tt))
    half = PEER_KEY_DIM // 2
    z = jnp.zeros((PEER_HEADS, PEER_N_KEYS, half), subkeys.dtype)
    skbd = jnp.concatenate([jnp.concatenate([subkeys[:, 0], z], axis=-1),
                            jnp.concatenate([z, subkeys[:, 1]], axis=-1)], axis=1).astype(jnp.bfloat16)
    return pl.pallas_call(
        _peer_route_body, grid=(n // tt,),
        in_specs=[pl.BlockSpec((tt, D_MODEL), lambda i: (i, 0)),
                  pl.BlockSpec((D_MODEL, PEER_HEADS * PEER_KEY_DIM), lambda i: (0, 0)),
                  pl.BlockSpec((PEER_HEADS, 2 * PEER_N_KEYS, PEER_KEY_DIM), lambda i: (0, 0, 0)),
                  pl.BlockSpec((len(cells), tt), lambda i: (0, 0))],
        out_specs=[pl.BlockSpec((tt, PEER_SLOTS), lambda i: (i, 0)), pl.BlockSpec((tt, PEER_SLOTS), lambda i: (i, 0))],
        out_shape=[jax.ShapeDtypeStruct((n, PEER_SLOTS), jnp.int32), jax.ShapeDtypeStruct((n, PEER_SLOTS), jnp.float32)],
        scratch_shapes=[pltpu.VMEM((2 * PEER_N_KEYS, tt), jnp.float32),
                        pltpu.VMEM((len(cells), tt), jnp.float32),
                        pltpu.VMEM((2, PEER_TOPK, tt), jnp.float32),
                        pltpu.VMEM((2, PEER_TOPK, tt), jnp.int32),
                        pltpu.VMEM((PEER_SLOTS, tt), jnp.int32),
                        pltpu.VMEM((PEER_SLOTS, tt), jnp.float32)],
        compiler_params=pltpu.CompilerParams(dimension_semantics=("arbitrary",), vmem_limit_bytes=ROUTE_VMEM_LIMIT),
        name="peer_route")(xn, wq.astype(jnp.bfloat16), skbd, cid)


def peer_channel_mixer(xn, wq, subkeys, u_tab, v_tab):
    bsz, seq, d = xn.shape
    n_tok = bsz * seq
    x2 = xn.reshape(n_tok, d)
    idx, gates = peer_route(x2, wq, subkeys)
    coef = peer_u(pair_load_starts(idx), x2, gates, pack_table(u_tab, pad=True))
    out = peer_v(idx, coef, pack_table(v_tab))
    return out.reshape(bsz, seq, d)


def _residual_body(h_ref, y_ref, gt_ref, o_ref):
    o_ref[...] = h_ref[...] + gt_ref[0] * y_ref[...]


def _final_norm_body(h_ref, y_ref, gt_ref, g_ref, o_ref):
    h = h_ref[...] + gt_ref[0] * y_ref[...]
    o_ref[...] = h * lax.rsqrt(jnp.mean(h * h, axis=-1, keepdims=True) + EPS) * g_ref[...]


def gated_residual(h2, y2, gt, seq, gain=None):
    n, d = h2.shape
    tm = min(1024, seq)
    per = seq // tm
    row = pl.BlockSpec((tm, d), lambda i: (i, 0))
    specs = [row, row, pl.BlockSpec((1, 1, d), lambda i: (i // per, 0, 0))]
    args = [h2, y2, gt]
    if gain is not None:
        specs.append(pl.BlockSpec((1, d), lambda i: (0, 0)))
        args.append(gain.astype(jnp.float32).reshape(1, d))
    return pl.pallas_call(
        _residual_body if gain is None else _final_norm_body, grid=(n // tm,),
        in_specs=specs, out_specs=row, out_shape=jax.ShapeDtypeStruct((n, d), h2.dtype),
        name="gated_residual" if gain is None else "final_norm")(*args)


def kernel(x, c, positions, w_ada, b_ada, w_in, conv_w, a_log, dt_bias, norm_b_w,
           w_pa, w_pb, w_o, peer_wq, peer_subkeys, peer_u, peer_v, final_norm_w):
    bsz, seq, d = x.shape
    h = x.reshape(bsz * seq, d)
    for layer in range(DEPTH):
        mod = ada_modulation(c, w_ada[layer], b_ada[layer])
        sh1, sc1, gt1, sh2, sc2, gt2 = [m.reshape(bsz, 1, d) for m in jnp.split(mod, 6, axis=-1)]
        proj, qit, qat, w, kit, kat, vat, bg = in_proj(h, 1.0 + sc1, sh1, permute_w_in(w_in[layer]), positions,
                                                       a_log[layer], dt_bias[layer], bsz, seq)
        o_a = dsa_attention_pallas(qit, w, qat, kit, kat, vat)
        o_b = gated_delta_rule_pallas(proj, conv_w[layer], bg[:, B_HEADS:], bg[:, :B_HEADS], norm_b_w[layer],
                                      bsz, seq)
        h, n2 = merge(o_a, o_b.reshape(bsz * seq, B_V_WIDTH), proj, h, gt1,
                      1.0 + sc2, sh2, w_pa[layer], w_pb[layer], w_o[layer], seq)
        y2 = peer_channel_mixer(n2.reshape(bsz, seq, d), peer_wq[layer], peer_subkeys[layer], peer_u[layer],
                                peer_v[layer])
        last = layer == DEPTH - 1
        h = gated_residual(h, y2.reshape(bsz * seq, d), gt2, seq, final_norm_w if last else None)
    return h.reshape(bsz, seq, d)
```

```python
import functools

import jax, jax.numpy as jnp
from jax import lax
from jax.experimental import pallas as pl
from jax.experimental.pallas import tpu as pltpu

D_MODEL = 1024
DEPTH = 1

LANES = 128
SUBLANES = 8
MIB = 1024 * 1024
DSA_VMEM_LIMIT = 48 * MIB
GDN_VMEM_LIMIT = 40 * MIB
PROJ_VMEM_LIMIT = 48 * MIB
PEER_VMEM_LIMIT = 48 * MIB
ROUTE_VMEM_LIMIT = 32 * MIB

A_HEADS = 8
A_KV_HEADS = 2
A_HEAD_DIM = 64
IDX_HEADS = 16
IDX_DIM = 64
IDX_TOPK_MAX = 256
B_HEADS = 8
B_KEY_DIM = 64
B_VAL_DIM = 64
CONV_WIDTH = 4
CHUNK = 64
ROPE_THETA = 500000.0
ROPE_FRACTION_DEN = 4
PEER_HEADS = 8
PEER_KEY_DIM = 128
PEER_N_KEYS = 128
PEER_TOPK = 16
EPS = 1e-6

A_WIDTH = A_HEADS * A_HEAD_DIM
KV_WIDTH = A_KV_HEADS * A_HEAD_DIM
B_QK_WIDTH = B_HEADS * B_KEY_DIM
B_V_WIDTH = B_HEADS * B_VAL_DIM
IN_SPLITS = (A_WIDTH, KV_WIDTH, KV_WIDTH, IDX_HEADS * IDX_DIM, IDX_DIM, IDX_HEADS,
             B_QK_WIDTH, B_QK_WIDTH, B_V_WIDTH, B_V_WIDTH, B_HEADS, B_HEADS, D_MODEL, D_MODEL)


DSA_TQ = 256
DSA_TK = 256
INT_MIN = -2**31
I16_MIN = -2**15
NEG_BIG = -1e30
PAD_ID = 2**30


def _dsa_body(topk, idx_bits, qit_ref, w_ref, qat_ref, ki_ref, ka_ref, vat_ref, o_ref, key_ref, hi_ref, lo_ref):
    tq, tk = DSA_TQ, DSA_TK
    qb = pl.program_id(1)
    n_kv = qb + 1
    t_glob = qb * tq + lax.broadcasted_iota(jnp.int32, (1, tq), 1)
    row = lax.broadcasted_iota(jnp.int32, (tk, 1), 0)
    f32 = jnp.float32

    def p1(j, carry):
        kt = ki_ref[0, j]
        score = jnp.zeros((tk, tq), f32)
        for h in range(IDX_HEADS):
            lt = jnp.dot(kt, qit_ref[0, h], preferred_element_type=f32)
            score = score + w_ref[0, h:h + 1, :] * jnp.maximum(lt, 0.0)
        bits = lax.bitcast_convert_type(score + 0.0, jnp.int32)
        skey = jnp.where(bits >= 0, bits, bits ^ jnp.int32(0x7FFFFFFF))
        skey = jnp.where(j * tk + row <= t_glob, skey, jnp.int32(INT_MIN))
        key_ref[j] = skey
        hi_ref[j] = lax.shift_right_arithmetic(skey, 16).astype(jnp.int16)
        return carry
    lax.fori_loop(0, n_kv, p1, 0)

    key_ref[n_kv] = jnp.full((tk, tq), INT_MIN, jnp.int32)
    hi_ref[n_kv] = jnp.full((tk, tq), I16_MIN, jnp.int16)
    lo_ref[n_kv] = jnp.full((tk, tq), I16_MIN, jnp.int16)

    def count(pred):
        def body(jj, acc):
            for j in (2 * jj, 2 * jj + 1):
                hit = jnp.where(pred(key_ref[j], j * tk + row), 1.0, 0.0)
                acc = acc + jnp.sum(hit.reshape(tk // SUBLANES, SUBLANES, tq), axis=0)
            return acc
        acc = lax.fori_loop(0, (n_kv + 1) // 2, body, jnp.zeros((SUBLANES, tq), f32))
        return jnp.sum(acc, axis=0, keepdims=True)

    kf = jnp.float32(topk)
    half = 16
    pack = 2 * SUBLANES

    def count16(ref, pred):
        def body(jj, acc):
            for j in (2 * jj, 2 * jj + 1):
                hit = jnp.where(pred(ref[j]), jnp.bfloat16(1), jnp.bfloat16(0)).reshape(tk // pack, pack, tq)
                part = hit[0]
                for g in range(1, tk // pack):
                    part = part + hit[g]
                acc = acc + part.astype(f32)
            return acc
        acc = lax.fori_loop(0, (n_kv + 1) // 2, body, jnp.zeros((pack, tq), f32))
        return jnp.sum(acc, axis=0, keepdims=True)

    def search16(ref, base):
        def step(i, u):
            cand_u = u | lax.shift_left(jnp.int32(1), half - 1 - i)
            cand = (cand_u + I16_MIN).astype(jnp.int16)
            c = base + count16(ref, lambda v: v >= cand)
            return jnp.where(c >= kf, cand_u, u)
        return lax.fori_loop(0, half, step, jnp.zeros((1, tq), jnp.int32))

    hi_u = search16(hi_ref, jnp.zeros((1, tq), f32))
    hi_s = hi_u + I16_MIN
    hi_s16 = hi_s.astype(jnp.int16)
    above = count16(hi_ref, lambda v: v > hi_s16)

    def lows(j, carry):
        lo = ((key_ref[j] & 0xFFFF) + I16_MIN).astype(jnp.int16)
        lo_ref[j] = jnp.where(hi_ref[j] == hi_s16, lo, jnp.int16(I16_MIN))
        return carry
    lax.fori_loop(0, n_kv, lows, 0)
    lo_u = search16(lo_ref, above)
    kth = hi_s * 65536 + lo_u
    c_gt = count(lambda k, s: k > kth)
    c_ge = count(lambda k, s: k >= kth)
    short = kth == jnp.int32(INT_MIN)
    x0 = jnp.where(short, jnp.int32(-1), jnp.int32(PAD_ID))
    need = kf - c_gt
    has_tie = jnp.max(jnp.where(jnp.logical_and(c_ge > kf, jnp.logical_not(short)), 1.0, 0.0)) > 0.0

    def tie_search():
        def step(i, x):
            bit = lax.shift_left(jnp.int32(1), idx_bits - 1 - i)
            probe = x + bit - 1
            c = count(lambda k, s: jnp.logical_and(k == kth, s <= probe))
            return jnp.where(c < need, x + bit, x)
        x = lax.fori_loop(0, idx_bits, step, jnp.zeros((1, tq), jnp.int32))
        return jnp.where(short, jnp.int32(-1), x)
    x_lim = lax.cond(has_tie, tie_search, lambda: x0)

    rep = A_HEADS // A_KV_HEADS
    hs = range(A_HEADS)

    def p3(j, carry):
        m, l, acc = carry
        skey = key_ref[j]
        s_idx = j * tk + row
        sel = jnp.logical_or(skey > kth, jnp.logical_and(skey == kth, s_idx <= x_lim))
        kt = [ka_ref[0, g, j] for g in range(A_KV_HEADS)]
        vt = [vat_ref[0, g, j] for g in range(A_KV_HEADS)]
        s = [jnp.where(sel, jnp.dot(kt[h // rep], qat_ref[0, h], preferred_element_type=f32), NEG_BIG) for h in hs]
        m_new = [jnp.maximum(m[h], jnp.max(s[h], axis=0, keepdims=True)) for h in hs]
        alpha = [jnp.exp(m[h] - m_new[h]) for h in hs]
        p = [jnp.exp(s[h] - m_new[h]) for h in hs]
        l_new = [alpha[h] * l[h] + jnp.sum(p[h], axis=0, keepdims=True) for h in hs]
        acc_new = [alpha[h] * acc[h] + jnp.dot(vt[h // rep], p[h].astype(jnp.bfloat16), preferred_element_type=f32)
                   for h in hs]
        return tuple(m_new), tuple(l_new), tuple(acc_new)

    init = (tuple(jnp.full((1, tq), NEG_BIG, f32) for _ in hs), tuple(jnp.zeros((1, tq), f32) for _ in hs),
            tuple(jnp.zeros((A_HEAD_DIM, tq), f32) for _ in hs))
    _, l_fin, acc_fin = lax.fori_loop(0, n_kv, p3, init)
    for h in hs:
        o_ref[0, h] = acc_fin[h] / l_fin[h]


ROPE_HALF = A_HEAD_DIM // ROPE_FRACTION_DEN // 2


def _rope_t(xt, n_heads, cos, sin, scale=None):
    outs = []
    for h in range(n_heads):
        b = h * A_HEAD_DIM
        x1, x2, rest = xt[b:b + ROPE_HALF], xt[b + ROPE_HALF:b + 2 * ROPE_HALF], xt[b + 2 * ROPE_HALF:b + A_HEAD_DIM]
        o = jnp.concatenate([x1 * cos - x2 * sin, x2 * cos + x1 * sin, rest], axis=0)
        outs.append(o if scale is None else o * scale)
    return outs


def dsa_attention_pallas(qit, w, qat, kit, kat, vat):
    bsz, seq = qit.shape[0], qit.shape[-1]
    tq, tk = DSA_TQ, DSA_TK
    topk = min(IDX_TOPK_MAX, seq // 4)
    n_kv = seq // tk
    return pl.pallas_call(
        functools.partial(_dsa_body, topk, (seq - 1).bit_length()),
        grid=(bsz, seq // tq),
        in_specs=[
            pl.BlockSpec((1, IDX_HEADS, IDX_DIM, tq), lambda b, q: (b, 0, 0, q)),
            pl.BlockSpec((1, IDX_HEADS, tq), lambda b, q: (b, 0, q)),
            pl.BlockSpec((1, A_HEADS, A_HEAD_DIM, tq), lambda b, q: (b, 0, 0, q)),
            pl.BlockSpec((1, n_kv, tk, IDX_DIM), lambda b, q: (b, 0, 0, 0)),
            pl.BlockSpec((1, A_KV_HEADS, n_kv, tk, A_HEAD_DIM), lambda b, q: (b, 0, 0, 0, 0)),
            pl.BlockSpec((1, A_KV_HEADS, n_kv, A_HEAD_DIM, tk), lambda b, q: (b, 0, 0, 0, 0)),
        ],
        out_specs=pl.BlockSpec((1, A_HEADS, A_HEAD_DIM, tq), lambda b, q: (b, 0, 0, q)),
        out_shape=jax.ShapeDtypeStruct((bsz, A_HEADS, A_HEAD_DIM, seq), jnp.float32),
        scratch_shapes=[
            pltpu.VMEM((n_kv + 1, tk, tq), jnp.int32),
            pltpu.VMEM((n_kv + 1, tk, tq), jnp.int16),
            pltpu.VMEM((n_kv + 1, tk, tq), jnp.int16),
        ],
        compiler_params=pltpu.CompilerParams(dimension_semantics=("arbitrary", "arbitrary"),
                                             vmem_limit_bytes=DSA_VMEM_LIMIT),
        name="dsa_attention",
    )(qit, w, qat, kit, kat, vat)


GDN_G = 8
GDN_BASE = 8
GDN_PAD = 8


def _gdn_body(q_ref, k_ref, v_ref, cw_ref, gc_ref, bt_ref, nw_ref, o_ref, s_ref, xs_ref, cs_ref, ob_ref):
    c_sz = CHUNK
    f32, bf = jnp.float32, jnp.bfloat16

    rows = GDN_G * c_sz
    halo = CONV_WIDTH - 1

    @pl.when(pl.program_id(1) == 0)
    def _():
        s_ref[...] = jnp.zeros(s_ref.shape, f32)
        xs_ref[:, 0:GDN_PAD, :] = jnp.zeros((3, GDN_PAD, xs_ref.shape[-1]), f32)

    for j, ref in enumerate((q_ref, k_ref, v_ref)):
        xs_ref[j, GDN_PAD:GDN_PAD + rows, :] = ref[0]
        acc = xs_ref[j, GDN_PAD - halo:GDN_PAD - halo + rows, :] * cw_ref[j, 0:1, :]
        for i in range(1, CONV_WIDTH):
            acc = acc + xs_ref[j, GDN_PAD - halo + i:GDN_PAD - halo + i + rows, :] * cw_ref[j, i:i + 1, :]
        y = acc * _sigmoid(acc)
        xs_ref[j, GDN_PAD - halo:GDN_PAD, :] = xs_ref[j, GDN_PAD + rows - halo:GDN_PAD + rows, :]
        for h in range(B_HEADS):
            piece = y[:, h * B_KEY_DIM:(h + 1) * B_KEY_DIM]
            if j < 2:
                piece = piece * lax.rsqrt(jnp.sum(piece * piece, axis=-1, keepdims=True) + EPS)
            cs_ref[j, h] = piece * (B_KEY_DIM ** -0.5) if j == 0 else piece

    ri = lax.broadcasted_iota(jnp.int32, (c_sz, c_sz), 0)
    ci = lax.broadcasted_iota(jnp.int32, (c_sz, c_sz), 1)
    incl, strict = ri >= ci, ri > ci
    eye = jnp.where(ri == ci, 1.0, 0.0).astype(f32)
    blk = lambda w: (ri // w) == (ci // w)
    diag8 = blk(GDN_BASE)
    sub_blocks = []
    w = GDN_BASE
    while w < c_sz:
        sub_blocks.append(jnp.logical_and(blk(2 * w), jnp.logical_not(blk(w))))
        w *= 2
    nt = (((1,), (1,)), ((), ()))
    dot = lambda a, b: jnp.dot(a.astype(bf), b.astype(bf), preferred_element_type=f32)
    dot_nt = lambda a, b: lax.dot_general(a.astype(bf), b.astype(bf), nt, preferred_element_type=f32)

    def split(a):
        hi = a.astype(bf)
        return hi, (a - hi.astype(f32)).astype(bf)

    def hp(a, b):
        a_hi, a_lo = split(a)
        b_hi, b_lo = split(b)
        mm = lambda u, w: jnp.dot(u, w, preferred_element_type=f32)
        return mm(a_hi, b_hi) + (mm(a_hi, b_lo) + mm(a_lo, b_hi))

    hs = range(B_HEADS)

    def chunk_pair(i, carry):
        cs = [2 * i, 2 * i + 1]
        r0 = [pl.multiple_of(c * c_sz, c_sz) for c in cs]
        items = [(j, h) for j in range(2) for h in hs]
        hd = lambda a, j, h: cs_ref[a, h, pl.ds(r0[j], c_sz), :]
        q = [hd(0, j, h) for j, h in items]
        k = [hd(1, j, h) for j, h in items]
        v = [hd(2, j, h) for j, h in items]
        gc8 = [gc_ref[0, c] for c in cs]
        gct = [g8.T for g8 in gc8]
        btt = [bt_ref[0, c].T for c in cs]
        gcr = [gc8[j][h:h + 1, :] for j, h in items]
        n_it = range(len(items))
        gcc = [gct[j][:, h:h + 1] for j, h in items]
        beta = [btt[j][:, h:h + 1] for j, h in items]
        decay = [jnp.exp(jnp.where(incl, gcc[n] - gcr[n], -jnp.inf)) for n in n_it]
        kb = [k[n] * beta[n] for n in n_it]
        vb = [v[n] * beta[n] for n in n_it]
        low = [jnp.where(strict, dot_nt(kb[n], k[n]) * decay[n], 0.0) for n in n_it]
        dg = [jnp.where(diag8, low[n], 0.0) for n in n_it]
        t = [eye - dg[n] for n in n_it]
        p = [hp(dg[n], dg[n]) for n in n_it]
        t = [hp(t[n], eye + p[n]) for n in n_it]
        p = [hp(p[n], p[n]) for n in n_it]
        t = [hp(t[n], eye + p[n]) for n in n_it]
        for below in sub_blocks:
            lb = [jnp.where(below, low[n], 0.0) for n in n_it]
            lt = [hp(lb[n], t[n]) for n in n_it]
            t = [t[n] - hp(t[n], lt[n]) for n in n_it]
        u = [dot(t[n], vb[n]) for n in n_it]
        kcd = [dot(t[n], kb[n] * jnp.exp(gcc[n])) for n in n_it]
        intra = [dot_nt(q[n], k[n]) * decay[n] for n in n_it]
        qg = [q[n] * jnp.exp(gcc[n]) for n in n_it]
        glast = [g[:, c_sz - 1:c_sz] for g in gcr]
        kdt = [(k[n] * jnp.exp(glast[n] - gcc[n])).T for n in n_it]
        s = [s_ref[h] for h in hs]
        for j in range(2):
            ix = [j * B_HEADS + h for h in hs]
            v_new = [u[ix[h]] - dot(kcd[ix[h]], s[h]) for h in hs]
            out = [dot(qg[ix[h]], s[h]) + dot(intra[ix[h]], v_new[h]) for h in hs]
            s = [s[h] * jnp.exp(glast[ix[h]]) + dot(kdt[ix[h]], v_new[h]) for h in hs]
            for h in hs:
                o = out[h]
                o = o * lax.rsqrt(jnp.mean(o * o, axis=-1, keepdims=True) + EPS) * nw_ref[...]
                ob_ref[h, pl.ds(r0[j], c_sz), :] = o
        for h in hs:
            s_ref[h] = s[h]
        return carry
    lax.fori_loop(0, GDN_G // 2, chunk_pair, 0)
    o_ref[0] = jnp.concatenate([ob_ref[h] for h in hs], axis=-1)


def gated_delta_rule_pallas(proj, conv_w, g, beta, norm_w, bsz, seq):
    nh, dk, dv = B_HEADS, B_KEY_DIM, B_VAL_DIM
    n = seq // CHUNK
    chunked = lambda t: jnp.moveaxis(t.reshape(bsz, nh, n, CHUNK), 1, 2)
    gc = chunked(jnp.cumsum(g.reshape(bsz, nh, n, CHUNK), axis=-1))
    side = pl.BlockSpec((1, GDN_G, nh, CHUNK), lambda b, c: (b, c, 0, 0))
    rows = GDN_G * CHUNK
    width = nh * dk

    def col(name):
        off, w = PROJ_DST[name]
        assert w == width
        return pl.BlockSpec((1, rows, w), lambda b, c: (b, c, off // w))
    cw = jnp.transpose(conv_w.astype(jnp.float32).reshape(CONV_WIDTH, 3, width), (1, 0, 2))
    proj3 = proj.reshape(bsz, seq, proj.shape[-1])
    return pl.pallas_call(
        _gdn_body, grid=(bsz, n // GDN_G),
        in_specs=[col("qb"), col("kb"), col("vb"),
                  pl.BlockSpec((3, CONV_WIDTH, width), lambda b, c: (0, 0, 0)),
                  side, side, pl.BlockSpec((1, dv), lambda b, c: (0, 0))],
        out_specs=pl.BlockSpec((1, rows, nh * dv), lambda b, c: (b, c, 0)),
        out_shape=jax.ShapeDtypeStruct((bsz, seq, nh * dv), jnp.float32),
        scratch_shapes=[pltpu.VMEM((nh, dk, dv), jnp.float32),
                        pltpu.VMEM((3, GDN_PAD + rows, width), jnp.float32),
                        pltpu.VMEM((3, nh, rows, dk), jnp.float32),
                        pltpu.VMEM((nh, rows, dv), jnp.float32)],
        compiler_params=pltpu.CompilerParams(dimension_semantics=("arbitrary", "arbitrary"),
                                             vmem_limit_bytes=GDN_VMEM_LIMIT),
        name="gated_delta_rule")(proj3, proj3, proj3, cw, gc, chunked(beta), norm_w.astype(jnp.float32).reshape(1, dv))


IN_NAMES = ("qa", "ka", "va", "qi", "ki", "wi", "qb", "kb", "vb", "zb", "bb", "ab", "gate_a", "gate_b")
PROJ_ORDER = ("gate_a", "gate_b", "qb", "kb", "vb", "zb", "qi", "qa", "ka", "va", "ki", "wi", "bb", "ab")


def _proj_layout():
    src, off = {}, 0
    for name, w in zip(IN_NAMES, IN_SPLITS):
        src[name] = (off, w)
        off += w
    dst, off = {}, 0
    for name in PROJ_ORDER:
        dst[name] = (off, src[name][1])
        off += src[name][1]
    return src, dst, -(-off // LANES) * LANES


PROJ_SRC, PROJ_DST, PROJ_WIDTH = _proj_layout()
MAIN_WIDTH = PROJ_DST["qi"][0]
PROJ_TM = 512


def permute_w_in(w_in):
    cols = [w_in[:, PROJ_SRC[n][0]:PROJ_SRC[n][0] + PROJ_SRC[n][1]] for n in PROJ_ORDER]
    cols.append(jnp.zeros((w_in.shape[0], PROJ_WIDTH - sum(c.shape[1] for c in cols)), w_in.dtype))
    return jnp.concatenate(cols, axis=1).astype(jnp.bfloat16)


def _modulated_norm(x, scale, shift):
    return x * lax.rsqrt(jnp.mean(x * x, axis=-1, keepdims=True) + EPS) * scale + shift


def _sigmoid(v):
    return 1.0 / (1.0 + jnp.exp(-v))


def _ada_body(c_ref, w_ref, b_ref, o_ref):
    c = c_ref[...]
    o_ref[...] = jnp.dot((c * _sigmoid(c)).astype(jnp.bfloat16), w_ref[...].astype(jnp.bfloat16),
                         preferred_element_type=jnp.float32) + b_ref[...]


def ada_modulation(c, w_ada, b_ada):
    bsz, d = c.shape
    n = w_ada.shape[1]
    tn = D_MODEL
    return pl.pallas_call(
        _ada_body, grid=(n // tn,),
        in_specs=[pl.BlockSpec((bsz, d), lambda j: (0, 0)), pl.BlockSpec((d, tn), lambda j: (0, j)),
                  pl.BlockSpec((1, tn), lambda j: (0, j))],
        out_specs=pl.BlockSpec((bsz, tn), lambda j: (0, j)),
        out_shape=jax.ShapeDtypeStruct((bsz, n), jnp.float32),
        name="ada_modulation")(c, w_ada, b_ada.reshape(1, n))


def _in_proj_body(x_ref, sc_ref, sh_ref, w_ref, cos_ref, sin_ref, dp_ref,
                  o_ref, qit_ref, qat_ref, wt_ref, kit_ref, kat_ref, vat_ref, bg_ref):
    bf, f32 = jnp.bfloat16, jnp.float32
    n1 = _modulated_norm(x_ref[...], sc_ref[0], sh_ref[0]).astype(bf)
    o_ref[...] = jnp.dot(n1, w_ref[:, 0:MAIN_WIDTH], preferred_element_type=f32)
    pa = jnp.dot(n1, w_ref[:, MAIN_WIDTH:PROJ_WIDTH], preferred_element_type=f32)

    def piece(name, width=None):
        off = PROJ_DST[name][0] - MAIN_WIDTH
        return pa[:, off:off + (width or PROJ_DST[name][1])]
    cos, sin = cos_ref[0], sin_ref[0]
    for h, o in enumerate(_rope_t(piece("qi").T, IDX_HEADS, cos, sin)):
        qit_ref[0, h] = o.astype(bf)
    for h, o in enumerate(_rope_t(piece("qa").T, A_HEADS, cos, sin, A_HEAD_DIM ** -0.5)):
        qat_ref[0, h] = o.astype(bf)
    kat = _rope_t(piece("ka").T, A_KV_HEADS, cos, sin)
    vt = piece("va").T
    smt = piece("ki", LANES).T
    kit = _rope_t(smt[0:IDX_DIM], 1, cos, sin)[0]
    for u in range(pa.shape[0] // DSA_TK):
        cols = slice(u * DSA_TK, (u + 1) * DSA_TK)
        for g in range(A_KV_HEADS):
            kat_ref[0, g, u] = kat[g][:, cols].T.astype(bf)
            vat_ref[0, g, u] = vt[g * A_HEAD_DIM:(g + 1) * A_HEAD_DIM, cols].astype(bf)
        kit_ref[0, u] = kit[:, cols].T.astype(bf)
    wt_ref[0] = smt[IDX_DIM:IDX_DIM + IDX_HEADS] * ((IDX_HEADS ** -0.5) * (IDX_DIM ** -0.5))
    b0 = IDX_DIM + IDX_HEADS
    bb, ab = smt[b0:b0 + B_HEADS], smt[b0 + B_HEADS:b0 + 2 * B_HEADS]
    a_log, dt_bias = dp_ref[:, 0:1], dp_ref[:, 1:2]
    bg_ref[0] = jnp.concatenate([_sigmoid(bb), -jnp.exp(a_log) * jax.nn.softplus(ab + dt_bias)], axis=0)


def in_proj(x2, scale, shift, w_perm, positions, a_log, dt_bias, bsz, seq):
    n, d = x2.shape
    tm = PROJ_TM
    per = seq // tm
    sub = tm // DSA_TK
    assert PROJ_ORDER[-4:] == ("ki", "wi", "bb", "ab") and PROJ_DST["ki"][0] + LANES == PROJ_WIDTH
    vec = pl.BlockSpec((1, 1, d), lambda i: (i // per, 0, 0))
    rd = A_HEAD_DIM // ROPE_FRACTION_DEN
    inv_freq = jnp.power(jnp.float32(ROPE_THETA), -jnp.arange(ROPE_HALF, dtype=jnp.float32) * (2.0 / rd))
    ang = positions.astype(jnp.float32)[:, None, :] * inv_freq[None, :, None]
    trig = pl.BlockSpec((1, ROPE_HALF, tm), lambda i: (i // per, 0, i % per))
    bf, f32 = jnp.bfloat16, jnp.float32
    sd = jax.ShapeDtypeStruct
    n_kv = seq // DSA_TK
    return pl.pallas_call(
        _in_proj_body, grid=(n // tm,),
        in_specs=[pl.BlockSpec((tm, d), lambda i: (i, 0)), vec, vec,
                  pl.BlockSpec((d, PROJ_WIDTH), lambda i: (0, 0), pipeline_mode=pl.Buffered(1)),
                  trig, trig, pl.BlockSpec((B_HEADS, 2), lambda i: (0, 0))],
        out_specs=[pl.BlockSpec((tm, MAIN_WIDTH), lambda i: (i, 0)),
                   pl.BlockSpec((1, IDX_HEADS, IDX_DIM, tm), lambda i: (i // per, 0, 0, i % per)),
                   pl.BlockSpec((1, A_HEADS, A_HEAD_DIM, tm), lambda i: (i // per, 0, 0, i % per)),
                   pl.BlockSpec((1, IDX_HEADS, tm), lambda i: (i // per, 0, i % per)),
                   pl.BlockSpec((1, sub, DSA_TK, IDX_DIM), lambda i: (i // per, i % per, 0, 0)),
                   pl.BlockSpec((1, A_KV_HEADS, sub, DSA_TK, A_HEAD_DIM), lambda i: (i // per, 0, i % per, 0, 0)),
                   pl.BlockSpec((1, A_KV_HEADS, sub, A_HEAD_DIM, DSA_TK), lambda i: (i // per, 0, i % per, 0, 0)),
                   pl.BlockSpec((1, 2 * B_HEADS, tm), lambda i: (i // per, 0, i % per))],
        out_shape=[sd((n, MAIN_WIDTH), f32), sd((bsz, IDX_HEADS, IDX_DIM, seq), bf),
                   sd((bsz, A_HEADS, A_HEAD_DIM, seq), bf), sd((bsz, IDX_HEADS, seq), f32),
                   sd((bsz, n_kv, DSA_TK, IDX_DIM), bf), sd((bsz, A_KV_HEADS, n_kv, DSA_TK, A_HEAD_DIM), bf),
                   sd((bsz, A_KV_HEADS, n_kv, A_HEAD_DIM, DSA_TK), bf), sd((bsz, 2 * B_HEADS, seq), f32)],
        compiler_params=pltpu.CompilerParams(dimension_semantics=("arbitrary",), vmem_limit_bytes=PROJ_VMEM_LIMIT),
        name="in_proj")(x2, scale, shift, w_perm, jnp.cos(ang), jnp.sin(ang),
                        jnp.stack([a_log, dt_bias], axis=-1).astype(f32))


def _merge_body(oa_ref, ob_ref, z_ref, ga_ref, gb_ref, x_ref, gt_ref, sc_ref, sh_ref, wpa_ref, wpb_ref, wo_ref,
                h_ref, n2_ref):
    f32, bf = jnp.float32, jnp.bfloat16
    z = z_ref[...]
    ob = ob_ref[...] * (z * _sigmoid(z))
    oa = oa_ref[0].reshape(A_WIDTH, oa_ref.shape[-1]).T
    ya = jnp.dot(oa.astype(bf), wpa_ref[...], preferred_element_type=f32)
    yb = jnp.dot(ob.astype(bf), wpb_ref[...], preferred_element_type=f32)
    merged = _sigmoid(ga_ref[...]) * ya + _sigmoid(gb_ref[...]) * yb
    y1 = jnp.dot(merged.astype(bf), wo_ref[...], preferred_element_type=f32)
    h = x_ref[...] + gt_ref[0] * y1
    h_ref[...] = h
    n2_ref[...] = _modulated_norm(h, sc_ref[0], sh_ref[0])


def merge(o_a, o_b, proj, x2, gt1, scale2, shift2, w_pa, w_pb, w_o, seq):
    n, d = x2.shape
    tm = PROJ_TM
    per = seq // tm
    bf = jnp.bfloat16
    vec = pl.BlockSpec((1, 1, d), lambda i: (i // per, 0, 0))

    def col(name):
        off, w = PROJ_DST[name]
        return pl.BlockSpec((tm, w), lambda i: (i, off // w))
    row = lambda w: pl.BlockSpec((tm, w), lambda i: (i, 0))
    res = lambda a, b: pl.BlockSpec((a, b), lambda i: (0, 0))
    return pl.pallas_call(
        _merge_body, grid=(n // tm,),
        in_specs=[pl.BlockSpec((1, A_HEADS, A_HEAD_DIM, tm), lambda i: (i // per, 0, 0, i % per)), row(B_V_WIDTH), col("zb"), col("gate_a"), col("gate_b"), row(d), vec, vec, vec,
                  res(A_WIDTH, d), res(B_V_WIDTH, d), res(d, d)],
        out_specs=[row(d), row(d)],
        out_shape=[jax.ShapeDtypeStruct((n, d), jnp.float32), jax.ShapeDtypeStruct((n, d), jnp.float32)],
        compiler_params=pltpu.CompilerParams(dimension_semantics=("arbitrary",), vmem_limit_bytes=PROJ_VMEM_LIMIT),
        name="merge")(o_a, o_b, proj, proj, proj, x2, gt1, scale2, shift2,
                      w_pa.astype(bf), w_pb.astype(bf), w_o.astype(bf))


PEER_SLOTS = PEER_HEADS * PEER_TOPK
PEER_TB = 128
HALF_ROWS = 4
HI_MASK = -65536


def pack_table(tab, pad=False):
    bits = lax.bitcast_convert_type(tab.astype(jnp.bfloat16), jnp.uint16).astype(jnp.uint32)
    half = tab.shape[1] // 2
    word = bits[:, :half] | (bits[:, half:] << 16)
    rows = lax.bitcast_convert_type(word, jnp.int32).reshape(tab.shape[0] * HALF_ROWS, LANES)
    return jnp.pad(rows, ((HALF_ROWS, HALF_ROWS), (0, 0))) if pad else rows


def pair_load_starts(idx):
    even = (1 - jnp.arange(idx.shape[1], dtype=jnp.int32) % 2)[None, :]
    return idx + HALF_ROWS * even


def _table_rows(tab_ref, starts, k):
    span = 2 * HALF_ROWS
    a = tab_ref[pl.ds(pl.multiple_of(starts[k], HALF_ROWS), span), :]
    b = tab_ref[pl.ds(pl.multiple_of(starts[k + 1], HALF_ROWS), span), :]
    sub = lax.broadcasted_iota(jnp.int32, (span, LANES), 0)
    return _unpack(jnp.where(sub < HALF_ROWS, a, b))


def _table_row(tab_ref, rows, k):
    return _unpack(tab_ref[pl.ds(pl.multiple_of(rows[k], HALF_ROWS), HALF_ROWS), :])


def _unpack(w):
    lo = lax.bitcast_convert_type(lax.shift_left(w, 16), jnp.float32)
    hi = lax.bitcast_convert_type(w & jnp.int32(HI_MASK), jnp.float32)
    return lo, hi


def _peer_u_body(idx_ref, x_ref, gate_ref, tab_ref, o_ref, s_ref, sb_ref):
    def lane_sums(t):
        a = jnp.sum(sb_ref[t].T, axis=0, keepdims=True)
        o_ref[pl.ds(t, 1), :] = 0.5 * a * (1.0 + lax.erf(a * (2.0 ** -0.5))) * gate_ref[pl.ds(t, 1), :]

    sb_ref[0] = jnp.zeros(sb_ref.shape[1:], jnp.float32)

    def tok(t, carry):
        lane_sums(jnp.maximum(t - 1, 0))
        x8 = x_ref[pl.ds(t, 1), :].reshape(2 * HALF_ROWS, LANES)
        xlo = jnp.concatenate([x8[0:HALF_ROWS]] * 2, axis=0)
        xhi = jnp.concatenate([x8[HALF_ROWS:2 * HALF_ROWS]] * 2, axis=0)
        starts = idx_ref.at[t]
        for k in range(0, PEER_SLOTS, 2):
            lo, hi = _table_rows(tab_ref, starts, k)
            s_ref[HALF_ROWS * k:HALF_ROWS * (k + 2), :] = lo * xlo + hi * xhi
        s4 = s_ref[pl.ds(0, PEER_SLOTS, stride=HALF_ROWS), :]
        for r in range(1, HALF_ROWS):
            s4 = s4 + s_ref[pl.ds(r, PEER_SLOTS, stride=HALF_ROWS), :]
        sb_ref[t] = s4
        return carry
    lax.fori_loop(0, PEER_TB, tok, 0, unroll=2)
    lane_sums(PEER_TB - 1)


def _peer_v_body(idx_ref, coef_ref, tab_ref, o_ref, cb_ref):
    def spread(t):
        return jnp.broadcast_to(coef_ref[pl.ds(t, 1), :], (PEER_SLOTS, LANES)).T

    def tok(t, weights):
        nxt = spread(jnp.minimum(t + 1, PEER_TB - 1))
        cb_ref[...] = weights
        nacc = 2
        acc = [jnp.zeros((HALF_ROWS, LANES), jnp.float32) for _ in range(2 * nacc)]
        rows = idx_ref.at[t]
        for k in range(PEER_SLOTS):
            lo, hi = _table_row(tab_ref, rows, k)
            c = jnp.broadcast_to(cb_ref[k:k + 1, :], (HALF_ROWS, LANES))
            a = k % nacc
            acc[2 * a] = acc[2 * a] + c * lo
            acc[2 * a + 1] = acc[2 * a + 1] + c * hi
        half = HALF_ROWS * LANES
        o_ref[pl.ds(t, 1), 0:half] = (acc[0] + acc[2]).reshape(1, half)
        o_ref[pl.ds(t, 1), half:2 * half] = (acc[1] + acc[3]).reshape(1, half)
        return nxt
    lax.fori_loop(0, PEER_TB, tok, spread(0))


def _table_spec(tab):
    return pl.BlockSpec(tab.shape, lambda i: (0, 0), pipeline_mode=pl.Buffered(1))


def peer_u(idx, x, gates, tab):
    n = idx.shape[0]
    tb = PEER_TB
    return pl.pallas_call(
        _peer_u_body, grid=(n // tb,),
        in_specs=[pl.BlockSpec((tb, PEER_SLOTS), lambda i: (i, 0), memory_space=pltpu.SMEM),
                  pl.BlockSpec((tb, x.shape[1]), lambda i: (i, 0)),
                  pl.BlockSpec((tb, PEER_SLOTS), lambda i: (i, 0)),
                  _table_spec(tab)],
        out_specs=pl.BlockSpec((tb, PEER_SLOTS), lambda i: (i, 0)),
        out_shape=jax.ShapeDtypeStruct((n, PEER_SLOTS), jnp.float32),
        scratch_shapes=[pltpu.VMEM((HALF_ROWS * PEER_SLOTS, LANES), jnp.float32),
                        pltpu.VMEM((tb, PEER_SLOTS, LANES), jnp.float32)],
        compiler_params=pltpu.CompilerParams(dimension_semantics=("arbitrary",), vmem_limit_bytes=PEER_VMEM_LIMIT),
        name="peer_u")(idx, x, gates, tab)


def peer_v(idx, coef, tab):
    n = idx.shape[0]
    tb = PEER_TB
    return pl.pallas_call(
        _peer_v_body, grid=(n // tb,),
        in_specs=[pl.BlockSpec((tb, PEER_SLOTS), lambda i: (i, 0), memory_space=pltpu.SMEM),
                  pl.BlockSpec((tb, PEER_SLOTS), lambda i: (i, 0)),
                  _table_spec(tab)],
        out_specs=pl.BlockSpec((tb, 2 * HALF_ROWS * LANES), lambda i: (i, 0)),
        out_shape=jax.ShapeDtypeStruct((n, 2 * HALF_ROWS * LANES), jnp.float32),
        scratch_shapes=[pltpu.VMEM((PEER_SLOTS, LANES), jnp.float32)],
        compiler_params=pltpu.CompilerParams(dimension_semantics=("arbitrary",), vmem_limit_bytes=PEER_VMEM_LIMIT),
        name="peer_v")(idx, coef, tab)


PEER_TT = 256


def _extract_top(ref, n_out, rid=None):
    rows, t = ref.shape
    if rid is None:
        rid = lax.broadcasted_iota(jnp.int32, (rows, t), 0)
    vals, idxs = [], []
    for _ in range(n_out):
        s = ref[...]
        m = jnp.max(s, axis=0, keepdims=True)
        ix = jnp.min(jnp.where(s == m, rid, jnp.int32(PAD_ID)), axis=0, keepdims=True)
        ref[...] = jnp.where(rid == ix, -jnp.inf, s)
        vals.append(m)
        idxs.append(ix)
    return vals, idxs


def _peer_route_body(x_ref, wq_ref, sk_ref, cid_ref, idx_ref, gate_ref, s_ref, cand_ref, v_ref, i_ref, et_ref, gt_ref):
    kk, nk = PEER_TOPK, PEER_N_KEYS
    f32 = jnp.float32
    q = jnp.dot(x_ref[...].astype(jnp.bfloat16), wq_ref[...], preferred_element_type=f32).astype(jnp.bfloat16)
    nt = (((1,), (1,)), ((), ()))
    row16 = lax.broadcasted_iota(jnp.int32, (kk, PEER_TT), 0)
    for h in range(PEER_HEADS):
        qh = q[:, h * PEER_KEY_DIM:(h + 1) * PEER_KEY_DIM]
        s_ref[...] = lax.dot_general(sk_ref[h], qh, nt, preferred_element_type=f32)
        for p in range(2):
            vals, idxs = _extract_top(s_ref.at[p * nk:(p + 1) * nk, :], kk)
            for i in range(kk):
                v_ref[p, i:i + 1, :] = vals[i]
                i_ref[p, i:i + 1, :] = idxs[i]
        off = 0
        for i in range(kk):
            n_j = kk // (i + 1)
            cand_ref[off:off + n_j, :] = v_ref[0, i:i + 1, :] + v_ref[1, 0:n_j, :]
            off += n_j
        cand_ref[off:, :] = jnp.full((cand_ref.shape[0] - off, PEER_TT), -jnp.inf, f32)
        vals, cis = _extract_top(cand_ref, kk, cid_ref[...])
        i1, i2 = i_ref[0], i_ref[1]
        es = [jnp.exp(v - vals[0]) for v in vals]
        den = es[0]
        for e in es[1:]:
            den = den + e
        for k in range(kk):
            ci = cis[k]
            e1 = jnp.sum(jnp.where(row16 == lax.shift_right_logical(ci, kk.bit_length() - 1), i1, 0), axis=0, keepdims=True)
            e2 = jnp.sum(jnp.where(row16 == (ci & (kk - 1)), i2, 0), axis=0, keepdims=True)
            et_ref[h * kk + k:h * kk + k + 1, :] = (e1 * nk + e2) * HALF_ROWS
            gt_ref[h * kk + k:h * kk + k + 1, :] = es[k] / den
    idx_ref[...] = lax.bitcast_convert_type(lax.bitcast_convert_type(et_ref[...], f32).T, jnp.int32)
    gate_ref[...] = gt_ref[...].T


def _pair_cells():
    kk = PEER_TOPK
    ids = [i * kk + j for i in range(kk) for j in range(kk // (i + 1))]
    return ids + [PAD_ID] * (-len(ids) % SUBLANES)


def peer_route(xn, wq, subkeys):
    n = xn.shape[0]
    tt = PEER_TT
    cells = _pair_cells()
    cid = jnp.broadcast_to(jnp.asarray(cells, jnp.int32)[:, None], (len(cells), tt))
    half = PEER_KEY_DIM // 2
    z = jnp.zeros((PEER_HEADS, PEER_N_KEYS, half), subkeys.dtype)
    skbd = jnp.concatenate([jnp.concatenate([subkeys[:, 0], z], axis=-1),
                            jnp.concatenate([z, subkeys[:, 1]], axis=-1)], axis=1).astype(jnp.bfloat16)
    return pl.pallas_call(
        _peer_route_body, grid=(n // tt,),
        in_specs=[pl.BlockSpec((tt, D_MODEL), lambda i: (i, 0)),
                  pl.BlockSpec((D_MODEL, PEER_HEADS * PEER_KEY_DIM), lambda i: (0, 0)),
                  pl.BlockSpec((PEER_HEADS, 2 * PEER_N_KEYS, PEER_KEY_DIM), lambda i: (0, 0, 0)),
                  pl.BlockSpec((len(cells), tt), lambda i: (0, 0))],
        out_specs=[pl.BlockSpec((tt, PEER_SLOTS), lambda i: (i, 0)), pl.BlockSpec((tt, PEER_SLOTS), lambda i: (i, 0))],
        out_shape=[jax.ShapeDtypeStruct((n, PEER_SLOTS), jnp.int32), jax.ShapeDtypeStruct((n, PEER_SLOTS), jnp.float32)],
        scratch_shapes=[pltpu.VMEM((2 * PEER_N_KEYS, tt), jnp.float32),
                        pltpu.VMEM((len(cells), tt), jnp.float32),
                        pltpu.VMEM((2, PEER_TOPK, tt), jnp.float32),
                        pltpu.VMEM((2, PEER_TOPK, tt), jnp.int32),
                        pltpu.VMEM((PEER_SLOTS, tt), jnp.int32),
                        pltpu.VMEM((PEER_SLOTS, tt), jnp.float32)],
        compiler_params=pltpu.CompilerParams(dimension_semantics=("arbitrary",), vmem_limit_bytes=ROUTE_VMEM_LIMIT),
        name="peer_route")(xn, wq.astype(jnp.bfloat16), skbd, cid)


def peer_channel_mixer(xn, wq, subkeys, u_tab, v_tab):
    bsz, seq, d = xn.shape
    n_tok = bsz * seq
    x2 = xn.reshape(n_tok, d)
    idx, gates = peer_route(x2, wq, subkeys)
    coef = peer_u(pair_load_starts(idx), x2, gates, pack_table(u_tab, pad=True))
    out = peer_v(idx, coef, pack_table(v_tab))
    return out.reshape(bsz, seq, d)


def _residual_body(h_ref, y_ref, gt_ref, o_ref):
    o_ref[...] = h_ref[...] + gt_ref[0] * y_ref[...]


def _final_norm_body(h_ref, y_ref, gt_ref, g_ref, o_ref):
    h = h_ref[...] + gt_ref[0] * y_ref[...]
    o_ref[...] = h * lax.rsqrt(jnp.mean(h * h, axis=-1, keepdims=True) + EPS) * g_ref[...]


def gated_residual(h2, y2, gt, seq, gain=None):
    n, d = h2.shape
    tm = min(1024, seq)
    per = seq // tm
    row = pl.BlockSpec((tm, d), lambda i: (i, 0))
    specs = [row, row, pl.BlockSpec((1, 1, d), lambda i: (i // per, 0, 0))]
    args = [h2, y2, gt]
    if gain is not None:
        specs.append(pl.BlockSpec((1, d), lambda i: (0, 0)))
        args.append(gain.astype(jnp.float32).reshape(1, d))
    return pl.pallas_call(
        _residual_body if gain is None else _final_norm_body, grid=(n // tm,),
        in_specs=specs, out_specs=row, out_shape=jax.ShapeDtypeStruct((n, d), h2.dtype),
        name="gated_residual" if gain is None else "final_norm")(*args)


def kernel(x, c, positions, w_ada, b_ada, w_in, conv_w, a_log, dt_bias, norm_b_w,
           w_pa, w_pb, w_o, peer_wq, peer_subkeys, peer_u, peer_v, final_norm_w):
    bsz, seq, d = x.shape
    h = x.reshape(bsz * seq, d)
    for layer in range(DEPTH):
        mod = ada_modulation(c, w_ada[layer], b_ada[layer])
        sh1, sc1, gt1, sh2, sc2, gt2 = [m.reshape(bsz, 1, d) for m in jnp.split(mod, 6, axis=-1)]
        proj, qit, qat, w, kit, kat, vat, bg = in_proj(h, 1.0 + sc1, sh1, permute_w_in(w_in[layer]), positions,
                                                       a_log[layer], dt_bias[layer], bsz, seq)
        o_a = dsa_attention_pallas(qit, w, qat, kit, kat, vat)
        o_b = gated_delta_rule_pallas(proj, conv_w[layer], bg[:, B_HEADS:], bg[:, :B_HEADS], norm_b_w[layer],
                                      bsz, seq)
        h, n2 = merge(o_a, o_b.reshape(bsz * seq, B_V_WIDTH), proj, h, gt1,
                      1.0 + sc2, sh2, w_pa[layer], w_pb[layer], w_o[layer], seq)
        y2 = peer_channel_mixer(n2.reshape(bsz, seq, d), peer_wq[layer], peer_subkeys[layer], peer_u[layer],
                                peer_v[layer])
        last = layer == DEPTH - 1
        h = gated_residual(h, y2.reshape(bsz * seq, d), gt2, seq, final_norm_w if last else None)
    return h.reshape(bsz, seq, d)
```
